```python
import math
import jax
import jax.numpy as jnp
from jax import lax
import numpy as np

D_MODEL = 4096
BATCH = 4
SEQ = 2048
DEPTH = 1
DEC_BATCH = 128
DEC_SEQ = 4
PAST_LEN = 16384
PAGE_SIZE = 128

MIX_W = D_MODEL
RET_HEADS = 8
RET_DK = MIX_W // 2 // RET_HEADS
RET_DV = MIX_W // 2 // RET_HEADS
GLA_HEADS = 4
GLA_DK = MIX_W // 4 // GLA_HEADS
GLA_DV = MIX_W // 2 // GLA_HEADS
GLA_RANK = 16
GLA_GATE_TEMP = 16.0
RET_CHUNK = 128
GLA_CHUNK = 16
ROPE_BASE = 10000.0
N_GROUPS = 4
EXPERTS_PER_GROUP = 8
N_EXPERTS = N_GROUPS * EXPERTS_PER_GROUP
TOP_K = 2
D_EXPERT = D_MODEL // 4
MOE_BLOCK = 128
PLE_DIM = 256
EPS = 1e-6

SPLITS = (RET_HEADS * RET_DK, RET_HEADS * RET_DK, RET_HEADS * RET_DV, RET_HEADS * RET_DV,
          GLA_HEADS * GLA_DK, GLA_HEADS * GLA_DK, GLA_HEADS * GLA_DV, GLA_HEADS * GLA_DV, GLA_RANK)
N_IN = sum(SPLITS)

kernel_name = "hymba_retention_gla_hmoe_step"


def rms_norm(x, g):
    xf = x.astype(jnp.float32)
    y = xf * lax.rsqrt(jnp.mean(xf * xf, axis=-1, keepdims=True) + EPS)
    return (y * g.astype(jnp.float32)).astype(x.dtype)


def rotary(x, pos):
    half = x.shape[-1] // 2
    inv = ROPE_BASE ** (-jnp.arange(half, dtype=jnp.float32) / half)
    ang = pos.astype(jnp.float32)[:, None] * inv[None, :]
    cos = jnp.cos(ang)[None, :, None, :]
    sin = jnp.sin(ang)[None, :, None, :]
    x1, x2 = x[..., :half], x[..., half:]
    return jnp.concatenate([x1 * cos - x2 * sin, x1 * sin + x2 * cos], axis=-1)


def to_chunks(x, c):
    b, l = x.shape[:2]
    return x.reshape(b, l // c, c, *x.shape[2:]).swapaxes(0, 1)


def from_chunks(x):
    n, b, c = x.shape[:3]
    return x.swapaxes(0, 1).reshape(b, n * c, *x.shape[3:])


def retention(q, k, v, s0):
    L = q.shape[1]
    c = math.gcd(L, RET_CHUNK)
    log_g = jnp.log1p(-(2.0 ** (-5.0 - jnp.arange(RET_HEADS, dtype=jnp.float32))))
    i = jnp.arange(c, dtype=jnp.float32)
    diff = i[:, None] - i[None, :]
    intra = jnp.where(diff[None] >= 0,
                      jnp.exp(jnp.maximum(diff, 0.0)[None] * log_g[:, None, None]), 0.0)
    q_dec = jnp.exp((i[:, None] + 1.0) * log_g[None, :])
    k_dec = jnp.exp((c - 1.0 - i)[:, None] * log_g[None, :])
    c_dec = jnp.exp(c * log_g)

    def step(s, xs):
        qc, kc, vc = xs
        att = jnp.einsum('bqhd,bkhd->bhqk', qc, kc) * intra[None]
        o = (jnp.einsum('bhqk,bkhe->bqhe', att, vc)
             + jnp.einsum('bqhd,bhde->bqhe', qc * q_dec[None, :, :, None], s))
        s = (s * c_dec[None, :, None, None]
             + jnp.einsum('bkhd,bkhe->bhde', kc * k_dec[None, :, :, None], vc))
        return s, o

    s, o = lax.scan(step, s0, (to_chunks(q, c), to_chunks(k, c), to_chunks(v, c)))
    return from_chunks(o), s


def gla(q, k, v, log_a, s0):
    L = q.shape[1]
    c = math.gcd(L, GLA_CHUNK)
    mask = jnp.tril(jnp.ones((c, c), dtype=bool))

    def step(s, xs):
        qc, kc, vc, ac = xs
        b = jnp.cumsum(ac, axis=1)
        qb = qc * jnp.exp(b)
        kb = kc * jnp.exp(-b)
        att = jnp.where(mask[None, None], jnp.einsum('bqhd,bkhd->bhqk', qb, kb), 0.0)
        o = jnp.einsum('bhqk,bkhe->bqhe', att, vc) + jnp.einsum('bqhd,bhde->bqhe', qb, s)
        b_last = b[:, -1]
        s = (s * jnp.exp(b_last)[..., None]
             + jnp.einsum('bkhd,bkhe->bhde', kc * jnp.exp(b_last[:, None] - b), vc))
        return s, o

    s, o = lax.scan(step, s0, (to_chunks(q, c), to_chunks(k, c), to_chunks(v, c), to_chunks(log_a, c)))
    return from_chunks(o), s


def mixer(h, s_ret, s_gla, pos, w_in, w_gla_up, b_gla, ret_norm_g, gla_norm_g, w_out):
    f32 = jnp.float32
    B, L, _ = h.shape
    proj = h @ w_in
    points, acc = [], 0
    for s in SPLITS[:-1]:
        acc += s
        points.append(acc)
    rq, rk, rv, rg, gq, gk, gv, gg, ga = jnp.split(proj, points, axis=-1)

    rq = rotary(rq.reshape(B, L, RET_HEADS, RET_DK).astype(f32), pos)
    rk = rotary(rk.reshape(B, L, RET_HEADS, RET_DK).astype(f32), pos) * (RET_DK ** -0.5)
    rv = rv.reshape(B, L, RET_HEADS, RET_DV).astype(f32)
    ro, s_ret = retention(rq, rk, rv, s_ret.astype(f32))
    mu = jnp.mean(ro, axis=-1, keepdims=True)
    var = jnp.mean(jnp.square(ro - mu), axis=-1, keepdims=True)
    ro = (ro - mu) * lax.rsqrt(var + EPS) * ret_norm_g.astype(f32)[None, None]
    ro = ro.reshape(B, L, RET_HEADS * RET_DV) * jax.nn.silu(rg.astype(f32))

    gq = gq.reshape(B, L, GLA_HEADS, GLA_DK).astype(f32)
    gk = gk.reshape(B, L, GLA_HEADS, GLA_DK).astype(f32) * (GLA_DK ** -0.5)
    gv = gv.reshape(B, L, GLA_HEADS, GLA_DV).astype(f32)
    z = (ga @ w_gla_up + b_gla).astype(f32)
    log_a = (jax.nn.log_sigmoid(z) / GLA_GATE_TEMP).reshape(B, L, GLA_HEADS, GLA_DK)
    go, s_gla = gla(gq, gk, gv, log_a, s_gla.astype(f32))
    go = go * lax.rsqrt(jnp.mean(go * go, axis=-1, keepdims=True) + EPS) * gla_norm_g.astype(f32)[None, None]
    go = go.reshape(B, L, GLA_HEADS * GLA_DV) * jax.nn.silu(gg.astype(f32))

    mixed = jnp.concatenate([ro, go], axis=-1).astype(h.dtype)
    return mixed @ w_out, s_ret, s_gla


def hier_moe(h, w_rg, b_rg, w_re, b_re, w_gate, w_up, w_down):
    f32 = jnp.float32
    B, L, D = h.shape
    xt = h.reshape(B * L, D)
    T = xt.shape[0]
    hf = xt.astype(f32)
    g_prob = jax.nn.softmax(hf @ w_rg.astype(f32) + b_rg.astype(f32), axis=-1)
    g_idx = jnp.argmax(g_prob, axis=-1).astype(jnp.int32)
    g_p = jnp.take_along_axis(g_prob, g_idx[:, None], axis=-1)
    e_logits = (hf @ w_re.astype(f32) + b_re.astype(f32)).reshape(T, N_GROUPS, EXPERTS_PER_GROUP)
    e_sel = jnp.take_along_axis(e_logits, g_idx[:, None, None], axis=1)[:, 0]
    e_prob = jax.nn.softmax(e_sel, axis=-1)
    top_p, top_i = lax.top_k(e_prob, TOP_K)
    gate = g_p * top_p / jnp.sum(top_p, axis=-1, keepdims=True)
    expert_idx = g_idx[:, None] * EXPERTS_PER_GROUP + top_i.astype(jnp.int32)

    A = T * TOP_K
    flat_e = expert_idx.reshape(A)
    flat_tok = jnp.repeat(jnp.arange(T, dtype=jnp.int32), TOP_K)
    flat_w = gate.reshape(A)
    order = jnp.argsort(flat_e)
    sorted_e = flat_e[order]
    counts = jnp.bincount(flat_e, length=N_EXPERTS)
    padded = (counts + MOE_BLOCK - 1) // MOE_BLOCK * MOE_BLOCK
    padded_end = jnp.cumsum(padded)
    padded_start = padded_end - padded
    start = jnp.cumsum(counts) - counts
    dest = padded_start[sorted_e] + (jnp.arange(A, dtype=jnp.int32) - start[sorted_e])
    n_blocks = -(-(A + N_EXPERTS * (MOE_BLOCK - 1)) // MOE_BLOCK)
    P = n_blocks * MOE_BLOCK
    row_tok = jnp.zeros((P,), jnp.int32).at[dest].set(flat_tok[order])
    row_w = jnp.zeros((P,), f32).at[dest].set(flat_w[order])
    block_expert = jnp.minimum(
        jnp.searchsorted(padded_end, jnp.arange(n_blocks) * MOE_BLOCK, side='right'),
        N_EXPERTS - 1).astype(jnp.int32)
    xs = xt[row_tok].reshape(n_blocks, MOE_BLOCK, D)

    def block_ffn(args):
        xb, e = args
        hid = jax.nn.silu(xb @ w_gate[e]) * (xb @ w_up[e])
        return hid @ w_down[e]

    yb = lax.map(block_ffn, (xs, block_expert)).reshape(P, D)
    y = jax.ops.segment_sum(yb * row_w[:, None].astype(yb.dtype), row_tok, num_segments=T)
    return y.reshape(B, L, D).astype(h.dtype)


def layer(x, p, s_ret, s_gla, pos, g_mix, w_in, w_gla_up, b_gla, ret_norm_g, gla_norm_g, w_out,
          g_moe, w_rg, b_rg, w_re, b_re, w_gate, w_up, w_down, w_pg, w_pp):
    m, s_ret, s_gla = mixer(rms_norm(x, g_mix), s_ret, s_gla, pos, w_in, w_gla_up, b_gla,
                            ret_norm_g, gla_norm_g, w_out)
    x = x + m
    x = x + hier_moe(rms_norm(x, g_moe), w_rg, b_rg, w_re, b_re, w_gate, w_up, w_down)
    x = x + jax.nn.sigmoid(x @ w_pg) * (p.astype(x.dtype) @ w_pp)
    return x, s_ret, s_gla


def setup_inputs(seed: int = 0) -> dict:
    key = jax.random.key(seed)
    ks = jax.random.split(key, 26)
    f32 = jnp.float32

    def nrm(k, shape, scale):
        return jax.random.normal(k, shape, f32) * scale

    gw = GLA_HEADS * GLA_DK
    return {
        "x_prompt": nrm(ks[0], (BATCH, SEQ, D_MODEL), 1.0),
        "x_sample": nrm(ks[1], (DEC_BATCH, DEC_SEQ, D_MODEL), 1.0),
        "state_ret": nrm(ks[2], (DEPTH, DEC_BATCH, RET_HEADS, RET_DK, RET_DV), 0.1),
        "state_gla": nrm(ks[3], (DEPTH, DEC_BATCH, GLA_HEADS, GLA_DK, GLA_DV), 0.1),
        "p_prompt": nrm(ks[4], (DEPTH, BATCH, SEQ, PLE_DIM), 1.0),
        "p_sample": nrm(ks[5], (DEPTH, DEC_BATCH, DEC_SEQ, PLE_DIM), 1.0),
        "g_mix": 1.0 + nrm(ks[6], (DEPTH, D_MODEL), 0.02),
        "w_in": nrm(ks[7], (DEPTH, D_MODEL, N_IN), D_MODEL ** -0.5),
        "w_gla_up": nrm(ks[8], (DEPTH, GLA_RANK, gw), GLA_RANK ** -0.5),
        "b_gla": nrm(ks[9], (DEPTH, gw), 0.1),
        "ret_norm_g": 1.0 + nrm(ks[10], (DEPTH, RET_HEADS, RET_DV), 0.02),
        "gla_norm_g": 1.0 + nrm(ks[11], (DEPTH, GLA_HEADS, GLA_DV), 0.02),
        "w_out": nrm(ks[12], (DEPTH, MIX_W, D_MODEL), MIX_W ** -0.5),
        "g_moe": 1.0 + nrm(ks[13], (DEPTH, D_MODEL), 0.02),
        "w_rg": nrm(ks[14], (DEPTH, D_MODEL, N_GROUPS), D_MODEL ** -0.5),
        "b_rg": nrm(ks[15], (DEPTH, N_GROUPS), 0.01),
        "w_re": nrm(ks[16], (DEPTH, D_MODEL, N_EXPERTS), D_MODEL ** -0.5),
        "b_re": nrm(ks[17], (DEPTH, N_EXPERTS), 0.01),
        "w_gate": nrm(ks[18], (DEPTH, N_EXPERTS, D_MODEL, D_EXPERT), D_MODEL ** -0.5),
        "w_up": nrm(ks[19], (DEPTH, N_EXPERTS, D_MODEL, D_EXPERT), D_MODEL ** -0.5),
        "w_down": nrm(ks[20], (DEPTH, N_EXPERTS, D_EXPERT, D_MODEL), D_EXPERT ** -0.5),
        "w_pg": nrm(ks[21], (DEPTH, D_MODEL, D_MODEL), D_MODEL ** -0.5),
        "w_pp": nrm(ks[22], (DEPTH, PLE_DIM, D_MODEL), PLE_DIM ** -0.5),
        "g_final": 1.0 + nrm(ks[23], (D_MODEL,), 0.02),
    }


def reference(x_prompt, x_sample, state_ret, state_gla, p_prompt, p_sample,
              g_mix, w_in, w_gla_up, b_gla, ret_norm_g, gla_norm_g, w_out,
              g_moe, w_rg, b_rg, w_re, b_re, w_gate, w_up, w_down, w_pg, w_pp, g_final):
    f32 = jnp.float32
    n_p, L_p = x_prompt.shape[0], x_prompt.shape[1]
    L_s = x_sample.shape[1]
    pos_p = jnp.arange(L_p, dtype=jnp.int32)
    pos_s = PAST_LEN + jnp.arange(L_s, dtype=jnp.int32)
    zero_ret = jnp.zeros((n_p, RET_HEADS, RET_DK, RET_DV), f32)
    zero_gla = jnp.zeros((n_p, GLA_HEADS, GLA_DK, GLA_DV), f32)
    xp, xs = x_prompt, x_sample
    ret_p, ret_s, gla_p, gla_s = [], [], [], []
    for l in range(DEPTH):
        lw = (g_mix[l], w_in[l], w_gla_up[l], b_gla[l], ret_norm_g[l], gla_norm_g[l], w_out[l],
              g_moe[l], w_rg[l], b_rg[l], w_re[l], b_re[l], w_gate[l], w_up[l], w_down[l],
              w_pg[l], w_pp[l])
        xp, sr, sg = layer(xp, p_prompt[l], zero_ret, zero_gla, pos_p, *lw)
        ret_p.append(sr)
        gla_p.append(sg)
        xs, sr, sg = layer(xs, p_sample[l], state_ret[l], state_gla[l], pos_s, *lw)
        ret_s.append(sr)
        gla_s.append(sg)
    y_prompt = rms_norm(xp, g_final)
    y_sample = rms_norm(xs, g_final)
    new_ret_prompt = jnp.stack(ret_p).astype(state_ret.dtype)
    new_ret_sample = jnp.stack(ret_s).astype(state_ret.dtype)
    new_gla_prompt = jnp.stack(gla_p).astype(state_gla.dtype)
    new_gla_sample = jnp.stack(gla_s).astype(state_gla.dtype)
    return (y_prompt, y_sample, new_ret_prompt, new_ret_sample, new_gla_prompt, new_gla_sample)
```

```python
import functools
import math

import jax
import jax.numpy as jnp
from jax import lax
from jax.experimental import pallas as pl
from jax.experimental.pallas import tpu as pltpu

F32 = jnp.float32
BF16 = jnp.bfloat16

D_MODEL = 4096
RET_HEADS = 8
RET_DK = 256
RET_DV = 256
GLA_HEADS = 4
GLA_DK = 256
GLA_DV = 512
GLA_RANK = 16
GLA_GATE_TEMP = 16.0
ROPE_BASE = 10000.0
PAST_LEN = 16384
N_GROUPS = 4
EXPERTS_PER_GROUP = 8
N_EXPERTS = N_GROUPS * EXPERTS_PER_GROUP
TOP_K = 2
D_EXPERT = D_MODEL // 4
EPS = 1e-6

RET_W = RET_HEADS * RET_DK
GLA_KW = GLA_HEADS * GLA_DK
GLA_VW = GLA_HEADS * GLA_DV
N_MAIN = 4 * RET_W + 2 * GLA_KW + 2 * GLA_VW
COL_RQ, COL_RK, COL_RV, COL_RG = 0, RET_W, 2 * RET_W, 3 * RET_W
COL_GQ = 4 * RET_W
COL_GK = COL_GQ + GLA_KW
COL_GV = COL_GK + GLA_KW
COL_GG = COL_GV + GLA_VW

LANES = 128
ROW_CHUNKS = D_MODEL // LANES
RET_CHUNK = 256
GLA_CHUNK = 128
GLA_SUB = 32
DEC_PAIR = 2
MOE_ROWS = 768
MOE_SUB = 256
MOE_TJ = 256
MOE_TN = 1024
VMEM_LIMIT = 56 * 1024 * 1024


def _cp(semantics, vmem=VMEM_LIMIT):
    return pltpu.CompilerParams(dimension_semantics=semantics, vmem_limit_bytes=vmem)


def _sigmoid(x):
    return 1.0 / (1.0 + jnp.exp(-x))


def _silu(x):
    return x * _sigmoid(x)


def _dot(a, b):
    return jnp.dot(a, b, preferred_element_type=F32)


def _dot_nt(a, b):
    return lax.dot_general(a, b, (((1,), (1,)), ((), ())), preferred_element_type=F32)


def _dot_tn(a, b):
    return lax.dot_general(a, b, (((0,), (0,)), ((), ())), preferred_element_type=F32)


def _norm_cast_kernel(x_ref, g_ref, o_ref):
    x = x_ref[...]
    ms = jnp.mean(x * x, axis=-1, keepdims=True)
    o_ref[...] = (x * lax.rsqrt(ms + EPS) * g_ref[...]).astype(o_ref.dtype)


def _norm_cast(x, g, tr, out_dtype):
    t, d = x.shape
    return pl.pallas_call(
        _norm_cast_kernel,
        grid=(t // tr,),
        in_specs=[pl.BlockSpec((tr, d), lambda i: (i, 0)), pl.BlockSpec((1, d), lambda i: (0, 0))],
        out_specs=pl.BlockSpec((tr, d), lambda i: (i, 0)),
        out_shape=jax.ShapeDtypeStruct((t, d), out_dtype),
        compiler_params=_cp(("parallel",)),
        name="norm_cast",
    )(x, g.reshape(1, d))


def _mm_kernel(a_ref, w_ref, o_ref):
    o_ref[...] = _dot(a_ref[...], w_ref[...])


def _mm_res_kernel(a_ref, w_ref, r_ref, o_ref):
    o_ref[...] = r_ref[...] + _dot(a_ref[...], w_ref[...])


def _mm_ple_kernel(a_ref, w_ref, p_ref, wp_ref, r_ref, o_ref):
    gate = _sigmoid(_dot(a_ref[...], w_ref[...]))
    emb = _dot(p_ref[...].astype(BF16), wp_ref[...])
    o_ref[...] = r_ref[...] + gate * emb


def _matmul(a, w, tm, tn, res=None, ple=None, name="matmul"):
    m, k = a.shape
    n = w.shape[1]
    a_spec = pl.BlockSpec((tm, k), lambda i, j: (i, 0))
    w_spec = pl.BlockSpec((k, tn), lambda i, j: (0, j))
    o_spec = pl.BlockSpec((tm, tn), lambda i, j: (i, j))
    if ple is not None:
        p, wp = ple
        kp = p.shape[1]
        kern = _mm_ple_kernel
        in_specs = [a_spec, w_spec, pl.BlockSpec((tm, kp), lambda i, j: (i, 0)),
                    pl.BlockSpec((kp, tn), lambda i, j: (0, j)), o_spec]
        args = (a, w, p, wp, res)
    elif res is not None:
        kern, in_specs, args = _mm_res_kernel, [a_spec, w_spec, o_spec], (a, w, res)
    else:
        kern, in_specs, args = _mm_kernel, [a_spec, w_spec], (a, w)
    return pl.pallas_call(
        kern,
        grid=(m // tm, n // tn),
        in_specs=in_specs,
        out_specs=o_spec,
        out_shape=jax.ShapeDtypeStruct((m, n), F32),
        compiler_params=_cp(("parallel", "parallel")),
        name=name,
    )(*args)


def _log_sigmoid(z):
    return jnp.minimum(z, 0.0) - jnp.log(1.0 + jnp.exp(-jnp.abs(z)))


def _loga_kernel(h_ref, wga_ref, wup_ref, b_ref, o_ref):
    ga = _dot(h_ref[...], wga_ref[...])
    z = _dot(ga.astype(BF16), wup_ref[...]) + b_ref[...]
    o_ref[...] = _log_sigmoid(z) * (1.0 / GLA_GATE_TEMP)


def _log_decay(h, w_ga, w_up, b, tr):
    t, d = h.shape
    return pl.pallas_call(
        _loga_kernel,
        grid=(t // tr,),
        in_specs=[pl.BlockSpec((tr, d), lambda i: (i, 0)),
                  pl.BlockSpec((d, LANES), lambda i: (0, 0)),
                  pl.BlockSpec((LANES, GLA_KW), lambda i: (0, 0)),
                  pl.BlockSpec((1, GLA_KW), lambda i: (0, 0))],
        out_specs=pl.BlockSpec((tr, GLA_KW), lambda i: (i, 0)),
        out_shape=jax.ShapeDtypeStruct((t, GLA_KW), F32),
        compiler_params=_cp(("parallel",)),
        name="gla_log_decay",
    )(h, w_ga, w_up, b.reshape(1, GLA_KW))


def _rotary(x, cos, sin):
    half = x.shape[-1] // 2
    x1, x2 = x[:, :half], x[:, half:]
    return jnp.concatenate([x1 * cos - x2 * sin, x1 * sin + x2 * cos], axis=-1)


def _group_norm_gate(o, gain, gate):
    mu = jnp.mean(o, axis=-1, keepdims=True)
    d = o - mu
    var = jnp.mean(d * d, axis=-1, keepdims=True)
    return d * lax.rsqrt(var + EPS) * gain * _silu(gate)


def _rms_gate(o, gain, gate):
    ms = jnp.mean(o * o, axis=-1, keepdims=True)
    return o * lax.rsqrt(ms + EPS) * gain * _silu(gate)


def _ret_prompt_kernel(lg_ref, q_ref, k_ref, v_ref, g_ref, cos_ref, sin_ref, gn_ref, o_ref, s_ref, state):
    c = RET_CHUNK
    i = pl.program_id(2)
    lg = lg_ref[pl.program_id(1)]

    @pl.when(i == 0)
    def _():
        state[...] = jnp.zeros_like(state)

    cos, sin = cos_ref[...], sin_ref[...]
    q = _rotary(q_ref[...], cos, sin)
    k = _rotary(k_ref[...], cos, sin) * (RET_DK ** -0.5)
    v = v_ref[...].astype(BF16)
    row = lax.broadcasted_iota(jnp.int32, (c, RET_DK), 0).astype(F32)
    q_dec = jnp.exp((row + 1.0) * lg)
    k_dec = jnp.exp((c - 1.0 - row) * lg)
    ii = lax.broadcasted_iota(jnp.int32, (c, c), 0)
    jj = lax.broadcasted_iota(jnp.int32, (c, c), 1)
    diff = (ii - jj).astype(F32)
    intra = jnp.where(diff >= 0, jnp.exp(jnp.maximum(diff, 0.0) * lg), 0.0)
    att = _dot_nt(q.astype(BF16), k.astype(BF16)) * intra
    s_old = state[...]
    o = _dot(att.astype(BF16), v) + _dot((q * q_dec).astype(BF16), s_old.astype(BF16))
    c_dec = jnp.exp(jnp.zeros((1, RET_DV), F32) + c * lg)
    s_new = s_old * c_dec + _dot_tn((k * k_dec).astype(BF16), v)
    state[...] = s_new
    o_ref[...] = _group_norm_gate(o, gn_ref[0], g_ref[...]).astype(o_ref.dtype)

    @pl.when(i == pl.num_programs(2) - 1)
    def _():
        s_ref[0, 0] = s_new


def _ret_prompt(proj, log_g, cos, sin, gn, batch, seq):
    c = RET_CHUNK
    nc = seq // c
    wblk = RET_W // RET_DK

    def col(group):
        return pl.BlockSpec((c, RET_DK), lambda b, h, i, lg: (b * nc + i, group * wblk + h))

    tab = pl.BlockSpec((c, RET_DK // 2), lambda b, h, i, lg: (i, 0))
    return pl.pallas_call(
        _ret_prompt_kernel,
        grid_spec=pltpu.PrefetchScalarGridSpec(
            num_scalar_prefetch=1,
            grid=(batch, RET_HEADS, nc),
            in_specs=[col(0), col(1), col(2), col(3), tab, tab,
                      pl.BlockSpec((1, 1, RET_DV), lambda b, h, i, lg: (h, 0, 0))],
            out_specs=[pl.BlockSpec((c, RET_DV), lambda b, h, i, lg: (b * nc + i, h)),
                       pl.BlockSpec((1, 1, RET_DK, RET_DV), lambda b, h, i, lg: (b, h, 0, 0))],
            scratch_shapes=[pltpu.VMEM((RET_DK, RET_DV), F32)],
        ),
        out_shape=[jax.ShapeDtypeStruct((batch * seq, RET_W), BF16),
                   jax.ShapeDtypeStruct((batch, RET_HEADS, RET_DK, RET_DV), F32)],
        compiler_params=_cp(("parallel", "parallel", "arbitrary")),
        name="retention_prompt",
    )(log_g, proj, proj, proj, proj, cos, sin, gn.reshape(RET_HEADS, 1, RET_DV))


def _ret_sample_kernel(log_g, seq, q_ref, k_ref, v_ref, g_ref, cos_ref, sin_ref, gn_ref, s_in, o_ref, s_out):
    rows = DEC_PAIR * seq
    cos, sin = cos_ref[...], sin_ref[...]
    rid = lax.broadcasted_iota(jnp.int32, (rows, RET_DK), 0)
    pos = (rid % seq).astype(F32)
    batch_of_row = rid // seq
    ii = lax.broadcasted_iota(jnp.int32, (rows, rows), 0)
    jj = lax.broadcasted_iota(jnp.int32, (rows, rows), 1)
    visible = (ii // seq == jj // seq) & (ii >= jj)
    diff = jnp.maximum(ii - jj, 0).astype(F32)
    for h in range(RET_HEADS):
        lg = log_g[h]
        sl = slice(h * RET_DK, (h + 1) * RET_DK)
        q = _rotary(q_ref[:, sl], cos, sin)
        k = _rotary(k_ref[:, sl], cos, sin) * (RET_DK ** -0.5)
        v = v_ref[:, sl].astype(BF16)
        intra = jnp.where(visible, jnp.exp(diff * lg), 0.0)
        att = _dot_nt(q.astype(BF16), k.astype(BF16)) * intra
        o = _dot(att.astype(BF16), v)
        qd = (q * jnp.exp((pos + 1.0) * lg)).astype(BF16)
        kd = k * jnp.exp((seq - 1.0 - pos) * lg)
        for b in range(DEC_PAIR):
            mine = batch_of_row == b
            s_old = s_in[b, h]
            o = o + jnp.where(mine, _dot(qd, s_old.astype(BF16)), 0.0)
            s_out[b, h] = s_old * math.exp(seq * lg) + _dot_tn(jnp.where(mine, kd, 0.0).astype(BF16), v)
        o_ref[:, sl] = _group_norm_gate(o, gn_ref[h], g_ref[:, sl]).astype(o_ref.dtype)


def _ret_sample(proj, row0, log_g, cos, sin, gn, state, batch, seq):
    rows = DEC_PAIR * seq
    blk0 = row0 // rows

    def col(group):
        return pl.BlockSpec((rows, RET_W), lambda i: (i + blk0, group))

    tab = pl.BlockSpec((rows, RET_DK // 2), lambda i: (0, 0))
    st = pl.BlockSpec((DEC_PAIR, RET_HEADS, RET_DK, RET_DV), lambda i: (i, 0, 0, 0))
    return pl.pallas_call(
        functools.partial(_ret_sample_kernel, log_g, seq),
        grid=(batch // DEC_PAIR,),
        in_specs=[col(0), col(1), col(2), col(3), tab, tab,
                  pl.BlockSpec((RET_HEADS, 1, RET_DV), lambda i: (0, 0, 0)), st],
        out_specs=[pl.BlockSpec((rows, RET_W), lambda i: (i, 0)), st],
        out_shape=[jax.ShapeDtypeStruct((batch * seq, RET_W), BF16),
                   jax.ShapeDtypeStruct(state.shape, F32)],
        compiler_params=_cp(("parallel",)),
        name="retention_sample",
    )(proj, proj, proj, proj, cos, sin, gn.reshape(RET_HEADS, 1, RET_DV), state)


def _split3(x):
    hi = x.astype(BF16)
    r1 = x - hi.astype(F32)
    mid = r1.astype(BF16)
    lo = (r1 - mid.astype(F32)).astype(BF16)
    return hi, mid, lo


def _column_scale(row_vec, width):
    n = row_vec.shape[-1]
    t = jnp.transpose(jnp.broadcast_to(row_vec, (LANES, n)))
    return jnp.concatenate([t] * (width // LANES), axis=-1)


def _gla_prompt_kernel(q_ref, k_ref, v_ref, g_ref, la_ref, gn_ref, o_ref, s_ref, state):
    c, sub = GLA_CHUNK, GLA_SUB
    nsub = c // sub
    i = pl.program_id(2)

    @pl.when(i == 0)
    def _():
        state[...] = jnp.zeros_like(state)

    q = q_ref[...]
    k = k_ref[...] * (GLA_DK ** -0.5)
    v = v_ref[...].astype(BF16)
    ii = lax.broadcasted_iota(jnp.int32, (c, c), 0)
    jj = lax.broadcasted_iota(jnp.int32, (c, c), 1)
    causal = ii >= jj
    tri = jnp.where(causal, 1.0, 0.0).astype(BF16)
    hi, mid, lo = _split3(la_ref[...])
    b = _dot(tri, hi) + _dot(tri, mid) + _dot(tri, lo)
    mids = [b[s * sub + sub // 2 - 1: s * sub + sub // 2, :] for s in range(nsub)]
    ref_lvl = jnp.concatenate([jnp.broadcast_to(m, (sub, GLA_DK)) for m in mids], axis=0)
    qd = (q * jnp.exp(b - ref_lvl)).astype(BF16)
    kd = k * jnp.exp(ref_lvl - b)
    blk = lax.broadcasted_iota(jnp.int32, (c, GLA_DK), 0) // sub
    rows = []
    for s in range(nsub):
        scale = jnp.where(blk <= s, jnp.exp(jnp.minimum(mids[s] - ref_lvl, 0.0)), 0.0)
        rows.append(_dot_nt(qd[s * sub:(s + 1) * sub], (kd * scale).astype(BF16)))
    att = jnp.where(causal, jnp.concatenate(rows, axis=0), 0.0)
    s_old = state[...]
    o = _dot(att.astype(BF16), v) + _dot((q * jnp.exp(b)).astype(BF16), s_old.astype(BF16))
    b_last = b[c - 1:c, :]
    k_rem = (k * jnp.exp(b_last - b)).astype(BF16)
    s_new = s_old * _column_scale(jnp.exp(b_last), GLA_DV) + _dot_tn(k_rem, v)
    state[...] = s_new
    o_ref[...] = _rms_gate(o, gn_ref[0], g_ref[...]).astype(o_ref.dtype)

    @pl.when(i == pl.num_programs(2) - 1)
    def _():
        s_ref[0, 0] = s_new


def _gla_prompt(proj, log_a, gn, batch, seq):
    c = GLA_CHUNK
    nc = seq // c

    def col(start, width):
        return pl.BlockSpec((c, width), lambda b, h, i: (b * nc + i, start // width + h))

    return pl.pallas_call(
        _gla_prompt_kernel,
        grid=(batch, GLA_HEADS, nc),
        in_specs=[col(COL_GQ, GLA_DK), col(COL_GK, GLA_DK), col(COL_GV, GLA_DV), col(COL_GG, GLA_DV),
                  pl.BlockSpec((c, GLA_DK), lambda b, h, i: (b * nc + i, h)),
                  pl.BlockSpec((1, 1, GLA_DV), lambda b, h, i: (h, 0, 0))],
        out_specs=[pl.BlockSpec((c, GLA_DV), lambda b, h, i: (b * nc + i, h)),
                   pl.BlockSpec((1, 1, GLA_DK, GLA_DV), lambda b, h, i: (b, h, 0, 0))],
        out_shape=[jax.ShapeDtypeStruct((batch * seq, GLA_VW), BF16),
                   jax.ShapeDtypeStruct((batch, GLA_HEADS, GLA_DK, GLA_DV), F32)],
        scratch_shapes=[pltpu.VMEM((GLA_DK, GLA_DV), F32)],
        compiler_params=_cp(("parallel", "parallel", "arbitrary")),
        name="gla_prompt",
    )(proj, proj, proj, proj, log_a, gn.reshape(GLA_HEADS, 1, GLA_DV))


def _gla_sample_kernel(seq, q_ref, k_ref, v_ref, g_ref, la_ref, gn_ref, s_in, o_ref, s_out):
    rows = DEC_PAIR * seq
    rid = lax.broadcasted_iota(jnp.int32, (rows, GLA_DK), 0)
    pos = rid % seq
    batch_of_row = rid // seq
    ii = lax.broadcasted_iota(jnp.int32, (rows, rows), 0)
    jj = lax.broadcasted_iota(jnp.int32, (rows, rows), 1)
    visible = (ii // seq == jj // seq) & (ii >= jj)
    for h in range(GLA_HEADS):
        ks = slice(h * GLA_DK, (h + 1) * GLA_DK)
        vs = slice(h * GLA_DV, (h + 1) * GLA_DV)
        la = la_ref[:, ks]
        b = la
        for d in range(1, seq):
            b = b + jnp.where(pos >= d, pltpu.roll(la, d, axis=0), 0.0)
        q = q_ref[:, ks]
        k = k_ref[:, ks] * (GLA_DK ** -0.5)
        v = v_ref[:, vs].astype(BF16)
        qb = (q * jnp.exp(b)).astype(BF16)
        kb = (k * jnp.exp(-b)).astype(BF16)
        att = jnp.where(visible, _dot_nt(qb, kb), 0.0)
        o = _dot(att.astype(BF16), v)
        for bi in range(DEC_PAIR):
            mine = batch_of_row == bi
            last = bi * seq + seq - 1
            b_last = b[last:last + 1, :]
            s_old = s_in[bi, h]
            o = o + jnp.where(mine[:, :1], _dot(qb, s_old.astype(BF16)), 0.0)
            k_rem = jnp.where(mine, k * jnp.exp(b_last - b), 0.0).astype(BF16)
            s_out[bi, h] = s_old * _column_scale(jnp.exp(b_last), GLA_DV) + _dot_tn(k_rem, v)
        o_ref[:, vs] = _rms_gate(o, gn_ref[h], g_ref[:, vs]).astype(o_ref.dtype)


def _gla_sample(proj, row0, log_a, gn, state, batch, seq):
    rows = DEC_PAIR * seq
    blk0 = row0 // rows
    st =pl.BlockSpec((DEC_PAIR, GLA_HEADS, GLA_DK, GLA_DV), lambda i: (i, 0, 0, 0))
    return pl.pallas_call(
        functools.partial(_gla_sample_kernel, seq),
        grid=(batch // DEC_PAIR,),
        in_specs=[pl.BlockSpec((rows, GLA_KW), lambda i: (i + blk0, COL_GQ // GLA_KW)),
                  pl.BlockSpec((rows, GLA_KW), lambda i: (i + blk0, COL_GK // GLA_KW)),
                  pl.BlockSpec((rows, GLA_VW), lambda i: (i + blk0, COL_GV // GLA_VW)),
                  pl.BlockSpec((rows, GLA_VW), lambda i: (i + blk0, COL_GG // GLA_VW)),
                  pl.BlockSpec((rows, GLA_KW), lambda i: (i + blk0, 0)),
                  pl.BlockSpec((GLA_HEADS, 1, GLA_DV), lambda i: (0, 0, 0)), st],
        out_specs=[pl.BlockSpec((rows, GLA_VW), lambda i: (i, 0)), st],
        out_shape=[jax.ShapeDtypeStruct((batch * seq, GLA_VW), BF16),
                   jax.ShapeDtypeStruct(state.shape, F32)],
        compiler_params=_cp(("parallel",)),
        name="gla_sample",
    )(proj, proj, proj, proj, log_a, gn.reshape(GLA_HEADS, 1, GLA_DV), state)


def _router_kernel(x_ref, g_ref, wr_ref, br_ref, hm_ref, idx_ref, gate_ref):
    tr = x_ref.shape[0]
    x = x_ref[...]
    ms = jnp.mean(x * x, axis=-1, keepdims=True)
    hm = x * lax.rsqrt(ms + EPS) * g_ref[...]
    for s in range(ROW_CHUNKS):
        hm_ref[pl.ds(s, tr, stride=ROW_CHUNKS), :] = hm[:, s * LANES:(s + 1) * LANES]
    h1, h2, _ = _split3(hm)
    w = wr_ref[...]
    w1 = w.astype(BF16)
    w2 = (w - w1.astype(F32)).astype(BF16)
    logits = _dot(h1, w1) + _dot(h1, w2) + _dot(h2, w1) + br_ref[...]
    lane = lax.broadcasted_iota(jnp.int32, (tr, LANES), 1).astype(F32)
    neg, far = -1e30, 1e9
    is_group = lane < N_GROUPS
    gl = jnp.where(is_group, logits, neg)
    gmax = jnp.max(gl, axis=-1, keepdims=True)
    gidx = jnp.min(jnp.where(gl == gmax, lane, far), axis=-1, keepdims=True)
    gsum = jnp.sum(jnp.where(is_group, jnp.exp(gl - gmax), 0.0), axis=-1, keepdims=True)
    g_p = 1.0 / gsum
    lo = N_GROUPS + EXPERTS_PER_GROUP * gidx
    in_sel = (lane >= lo) & (lane < lo + EXPERTS_PER_GROUP)
    el = jnp.where(in_sel, logits, neg)
    emax = jnp.max(el, axis=-1, keepdims=True)
    e1 = jnp.min(jnp.where(el == emax, lane, far), axis=-1, keepdims=True)
    esum = jnp.sum(jnp.where(in_sel, jnp.exp(el - emax), 0.0), axis=-1, keepdims=True)
    el2 = jnp.where(lane == e1, neg, el)
    m2 = jnp.max(el2, axis=-1, keepdims=True)
    e2 = jnp.min(jnp.where(el2 == m2, lane, far), axis=-1, keepdims=True)
    p1 = 1.0 / esum
    p2 = jnp.exp(m2 - emax) / esum
    den = p1 + p2
    idx_ref[...] = jnp.where(lane == 0, e1 - N_GROUPS, jnp.where(lane == 1, e2 - N_GROUPS, 0.0)).astype(jnp.int32)
    gate_ref[...] = jnp.where(lane == 0, g_p * p1 / den, jnp.where(lane == 1, g_p * p2 / den, 0.0))


def _router(x, g, w_r, b_r, tr):
    t, d = x.shape
    return pl.pallas_call(
        _router_kernel,
        grid=(t // tr,),
        in_specs=[pl.BlockSpec((tr, d), lambda i: (i, 0)), pl.BlockSpec((1, d), lambda i: (0, 0)),
                  pl.BlockSpec((d, LANES), lambda i: (0, 0)), pl.BlockSpec((1, LANES), lambda i: (0, 0))],
        out_specs=[pl.BlockSpec((tr * ROW_CHUNKS, LANES), lambda i: (i, 0)),
                   pl.BlockSpec((tr, LANES), lambda i: (i, 0)),
                   pl.BlockSpec((tr, LANES), lambda i: (i, 0))],
        out_shape=[jax.ShapeDtypeStruct((t * ROW_CHUNKS, LANES), F32),
                   jax.ShapeDtypeStruct((t, LANES), jnp.int32),
                   jax.ShapeDtypeStruct((t, LANES), F32)],
        compiler_params=_cp(("parallel",)),
        name="moe_router",
    )(x, g.reshape(1, d), w_r, b_r)


def _moe_kernel(item_e, item_start, item_n, item_used, tok, dst,
                hm_hbm, wg_ref, wu_ref, wd_ref, y_hbm,
                stage, xb, hid, wgb, wub, wdb, sem_in, sem_out):
    del item_e, item_used
    i = pl.program_id(0)
    j = pl.program_id(1)
    n = item_n[i]
    s0 = item_start[i]
    nsub = (n + MOE_SUB - 1) // MOE_SUB
    n_up = D_EXPERT // MOE_TJ

    def row_copy_in(r):
        t = pl.multiple_of(tok[s0 + r] * ROW_CHUNKS, ROW_CHUNKS)
        return pltpu.make_async_copy(hm_hbm.at[pl.ds(t, ROW_CHUNKS), :],
                                     stage.at[pl.ds(r * ROW_CHUNKS, ROW_CHUNKS), :], sem_in)

    def row_copy_out(r):
        a = pl.multiple_of(dst[s0 + r] * ROW_CHUNKS, ROW_CHUNKS)
        return pltpu.make_async_copy(stage.at[pl.ds(r * ROW_CHUNKS, ROW_CHUNKS), :],
                                     y_hbm.at[pl.ds(a, ROW_CHUNKS), :], sem_out)

    @pl.when((i == 0) & (j == 0))
    def _():
        stage[...] = jnp.zeros_like(stage)

    @pl.when((j == 0) & (n > 0))
    def _():
        def issue(r, carry):
            row_copy_in(r).start()
            return carry
        lax.fori_loop(0, n, issue, 0)

        def wait(r, carry):
            row_copy_in(r).wait()
            return carry
        lax.fori_loop(0, n, wait, 0)

        def convert(sb, carry):
            r0 = pl.multiple_of(sb * MOE_SUB, MOE_SUB)
            for s in range(ROW_CHUNKS):
                xb[pl.ds(r0, MOE_SUB), s * LANES:(s + 1) * LANES] = (
                    stage[pl.ds(r0 * ROW_CHUNKS + s, MOE_SUB, stride=ROW_CHUNKS), :].astype(BF16))
            return carry
        lax.fori_loop(0, nsub, convert, 0)

    @pl.when((j < n_up) & (n > 0))
    def _():
        wgb[...] = wg_ref[0].astype(BF16)
        wub[...] = wu_ref[0].astype(BF16)

        def body(sb, carry):
            r0 = pl.multiple_of(sb * MOE_SUB, MOE_SUB)
            x = xb[pl.ds(r0, MOE_SUB), :]
            act = _silu(_dot(x, wgb[...])) * _dot(x, wub[...])
            hid[j, pl.ds(r0, MOE_SUB), :] = act.astype(BF16)
            return carry
        lax.fori_loop(0, nsub, body, 0)

    @pl.when((j >= n_up) & (n > 0))
    def _():
        wdb[...] = wd_ref[0].astype(BF16)
        jt = j - n_up

        def body(sb, carry):
            r0 = pl.multiple_of(sb * MOE_SUB, MOE_SUB)
            y = _dot(hid[0, pl.ds(r0, MOE_SUB), :], wdb[0:MOE_TJ, :])
            for c in range(1, n_up):
                y = y + _dot(hid[c, pl.ds(r0, MOE_SUB), :], wdb[c * MOE_TJ:(c + 1) * MOE_TJ, :])
            for s in range(MOE_TN // LANES):
                base = r0 * ROW_CHUNKS + jt * (MOE_TN // LANES) + s
                stage[pl.ds(base, MOE_SUB, stride=ROW_CHUNKS), :] = y[:, s * LANES:(s + 1) * LANES]
            return carry
        lax.fori_loop(0, nsub, body, 0)

    @pl.when((j == pl.num_programs(1) - 1) & (n > 0))
    def _():
        def issue(r, carry):
            row_copy_out(r).start()
            return carry
        lax.fori_loop(0, n, issue, 0)

        def wait(r, carry):
            row_copy_out(r).wait()
            return carry
        lax.fori_loop(0, n, wait, 0)


def _moe_experts(hm_rows, w_gate, w_up, w_down, items, tok, dst, n_assign):
    item_e, item_start, item_n, item_used = items
    n_items = item_e.shape[0]
    n_up = D_EXPERT // MOE_TJ
    n_dn = D_MODEL // MOE_TN

    def up_map(i, j, e, st, n, used, tok, dst):
        return (e[i], 0, jnp.where(used[i] > 0, jnp.minimum(j, n_up - 1), n_up - 1))

    def dn_map(i, j, e, st, n, used, tok, dst):
        return (e[i], 0, jnp.where(used[i] > 0, jnp.maximum(j - n_up, 0), n_dn - 1))

    return pl.pallas_call(
        _moe_kernel,
        grid_spec=pltpu.PrefetchScalarGridSpec(
            num_scalar_prefetch=6,
            grid=(n_items, n_up + n_dn),
            in_specs=[pl.BlockSpec(memory_space=pl.ANY),
                      pl.BlockSpec((1, D_MODEL, MOE_TJ), up_map),
                      pl.BlockSpec((1, D_MODEL, MOE_TJ), up_map),
                      pl.BlockSpec((1, D_EXPERT, MOE_TN), dn_map)],
            out_specs=pl.BlockSpec(memory_space=pl.ANY),
            scratch_shapes=[pltpu.VMEM((MOE_ROWS * ROW_CHUNKS, LANES), F32),
                            pltpu.VMEM((MOE_ROWS, D_MODEL), BF16),
                            pltpu.VMEM((n_up, MOE_ROWS, MOE_TJ), BF16),
                            pltpu.VMEM((D_MODEL, MOE_TJ), BF16),
                            pltpu.VMEM((D_MODEL, MOE_TJ), BF16),
                            pltpu.VMEM((D_EXPERT, MOE_TN), BF16),
                            pltpu.SemaphoreType.DMA(()),
                            pltpu.SemaphoreType.DMA(())],
        ),
        out_shape=jax.ShapeDtypeStruct((n_assign * ROW_CHUNKS, LANES), F32),
        compiler_params=_cp(("arbitrary", "arbitrary")),
        name="moe_experts",
    )(item_e, item_start, item_n, item_used, tok, dst, hm_rows, w_gate, w_up, w_down)


def _moe_plan(expert_idx):
    n_assign = expert_idx.size
    flat_e = expert_idx.reshape(n_assign)
    order = jnp.argsort(flat_e).astype(jnp.int32)
    counts = jnp.bincount(flat_e, length=N_EXPERTS).astype(jnp.int32)
    start = jnp.cumsum(counts) - counts
    per_e = (counts + MOE_ROWS - 1) // MOE_ROWS
    item_end = jnp.cumsum(per_e)
    n_items = n_assign // MOE_ROWS + N_EXPERTS
    ids = jnp.arange(n_items, dtype=jnp.int32)
    used = ids < item_end[-1]
    last = jnp.maximum(item_end[-1] - 1, 0)
    e_of = jnp.minimum(jnp.searchsorted(item_end, jnp.minimum(ids, last), side="right"), N_EXPERTS - 1).astype(jnp.int32)
    local = jnp.minimum(ids, last) - (item_end - per_e)[e_of]
    item_start = start[e_of] + local * MOE_ROWS
    item_n = jnp.where(used, jnp.clip(counts[e_of] - local * MOE_ROWS, 0, MOE_ROWS), 0)
    items = (e_of, item_start.astype(jnp.int32), item_n.astype(jnp.int32), used.astype(jnp.int32))
    return items, order // TOP_K, order


def _combine_kernel(x_ref, y_ref, gate_ref, o_ref, ob_ref):
    tr = x_ref.shape[0]
    g0 = gate_ref[:, 0:1]
    g1 = gate_ref[:, 1:2]
    for s in range(ROW_CHUNKS):
        y0 = y_ref[pl.ds(s, tr, stride=TOP_K * ROW_CHUNKS), :]
        y1 = y_ref[pl.ds(ROW_CHUNKS + s, tr, stride=TOP_K * ROW_CHUNKS), :]
        sl = slice(s * LANES, (s + 1) * LANES)
        x = x_ref[:, sl] + (y0 * g0 + y1 * g1)
        o_ref[:, sl] = x
        ob_ref[:, sl] = x.astype(BF16)


def _combine(x, y_rows, gate, tr):
    t, d = x.shape
    return pl.pallas_call(
        _combine_kernel,
        grid=(t // tr,),
        in_specs=[pl.BlockSpec((tr, d), lambda i: (i, 0)),
                  pl.BlockSpec((tr * TOP_K * ROW_CHUNKS, LANES), lambda i: (i, 0)),
                  pl.BlockSpec((tr, LANES), lambda i: (i, 0))],
        out_specs=[pl.BlockSpec((tr, d), lambda i: (i, 0)), pl.BlockSpec((tr, d), lambda i: (i, 0))],
        out_shape=[jax.ShapeDtypeStruct((t, d), F32), jax.ShapeDtypeStruct((t, d), BF16)],
        compiler_params=_cp(("parallel",)),
        name="moe_combine",
    )(x, y_rows, gate)


def _rope_tables(pos):
    half = RET_DK // 2
    inv = ROPE_BASE ** (-jnp.arange(half, dtype=F32) / half)
    ang = pos.astype(F32)[:, None] * inv[None, :]
    return jnp.cos(ang), jnp.sin(ang)


def kernel(x_prompt, x_sample, state_ret, state_gla, p_prompt, p_sample, g_mix, w_in, w_gla_up, b_gla,
           ret_norm_g, gla_norm_g, w_out, g_moe, w_rg, b_rg, w_re, b_re, w_gate, w_up, w_down, w_pg, w_pp,
           g_final):
    n_p, l_p, d = x_prompt.shape
    n_s, l_s, _ = x_sample.shape
    depth = g_mix.shape[0]
    t_p, t_s = n_p * l_p, n_s * l_s
    t = t_p + t_s
    tr = t // 32
    tm = t // 8

    log_g_py = [math.log1p(-(2.0 ** (-5.0 - h))) for h in range(RET_HEADS)]
    log_g = jnp.asarray(log_g_py, dtype=F32)
    cos_p, sin_p = _rope_tables(jnp.arange(l_p, dtype=jnp.int32))
    cos_s, sin_s = _rope_tables(PAST_LEN + jnp.arange(l_s, dtype=jnp.int32))
    cos_s, sin_s = jnp.tile(cos_s, (DEC_PAIR, 1)), jnp.tile(sin_s, (DEC_PAIR, 1))

    x = jnp.concatenate([x_prompt.reshape(t_p, d), x_sample.reshape(t_s, d)], axis=0)
    ret_p, ret_s, gla_p, gla_s = [], [], [], []
    for l in range(depth):
        w_in_main = w_in[l][:, :N_MAIN].astype(BF16)
        w_ga = jnp.pad(w_in[l][:, N_MAIN:], ((0, 0), (0, LANES - GLA_RANK))).astype(BF16)
        w_gup = jnp.pad(w_gla_up[l], ((0, LANES - GLA_RANK), (0, 0))).astype(BF16)
        w_r = jnp.pad(jnp.concatenate([w_rg[l], w_re[l]], axis=1),
                      ((0, 0), (0, LANES - N_GROUPS - N_EXPERTS)))
        b_r = jnp.pad(jnp.concatenate([b_rg[l], b_re[l]]), (0, LANES - N_GROUPS - N_EXPERTS)).reshape(1, LANES)
        p = jnp.concatenate([p_prompt[l].reshape(t_p, -1), p_sample[l].reshape(t_s, -1)], axis=0)

        h = _norm_cast(x, g_mix[l], tr, BF16)
        proj = _matmul(h, w_in_main, tm, 1024, name="in_proj")
        log_a = _log_decay(h, w_ga, w_gup, b_gla[l], tr)
        ro_p, sr_p = _ret_prompt(proj, log_g, cos_p, sin_p, ret_norm_g[l], n_p, l_p)
        go_p, sg_p = _gla_prompt(proj, log_a, gla_norm_g[l], n_p, l_p)
        ro_s, sr_s = _ret_sample(proj, t_p, log_g_py, cos_s, sin_s, ret_norm_g[l], state_ret[l], n_s, l_s)
        go_s, sg_s = _gla_sample(proj, t_p, log_a, gla_norm_g[l], state_gla[l], n_s, l_s)
        ret_p.append(sr_p)
        ret_s.append(sr_s)
        gla_p.append(sg_p)
        gla_s.append(sg_s)
        mixed = jnp.concatenate([jnp.concatenate([ro_p, go_p], axis=1),
                                 jnp.concatenate([ro_s, go_s], axis=1)], axis=0)
        x = _matmul(mixed, w_out[l].astype(BF16), tm, 512, res=x, name="out_proj")

        hm_rows, idx, gate = _router(x, g_moe[l], w_r, b_r, tr)
        items, tok, dst = _moe_plan(idx[:, :TOP_K])
        y_rows = _moe_experts(hm_rows, w_gate[l], w_up[l], w_down[l], items, tok, dst, t * TOP_K)
        x, xb = _combine(x, y_rows, gate, tr)

        x = _matmul(xb, w_pg[l].astype(BF16), tm, 512, res=x, ple=(p, w_pp[l].astype(BF16)), name="ple")

    y = _norm_cast(x, g_final, tr, F32)
    y_prompt = y[:t_p].reshape(n_p, l_p, d)
    y_sample = y[t_p:].reshape(n_s, l_s, d)
    return (y_prompt, y_sample,
            jnp.stack(ret_p).astype(state_ret.dtype), jnp.stack(ret_s).astype(state_ret.dtype),
            jnp.stack(gla_p).astype(state_gla.dtype), jnp.stack(gla_s).astype(state_gla.dtype))
```

```python
import functools
import math

import jax
import jax.numpy as jnp
from jax import lax
from jax.experimental import pallas as pl
from jax.experimental.pallas import tpu as pltpu

F32 = jnp.float32
BF16 = jnp.bfloat16

D_MODEL = 4096
RET_HEADS = 8
RET_DK = 256
RET_DV = 256
GLA_HEADS = 4
GLA_DK = 256
GLA_DV = 512
GLA_RANK = 16
GLA_GATE_TEMP = 16.0
ROPE_BASE = 10000.0
PAST_LEN = 16384
N_GROUPS = 4
EXPERTS_PER_GROUP = 8
N_EXPERTS = N_GROUPS * EXPERTS_PER_GROUP
TOP_K = 2
D_EXPERT = D_MODEL // 4
EPS = 1e-6

RET_W = RET_HEADS * RET_DK
GLA_KW = GLA_HEADS * GLA_DK
GLA_VW = GLA_HEADS * GLA_DV
N_MAIN = 4 * RET_W + 2 * GLA_KW + 2 * GLA_VW
COL_GQ = 4 * RET_W
COL_GK = COL_GQ + GLA_KW
COL_GV = COL_GK + GLA_KW
COL_GG = COL_GV + GLA_VW

LANES = 128
ROW_TILE = 256
LHS_TILE = 512
RET_CHUNK = 256
GLA_CHUNK = 128
GLA_SUB = 32
DEC_PAIR = 2
MOE_ROWS = 768
MOE_SUB = 256
MOE_TJ = 256
MOE_TN = 1024
VMEM_LIMIT = 56 * 1024 * 1024


def _cp(semantics, vmem=VMEM_LIMIT):
    return pltpu.CompilerParams(dimension_semantics=semantics, vmem_limit_bytes=vmem)


def _sigmoid(x):
    return 1.0 / (1.0 + jnp.exp(-x))


def _silu(x):
    return x * _sigmoid(x)


def _dot(a, b):
    return jnp.dot(a, b, preferred_element_type=F32)


def _dot_nt(a, b):
    return lax.dot_general(a, b, (((1,), (1,)), ((), ())), preferred_element_type=F32)


def _dot_tn(a, b):
    return lax.dot_general(a, b, (((0,), (0,)), ((), ())), preferred_element_type=F32)


def _rms_norm(x, g):
    ms = jnp.mean(x * x, axis=-1, keepdims=True)
    return x * lax.rsqrt(ms + EPS) * g


def _two_group_specs(shape_tail, tile, n_p_tiles):
    zeros = (0,) * len(shape_tail)

    def p_map(*idx):
        return (jnp.minimum(idx[-1], n_p_tiles - 1),) + zeros

    def s_map(*idx):
        return (jnp.maximum(idx[-1] - n_p_tiles, 0),) + zeros

    return pl.BlockSpec((tile,) + shape_tail, p_map), pl.BlockSpec((tile,) + shape_tail, s_map)


def _norm_in_kernel(n_p_tiles, xp_ref, xs_ref, g_ref, o_ref):
    i = pl.program_id(0)

    @pl.when(i < n_p_tiles)
    def _():
        o_ref[...] = _rms_norm(xp_ref[...], g_ref[...]).astype(o_ref.dtype)

    @pl.when(i >= n_p_tiles)
    def _():
        o_ref[...] = _rms_norm(xs_ref[...], g_ref[...]).astype(o_ref.dtype)


def _norm_in(x_p, x_s, g):
    (t_p, d), t_s = x_p.shape, x_s.shape[0]
    tr = ROW_TILE
    p_spec, s_spec = _two_group_specs((d,), tr, t_p // tr)
    return pl.pallas_call(
        functools.partial(_norm_in_kernel, t_p // tr),
        grid=((t_p + t_s) // tr,),
        in_specs=[p_spec, s_spec, pl.BlockSpec((1, d), lambda i: (0, 0))],
        out_specs=pl.BlockSpec((tr, d), lambda i: (i, 0)),
        out_shape=jax.ShapeDtypeStruct((t_p + t_s, d), BF16),
        compiler_params=_cp(("arbitrary",)),
        name="norm_in",
    )(x_p, x_s, g.reshape(1, d))


def _norm_out_kernel(x_ref, g_ref, o_ref):
    o_ref[...] = _rms_norm(x_ref[...], g_ref[...])


def _norm_out(x, g, row0, n_rows):
    d = x.shape[1]
    tr = ROW_TILE
    blk0 = row0 // tr
    return pl.pallas_call(
        _norm_out_kernel,
        grid=(n_rows // tr,),
        in_specs=[pl.BlockSpec((tr, d), lambda i: (i + blk0, 0)), pl.BlockSpec((1, d), lambda i: (0, 0))],
        out_specs=pl.BlockSpec((tr, d), lambda i: (i, 0)),
        out_shape=jax.ShapeDtypeStruct((n_rows, d), F32),
        compiler_params=_cp(("parallel",)),
        name="norm_out",
    )(x, g.reshape(1, d))


def _in_proj_kernel(a_ref, w_ref, o_ref, wb):
    @pl.when(pl.program_id(1) == 0)
    def _():
        wb[...] = w_ref[...].astype(BF16)

    o_ref[...] = _dot(a_ref[...], wb[...])


def _in_proj(h, w, n_cols, tm, tn):
    t, d = h.shape
    return pl.pallas_call(
        _in_proj_kernel,
        grid=(n_cols // tn, t // tm),
        in_specs=[pl.BlockSpec((tm, d), lambda j, i: (i, 0)), pl.BlockSpec((d, tn), lambda j, i: (0, j))],
        out_specs=pl.BlockSpec((tm, tn), lambda j, i: (i, j)),
        out_shape=jax.ShapeDtypeStruct((t, n_cols), F32),
        scratch_shapes=[pltpu.VMEM((d, tn), BF16)],
        compiler_params=_cp(("arbitrary", "arbitrary")),
        name="in_proj",
    )(h, w)


def _out_proj_kernel(n_p_tiles, rp_ref, rs_ref, gp_ref, gs_ref, w_ref, o_ref, wb):
    i = pl.program_id(1)

    @pl.when(i == 0)
    def _():
        wb[...] = w_ref[...].astype(BF16)

    @pl.when(i < n_p_tiles)
    def _():
        o_ref[...] = _dot(rp_ref[...], wb[:RET_W, :]) + _dot(gp_ref[...], wb[RET_W:, :])

    @pl.when(i >= n_p_tiles)
    def _():
        o_ref[...] = _dot(rs_ref[...], wb[:RET_W, :]) + _dot(gs_ref[...], wb[RET_W:, :])


def _out_proj(ro_p, ro_s, go_p, go_s, w, tn):
    t_p, t_s = ro_p.shape[0], ro_s.shape[0]
    d_in, d_out = w.shape
    tm = LHS_TILE
    rp_spec, rs_spec = _two_group_specs((RET_W,), tm, t_p // tm)
    gp_spec, gs_spec = _two_group_specs((GLA_VW,), tm, t_p // tm)
    return pl.pallas_call(
        functools.partial(_out_proj_kernel, t_p // tm),
        grid=(d_out // tn, (t_p + t_s) // tm),
        in_specs=[rp_spec, rs_spec, gp_spec, gs_spec, pl.BlockSpec((d_in, tn), lambda j, i: (0, j))],
        out_specs=pl.BlockSpec((tm, tn), lambda j, i: (i, j)),
        out_shape=jax.ShapeDtypeStruct((t_p + t_s, d_out), F32),
        scratch_shapes=[pltpu.VMEM((d_in, tn), BF16)],
        compiler_params=_cp(("arbitrary", "arbitrary")),
        name="out_proj",
    )(ro_p, ro_s, go_p, go_s, w)


def _ple_kernel(a_ref, w_ref, p_ref, wp_ref, r_ref, o_ref, wb):
    @pl.when(pl.program_id(1) == 0)
    def _():
        wb[...] = w_ref[...].astype(BF16)

    gate = _sigmoid(_dot(a_ref[...], wb[...]))
    emb = _dot(p_ref[...].astype(BF16), wp_ref[...].astype(BF16))
    o_ref[...] = r_ref[...] + gate * emb


def _ple(xb, x, p, w_pg, w_pp, tm, tn):
    t, d = x.shape
    kp = p.shape[1]
    o_spec = pl.BlockSpec((tm, tn), lambda j, i: (i, j))
    return pl.pallas_call(
        _ple_kernel,
        grid=(d // tn, t // tm),
        in_specs=[pl.BlockSpec((tm, d), lambda j, i: (i, 0)), pl.BlockSpec((d, tn), lambda j, i: (0, j)),
                  pl.BlockSpec((tm, kp), lambda j, i: (i, 0)), pl.BlockSpec((kp, tn), lambda j, i: (0, j)),
                  o_spec],
        out_specs=o_spec,
        out_shape=jax.ShapeDtypeStruct((t, d), F32),
        scratch_shapes=[pltpu.VMEM((d, tn), BF16)],
        compiler_params=_cp(("arbitrary", "arbitrary")),
        name="ple",
    )(xb, w_pg, p, w_pp, x)


def _log_sigmoid(z):
    return jnp.minimum(z, 0.0) - jnp.log(1.0 + jnp.exp(-jnp.abs(z)))


def _loga_kernel(h_ref, wga_ref, wup_ref, b_ref, o_ref):
    ga = _dot(h_ref[...], wga_ref[...])
    z = _dot(ga.astype(BF16), wup_ref[...]) + b_ref[...]
    o_ref[...] = _log_sigmoid(z) * (1.0 / GLA_GATE_TEMP)


def _log_decay(h, w_ga, w_up, b):
    t, d = h.shape
    tr = ROW_TILE
    return pl.pallas_call(
        _loga_kernel,
        grid=(t // tr,),
        in_specs=[pl.BlockSpec((tr, d), lambda i: (i, 0)),
                  pl.BlockSpec((d, LANES), lambda i: (0, 0)),
                  pl.BlockSpec((LANES, GLA_KW), lambda i: (0, 0)),
                  pl.BlockSpec((1, GLA_KW), lambda i: (0, 0))],
        out_specs=pl.BlockSpec((tr, GLA_KW), lambda i: (i, 0)),
        out_shape=jax.ShapeDtypeStruct((t, GLA_KW), F32),
        compiler_params=_cp(("parallel",)),
        name="gla_log_decay",
    )(h, w_ga, w_up, b.reshape(1, GLA_KW))


def _rotary(x, cos, sin):
    half = x.shape[-1] // 2
    x1, x2 = x[:, :half], x[:, half:]
    return jnp.concatenate([x1 * cos - x2 * sin, x1 * sin + x2 * cos], axis=-1)


def _group_norm_gate(o, gain, gate):
    mu = jnp.mean(o, axis=-1, keepdims=True)
    d = o - mu
    var = jnp.mean(d * d, axis=-1, keepdims=True)
    return d * lax.rsqrt(var + EPS) * gain * _silu(gate)


def _rms_gate(o, gain, gate):
    ms = jnp.mean(o * o, axis=-1, keepdims=True)
    return o * lax.rsqrt(ms + EPS) * gain * _silu(gate)


def _ret_prompt_kernel(lg_ref, q_ref, k_ref, v_ref, g_ref, cos_ref, sin_ref, gn_ref, o_ref, s_ref, state):
    c = RET_CHUNK
    i = pl.program_id(2)
    lg = lg_ref[pl.program_id(1)]

    @pl.when(i == 0)
    def _():
        state[...] = jnp.zeros_like(state)

    cos, sin = cos_ref[...], sin_ref[...]
    q = _rotary(q_ref[...], cos, sin)
    k = _rotary(k_ref[...], cos, sin) * (RET_DK ** -0.5)
    v = v_ref[...].astype(BF16)
    row = lax.broadcasted_iota(jnp.int32, (c, RET_DK), 0).astype(F32)
    q_dec = jnp.exp((row + 1.0) * lg)
    k_dec = jnp.exp((c - 1.0 - row) * lg)
    ii = lax.broadcasted_iota(jnp.int32, (c, c), 0)
    jj = lax.broadcasted_iota(jnp.int32, (c, c), 1)
    diff = (ii - jj).astype(F32)
    intra = jnp.where(diff >= 0, jnp.exp(jnp.maximum(diff, 0.0) * lg), 0.0)
    att = _dot_nt(q.astype(BF16), k.astype(BF16)) * intra
    s_old = state[...]
    o = _dot(att.astype(BF16), v) + _dot((q * q_dec).astype(BF16), s_old.astype(BF16))
    c_dec = jnp.exp(jnp.zeros((1, RET_DV), F32) + c * lg)
    s_new = s_old * c_dec + _dot_tn((k * k_dec).astype(BF16), v)
    state[...] = s_new
    o_ref[...] = _group_norm_gate(o, gn_ref[0], g_ref[...]).astype(o_ref.dtype)

    @pl.when(i == pl.num_programs(2) - 1)
    def _():
        s_ref[0, 0] = s_new


def _ret_prompt(proj, log_g, cos, sin, gn, batch, seq):
    c = RET_CHUNK
    nc = seq // c
    wblk = RET_W // RET_DK

    def col(group):
        return pl.BlockSpec((c, RET_DK), lambda b, h, i, lg: (b * nc + i, group * wblk + h))

    tab = pl.BlockSpec((c, RET_DK // 2), lambda b, h, i, lg: (i, 0))
    return pl.pallas_call(
        _ret_prompt_kernel,
        grid_spec=pltpu.PrefetchScalarGridSpec(
            num_scalar_prefetch=1,
            grid=(batch, RET_HEADS, nc),
            in_specs=[col(0), col(1), col(2), col(3), tab, tab,
                      pl.BlockSpec((1, 1, RET_DV), lambda b, h, i, lg: (h, 0, 0))],
            out_specs=[pl.BlockSpec((c, RET_DV), lambda b, h, i, lg: (b * nc + i, h)),
                       pl.BlockSpec((1, 1, RET_DK, RET_DV), lambda b, h, i, lg: (b, h, 0, 0))],
            scratch_shapes=[pltpu.VMEM((RET_DK, RET_DV), F32)],
        ),
        out_shape=[jax.ShapeDtypeStruct((batch * seq, RET_W), BF16),
                   jax.ShapeDtypeStruct((batch, RET_HEADS, RET_DK, RET_DV), F32)],
        compiler_params=_cp(("parallel", "parallel", "arbitrary")),
        name="retention_prompt",
    )(log_g, proj, proj, proj, proj, cos, sin, gn.reshape(RET_HEADS, 1, RET_DV))


def _ret_sample_kernel(log_g, seq, q_ref, k_ref, v_ref, g_ref, cos_ref, sin_ref, gn_ref, s_in, o_ref, s_out):
    rows = DEC_PAIR * seq
    cos, sin = cos_ref[...], sin_ref[...]
    rid = lax.broadcasted_iota(jnp.int32, (rows, RET_DK), 0)
    pos = (rid % seq).astype(F32)
    batch_of_row = rid // seq
    ii = lax.broadcasted_iota(jnp.int32, (rows, rows), 0)
    jj = lax.broadcasted_iota(jnp.int32, (rows, rows), 1)
    visible = (ii // seq == jj // seq) & (ii >= jj)
    diff = jnp.maximum(ii - jj, 0).astype(F32)
    for h in range(RET_HEADS):
        lg = log_g[h]
        sl = slice(h * RET_DK, (h + 1) * RET_DK)
        q = _rotary(q_ref[:, sl], cos, sin)
        k = _rotary(k_ref[:, sl], cos, sin) * (RET_DK ** -0.5)
        v = v_ref[:, sl].astype(BF16)
        intra = jnp.where(visible, jnp.exp(diff * lg), 0.0)
        att = _dot_nt(q.astype(BF16), k.astype(BF16)) * intra
        o = _dot(att.astype(BF16), v)
        qd = (q * jnp.exp((pos + 1.0) * lg)).astype(BF16)
        kd = k * jnp.exp((seq - 1.0 - pos) * lg)
        for b in range(DEC_PAIR):
            mine = batch_of_row == b
            s_old = s_in[b, h]
            o = o + jnp.where(mine, _dot(qd, s_old.astype(BF16)), 0.0)
            s_out[b, h] = s_old * math.exp(seq * lg) + _dot_tn(jnp.where(mine, kd, 0.0).astype(BF16), v)
        o_ref[:, sl] = _group_norm_gate(o, gn_ref[h], g_ref[:, sl]).astype(o_ref.dtype)


def _ret_sample(proj, row0, log_g, cos, sin, gn, state, batch, seq):
    rows = DEC_PAIR * seq
    blk0 = row0 // rows

    def col(group):
        return pl.BlockSpec((rows, RET_W), lambda i: (i + blk0, group))

    tab = pl.BlockSpec((rows, RET_DK // 2), lambda i: (0, 0))
    st = pl.BlockSpec((DEC_PAIR, RET_HEADS, RET_DK, RET_DV), lambda i: (i, 0, 0, 0))
    return pl.pallas_call(
        functools.partial(_ret_sample_kernel, log_g, seq),
        grid=(batch // DEC_PAIR,),
        in_specs=[col(0), col(1), col(2), col(3), tab, tab,
                  pl.BlockSpec((RET_HEADS, 1, RET_DV), lambda i: (0, 0, 0)), st],
        out_specs=[pl.BlockSpec((rows, RET_W), lambda i: (i, 0)), st],
        out_shape=[jax.ShapeDtypeStruct((batch * seq, RET_W), BF16),
                   jax.ShapeDtypeStruct(state.shape, F32)],
        compiler_params=_cp(("parallel",)),
        name="retention_sample",
    )(proj, proj, proj, proj, cos, sin, gn.reshape(RET_HEADS, 1, RET_DV), state)


def _split3(x):
    hi = x.astype(BF16)
    r1 = x - hi.astype(F32)
    mid = r1.astype(BF16)
    lo = (r1 - mid.astype(F32)).astype(BF16)
    return hi, mid, lo


def _column_scale(row_vec, width):
    n = row_vec.shape[-1]
    t = jnp.transpose(jnp.broadcast_to(row_vec, (LANES, n)))
    return jnp.concatenate([t] * (width // LANES), axis=-1)


def _gla_prompt_kernel(q_ref, k_ref, v_ref, g_ref, la_ref, gn_ref, o_ref, s_ref, state):
    c, sub = GLA_CHUNK, GLA_SUB
    nsub = c // sub
    i = pl.program_id(2)

    @pl.when(i == 0)
    def _():
        state[...] = jnp.zeros_like(state)

    q = q_ref[...]
    k = k_ref[...] * (GLA_DK ** -0.5)
    v = v_ref[...].astype(BF16)
    ii = lax.broadcasted_iota(jnp.int32, (c, c), 0)
    jj = lax.broadcasted_iota(jnp.int32, (c, c), 1)
    causal = ii >= jj
    tri = jnp.where(causal, 1.0, 0.0).astype(BF16)
    hi, mid, lo = _split3(la_ref[...])
    b = _dot(tri, hi) + _dot(tri, mid) + _dot(tri, lo)
    mids = [b[s * sub + sub // 2 - 1: s * sub + sub // 2, :] for s in range(nsub)]
    ref_lvl = jnp.concatenate([jnp.broadcast_to(m, (sub, GLA_DK)) for m in mids], axis=0)
    qd = (q * jnp.exp(b - ref_lvl)).astype(BF16)
    kd = k * jnp.exp(ref_lvl - b)
    blk = lax.broadcasted_iota(jnp.int32, (c, GLA_DK), 0) // sub
    rows = []
    for s in range(nsub):
        scale = jnp.where(blk <= s, jnp.exp(jnp.minimum(mids[s] - ref_lvl, 0.0)), 0.0)
        rows.append(_dot_nt(qd[s * sub:(s + 1) * sub], (kd * scale).astype(BF16)))
    att = jnp.where(causal, jnp.concatenate(rows, axis=0), 0.0)
    s_old = state[...]
    o = _dot(att.astype(BF16), v) + _dot((q * jnp.exp(b)).astype(BF16), s_old.astype(BF16))
    b_last = b[c - 1:c, :]
    k_rem = (k * jnp.exp(b_last - b)).astype(BF16)
    s_new = s_old * _column_scale(jnp.exp(b_last), GLA_DV) + _dot_tn(k_rem, v)
    state[...] = s_new
    o_ref[...] = _rms_gate(o, gn_ref[0], g_ref[...]).astype(o_ref.dtype)

    @pl.when(i == pl.num_programs(2) - 1)
    def _():
        s_ref[0, 0] = s_new


def _gla_prompt(proj, log_a, gn, batch, seq):
    c = GLA_CHUNK
    nc = seq // c

    def col(start, width):
        return pl.BlockSpec((c, width), lambda b, h, i: (b * nc + i, start // width + h))

    return pl.pallas_call(
        _gla_prompt_kernel,
        grid=(batch, GLA_HEADS, nc),
        in_specs=[col(COL_GQ, GLA_DK), col(COL_GK, GLA_DK), col(COL_GV, GLA_DV), col(COL_GG, GLA_DV),
                  pl.BlockSpec((c, GLA_DK), lambda b, h, i: (b * nc + i, h)),
                  pl.BlockSpec((1, 1, GLA_DV), lambda b, h, i: (h, 0, 0))],
        out_specs=[pl.BlockSpec((c, GLA_DV), lambda b, h, i: (b * nc + i, h)),
                   pl.BlockSpec((1, 1, GLA_DK, GLA_DV), lambda b, h, i: (b, h, 0, 0))],
        out_shape=[jax.ShapeDtypeStruct((batch * seq, GLA_VW), BF16),
                   jax.ShapeDtypeStruct((batch, GLA_HEADS, GLA_DK, GLA_DV), F32)],
        scratch_shapes=[pltpu.VMEM((GLA_DK, GLA_DV), F32)],
        compiler_params=_cp(("parallel", "parallel", "arbitrary")),
        name="gla_prompt",
    )(proj, proj, proj, proj, log_a, gn.reshape(GLA_HEADS, 1, GLA_DV))


def _gla_sample_kernel(seq, q_ref, k_ref, v_ref, g_ref, la_ref, gn_ref, s_in, o_ref, s_out):
    rows = DEC_PAIR * seq
    rid = lax.broadcasted_iota(jnp.int32, (rows, GLA_DK), 0)
    pos = rid % seq
    batch_of_row = rid // seq
    ii = lax.broadcasted_iota(jnp.int32, (rows, rows), 0)
    jj = lax.broadcasted_iota(jnp.int32, (rows, rows), 1)
    visible = (ii // seq == jj // seq) & (ii >= jj)
    for h in range(GLA_HEADS):
        ks = slice(h * GLA_DK, (h + 1) * GLA_DK)
        vs = slice(h * GLA_DV, (h + 1) * GLA_DV)
        la = la_ref[:, ks]
        b = la
        for d in range(1, seq):
            b = b + jnp.where(pos >= d, pltpu.roll(la, d, axis=0), 0.0)
        q = q_ref[:, ks]
        k = k_ref[:, ks] * (GLA_DK ** -0.5)
        v = v_ref[:, vs].astype(BF16)
        qb = (q * jnp.exp(b)).astype(BF16)
        kb = (k * jnp.exp(-b)).astype(BF16)
        att = jnp.where(visible, _dot_nt(qb, kb), 0.0)
        o = _dot(att.astype(BF16), v)
        for bi in range(DEC_PAIR):
            mine = batch_of_row == bi
            last = bi * seq + seq - 1
            b_last = b[last:last + 1, :]
            s_old = s_in[bi, h]
            o = o + jnp.where(mine[:, :1], _dot(qb, s_old.astype(BF16)), 0.0)
            k_rem = jnp.where(mine, k * jnp.exp(b_last - b), 0.0).astype(BF16)
            s_out[bi, h] = s_old * _column_scale(jnp.exp(b_last), GLA_DV) + _dot_tn(k_rem, v)
        o_ref[:, vs] = _rms_gate(o, gn_ref[h], g_ref[:, vs]).astype(o_ref.dtype)


def _gla_sample(proj, row0, log_a, gn, state, batch, seq):
    rows = DEC_PAIR * seq
    blk0 = row0 // rows
    st = pl.BlockSpec((DEC_PAIR, GLA_HEADS, GLA_DK, GLA_DV), lambda i: (i, 0, 0, 0))
    return pl.pallas_call(
        functools.partial(_gla_sample_kernel, seq),
        grid=(batch // DEC_PAIR,),
        in_specs=[pl.BlockSpec((rows, GLA_KW), lambda i: (i + blk0, COL_GQ // GLA_KW)),
                  pl.BlockSpec((rows, GLA_KW), lambda i: (i + blk0, COL_GK // GLA_KW)),
                  pl.BlockSpec((rows, GLA_VW), lambda i: (i + blk0, COL_GV // GLA_VW)),
                  pl.BlockSpec((rows, GLA_VW), lambda i: (i + blk0, COL_GG // GLA_VW)),
                  pl.BlockSpec((rows, GLA_KW), lambda i: (i + blk0, 0)),
                  pl.BlockSpec((GLA_HEADS, 1, GLA_DV), lambda i: (0, 0, 0)), st],
        out_specs=[pl.BlockSpec((rows, GLA_VW), lambda i: (i, 0)), st],
        out_shape=[jax.ShapeDtypeStruct((batch * seq, GLA_VW), BF16),
                   jax.ShapeDtypeStruct(state.shape, F32)],
        compiler_params=_cp(("parallel",)),
        name="gla_sample",
    )(proj, proj, proj, proj, log_a, gn.reshape(GLA_HEADS, 1, GLA_DV), state)


def _router_kernel(n_p_tiles, m_ref, xp_ref, xs_ref, g_ref, wr_ref, br_ref, x_ref, hm_ref, idx_ref, gate_ref):
    tr = m_ref.shape[0]
    i = pl.program_id(0)

    @pl.when(i < n_p_tiles)
    def _():
        x_ref[...] = xp_ref[...] + m_ref[...]

    @pl.when(i >= n_p_tiles)
    def _():
        x_ref[...] = xs_ref[...] + m_ref[...]

    hm = _rms_norm(x_ref[...], g_ref[...])
    hm_ref[...] = hm
    h1, h2, _ = _split3(hm)
    w = wr_ref[...]
    w1 = w.astype(BF16)
    w2 = (w - w1.astype(F32)).astype(BF16)
    logits = _dot(h1, w1) + _dot(h1, w2) + _dot(h2, w1) + br_ref[...]
    lane = lax.broadcasted_iota(jnp.int32, (tr, LANES), 1).astype(F32)
    neg, far = -1e30, 1e9
    is_group = lane < N_GROUPS
    gl = jnp.where(is_group, logits, neg)
    gmax = jnp.max(gl, axis=-1, keepdims=True)
    gidx = jnp.min(jnp.where(gl == gmax, lane, far), axis=-1, keepdims=True)
    gsum = jnp.sum(jnp.where(is_group, jnp.exp(gl - gmax), 0.0), axis=-1, keepdims=True)
    g_p = 1.0 / gsum
    lo = N_GROUPS + EXPERTS_PER_GROUP * gidx
    in_sel = (lane >= lo) & (lane < lo + EXPERTS_PER_GROUP)
    el = jnp.where(in_sel, logits, neg)
    emax = jnp.max(el, axis=-1, keepdims=True)
    e1 = jnp.min(jnp.where(el == emax, lane, far), axis=-1, keepdims=True)
    esum = jnp.sum(jnp.where(in_sel, jnp.exp(el - emax), 0.0), axis=-1, keepdims=True)
    el2 = jnp.where(lane == e1, neg, el)
    m2 = jnp.max(el2, axis=-1, keepdims=True)
    e2 = jnp.min(jnp.where(el2 == m2, lane, far), axis=-1, keepdims=True)
    p1 = 1.0 / esum
    p2 = jnp.exp(m2 - emax) / esum
    den = p1 + p2
    idx_ref[...] = jnp.where(lane == 0, e1 - N_GROUPS, jnp.where(lane == 1, e2 - N_GROUPS, 0.0)).astype(jnp.int32)
    gate_ref[...] = jnp.where(lane == 0, g_p * p1 / den, jnp.where(lane == 1, g_p * p2 / den, 0.0))


def _router(m, x_p, x_s, g, w_r, b_r):
    t, d = m.shape
    t_p = x_p.shape[0]
    tr = ROW_TILE
    p_spec, s_spec = _two_group_specs((d,), tr, t_p // tr)
    row = pl.BlockSpec((tr, d), lambda i: (i, 0))
    lane_row = pl.BlockSpec((tr, LANES), lambda i: (i, 0))
    return pl.pallas_call(
        functools.partial(_router_kernel, t_p // tr),
        grid=(t // tr,),
        in_specs=[row, p_spec, s_spec, pl.BlockSpec((1, d), lambda i: (0, 0)),
                  pl.BlockSpec((d, LANES), lambda i: (0, 0)), pl.BlockSpec((1, LANES), lambda i: (0, 0))],
        out_specs=[row, row, lane_row, lane_row],
        out_shape=[jax.ShapeDtypeStruct((t, d), F32), jax.ShapeDtypeStruct((t, d), F32),
                   jax.ShapeDtypeStruct((t, LANES), jnp.int32), jax.ShapeDtypeStruct((t, LANES), F32)],
        compiler_params=_cp(("arbitrary",)),
        name="moe_router",
    )(m, x_p, x_s, g.reshape(1, d), w_r, b_r)


def _moe_kernel(item_e, item_start, item_n, item_used, tok, dst,
                hm_hbm, wg_ref, wu_ref, wd_ref, y_hbm,
                stage, xb, hid, wgb, wub, wdb, sem_in, sem_out):
    del item_e, item_used
    i = pl.program_id(0)
    j = pl.program_id(1)
    n = item_n[i]
    s0 = item_start[i]
    nsub = (n + MOE_SUB - 1) // MOE_SUB
    n_up = D_EXPERT // MOE_TJ
    n_dn = D_MODEL // MOE_TN

    def row_copy_in(r):
        return pltpu.make_async_copy(hm_hbm.at[pl.ds(tok[s0 + r], 1), :], stage.at[pl.ds(r, 1), :], sem_in)

    def row_copy_out(r):
        return pltpu.make_async_copy(stage.at[pl.ds(r, 1), :], y_hbm.at[pl.ds(dst[s0 + r], 1), :], sem_out)

    def for_rows(fn):
        def body(r, carry):
            fn(r)
            return carry
        lax.fori_loop(0, n, body, 0)

    def for_sub_blocks(fn):
        def body(sb, carry):
            fn(pl.multiple_of(sb * MOE_SUB, MOE_SUB))
            return carry
        lax.fori_loop(0, nsub, body, 0)

    @pl.when((i == 0) & (j == 0))
    def _():
        stage[...] = jnp.zeros_like(stage)

    @pl.when((j == 0) & (n > 0))
    def _():
        for_rows(lambda r: row_copy_in(r).start())
        for_rows(lambda r: row_copy_in(r).wait())

        def convert(r0):
            xb[pl.ds(r0, MOE_SUB), :] = stage[pl.ds(r0, MOE_SUB), :].astype(BF16)
        for_sub_blocks(convert)

    @pl.when((j < n_up) & (n > 0))
    def _():
        wgb[...] = wg_ref[0].astype(BF16)
        wub[...] = wu_ref[0].astype(BF16)

        def up(r0):
            x = xb[pl.ds(r0, MOE_SUB), :]
            act = _silu(_dot(x, wgb[...])) * _dot(x, wub[...])
            hid[j, pl.ds(r0, MOE_SUB), :] = act.astype(BF16)
        for_sub_blocks(up)

    for c in range(n_dn):
        @pl.when((j == n_up + c) & (n > 0))
        def _(c=c):
            wdb[...] = wd_ref[0].astype(BF16)

            def down(r0):
                y = _dot(hid[0, pl.ds(r0, MOE_SUB), :], wdb[0:MOE_TJ, :])
                for u in range(1, n_up):
                    y = y + _dot(hid[u, pl.ds(r0, MOE_SUB), :], wdb[u * MOE_TJ:(u + 1) * MOE_TJ, :])
                stage[pl.ds(r0, MOE_SUB), c * MOE_TN:(c + 1) * MOE_TN] = y
            for_sub_blocks(down)

    @pl.when((j == pl.num_programs(1) - 1) & (n > 0))
    def _():
        for_rows(lambda r: row_copy_out(r).start())
        for_rows(lambda r: row_copy_out(r).wait())


def _moe_experts(hm, w_gate, w_up, w_down, items, tok, dst, n_assign):
    item_e, item_start, item_n, item_used = items
    n_items = item_e.shape[0]
    n_up = D_EXPERT // MOE_TJ
    n_dn = D_MODEL // MOE_TN

    def up_map(i, j, e, st, n, used, tok, dst):
        return (e[i], 0, jnp.where(used[i] > 0, jnp.minimum(j, n_up - 1), n_up - 1))

    def dn_map(i, j, e, st, n, used, tok, dst):
        return (e[i], 0, jnp.where(used[i] > 0, jnp.maximum(j - n_up, 0), n_dn - 1))

    return pl.pallas_call(
        _moe_kernel,
        grid_spec=pltpu.PrefetchScalarGridSpec(
            num_scalar_prefetch=6,
            grid=(n_items, n_up + n_dn),
            in_specs=[pl.BlockSpec(memory_space=pl.ANY),
                      pl.BlockSpec((1, D_MODEL, MOE_TJ), up_map),
                      pl.BlockSpec((1, D_MODEL, MOE_TJ), up_map),
                      pl.BlockSpec((1, D_EXPERT, MOE_TN), dn_map)],
            out_specs=pl.BlockSpec(memory_space=pl.ANY),
            scratch_shapes=[pltpu.VMEM((MOE_ROWS, D_MODEL), F32),
                            pltpu.VMEM((MOE_ROWS, D_MODEL), BF16),
                            pltpu.VMEM((n_up, MOE_ROWS, MOE_TJ), BF16),
                            pltpu.VMEM((D_MODEL, MOE_TJ), BF16),
                            pltpu.VMEM((D_MODEL, MOE_TJ), BF16),
                            pltpu.VMEM((D_EXPERT, MOE_TN), BF16),
                            pltpu.SemaphoreType.DMA(()),
                            pltpu.SemaphoreType.DMA(())],
        ),
        out_shape=jax.ShapeDtypeStruct((n_assign, D_MODEL), F32),
        compiler_params=_cp(("arbitrary", "arbitrary")),
        name="moe_experts",
    )(item_e, item_start, item_n, item_used, tok, dst, hm, w_gate, w_up, w_down)


def _moe_plan(expert_idx):
    n_assign = expert_idx.size
    flat_e = expert_idx.reshape(n_assign)
    order = jnp.argsort(flat_e).astype(jnp.int32)
    counts = jnp.bincount(flat_e, length=N_EXPERTS).astype(jnp.int32)
    start = jnp.cumsum(counts) - counts
    per_e = (counts + MOE_ROWS - 1) // MOE_ROWS
    item_end = jnp.cumsum(per_e)
    n_items = n_assign // MOE_ROWS + N_EXPERTS
    ids = jnp.arange(n_items, dtype=jnp.int32)
    used = ids < item_end[-1]
    last = jnp.maximum(item_end[-1] - 1, 0)
    e_of = jnp.minimum(jnp.searchsorted(item_end, jnp.minimum(ids, last), side="right"), N_EXPERTS - 1).astype(jnp.int32)
    local = jnp.minimum(ids, last) - (item_end - per_e)[e_of]
    item_start = start[e_of] + local * MOE_ROWS
    item_n = jnp.where(used, jnp.clip(counts[e_of] - local * MOE_ROWS, 0, MOE_ROWS), 0)
    items = (e_of, item_start.astype(jnp.int32), item_n.astype(jnp.int32), used.astype(jnp.int32))
    token = order // TOP_K
    dst_row = (order % TOP_K) * (n_assign // TOP_K) + token
    return items, token, dst_row


def _combine_kernel(x_ref, y0_ref, y1_ref, gate_ref, o_ref, ob_ref):
    x = x_ref[...] + (y0_ref[...] * gate_ref[:, 0:1] + y1_ref[...] * gate_ref[:, 1:2])
    o_ref[...] = x
    ob_ref[...] = x.astype(BF16)


def _combine(x, y, gate):
    t, d = x.shape
    tr = ROW_TILE
    row = pl.BlockSpec((tr, d), lambda i: (i, 0))
    return pl.pallas_call(
        _combine_kernel,
        grid=(t // tr,),
        in_specs=[row, row, pl.BlockSpec((tr, d), lambda i: (i + t // tr, 0)),
                  pl.BlockSpec((tr, LANES), lambda i: (i, 0))],
        out_specs=[row, row],
        out_shape=[jax.ShapeDtypeStruct((t, d), F32), jax.ShapeDtypeStruct((t, d), BF16)],
        compiler_params=_cp(("parallel",)),
        name="moe_combine",
    )(x, y, y, gate)


def _rope_tables(pos):
    half = RET_DK // 2
    inv = ROPE_BASE ** (-jnp.arange(half, dtype=F32) / half)
    ang = pos.astype(F32)[:, None] * inv[None, :]
    return jnp.cos(ang), jnp.sin(ang)


def kernel(x_prompt, x_sample, state_ret, state_gla, p_prompt, p_sample, g_mix, w_in, w_gla_up, b_gla,
           ret_norm_g, gla_norm_g, w_out, g_moe, w_rg, b_rg, w_re, b_re, w_gate, w_up, w_down, w_pg, w_pp,
           g_final):
    n_p, l_p, d = x_prompt.shape
    n_s, l_s, _ = x_sample.shape
    depth = g_mix.shape[0]
    t_p, t_s = n_p * l_p, n_s * l_s
    t = t_p + t_s

    log_g_py = [math.log1p(-(2.0 ** (-5.0 - h))) for h in range(RET_HEADS)]
    log_g = jnp.asarray(log_g_py, dtype=F32)
    cos_p, sin_p = _rope_tables(jnp.arange(l_p, dtype=jnp.int32))
    cos_s, sin_s = _rope_tables(PAST_LEN + jnp.arange(l_s, dtype=jnp.int32))
    cos_s, sin_s = jnp.tile(cos_s, (DEC_PAIR, 1)), jnp.tile(sin_s, (DEC_PAIR, 1))

    x_p = x_prompt.reshape(t_p, d)
    x_s = x_sample.reshape(t_s, d)
    x = None
    ret_p, ret_s, gla_p, gla_s = [], [], [], []
    for l in range(depth):
        w_ga = jnp.pad(w_in[l][:, N_MAIN:], ((0, 0), (0, LANES - GLA_RANK))).astype(BF16)
        w_gup = jnp.pad(w_gla_up[l], ((0, LANES - GLA_RANK), (0, 0))).astype(BF16)
        w_r = jnp.pad(jnp.concatenate([w_rg[l], w_re[l]], axis=1),
                      ((0, 0), (0, LANES - N_GROUPS - N_EXPERTS)))
        b_r = jnp.pad(jnp.concatenate([b_rg[l], b_re[l]]), (0, LANES - N_GROUPS - N_EXPERTS)).reshape(1, LANES)
        p = jnp.concatenate([p_prompt[l].reshape(t_p, -1), p_sample[l].reshape(t_s, -1)], axis=0)
        if x is not None:
            x_p, x_s = x[:t_p], x[t_p:]

        h = _norm_in(x_p, x_s, g_mix[l])
        proj = _in_proj(h, w_in[l], N_MAIN, t // 8, 512)
        log_a = _log_decay(h, w_ga, w_gup, b_gla[l])
        ro_p, sr_p = _ret_prompt(proj, log_g, cos_p, sin_p, ret_norm_g[l], n_p, l_p)
        go_p, sg_p = _gla_prompt(proj, log_a, gla_norm_g[l], n_p, l_p)
        ro_s, sr_s = _ret_sample(proj, t_p, log_g_py, cos_s, sin_s, ret_norm_g[l], state_ret[l], n_s, l_s)
        go_s, sg_s = _gla_sample(proj, t_p, log_a, gla_norm_g[l], state_gla[l], n_s, l_s)
        ret_p.append(sr_p)
        ret_s.append(sr_s)
        gla_p.append(sg_p)
        gla_s.append(sg_s)
        mix = _out_proj(ro_p, ro_s, go_p, go_s, w_out[l], 512)

        x, hm, idx, gate = _router(mix, x_p, x_s, g_moe[l], w_r, b_r)
        items, tok, dst = _moe_plan(idx[:, :TOP_K])
        y = _moe_experts(hm, w_gate[l], w_up[l], w_down[l], items, tok, dst, t * TOP_K)
        x, xb = _combine(x, y, gate)

        x = _ple(xb, x, p, w_pg[l], w_pp[l], t // 8, 512)

    y_prompt = _norm_out(x, g_final, 0, t_p).reshape(n_p, l_p, d)
    y_sample = _norm_out(x, g_final, t_p, t_s).reshape(n_s, l_s, d)
    return (y_prompt, y_sample,
            jnp.stack(ret_p).astype(state_ret.dtype), jnp.stack(ret_s).astype(state_ret.dtype),
            jnp.stack(gla_p).astype(state_gla.dtype), jnp.stack(gla_s).astype(state_gla.dtype))
```

```python
import functools
import math

import jax
import jax.numpy as jnp
from jax import lax
from jax.experimental import pallas as pl
from jax.experimental.pallas import tpu as pltpu

F32 = jnp.float32
BF16 = jnp.bfloat16

D_MODEL = 4096
RET_HEADS = 8
RET_DK = 256
RET_DV = 256
GLA_HEADS = 4
GLA_DK = 256
GLA_DV = 512
GLA_RANK = 16
GLA_GATE_TEMP = 16.0
ROPE_BASE = 10000.0
PAST_LEN = 16384
N_GROUPS = 4
EXPERTS_PER_GROUP = 8
N_EXPERTS = N_GROUPS * EXPERTS_PER_GROUP
TOP_K = 2
D_EXPERT = D_MODEL // 4
EPS = 1e-6

RET_W = RET_HEADS * RET_DK
GLA_KW = GLA_HEADS * GLA_DK
GLA_VW = GLA_HEADS * GLA_DV
N_MAIN = 4 * RET_W + 2 * GLA_KW + 2 * GLA_VW
COL_GQ = 4 * RET_W
COL_GK = COL_GQ + GLA_KW
COL_GV = COL_GK + GLA_KW
COL_GG = COL_GV + GLA_VW

LANES = 128
ROW_TILE = 256
LHS_TILE = 512
RET_CHUNK = 256
GLA_CHUNK = 128
GLA_SUB = 32
DEC_PAIR = 2
MOE_ROWS = 768
MOE_SUB = 256
MOE_TJ = 256
MOE_TN = 512
MOE_DMA_UNROLL = 8
VMEM_LIMIT = 56 * 1024 * 1024


def _cp(semantics, vmem=VMEM_LIMIT):
    return pltpu.CompilerParams(dimension_semantics=semantics, vmem_limit_bytes=vmem)


def _sigmoid(x):
    return 1.0 / (1.0 + jnp.exp(-x))


def _silu(x):
    return x * _sigmoid(x)


def _dot(a, b):
    return jnp.dot(a, b, preferred_element_type=F32)


def _dot_nt(a, b):
    return lax.dot_general(a, b, (((1,), (1,)), ((), ())), preferred_element_type=F32)


def _dot_tn(a, b):
    return lax.dot_general(a, b, (((0,), (0,)), ((), ())), preferred_element_type=F32)


def _rms_norm(x, g):
    ms = jnp.mean(x * x, axis=-1, keepdims=True)
    return x * lax.rsqrt(ms + EPS) * g


def _two_group_specs(shape_tail, tile, n_p_tiles):
    zeros = (0,) * len(shape_tail)

    def p_map(*idx):
        return (jnp.minimum(idx[-1], n_p_tiles - 1),) + zeros

    def s_map(*idx):
        return (jnp.maximum(idx[-1] - n_p_tiles, 0),) + zeros

    return pl.BlockSpec((tile,) + shape_tail, p_map), pl.BlockSpec((tile,) + shape_tail, s_map)


def _norm_in_kernel(n_p_tiles, xp_ref, xs_ref, g_ref, o_ref):
    i = pl.program_id(0)

    @pl.when(i < n_p_tiles)
    def _():
        o_ref[...] = _rms_norm(xp_ref[...], g_ref[...]).astype(o_ref.dtype)

    @pl.when(i >= n_p_tiles)
    def _():
        o_ref[...] = _rms_norm(xs_ref[...], g_ref[...]).astype(o_ref.dtype)


def _norm_in(x_p, x_s, g):
    (t_p, d), t_s = x_p.shape, x_s.shape[0]
    tr = ROW_TILE
    p_spec, s_spec = _two_group_specs((d,), tr, t_p // tr)
    return pl.pallas_call(
        functools.partial(_norm_in_kernel, t_p // tr),
        grid=((t_p + t_s) // tr,),
        in_specs=[p_spec, s_spec, pl.BlockSpec((1, d), lambda i: (0, 0))],
        out_specs=pl.BlockSpec((tr, d), lambda i: (i, 0)),
        out_shape=jax.ShapeDtypeStruct((t_p + t_s, d), BF16),
        compiler_params=_cp(("arbitrary",)),
        name="norm_in",
    )(x_p, x_s, g.reshape(1, d))


def _norm_out_kernel(x_ref, g_ref, o_ref):
    o_ref[...] = _rms_norm(x_ref[...], g_ref[...])


def _norm_out(x, g, row0, n_rows):
    d = x.shape[1]
    tr = ROW_TILE
    blk0 = row0 // tr
    return pl.pallas_call(
        _norm_out_kernel,
        grid=(n_rows // tr,),
        in_specs=[pl.BlockSpec((tr, d), lambda i: (i + blk0, 0)), pl.BlockSpec((1, d), lambda i: (0, 0))],
        out_specs=pl.BlockSpec((tr, d), lambda i: (i, 0)),
        out_shape=jax.ShapeDtypeStruct((n_rows, d), F32),
        compiler_params=_cp(("parallel",)),
        name="norm_out",
    )(x, g.reshape(1, d))


def _in_proj_kernel(a_ref, wt_ref, o_ref, wb):
    @pl.when(pl.program_id(1) == 0)
    def _():
        wb[...] = wt_ref[...].astype(BF16)

    o_ref[...] = _dot_nt(a_ref[...], wb[...])


def _in_proj(h, w_t, n_cols, tm, tn):
    t, d = h.shape
    return pl.pallas_call(
        _in_proj_kernel,
        grid=(n_cols // tn, t // tm),
        in_specs=[pl.BlockSpec((tm, d), lambda j, i: (i, 0)), pl.BlockSpec((tn, d), lambda j, i: (j, 0))],
        out_specs=pl.BlockSpec((tm, tn), lambda j, i: (i, j)),
        out_shape=jax.ShapeDtypeStruct((t, n_cols), F32),
        scratch_shapes=[pltpu.VMEM((tn, d), BF16)],
        compiler_params=_cp(("arbitrary", "arbitrary")),
        name="in_proj",
    )(h, w_t)


def _out_proj_kernel(n_p_tiles, rp_ref, rs_ref, gp_ref, gs_ref, w_ref, o_ref, wb):
    i = pl.program_id(1)

    @pl.when(i == 0)
    def _():
        wb[...] = w_ref[...].astype(BF16)

    @pl.when(i < n_p_tiles)
    def _():
        o_ref[...] = _dot(rp_ref[...], wb[:RET_W, :]) + _dot(gp_ref[...], wb[RET_W:, :])

    @pl.when(i >= n_p_tiles)
    def _():
        o_ref[...] = _dot(rs_ref[...], wb[:RET_W, :]) + _dot(gs_ref[...], wb[RET_W:, :])


def _out_proj(ro_p, ro_s, go_p, go_s, w, tn):
    t_p, t_s = ro_p.shape[0], ro_s.shape[0]
    d_in, d_out = w.shape
    tm = LHS_TILE
    rp_spec, rs_spec = _two_group_specs((RET_W,), tm, t_p // tm)
    gp_spec, gs_spec = _two_group_specs((GLA_VW,), tm, t_p // tm)
    return pl.pallas_call(
        functools.partial(_out_proj_kernel, t_p // tm),
        grid=(d_out // tn, (t_p + t_s) // tm),
        in_specs=[rp_spec, rs_spec, gp_spec, gs_spec, pl.BlockSpec((d_in, tn), lambda j, i: (0, j))],
        out_specs=pl.BlockSpec((tm, tn), lambda j, i: (i, j)),
        out_shape=jax.ShapeDtypeStruct((t_p + t_s, d_out), F32),
        scratch_shapes=[pltpu.VMEM((d_in, tn), BF16)],
        compiler_params=_cp(("arbitrary", "arbitrary")),
        name="out_proj",
    )(ro_p, ro_s, go_p, go_s, w)


def _ple_kernel(a_ref, w_ref, p_ref, wp_ref, r_ref, o_ref, wb):
    @pl.when(pl.program_id(1) == 0)
    def _():
        wb[...] = w_ref[...].astype(BF16)

    gate = _sigmoid(_dot(a_ref[...], wb[...]))
    emb = _dot(p_ref[...].astype(BF16), wp_ref[...].astype(BF16))
    o_ref[...] = r_ref[...] + gate * emb


def _ple(xb, x, p, w_pg, w_pp, tm, tn):
    t, d = x.shape
    kp = p.shape[1]
    o_spec = pl.BlockSpec((tm, tn), lambda j, i: (i, j))
    return pl.pallas_call(
        _ple_kernel,
        grid=(d // tn, t // tm),
        in_specs=[pl.BlockSpec((tm, d), lambda j, i: (i, 0)), pl.BlockSpec((d, tn), lambda j, i: (0, j)),
                  pl.BlockSpec((tm, kp), lambda j, i: (i, 0)), pl.BlockSpec((kp, tn), lambda j, i: (0, j)),
                  o_spec],
        out_specs=o_spec,
        out_shape=jax.ShapeDtypeStruct((t, d), F32),
        scratch_shapes=[pltpu.VMEM((d, tn), BF16)],
        compiler_params=_cp(("arbitrary", "arbitrary")),
        name="ple",
    )(xb, w_pg, p, w_pp, x)


def _log_sigmoid(z):
    return jnp.minimum(z, 0.0) - jnp.log(1.0 + jnp.exp(-jnp.abs(z)))


def _loga_kernel(h_ref, wga_ref, wup_ref, b_ref, o_ref):
    ga = _dot_nt(h_ref[...], wga_ref[...])
    z = _dot(ga.astype(BF16), wup_ref[...]) + b_ref[...]
    o_ref[...] = _log_sigmoid(z) * (1.0 / GLA_GATE_TEMP)


def _log_decay(h, w_ga, w_up, b):
    t, d = h.shape
    tr = ROW_TILE
    return pl.pallas_call(
        _loga_kernel,
        grid=(t // tr,),
        in_specs=[pl.BlockSpec((tr, d), lambda i: (i, 0)),
                  pl.BlockSpec((LANES, d), lambda i: (0, 0)),
                  pl.BlockSpec((LANES, GLA_KW), lambda i: (0, 0)),
                  pl.BlockSpec((1, GLA_KW), lambda i: (0, 0))],
        out_specs=pl.BlockSpec((tr, GLA_KW), lambda i: (i, 0)),
        out_shape=jax.ShapeDtypeStruct((t, GLA_KW), F32),
        compiler_params=_cp(("parallel",)),
        name="gla_log_decay",
    )(h, w_ga, w_up, b.reshape(1, GLA_KW))


def _rotary(x, cos, sin):
    half = x.shape[-1] // 2
    x1, x2 = x[:, :half], x[:, half:]
    return jnp.concatenate([x1 * cos - x2 * sin, x1 * sin + x2 * cos], axis=-1)


def _group_norm_gate(o, gain, gate):
    mu = jnp.mean(o, axis=-1, keepdims=True)
    d = o - mu
    var = jnp.mean(d * d, axis=-1, keepdims=True)
    return d * lax.rsqrt(var + EPS) * gain * _silu(gate)


def _rms_gate(o, gain, gate):
    ms = jnp.mean(o * o, axis=-1, keepdims=True)
    return o * lax.rsqrt(ms + EPS) * gain * _silu(gate)


def _ret_prompt_kernel(lg_ref, q_ref, k_ref, v_ref, g_ref, cos_ref, sin_ref, gn_ref, o_ref, s_ref, state):
    c = RET_CHUNK
    i = pl.program_id(2)
    lg = lg_ref[pl.program_id(1)]

    @pl.when(i == 0)
    def _():
        state[...] = jnp.zeros_like(state)

    cos, sin = cos_ref[...], sin_ref[...]
    q = _rotary(q_ref[...], cos, sin)
    k = _rotary(k_ref[...], cos, sin) * (RET_DK ** -0.5)
    v = v_ref[...].astype(BF16)
    row = lax.broadcasted_iota(jnp.int32, (c, RET_DK), 0).astype(F32)
    q_dec = jnp.exp((row + 1.0) * lg)
    k_dec = jnp.exp((c - 1.0 - row) * lg)
    ii = lax.broadcasted_iota(jnp.int32, (c, c), 0)
    jj = lax.broadcasted_iota(jnp.int32, (c, c), 1)
    diff = (ii - jj).astype(F32)
    intra = jnp.where(diff >= 0, jnp.exp(jnp.maximum(diff, 0.0) * lg), 0.0)
    att = _dot_nt(q.astype(BF16), k.astype(BF16)) * intra
    s_old = state[...]
    o = _dot(att.astype(BF16), v) + _dot((q * q_dec).astype(BF16), s_old.astype(BF16))
    c_dec = jnp.exp(jnp.zeros((1, RET_DV), F32) + c * lg)
    s_new = s_old * c_dec + _dot_tn((k * k_dec).astype(BF16), v)
    state[...] = s_new
    o_ref[...] = _group_norm_gate(o, gn_ref[0], g_ref[...]).astype(o_ref.dtype)

    @pl.when(i == pl.num_programs(2) - 1)
    def _():
        s_ref[0, 0] = s_new


def _ret_prompt(proj, log_g, cos, sin, gn, batch, seq):
    c = RET_CHUNK
    nc = seq // c
    wblk = RET_W // RET_DK

    def col(group):
        return pl.BlockSpec((c, RET_DK), lambda b, h, i, lg: (b * nc + i, group * wblk + h))

    tab = pl.BlockSpec((c, RET_DK // 2), lambda b, h, i, lg: (i, 0))
    return pl.pallas_call(
        _ret_prompt_kernel,
        grid_spec=pltpu.PrefetchScalarGridSpec(
            num_scalar_prefetch=1,
            grid=(batch, RET_HEADS, nc),
            in_specs=[col(0), col(1), col(2), col(3), tab, tab,
                      pl.BlockSpec((1, 1, RET_DV), lambda b, h, i, lg: (h, 0, 0))],
            out_specs=[pl.BlockSpec((c, RET_DV), lambda b, h, i, lg: (b * nc + i, h)),
                       pl.BlockSpec((1, 1, RET_DK, RET_DV), lambda b, h, i, lg: (b, h, 0, 0))],
            scratch_shapes=[pltpu.VMEM((RET_DK, RET_DV), F32)],
        ),
        out_shape=[jax.ShapeDtypeStruct((batch * seq, RET_W), BF16),
                   jax.ShapeDtypeStruct((batch, RET_HEADS, RET_DK, RET_DV), F32)],
        compiler_params=_cp(("parallel", "parallel", "arbitrary")),
        name="retention_prompt",
    )(log_g, proj, proj, proj, proj, cos, sin, gn.reshape(RET_HEADS, 1, RET_DV))


def _ret_sample_kernel(log_g, seq, q_ref, k_ref, v_ref, g_ref, cos_ref, sin_ref, gn_ref, s_in, o_ref, s_out):
    rows = DEC_PAIR * seq
    cos, sin = cos_ref[...], sin_ref[...]
    rid = lax.broadcasted_iota(jnp.int32, (rows, RET_DK), 0)
    pos = (rid % seq).astype(F32)
    batch_of_row = rid // seq
    ii = lax.broadcasted_iota(jnp.int32, (rows, rows), 0)
    jj = lax.broadcasted_iota(jnp.int32, (rows, rows), 1)
    visible = (ii // seq == jj // seq) & (ii >= jj)
    diff = jnp.maximum(ii - jj, 0).astype(F32)
    for h in range(RET_HEADS):
        lg = log_g[h]
        sl = slice(h * RET_DK, (h + 1) * RET_DK)
        q = _rotary(q_ref[:, sl], cos, sin)
        k = _rotary(k_ref[:, sl], cos, sin) * (RET_DK ** -0.5)
        v = v_ref[:, sl].astype(BF16)
        intra = jnp.where(visible, jnp.exp(diff * lg), 0.0)
        att = _dot_nt(q.astype(BF16), k.astype(BF16)) * intra
        o = _dot(att.astype(BF16), v)
        qd = (q * jnp.exp((pos + 1.0) * lg)).astype(BF16)
        kd = k * jnp.exp((seq - 1.0 - pos) * lg)
        for b in range(DEC_PAIR):
            mine = batch_of_row == b
            s_old = s_in[b, h]
            o = o + jnp.where(mine, _dot(qd, s_old.astype(BF16)), 0.0)
            s_out[b, h] = s_old * math.exp(seq * lg) + _dot_tn(jnp.where(mine, kd, 0.0).astype(BF16), v)
        o_ref[:, sl] = _group_norm_gate(o, gn_ref[h], g_ref[:, sl]).astype(o_ref.dtype)


def _ret_sample(proj, row0, log_g, cos, sin, gn, state, batch, seq):
    rows = DEC_PAIR * seq
    blk0 = row0 // rows

    def col(group):
        return pl.BlockSpec((rows, RET_W), lambda i: (i + blk0, group))

    tab = pl.BlockSpec((rows, RET_DK // 2), lambda i: (0, 0))
    st = pl.BlockSpec((DEC_PAIR, RET_HEADS, RET_DK, RET_DV), lambda i: (i, 0, 0, 0))
    return pl.pallas_call(
        functools.partial(_ret_sample_kernel, log_g, seq),
        grid=(batch // DEC_PAIR,),
        in_specs=[col(0), col(1), col(2), col(3), tab, tab,
                  pl.BlockSpec((RET_HEADS, 1, RET_DV), lambda i: (0, 0, 0)), st],
        out_specs=[pl.BlockSpec((rows, RET_W), lambda i: (i, 0)), st],
        out_shape=[jax.ShapeDtypeStruct((batch * seq, RET_W), BF16),
                   jax.ShapeDtypeStruct(state.shape, F32)],
        compiler_params=_cp(("parallel",)),
        name="retention_sample",
    )(proj, proj, proj, proj, cos, sin, gn.reshape(RET_HEADS, 1, RET_DV), state)


def _split3(x):
    hi = x.astype(BF16)
    r1 = x - hi.astype(F32)
    mid = r1.astype(BF16)
    lo = (r1 - mid.astype(F32)).astype(BF16)
    return hi, mid, lo


def _column_scale(row_vec, width):
    n = row_vec.shape[-1]
    t = jnp.transpose(jnp.broadcast_to(row_vec, (LANES, n)))
    return jnp.concatenate([t] * (width // LANES), axis=-1)


def _gla_prompt_kernel(q_ref, k_ref, v_ref, g_ref, la_ref, gn_ref, o_ref, s_ref, state):
    c, sub = GLA_CHUNK, GLA_SUB
    nsub = c // sub
    i = pl.program_id(2)

    @pl.when(i == 0)
    def _():
        state[...] = jnp.zeros_like(state)

    q = q_ref[...]
    k = k_ref[...] * (GLA_DK ** -0.5)
    v = v_ref[...].astype(BF16)
    ii = lax.broadcasted_iota(jnp.int32, (c, c), 0)
    jj = lax.broadcasted_iota(jnp.int32, (c, c), 1)
    causal = ii >= jj
    tri = jnp.where(causal, 1.0, 0.0).astype(BF16)
    hi, mid, lo = _split3(la_ref[...])
    b = _dot(tri, hi) + _dot(tri, mid) + _dot(tri, lo)
    mids = [b[s * sub + sub // 2 - 1: s * sub + sub // 2, :] for s in range(nsub)]
    ref_lvl = jnp.concatenate([jnp.broadcast_to(m, (sub, GLA_DK)) for m in mids], axis=0)
    qd = (q * jnp.exp(b - ref_lvl)).astype(BF16)
    kd = k * jnp.exp(ref_lvl - b)
    blk = lax.broadcasted_iota(jnp.int32, (c, GLA_DK), 0) // sub
    rows = []
    for s in range(nsub):
        scale = jnp.where(blk <= s, jnp.exp(jnp.minimum(mids[s] - ref_lvl, 0.0)), 0.0)
        rows.append(_dot_nt(qd[s * sub:(s + 1) * sub], (kd * scale).astype(BF16)))
    att = jnp.where(causal, jnp.concatenate(rows, axis=0), 0.0)
    s_old = state[...]
    o = _dot(att.astype(BF16), v) + _dot((q * jnp.exp(b)).astype(BF16), s_old.astype(BF16))
    b_last = b[c - 1:c, :]
    k_rem = (k * jnp.exp(b_last - b)).astype(BF16)
    s_new = s_old * _column_scale(jnp.exp(b_last), GLA_DV) + _dot_tn(k_rem, v)
    state[...] = s_new
    o_ref[...] = _rms_gate(o, gn_ref[0], g_ref[...]).astype(o_ref.dtype)

    @pl.when(i == pl.num_programs(2) - 1)
    def _():
        s_ref[0, 0] = s_new


def _gla_prompt(proj, log_a, gn, batch, seq):
    c = GLA_CHUNK
    nc = seq // c

    def col(start, width):
        return pl.BlockSpec((c, width), lambda b, h, i: (b * nc + i, start // width + h))

    return pl.pallas_call(
        _gla_prompt_kernel,
        grid=(batch, GLA_HEADS, nc),
        in_specs=[col(COL_GQ, GLA_DK), col(COL_GK, GLA_DK), col(COL_GV, GLA_DV), col(COL_GG, GLA_DV),
                  pl.BlockSpec((c, GLA_DK), lambda b, h, i: (b * nc + i, h)),
                  pl.BlockSpec((1, 1, GLA_DV), lambda b, h, i: (h, 0, 0))],
        out_specs=[pl.BlockSpec((c, GLA_DV), lambda b, h, i: (b * nc + i, h)),
                   pl.BlockSpec((1, 1, GLA_DK, GLA_DV), lambda b, h, i: (b, h, 0, 0))],
        out_shape=[jax.ShapeDtypeStruct((batch * seq, GLA_VW), BF16),
                   jax.ShapeDtypeStruct((batch, GLA_HEADS, GLA_DK, GLA_DV), F32)],
        scratch_shapes=[pltpu.VMEM((GLA_DK, GLA_DV), F32)],
        compiler_params=_cp(("parallel", "parallel", "arbitrary")),
        name="gla_prompt",
    )(proj, proj, proj, proj, log_a, gn.reshape(GLA_HEADS, 1, GLA_DV))


def _gla_sample_kernel(seq, q_ref, k_ref, v_ref, g_ref, la_ref, gn_ref, s_in, o_ref, s_out):
    rows = DEC_PAIR * seq
    rid = lax.broadcasted_iota(jnp.int32, (rows, GLA_DK), 0)
    pos = rid % seq
    batch_of_row = rid // seq
    ii = lax.broadcasted_iota(jnp.int32, (rows, rows), 0)
    jj = lax.broadcasted_iota(jnp.int32, (rows, rows), 1)
    visible = (ii // seq == jj // seq) & (ii >= jj)
    for h in range(GLA_HEADS):
        ks = slice(h * GLA_DK, (h + 1) * GLA_DK)
        vs = slice(h * GLA_DV, (h + 1) * GLA_DV)
        la = la_ref[:, ks]
        b = la
        for d in range(1, seq):
            b = b + jnp.where(pos >= d, pltpu.roll(la, d, axis=0), 0.0)
        q = q_ref[:, ks]
        k = k_ref[:, ks] * (GLA_DK ** -0.5)
        v = v_ref[:, vs].astype(BF16)
        qb = (q * jnp.exp(b)).astype(BF16)
        kb = (k * jnp.exp(-b)).astype(BF16)
        att = jnp.where(visible, _dot_nt(qb, kb), 0.0)
        o = _dot(att.astype(BF16), v)
        for bi in range(DEC_PAIR):
            mine = batch_of_row == bi
            last = bi * seq + seq - 1
            b_last = b[last:last + 1, :]
            s_old = s_in[bi, h]
            o = o + jnp.where(mine[:, :1], _dot(qb, s_old.astype(BF16)), 0.0)
            k_rem = jnp.where(mine, k * jnp.exp(b_last - b), 0.0).astype(BF16)
            s_out[bi, h] = s_old * _column_scale(jnp.exp(b_last), GLA_DV) + _dot_tn(k_rem, v)
        o_ref[:, vs] = _rms_gate(o, gn_ref[h], g_ref[:, vs]).astype(o_ref.dtype)


def _gla_sample(proj, row0, log_a, gn, state, batch, seq):
    rows = DEC_PAIR * seq
    blk0 = row0 // rows
    st = pl.BlockSpec((DEC_PAIR, GLA_HEADS, GLA_DK, GLA_DV), lambda i: (i, 0, 0, 0))
    return pl.pallas_call(
        functools.partial(_gla_sample_kernel, seq),
        grid=(batch // DEC_PAIR,),
        in_specs=[pl.BlockSpec((rows, GLA_KW), lambda i: (i + blk0, COL_GQ // GLA_KW)),
                  pl.BlockSpec((rows, GLA_KW), lambda i: (i + blk0, COL_GK // GLA_KW)),
                  pl.BlockSpec((rows, GLA_VW), lambda i: (i + blk0, COL_GV // GLA_VW)),
                  pl.BlockSpec((rows, GLA_VW), lambda i: (i + blk0, COL_GG // GLA_VW)),
                  pl.BlockSpec((rows, GLA_KW), lambda i: (i + blk0, 0)),
                  pl.BlockSpec((GLA_HEADS, 1, GLA_DV), lambda i: (0, 0, 0)), st],
        out_specs=[pl.BlockSpec((rows, GLA_VW), lambda i: (i, 0)), st],
        out_shape=[jax.ShapeDtypeStruct((batch * seq, GLA_VW), BF16),
                   jax.ShapeDtypeStruct(state.shape, F32)],
        compiler_params=_cp(("parallel",)),
        name="gla_sample",
    )(proj, proj, proj, proj, log_a, gn.reshape(GLA_HEADS, 1, GLA_DV), state)


def _router_kernel(n_p_tiles, m_ref, xp_ref, xs_ref, g_ref, wr_ref, br_ref, x_ref, hm_ref, idx_ref, gate_ref):
    tr = m_ref.shape[0]
    i = pl.program_id(0)

    @pl.when(i < n_p_tiles)
    def _():
        x_ref[...] = xp_ref[...] + m_ref[...]

    @pl.when(i >= n_p_tiles)
    def _():
        x_ref[...] = xs_ref[...] + m_ref[...]

    hm = _rms_norm(x_ref[...], g_ref[...])
    hm_ref[...] = hm
    h1, h2, _ = _split3(hm)
    w = wr_ref[...]
    w1 = w.astype(BF16)
    w2 = (w - w1.astype(F32)).astype(BF16)
    logits = _dot(h1, w1) + _dot(h1, w2) + _dot(h2, w1) + br_ref[...]
    lane = lax.broadcasted_iota(jnp.int32, (tr, LANES), 1).astype(F32)
    neg, far = -1e30, 1e9
    is_group = lane < N_GROUPS
    gl = jnp.where(is_group, logits, neg)
    gmax = jnp.max(gl, axis=-1, keepdims=True)
    gidx = jnp.min(jnp.where(gl == gmax, lane, far), axis=-1, keepdims=True)
    gsum = jnp.sum(jnp.where(is_group, jnp.exp(gl - gmax), 0.0), axis=-1, keepdims=True)
    g_p = 1.0 / gsum
    lo = N_GROUPS + EXPERTS_PER_GROUP * gidx
    in_sel = (lane >= lo) & (lane < lo + EXPERTS_PER_GROUP)
    el = jnp.where(in_sel, logits, neg)
    emax = jnp.max(el, axis=-1, keepdims=True)
    e1 = jnp.min(jnp.where(el == emax, lane, far), axis=-1, keepdims=True)
    esum = jnp.sum(jnp.where(in_sel, jnp.exp(el - emax), 0.0), axis=-1, keepdims=True)
    el2 = jnp.where(lane == e1, neg, el)
    m2 = jnp.max(el2, axis=-1, keepdims=True)
    e2 = jnp.min(jnp.where(el2 == m2, lane, far), axis=-1, keepdims=True)
    p1 = 1.0 / esum
    p2 = jnp.exp(m2 - emax) / esum
    den = p1 + p2
    idx_ref[...] = jnp.where(lane == 0, e1 - N_GROUPS, jnp.where(lane == 1, e2 - N_GROUPS, 0.0)).astype(jnp.int32)
    gate_ref[...] = jnp.where(lane == 0, g_p * p1 / den, jnp.where(lane == 1, g_p * p2 / den, 0.0))


def _router(m, x_p, x_s, g, w_r, b_r):
    t, d = m.shape
    t_p = x_p.shape[0]
    tr = ROW_TILE
    p_spec, s_spec = _two_group_specs((d,), tr, t_p // tr)
    row = pl.BlockSpec((tr, d), lambda i: (i, 0))
    lane_row = pl.BlockSpec((tr, LANES), lambda i: (i, 0))
    return pl.pallas_call(
        functools.partial(_router_kernel, t_p // tr),
        grid=(t // tr,),
        in_specs=[row, p_spec, s_spec, pl.BlockSpec((1, d), lambda i: (0, 0)),
                  pl.BlockSpec((d, LANES), lambda i: (0, 0)), pl.BlockSpec((1, LANES), lambda i: (0, 0))],
        out_specs=[row, row, lane_row, lane_row],
        out_shape=[jax.ShapeDtypeStruct((t, d), F32), jax.ShapeDtypeStruct((t, d), F32),
                   jax.ShapeDtypeStruct((t, LANES), jnp.int32), jax.ShapeDtypeStruct((t, LANES), F32)],
        compiler_params=_cp(("arbitrary",)),
        name="moe_router",
    )(m, x_p, x_s, g.reshape(1, d), w_r, b_r)


def _moe_kernel(item_e, item_start, item_n, item_used, tok, dst,
                hm_hbm, wg_ref, wu_ref, wd_ref, y_hbm,
                rows_in, rows_out, hid, wgb, wub, wdb, sem_in, sem_out):
    del item_e, item_used
    i = pl.program_id(0)
    j = pl.program_id(1)
    n_items = pl.num_programs(0)
    n_steps = pl.num_programs(1)
    n_up = D_EXPERT // MOE_TJ
    n_dn = D_MODEL // MOE_TN
    n, s0 = item_n[i], item_start[i]
    nxt = jnp.minimum(i + 1, n_items - 1)
    n_next, s_next = jnp.where(i + 1 < n_items, item_n[nxt], 0), item_start[nxt]
    prv = jnp.maximum(i - 1, 0)
    n_prev, s_prev = jnp.where(i > 0, item_n[prv], 0), item_start[prv]
    nsub = (n + MOE_SUB - 1) // MOE_SUB

    def gather(s, r):
        return pltpu.make_async_copy(hm_hbm.at[pl.ds(tok[s + r], 1), :], rows_in.at[pl.ds(r, 1), :], sem_in)

    def scatter(s, r):
        return pltpu.make_async_copy(rows_out.at[pl.ds(r, 1), :], y_hbm.at[pl.ds(dst[s + r], 1), :], sem_out)

    def gathered_group(c):
        rows = pl.ds(pl.multiple_of(c * MOE_DMA_UNROLL, MOE_DMA_UNROLL), MOE_DMA_UNROLL)
        return pltpu.make_async_copy(hm_hbm.at[pl.ds(0, MOE_DMA_UNROLL), :], rows_in.at[rows, :], sem_in)

    def scattered_group(c):
        rows = pl.ds(pl.multiple_of(c * MOE_DMA_UNROLL, MOE_DMA_UNROLL), MOE_DMA_UNROLL)
        return pltpu.make_async_copy(rows_out.at[rows, :], y_hbm.at[pl.ds(0, MOE_DMA_UNROLL), :], sem_out)

    def for_rows(count, fn, group_fn=None):
        groups = count // MOE_DMA_UNROLL

        def group(c, carry):
            if group_fn is None:
                for u in range(MOE_DMA_UNROLL):
                    fn(c * MOE_DMA_UNROLL + u)
            else:
                group_fn(c)
            return carry
        lax.fori_loop(0, groups, group, 0)

        def single(r, carry):
            fn(r)
            return carry
        lax.fori_loop(groups * MOE_DMA_UNROLL, count, single, 0)

    def for_sub_blocks(fn):
        def body(sb, carry):
            fn(pl.multiple_of(sb * MOE_SUB, MOE_SUB))
            return carry
        lax.fori_loop(0, nsub, body, 0)

    @pl.when((i == 0) & (j == 0))
    def _():
        rows_in[...] = jnp.zeros_like(rows_in)
        for_rows(n, lambda r: gather(s0, r).start())

    @pl.when(j == 0)
    def _():
        for_rows(n, lambda r: gather(s0, r).wait(), lambda c: gathered_group(c).wait())

    @pl.when((j < n_up) & (n > 0))
    def _():
        wgb[...] = wg_ref[0].astype(BF16)
        wub[...] = wu_ref[0].astype(BF16)

        def up(r0):
            x = rows_in[pl.ds(r0, MOE_SUB), :].astype(BF16)
            act = _silu(_dot(x, wgb[...])) * _dot(x, wub[...])
            hid[j, pl.ds(r0, MOE_SUB), :] = act.astype(BF16)
        for_sub_blocks(up)

    @pl.when(j == n_up)
    def _():
        for_rows(n_next, lambda r: gather(s_next, r).start())
        for_rows(n_prev, lambda r: scatter(s_prev, r).wait(), lambda c: scattered_group(c).wait())

    for c in range(n_dn):
        @pl.when((j == n_up + c) & (n > 0))
        def _(c=c):
            wdb[...] = wd_ref[0].astype(BF16)

            def down(r0):
                act = jnp.concatenate([hid[u, pl.ds(r0, MOE_SUB), :] for u in range(n_up)], axis=-1)
                rows_out[pl.ds(r0, MOE_SUB), c * MOE_TN:(c + 1) * MOE_TN] = _dot(act, wdb[...])
            for_sub_blocks(down)

    @pl.when(j == n_steps - 1)
    def _():
        for_rows(n, lambda r: scatter(s0, r).start())

    @pl.when((j == n_steps - 1) & (i == n_items - 1))
    def _():
        for_rows(n, lambda r: scatter(s0, r).wait(), lambda c: scattered_group(c).wait())


def _moe_experts(hm, w_gate, w_up, w_down, items, tok, dst, n_assign):
    item_e, item_start, item_n, item_used = items
    n_items = item_e.shape[0]
    n_up = D_EXPERT // MOE_TJ
    n_dn = D_MODEL // MOE_TN

    def up_map(i, j, e, st, n, used, tok, dst):
        return (e[i], 0, jnp.where(used[i] > 0, jnp.minimum(j, n_up - 1), n_up - 1))

    def dn_map(i, j, e, st, n, used, tok, dst):
        return (e[i], 0, jnp.where(used[i] > 0, jnp.maximum(j - n_up, 0), n_dn - 1))

    return pl.pallas_call(
        _moe_kernel,
        grid_spec=pltpu.PrefetchScalarGridSpec(
            num_scalar_prefetch=6,
            grid=(n_items, n_up + n_dn),
            in_specs=[pl.BlockSpec(memory_space=pl.ANY),
                      pl.BlockSpec((1, D_MODEL, MOE_TJ), up_map),
                      pl.BlockSpec((1, D_MODEL, MOE_TJ), up_map),
                      pl.BlockSpec((1, D_EXPERT, MOE_TN), dn_map)],
            out_specs=pl.BlockSpec(memory_space=pl.ANY),
            scratch_shapes=[pltpu.VMEM((MOE_ROWS, D_MODEL), F32),
                            pltpu.VMEM((MOE_ROWS, D_MODEL), F32),
                            pltpu.VMEM((n_up, MOE_ROWS, MOE_TJ), BF16),
                            pltpu.VMEM((D_MODEL, MOE_TJ), BF16),
                            pltpu.VMEM((D_MODEL, MOE_TJ), BF16),
                            pltpu.VMEM((D_EXPERT, MOE_TN), BF16),
                            pltpu.SemaphoreType.DMA(()),
                            pltpu.SemaphoreType.DMA(())],
        ),
        out_shape=jax.ShapeDtypeStruct((n_assign, D_MODEL), F32),
        compiler_params=_cp(("arbitrary", "arbitrary")),
        name="moe_experts",
    )(item_e, item_start, item_n, item_used, tok, dst, hm, w_gate, w_up, w_down)


def _moe_plan(expert_idx):
    n_assign = expert_idx.size
    flat_e = expert_idx.reshape(n_assign)
    order = jnp.argsort(flat_e).astype(jnp.int32)
    counts = jnp.bincount(flat_e, length=N_EXPERTS).astype(jnp.int32)
    start = jnp.cumsum(counts) - counts
    per_e = (counts + MOE_ROWS - 1) // MOE_ROWS
    item_end = jnp.cumsum(per_e)
    n_items = n_assign // MOE_ROWS + N_EXPERTS
    ids = jnp.arange(n_items, dtype=jnp.int32)
    used = ids < item_end[-1]
    last = jnp.maximum(item_end[-1] - 1, 0)
    e_of = jnp.minimum(jnp.searchsorted(item_end, jnp.minimum(ids, last), side="right"), N_EXPERTS - 1).astype(jnp.int32)
    local = jnp.minimum(ids, last) - (item_end - per_e)[e_of]
    item_start = start[e_of] + local * MOE_ROWS
    item_n = jnp.where(used, jnp.clip(counts[e_of] - local * MOE_ROWS, 0, MOE_ROWS), 0)
    items = (e_of, item_start.astype(jnp.int32), item_n.astype(jnp.int32), used.astype(jnp.int32))
    token = order // TOP_K
    dst_row = (order % TOP_K) * (n_assign // TOP_K) + token
    return items, token, dst_row


def _combine_kernel(x_ref, y0_ref, y1_ref, gate_ref, o_ref, ob_ref):
    x = x_ref[...] + (y0_ref[...] * gate_ref[:, 0:1] + y1_ref[...] * gate_ref[:, 1:2])
    o_ref[...] = x
    ob_ref[...] = x.astype(BF16)


def _combine(x, y, gate):
    t, d = x.shape
    tr = ROW_TILE
    row = pl.BlockSpec((tr, d), lambda i: (i, 0))
    return pl.pallas_call(
        _combine_kernel,
        grid=(t // tr,),
        in_specs=[row, row, pl.BlockSpec((tr, d), lambda i: (i + t // tr, 0)),
                  pl.BlockSpec((tr, LANES), lambda i: (i, 0))],
        out_specs=[row, row],
        out_shape=[jax.ShapeDtypeStruct((t, d), F32), jax.ShapeDtypeStruct((t, d), BF16)],
        compiler_params=_cp(("parallel",)),
        name="moe_combine",
    )(x, y, y, gate)


def _rope_tables(pos):
    half = RET_DK // 2
    inv = ROPE_BASE ** (-jnp.arange(half, dtype=F32) / half)
    ang = pos.astype(F32)[:, None] * inv[None, :]
    return jnp.cos(ang), jnp.sin(ang)


def kernel(x_prompt, x_sample, state_ret, state_gla, p_prompt, p_sample, g_mix, w_in, w_gla_up, b_gla,
           ret_norm_g, gla_norm_g, w_out, g_moe, w_rg, b_rg, w_re, b_re, w_gate, w_up, w_down, w_pg, w_pp,
           g_final):
    n_p, l_p, d = x_prompt.shape
    n_s, l_s, _ = x_sample.shape
    depth = g_mix.shape[0]
    t_p, t_s = n_p * l_p, n_s * l_s
    t = t_p + t_s

    log_g_py = [math.log1p(-(2.0 ** (-5.0 - h))) for h in range(RET_HEADS)]
    log_g = jnp.asarray(log_g_py, dtype=F32)
    cos_p, sin_p = _rope_tables(jnp.arange(l_p, dtype=jnp.int32))
    cos_s, sin_s = _rope_tables(PAST_LEN + jnp.arange(l_s, dtype=jnp.int32))
    cos_s, sin_s = jnp.tile(cos_s, (DEC_PAIR, 1)), jnp.tile(sin_s, (DEC_PAIR, 1))

    x_p = x_prompt.reshape(t_p, d)
    x_s = x_sample.reshape(t_s, d)
    x = None
    ret_p, ret_s, gla_p, gla_s = [], [], [], []
    for l in range(depth):
        w_in_t = jnp.swapaxes(w_in[l], 0, 1)
        w_ga = jnp.pad(w_in_t[N_MAIN:], ((0, LANES - GLA_RANK), (0, 0))).astype(BF16)
        w_gup = jnp.pad(w_gla_up[l], ((0, LANES - GLA_RANK), (0, 0))).astype(BF16)
        w_r = jnp.pad(jnp.concatenate([w_rg[l], w_re[l]], axis=1),
                      ((0, 0), (0, LANES - N_GROUPS - N_EXPERTS)))
        b_r = jnp.pad(jnp.concatenate([b_rg[l], b_re[l]]), (0, LANES - N_GROUPS - N_EXPERTS)).reshape(1, LANES)
        p = jnp.concatenate([p_prompt[l].reshape(t_p, -1), p_sample[l].reshape(t_s, -1)], axis=0)
        if x is not None:
            x_p, x_s = x[:t_p], x[t_p:]

        h = _norm_in(x_p, x_s, g_mix[l])
        proj = _in_proj(h, w_in_t, N_MAIN, t // 8, 512)
        log_a = _log_decay(h, w_ga, w_gup, b_gla[l])
        ro_p, sr_p = _ret_prompt(proj, log_g, cos_p, sin_p, ret_norm_g[l], n_p, l_p)
        go_p, sg_p = _gla_prompt(proj, log_a, gla_norm_g[l], n_p, l_p)
        ro_s, sr_s = _ret_sample(proj, t_p, log_g_py, cos_s, sin_s, ret_norm_g[l], state_ret[l], n_s, l_s)
        go_s, sg_s = _gla_sample(proj, t_p, log_a, gla_norm_g[l], state_gla[l], n_s, l_s)
        ret_p.append(sr_p)
        ret_s.append(sr_s)
        gla_p.append(sg_p)
        gla_s.append(sg_s)
        mix = _out_proj(ro_p, ro_s, go_p, go_s, w_out[l], 512)

        x, hm, idx, gate = _router(mix, x_p, x_s, g_moe[l], w_r, b_r)
        items, tok, dst = _moe_plan(idx[:, :TOP_K])
        y = _moe_experts(hm, w_gate[l], w_up[l], w_down[l], items, tok, dst, t * TOP_K)
        x, xb = _combine(x, y, gate)

        x = _ple(xb, x, p, w_pg[l], w_pp[l], t // 8, 512)

    y_prompt = _norm_out(x, g_final, 0, t_p).reshape(n_p, l_p, d)
    y_sample = _norm_out(x, g_final, t_p, t_s).reshape(n_s, l_s, d)
    return (y_prompt, y_sample,
            jnp.stack(ret_p).astype(state_ret.dtype), jnp.stack(ret_s).astype(state_ret.dtype),
            jnp.stack(gla_p).astype(state_gla.dtype), jnp.stack(gla_s).astype(state_gla.dtype))
```

```python
import functools
import math

import jax
import jax.numpy as jnp
from jax import lax
from jax.experimental import pallas as pl
from jax.experimental.pallas import tpu as pltpu

F32 = jnp.float32
BF16 = jnp.bfloat16

D_MODEL = 4096
RET_HEADS = 8
RET_DK = 256
RET_DV = 256
GLA_HEADS = 4
GLA_DK = 256
GLA_DV = 512
GLA_RANK = 16
GLA_GATE_TEMP = 16.0
ROPE_BASE = 10000.0
PAST_LEN = 16384
N_GROUPS = 4
EXPERTS_PER_GROUP = 8
N_EXPERTS = N_GROUPS * EXPERTS_PER_GROUP
TOP_K = 2
D_EXPERT = D_MODEL // 4
EPS = 1e-6

RET_W = RET_HEADS * RET_DK
GLA_KW = GLA_HEADS * GLA_DK
GLA_VW = GLA_HEADS * GLA_DV
N_MAIN = 4 * RET_W + 2 * GLA_KW + 2 * GLA_VW
COL_GQ = 4 * RET_W
COL_GK = COL_GQ + GLA_KW
COL_GV = COL_GK + GLA_KW
COL_GG = COL_GV + GLA_VW

LANES = 128
ROW_TILE = 256
LHS_TILE = 512
RET_CHUNK = 256
GLA_CHUNK = 128
GLA_SUB = 32
DEC_PAIR = 2
MOE_ROWS = 768
MOE_BLK = 128
MOE_TJ = 256
MOE_TN = 1024
MOE_DMA_UNROLL = 8
VMEM_LIMIT = 56 * 1024 * 1024


def _cp(semantics, vmem=VMEM_LIMIT):
    return pltpu.CompilerParams(dimension_semantics=semantics, vmem_limit_bytes=vmem)


def _sigmoid(x):
    return 1.0 / (1.0 + jnp.exp(-x))


def _silu(x):
    return x * _sigmoid(x)


def _dot(a, b):
    return jnp.dot(a, b, preferred_element_type=F32)


def _dot_nt(a, b):
    return lax.dot_general(a, b, (((1,), (1,)), ((), ())), preferred_element_type=F32)


def _dot_tn(a, b):
    return lax.dot_general(a, b, (((0,), (0,)), ((), ())), preferred_element_type=F32)


def _rms_norm(x, g):
    ms = jnp.mean(x * x, axis=-1, keepdims=True)
    return x * lax.rsqrt(ms + EPS) * g


def _two_group_specs(shape_tail, tile, n_p_tiles):
    zeros = (0,) * len(shape_tail)

    def p_map(*idx):
        return (jnp.minimum(idx[-1], n_p_tiles - 1),) + zeros

    def s_map(*idx):
        return (jnp.maximum(idx[-1] - n_p_tiles, 0),) + zeros

    return pl.BlockSpec((tile,) + shape_tail, p_map), pl.BlockSpec((tile,) + shape_tail, s_map)


def _norm_in_kernel(n_p_tiles, xp_ref, xs_ref, g_ref, o_ref):
    i = pl.program_id(0)

    @pl.when(i < n_p_tiles)
    def _():
        o_ref[...] = _rms_norm(xp_ref[...], g_ref[...]).astype(o_ref.dtype)

    @pl.when(i >= n_p_tiles)
    def _():
        o_ref[...] = _rms_norm(xs_ref[...], g_ref[...]).astype(o_ref.dtype)


def _norm_in(x_p, x_s, g):
    (t_p, d), t_s = x_p.shape, x_s.shape[0]
    tr = ROW_TILE
    p_spec, s_spec = _two_group_specs((d,), tr, t_p // tr)
    return pl.pallas_call(
        functools.partial(_norm_in_kernel, t_p // tr),
        grid=((t_p + t_s) // tr,),
        in_specs=[p_spec, s_spec, pl.BlockSpec((1, d), lambda i: (0, 0))],
        out_specs=pl.BlockSpec((tr, d), lambda i: (i, 0)),
        out_shape=jax.ShapeDtypeStruct((t_p + t_s, d), BF16),
        compiler_params=_cp(("arbitrary",)),
        name="norm_in",
    )(x_p, x_s, g.reshape(1, d))


def _norm_out_kernel(x_ref, g_ref, o_ref):
    o_ref[...] = _rms_norm(x_ref[...], g_ref[...])


def _norm_out(x, g, row0, n_rows):
    d = x.shape[1]
    tr = ROW_TILE
    blk0 = row0 // tr
    return pl.pallas_call(
        _norm_out_kernel,
        grid=(n_rows // tr,),
        in_specs=[pl.BlockSpec((tr, d), lambda i: (i + blk0, 0)), pl.BlockSpec((1, d), lambda i: (0, 0))],
        out_specs=pl.BlockSpec((tr, d), lambda i: (i, 0)),
        out_shape=jax.ShapeDtypeStruct((n_rows, d), F32),
        compiler_params=_cp(("parallel",)),
        name="norm_out",
    )(x, g.reshape(1, d))


def _in_proj_kernel(a_ref, wt_ref, o_ref, wb):
    @pl.when(pl.program_id(1) == 0)
    def _():
        wb[...] = wt_ref[...].astype(BF16)

    o_ref[...] = _dot_nt(a_ref[...], wb[...])


def _in_proj(h, w_t, n_cols, tm, tn):
    t, d = h.shape
    return pl.pallas_call(
        _in_proj_kernel,
        grid=(n_cols // tn, t // tm),
        in_specs=[pl.BlockSpec((tm, d), lambda j, i: (i, 0)), pl.BlockSpec((tn, d), lambda j, i: (j, 0))],
        out_specs=pl.BlockSpec((tm, tn), lambda j, i: (i, j)),
        out_shape=jax.ShapeDtypeStruct((t, n_cols), F32),
        scratch_shapes=[pltpu.VMEM((tn, d), BF16)],
        compiler_params=_cp(("arbitrary", "arbitrary")),
        name="in_proj",
    )(h, w_t)


def _out_proj_kernel(n_p_tiles, rp_ref, rs_ref, gp_ref, gs_ref, w_ref, o_ref, wb):
    i = pl.program_id(1)

    @pl.when(i == 0)
    def _():
        wb[...] = w_ref[...].astype(BF16)

    @pl.when(i < n_p_tiles)
    def _():
        o_ref[...] = _dot(rp_ref[...], wb[:RET_W, :]) + _dot(gp_ref[...], wb[RET_W:, :])

    @pl.when(i >= n_p_tiles)
    def _():
        o_ref[...] = _dot(rs_ref[...], wb[:RET_W, :]) + _dot(gs_ref[...], wb[RET_W:, :])


def _out_proj(ro_p, ro_s, go_p, go_s, w, tn):
    t_p, t_s = ro_p.shape[0], ro_s.shape[0]
    d_in, d_out = w.shape
    tm = LHS_TILE
    rp_spec, rs_spec = _two_group_specs((RET_W,), tm, t_p // tm)
    gp_spec, gs_spec = _two_group_specs((GLA_VW,), tm, t_p // tm)
    return pl.pallas_call(
        functools.partial(_out_proj_kernel, t_p // tm),
        grid=(d_out // tn, (t_p + t_s) // tm),
        in_specs=[rp_spec, rs_spec, gp_spec, gs_spec, pl.BlockSpec((d_in, tn), lambda j, i: (0, j))],
        out_specs=pl.BlockSpec((tm, tn), lambda j, i: (i, j)),
        out_shape=jax.ShapeDtypeStruct((t_p + t_s, d_out), F32),
        scratch_shapes=[pltpu.VMEM((d_in, tn), BF16)],
        compiler_params=_cp(("arbitrary", "arbitrary")),
        name="out_proj",
    )(ro_p, ro_s, go_p, go_s, w)


def _ple_kernel(a_ref, w_ref, p_ref, wp_ref, r_ref, o_ref, wb):
    @pl.when(pl.program_id(1) == 0)
    def _():
        wb[...] = w_ref[...].astype(BF16)

    gate = _sigmoid(_dot(a_ref[...], wb[...]))
    emb = _dot(p_ref[...].astype(BF16), wp_ref[...].astype(BF16))
    o_ref[...] = r_ref[...] + gate * emb


def _ple(xb, x, p, w_pg, w_pp, tm, tn):
    t, d = x.shape
    kp = p.shape[1]
    o_spec = pl.BlockSpec((tm, tn), lambda j, i: (i, j))
    return pl.pallas_call(
        _ple_kernel,
        grid=(d // tn, t // tm),
        in_specs=[pl.BlockSpec((tm, d), lambda j, i: (i, 0)), pl.BlockSpec((d, tn), lambda j, i: (0, j)),
                  pl.BlockSpec((tm, kp), lambda j, i: (i, 0)), pl.BlockSpec((kp, tn), lambda j, i: (0, j)),
                  o_spec],
        out_specs=o_spec,
        out_shape=jax.ShapeDtypeStruct((t, d), F32),
        scratch_shapes=[pltpu.VMEM((d, tn), BF16)],
        compiler_params=_cp(("arbitrary", "arbitrary")),
        name="ple",
    )(xb, w_pg, p, w_pp, x)


def _log_sigmoid(z):
    return jnp.minimum(z, 0.0) - jnp.log(1.0 + jnp.exp(-jnp.abs(z)))


def _loga_kernel(h_ref, wga_ref, wup_ref, b_ref, o_ref):
    ga = _dot_nt(h_ref[...], wga_ref[...])
    z = _dot(ga.astype(BF16), wup_ref[...]) + b_ref[...]
    o_ref[...] = _log_sigmoid(z) * (1.0 / GLA_GATE_TEMP)


def _log_decay(h, w_ga, w_up, b):
    t, d = h.shape
    tr = ROW_TILE
    return pl.pallas_call(
        _loga_kernel,
        grid=(t // tr,),
        in_specs=[pl.BlockSpec((tr, d), lambda i: (i, 0)),
                  pl.BlockSpec((LANES, d), lambda i: (0, 0)),
                  pl.BlockSpec((LANES, GLA_KW), lambda i: (0, 0)),
                  pl.BlockSpec((1, GLA_KW), lambda i: (0, 0))],
        out_specs=pl.BlockSpec((tr, GLA_KW), lambda i: (i, 0)),
        out_shape=jax.ShapeDtypeStruct((t, GLA_KW), F32),
        compiler_params=_cp(("parallel",)),
        name="gla_log_decay",
    )(h, w_ga, w_up, b.reshape(1, GLA_KW))


def _rotary(x, cos, sin):
    half = x.shape[-1] // 2
    x1, x2 = x[:, :half], x[:, half:]
    return jnp.concatenate([x1 * cos - x2 * sin, x1 * sin + x2 * cos], axis=-1)


def _group_norm_gate(o, gain, gate):
    mu = jnp.mean(o, axis=-1, keepdims=True)
    d = o - mu
    var = jnp.mean(d * d, axis=-1, keepdims=True)
    return d * lax.rsqrt(var + EPS) * gain * _silu(gate)


def _rms_gate(o, gain, gate):
    ms = jnp.mean(o * o, axis=-1, keepdims=True)
    return o * lax.rsqrt(ms + EPS) * gain * _silu(gate)


def _ret_prompt_kernel(lg_ref, q_ref, k_ref, v_ref, g_ref, cos_ref, sin_ref, gn_ref, o_ref, s_ref, state):
    c = RET_CHUNK
    i = pl.program_id(2)
    lg = lg_ref[pl.program_id(1)]

    @pl.when(i == 0)
    def _():
        state[...] = jnp.zeros_like(state)

    cos, sin = cos_ref[...], sin_ref[...]
    q = _rotary(q_ref[...], cos, sin)
    k = _rotary(k_ref[...], cos, sin) * (RET_DK ** -0.5)
    v = v_ref[...].astype(BF16)
    row = lax.broadcasted_iota(jnp.int32, (c, RET_DK), 0).astype(F32)
    q_dec = jnp.exp((row + 1.0) * lg)
    k_dec = jnp.exp((c - 1.0 - row) * lg)
    ii = lax.broadcasted_iota(jnp.int32, (c, c), 0)
    jj = lax.broadcasted_iota(jnp.int32, (c, c), 1)
    diff = (ii - jj).astype(F32)
    intra = jnp.where(diff >= 0, jnp.exp(jnp.maximum(diff, 0.0) * lg), 0.0)
    att = _dot_nt(q.astype(BF16), k.astype(BF16)) * intra
    s_old = state[...]
    o = _dot(att.astype(BF16), v) + _dot((q * q_dec).astype(BF16), s_old.astype(BF16))
    c_dec = jnp.exp(jnp.zeros((1, RET_DV), F32) + c * lg)
    s_new = s_old * c_dec + _dot_tn((k * k_dec).astype(BF16), v)
    state[...] = s_new
    o_ref[...] = _group_norm_gate(o, gn_ref[0], g_ref[...]).astype(o_ref.dtype)

    @pl.when(i == pl.num_programs(2) - 1)
    def _():
        s_ref[0, 0] = s_new


def _ret_prompt(proj, log_g, cos, sin, gn, batch, seq):
    c = RET_CHUNK
    nc = seq // c
    wblk = RET_W // RET_DK

    def col(group):
        return pl.BlockSpec((c, RET_DK), lambda b, h, i, lg: (b * nc + i, group * wblk + h))

    tab = pl.BlockSpec((c, RET_DK // 2), lambda b, h, i, lg: (i, 0))
    return pl.pallas_call(
        _ret_prompt_kernel,
        grid_spec=pltpu.PrefetchScalarGridSpec(
            num_scalar_prefetch=1,
            grid=(batch, RET_HEADS, nc),
            in_specs=[col(0), col(1), col(2), col(3), tab, tab,
                      pl.BlockSpec((1, 1, RET_DV), lambda b, h, i, lg: (h, 0, 0))],
            out_specs=[pl.BlockSpec((c, RET_DV), lambda b, h, i, lg: (b * nc + i, h)),
                       pl.BlockSpec((1, 1, RET_DK, RET_DV), lambda b, h, i, lg: (b, h, 0, 0))],
            scratch_shapes=[pltpu.VMEM((RET_DK, RET_DV), F32)],
        ),
        out_shape=[jax.ShapeDtypeStruct((batch * seq, RET_W), BF16),
                   jax.ShapeDtypeStruct((batch, RET_HEADS, RET_DK, RET_DV), F32)],
        compiler_params=_cp(("parallel", "parallel", "arbitrary")),
        name="retention_prompt",
    )(log_g, proj, proj, proj, proj, cos, sin, gn.reshape(RET_HEADS, 1, RET_DV))


def _ret_sample_kernel(log_g, seq, q_ref, k_ref, v_ref, g_ref, cos_ref, sin_ref, gn_ref, s_in, o_ref, s_out):
    rows = DEC_PAIR * seq
    cos, sin = cos_ref[...], sin_ref[...]
    rid = lax.broadcasted_iota(jnp.int32, (rows, RET_DK), 0)
    pos = (rid % seq).astype(F32)
    batch_of_row = rid // seq
    ii = lax.broadcasted_iota(jnp.int32, (rows, rows), 0)
    jj = lax.broadcasted_iota(jnp.int32, (rows, rows), 1)
    visible = (ii // seq == jj // seq) & (ii >= jj)
    diff = jnp.maximum(ii - jj, 0).astype(F32)
    for h in range(RET_HEADS):
        lg = log_g[h]
        sl = slice(h * RET_DK, (h + 1) * RET_DK)
        q = _rotary(q_ref[:, sl], cos, sin)
        k = _rotary(k_ref[:, sl], cos, sin) * (RET_DK ** -0.5)
        v = v_ref[:, sl].astype(BF16)
        intra = jnp.where(visible, jnp.exp(diff * lg), 0.0)
        att = _dot_nt(q.astype(BF16), k.astype(BF16)) * intra
        o = _dot(att.astype(BF16), v)
        qd = (q * jnp.exp((pos + 1.0) * lg)).astype(BF16)
        kd = k * jnp.exp((seq - 1.0 - pos) * lg)
        for b in range(DEC_PAIR):
            mine = batch_of_row == b
            s_old = s_in[b, h]
            o = o + jnp.where(mine, _dot(qd, s_old.astype(BF16)), 0.0)
            s_out[b, h] = s_old * math.exp(seq * lg) + _dot_tn(jnp.where(mine, kd, 0.0).astype(BF16), v)
        o_ref[:, sl] = _group_norm_gate(o, gn_ref[h], g_ref[:, sl]).astype(o_ref.dtype)


def _ret_sample(proj, row0, log_g, cos, sin, gn, state, batch, seq):
    rows = DEC_PAIR * seq
    blk0 = row0 // rows

    def col(group):
        return pl.BlockSpec((rows, RET_W), lambda i: (i + blk0, group))

    tab = pl.BlockSpec((rows, RET_DK // 2), lambda i: (0, 0))
    st = pl.BlockSpec((DEC_PAIR, RET_HEADS, RET_DK, RET_DV), lambda i: (i, 0, 0, 0))
    return pl.pallas_call(
        functools.partial(_ret_sample_kernel, log_g, seq),
        grid=(batch // DEC_PAIR,),
        in_specs=[col(0), col(1), col(2), col(3), tab, tab,
                  pl.BlockSpec((RET_HEADS, 1, RET_DV), lambda i: (0, 0, 0)), st],
        out_specs=[pl.BlockSpec((rows, RET_W), lambda i: (i, 0)), st],
        out_shape=[jax.ShapeDtypeStruct((batch * seq, RET_W), BF16),
                   jax.ShapeDtypeStruct(state.shape, F32)],
        compiler_params=_cp(("parallel",)),
        name="retention_sample",
    )(proj, proj, proj, proj, cos, sin, gn.reshape(RET_HEADS, 1, RET_DV), state)


def _split3(x):
    hi = x.astype(BF16)
    r1 = x - hi.astype(F32)
    mid = r1.astype(BF16)
    lo = (r1 - mid.astype(F32)).astype(BF16)
    return hi, mid, lo


def _column_scale(row_vec, width):
    n = row_vec.shape[-1]
    t = jnp.transpose(jnp.broadcast_to(row_vec, (LANES, n)))
    return jnp.concatenate([t] * (width // LANES), axis=-1)


def _gla_prompt_kernel(q_ref, k_ref, v_ref, g_ref, la_ref, gn_ref, o_ref, s_ref, state):
    c, sub = GLA_CHUNK, GLA_SUB
    nsub = c // sub
    i = pl.program_id(2)

    @pl.when(i == 0)
    def _():
        state[...] = jnp.zeros_like(state)

    q = q_ref[...]
    k = k_ref[...] * (GLA_DK ** -0.5)
    v = v_ref[...].astype(BF16)
    ii = lax.broadcasted_iota(jnp.int32, (c, c), 0)
    jj = lax.broadcasted_iota(jnp.int32, (c, c), 1)
    causal = ii >= jj
    tri = jnp.where(causal, 1.0, 0.0).astype(BF16)
    hi, mid, lo = _split3(la_ref[...])
    b = _dot(tri, hi) + _dot(tri, mid) + _dot(tri, lo)
    mids = [b[s * sub + sub // 2 - 1: s * sub + sub // 2, :] for s in range(nsub)]
    ref_lvl = jnp.concatenate([jnp.broadcast_to(m, (sub, GLA_DK)) for m in mids], axis=0)
    qd = (q * jnp.exp(b - ref_lvl)).astype(BF16)
    kd = k * jnp.exp(ref_lvl - b)
    blk = lax.broadcasted_iota(jnp.int32, (c, GLA_DK), 0) // sub
    rows = []
    for s in range(nsub):
        scale = jnp.where(blk <= s, jnp.exp(jnp.minimum(mids[s] - ref_lvl, 0.0)), 0.0)
        rows.append(_dot_nt(qd[s * sub:(s + 1) * sub], (kd * scale).astype(BF16)))
    att = jnp.where(causal, jnp.concatenate(rows, axis=0), 0.0)
    s_old = state[...]
    o = _dot(att.astype(BF16), v) + _dot((q * jnp.exp(b)).astype(BF16), s_old.astype(BF16))
    b_last = b[c - 1:c, :]
    k_rem = (k * jnp.exp(b_last - b)).astype(BF16)
    s_new = s_old * _column_scale(jnp.exp(b_last), GLA_DV) + _dot_tn(k_rem, v)
    state[...] = s_new
    o_ref[...] = _rms_gate(o, gn_ref[0], g_ref[...]).astype(o_ref.dtype)

    @pl.when(i == pl.num_programs(2) - 1)
    def _():
        s_ref[0, 0] = s_new


def _gla_prompt(proj, log_a, gn, batch, seq):
    c = GLA_CHUNK
    nc = seq // c

    def col(start, width):
        return pl.BlockSpec((c, width), lambda b, h, i: (b * nc + i, start // width + h))

    return pl.pallas_call(
        _gla_prompt_kernel,
        grid=(batch, GLA_HEADS, nc),
        in_specs=[col(COL_GQ, GLA_DK), col(COL_GK, GLA_DK), col(COL_GV, GLA_DV), col(COL_GG, GLA_DV),
                  pl.BlockSpec((c, GLA_DK), lambda b, h, i: (b * nc + i, h)),
                  pl.BlockSpec((1, 1, GLA_DV), lambda b, h, i: (h, 0, 0))],
        out_specs=[pl.BlockSpec((c, GLA_DV), lambda b, h, i: (b * nc + i, h)),
                   pl.BlockSpec((1, 1, GLA_DK, GLA_DV), lambda b, h, i: (b, h, 0, 0))],
        out_shape=[jax.ShapeDtypeStruct((batch * seq, GLA_VW), BF16),
                   jax.ShapeDtypeStruct((batch, GLA_HEADS, GLA_DK, GLA_DV), F32)],
        scratch_shapes=[pltpu.VMEM((GLA_DK, GLA_DV), F32)],
        compiler_params=_cp(("parallel", "parallel", "arbitrary")),
        name="gla_prompt",
    )(proj, proj, proj, proj, log_a, gn.reshape(GLA_HEADS, 1, GLA_DV))


def _gla_sample_kernel(seq, q_ref, k_ref, v_ref, g_ref, la_ref, gn_ref, s_in, o_ref, s_out):
    rows = DEC_PAIR * seq
    rid = lax.broadcasted_iota(jnp.int32, (rows, GLA_DK), 0)
    pos = rid % seq
    batch_of_row = rid // seq
    ii = lax.broadcasted_iota(jnp.int32, (rows, rows), 0)
    jj = lax.broadcasted_iota(jnp.int32, (rows, rows), 1)
    visible = (ii // seq == jj // seq) & (ii >= jj)
    for h in range(GLA_HEADS):
        ks = slice(h * GLA_DK, (h + 1) * GLA_DK)
        vs = slice(h * GLA_DV, (h + 1) * GLA_DV)
        la = la_ref[:, ks]
        b = la
        for d in range(1, seq):
            b = b + jnp.where(pos >= d, pltpu.roll(la, d, axis=0), 0.0)
        q = q_ref[:, ks]
        k = k_ref[:, ks] * (GLA_DK ** -0.5)
        v = v_ref[:, vs].astype(BF16)
        qb = (q * jnp.exp(b)).astype(BF16)
        kb = (k * jnp.exp(-b)).astype(BF16)
        att = jnp.where(visible, _dot_nt(qb, kb), 0.0)
        o = _dot(att.astype(BF16), v)
        for bi in range(DEC_PAIR):
            mine = batch_of_row == bi
            last = bi * seq + seq - 1
            b_last = b[last:last + 1, :]
            s_old = s_in[bi, h]
            o = o + jnp.where(mine[:, :1], _dot(qb, s_old.astype(BF16)), 0.0)
            k_rem = jnp.where(mine, k * jnp.exp(b_last - b), 0.0).astype(BF16)
            s_out[bi, h] = s_old * _column_scale(jnp.exp(b_last), GLA_DV) + _dot_tn(k_rem, v)
        o_ref[:, vs] = _rms_gate(o, gn_ref[h], g_ref[:, vs]).astype(o_ref.dtype)


def _gla_sample(proj, row0, log_a, gn, state, batch, seq):
    rows = DEC_PAIR * seq
    blk0 = row0 // rows
    st = pl.BlockSpec((DEC_PAIR, GLA_HEADS, GLA_DK, GLA_DV), lambda i: (i, 0, 0, 0))
    return pl.pallas_call(
        functools.partial(_gla_sample_kernel, seq),
        grid=(batch // DEC_PAIR,),
        in_specs=[pl.BlockSpec((rows, GLA_KW), lambda i: (i + blk0, COL_GQ // GLA_KW)),
                  pl.BlockSpec((rows, GLA_KW), lambda i: (i + blk0, COL_GK // GLA_KW)),
                  pl.BlockSpec((rows, GLA_VW), lambda i: (i + blk0, COL_GV // GLA_VW)),
                  pl.BlockSpec((rows, GLA_VW), lambda i: (i + blk0, COL_GG // GLA_VW)),
                  pl.BlockSpec((rows, GLA_KW), lambda i: (i + blk0, 0)),
                  pl.BlockSpec((GLA_HEADS, 1, GLA_DV), lambda i: (0, 0, 0)), st],
        out_specs=[pl.BlockSpec((rows, GLA_VW), lambda i: (i, 0)), st],
        out_shape=[jax.ShapeDtypeStruct((batch * seq, GLA_VW), BF16),
                   jax.ShapeDtypeStruct(state.shape, F32)],
        compiler_params=_cp(("parallel",)),
        name="gla_sample",
    )(proj, proj, proj, proj, log_a, gn.reshape(GLA_HEADS, 1, GLA_DV), state)


def _router_kernel(n_p_tiles, m_ref, xp_ref, xs_ref, g_ref, wr_ref, br_ref, x_ref, hm_ref, idx_ref, gate_ref):
    tr = m_ref.shape[0]
    i = pl.program_id(0)

    @pl.when(i < n_p_tiles)
    def _():
        x_ref[...] = xp_ref[...] + m_ref[...]

    @pl.when(i >= n_p_tiles)
    def _():
        x_ref[...] = xs_ref[...] + m_ref[...]

    hm = _rms_norm(x_ref[...], g_ref[...])
    hm_ref[...] = hm
    h1, h2, _ = _split3(hm)
    w = wr_ref[...]
    w1 = w.astype(BF16)
    w2 = (w - w1.astype(F32)).astype(BF16)
    logits = _dot(h1, w1) + _dot(h1, w2) + _dot(h2, w1) + br_ref[...]
    lane = lax.broadcasted_iota(jnp.int32, (tr, LANES), 1).astype(F32)
    neg, far = -1e30, 1e9
    is_group = lane < N_GROUPS
    gl = jnp.where(is_group, logits, neg)
    gmax = jnp.max(gl, axis=-1, keepdims=True)
    gidx = jnp.min(jnp.where(gl == gmax, lane, far), axis=-1, keepdims=True)
    gsum = jnp.sum(jnp.where(is_group, jnp.exp(gl - gmax), 0.0), axis=-1, keepdims=True)
    g_p = 1.0 / gsum
    lo = N_GROUPS + EXPERTS_PER_GROUP * gidx
    in_sel = (lane >= lo) & (lane < lo + EXPERTS_PER_GROUP)
    el = jnp.where(in_sel, logits, neg)
    emax = jnp.max(el, axis=-1, keepdims=True)
    e1 = jnp.min(jnp.where(el == emax, lane, far), axis=-1, keepdims=True)
    esum = jnp.sum(jnp.where(in_sel, jnp.exp(el - emax), 0.0), axis=-1, keepdims=True)
    el2 = jnp.where(lane == e1, neg, el)
    m2 = jnp.max(el2, axis=-1, keepdims=True)
    e2 = jnp.min(jnp.where(el2 == m2, lane, far), axis=-1, keepdims=True)
    p1 = 1.0 / esum
    p2 = jnp.exp(m2 - emax) / esum
    den = p1 + p2
    idx_ref[...] = jnp.where(lane == 0, e1 - N_GROUPS, jnp.where(lane == 1, e2 - N_GROUPS, 0.0)).astype(jnp.int32)
    gate_ref[...] = jnp.where(lane == 0, g_p * p1 / den, jnp.where(lane == 1, g_p * p2 / den, 0.0))


def _router(m, x_p, x_s, g, w_r, b_r):
    t, d = m.shape
    t_p = x_p.shape[0]
    tr = ROW_TILE
    p_spec, s_spec = _two_group_specs((d,), tr, t_p // tr)
    row = pl.BlockSpec((tr, d), lambda i: (i, 0))
    lane_row = pl.BlockSpec((tr, LANES), lambda i: (i, 0))
    return pl.pallas_call(
        functools.partial(_router_kernel, t_p // tr),
        grid=(t // tr,),
        in_specs=[row, p_spec, s_spec, pl.BlockSpec((1, d), lambda i: (0, 0)),
                  pl.BlockSpec((d, LANES), lambda i: (0, 0)), pl.BlockSpec((1, LANES), lambda i: (0, 0))],
        out_specs=[row, row, lane_row, lane_row],
        out_shape=[jax.ShapeDtypeStruct((t, d), F32), jax.ShapeDtypeStruct((t, d), F32),
                   jax.ShapeDtypeStruct((t, LANES), jnp.int32), jax.ShapeDtypeStruct((t, LANES), F32)],
        compiler_params=_cp(("arbitrary",)),
        name="moe_router",
    )(m, x_p, x_s, g.reshape(1, d), w_r, b_r)


def _moe_kernel(n_used, item_e, item_start, item_n, tok, dst,
                hm_hbm, wg_ref, wu_ref, wd_ref, y_hbm,
                rows_in, rows_out, hid, sem_in, sem_out):
    del n_used, item_e
    i = pl.program_id(0)
    j = pl.program_id(1)
    n_items = pl.num_programs(0)
    n_steps = pl.num_programs(1)
    n_up = D_EXPERT // MOE_TJ
    n_dn = D_MODEL // MOE_TN
    n, s0 = item_n[i], item_start[i]
    nxt = jnp.minimum(i + 1, n_items - 1)
    n_next, s_next = jnp.where(i + 1 < n_items, item_n[nxt], 0), item_start[nxt]
    prv = jnp.maximum(i - 1, 0)
    n_prev, s_prev = jnp.where(i > 0, item_n[prv], 0), item_start[prv]
    nblk = (n + MOE_BLK - 1) // MOE_BLK

    def gather(s, r):
        return pltpu.make_async_copy(hm_hbm.at[pl.ds(tok[s + r], 1), :], rows_in.at[pl.ds(r, 1), :], sem_in)

    def scatter(s, r):
        return pltpu.make_async_copy(rows_out.at[pl.ds(r, 1), :], y_hbm.at[pl.ds(dst[s + r], 1), :], sem_out)

    def gathered_group(c):
        rows = pl.ds(pl.multiple_of(c * MOE_DMA_UNROLL, MOE_DMA_UNROLL), MOE_DMA_UNROLL)
        return pltpu.make_async_copy(hm_hbm.at[pl.ds(0, MOE_DMA_UNROLL), :], rows_in.at[rows, :], sem_in)

    def scattered_group(c):
        rows = pl.ds(pl.multiple_of(c * MOE_DMA_UNROLL, MOE_DMA_UNROLL), MOE_DMA_UNROLL)
        return pltpu.make_async_copy(rows_out.at[rows, :], y_hbm.at[pl.ds(0, MOE_DMA_UNROLL), :], sem_out)

    def for_rows(count, fn, group_fn=None):
        groups = count // MOE_DMA_UNROLL

        def group(c, carry):
            if group_fn is None:
                for u in range(MOE_DMA_UNROLL):
                    fn(c * MOE_DMA_UNROLL + u)
            else:
                group_fn(c)
            return carry
        lax.fori_loop(0, groups, group, 0)

        def single(r, carry):
            fn(r)
            return carry
        lax.fori_loop(groups * MOE_DMA_UNROLL, count, single, 0)

    def for_row_count(fn):
        for k in range(1, MOE_ROWS // MOE_BLK + 1):
            @pl.when(nblk == k)
            def _(k=k):
                fn(k * MOE_BLK)

    @pl.when((i == 0) & (j == 0))
    def _():
        rows_in[...] = jnp.zeros_like(rows_in)
        for_rows(n, lambda r: gather(s0, r).start())

    @pl.when(j == 0)
    def _():
        for_rows(n, lambda r: gather(s0, r).wait(), lambda c: gathered_group(c).wait())

    @pl.when(j < n_up)
    def _():
        def up(m):
            x = rows_in[0:m, :].astype(BF16)
            act = _silu(_dot(x, wg_ref[0].astype(BF16))) * _dot(x, wu_ref[0].astype(BF16))
            hid[j, 0:m, :] = act.astype(BF16)
        for_row_count(up)

    @pl.when(j == n_up)
    def _():
        for_rows(n_next, lambda r: gather(s_next, r).start())
        for_rows(n_prev, lambda r: scatter(s_prev, r).wait(), lambda c: scattered_group(c).wait())

    @pl.when(j >= n_up)
    def _():
        col = pl.multiple_of((j - n_up) * MOE_TN, MOE_TN)

        def down(m):
            act = jnp.concatenate([hid[u, 0:m, :] for u in range(n_up)], axis=-1)
            rows_out[0:m, pl.ds(col, MOE_TN)] = _dot(act, wd_ref[0].astype(BF16))
        for_row_count(down)

    @pl.when(j == n_steps - 1)
    def _():
        for_rows(n, lambda r: scatter(s0, r).start())

    @pl.when((j == n_steps - 1) & (i == n_items - 1))
    def _():
        for_rows(n, lambda r: scatter(s0, r).wait(), lambda c: scattered_group(c).wait())


def _moe_experts(hm, w_gate, w_up, w_down, items, tok, dst, n_assign):
    n_used, item_e, item_start, item_n = items
    n_up = D_EXPERT // MOE_TJ
    n_dn = D_MODEL // MOE_TN

    def up_map(i, j, nu, e, st, n, tok, dst):
        return (e[i], 0, jnp.minimum(j, n_up - 1))

    def dn_map(i, j, nu, e, st, n, tok, dst):
        return (e[i], 0, jnp.maximum(j - n_up, 0))

    return pl.pallas_call(
        _moe_kernel,
        grid_spec=pltpu.PrefetchScalarGridSpec(
            num_scalar_prefetch=6,
            grid=(n_used[0], n_up + n_dn),
            in_specs=[pl.BlockSpec(memory_space=pl.ANY),
                      pl.BlockSpec((1, D_MODEL, MOE_TJ), up_map),
                      pl.BlockSpec((1, D_MODEL, MOE_TJ), up_map),
                      pl.BlockSpec((1, D_EXPERT, MOE_TN), dn_map)],
            out_specs=pl.BlockSpec(memory_space=pl.ANY),
            scratch_shapes=[pltpu.VMEM((MOE_ROWS, D_MODEL), F32),
                            pltpu.VMEM((MOE_ROWS, D_MODEL), F32),
                            pltpu.VMEM((n_up, MOE_ROWS, MOE_TJ), BF16),
                            pltpu.SemaphoreType.DMA(()),
                            pltpu.SemaphoreType.DMA(())],
        ),
        out_shape=jax.ShapeDtypeStruct((n_assign, D_MODEL), F32),
        compiler_params=_cp(("arbitrary", "arbitrary")),
        name="moe_experts",
    )(n_used, item_e, item_start, item_n, tok, dst, hm, w_gate, w_up, w_down)


def _moe_plan(expert_idx):
    n_assign = expert_idx.size
    flat_e = expert_idx.reshape(n_assign)
    order = jnp.argsort(flat_e).astype(jnp.int32)
    counts = jnp.bincount(flat_e, length=N_EXPERTS).astype(jnp.int32)
    start = jnp.cumsum(counts) - counts
    per_e = (counts + MOE_ROWS - 1) // MOE_ROWS
    item_end = jnp.cumsum(per_e)
    n_items = n_assign // MOE_ROWS + N_EXPERTS
    ids = jnp.arange(n_items, dtype=jnp.int32)
    used = ids < item_end[-1]
    last = jnp.maximum(item_end[-1] - 1, 0)
    e_of = jnp.minimum(jnp.searchsorted(item_end, jnp.minimum(ids, last), side="right"), N_EXPERTS - 1).astype(jnp.int32)
    local = jnp.minimum(ids, last) - (item_end - per_e)[e_of]
    item_start = start[e_of] + local * MOE_ROWS
    item_n = jnp.where(used, jnp.clip(counts[e_of] - local * MOE_ROWS, 0, MOE_ROWS), 0)
    items = (item_end[-1:].astype(jnp.int32), e_of, item_start.astype(jnp.int32), item_n.astype(jnp.int32))
    token = order // TOP_K
    dst_row = (order % TOP_K) * (n_assign // TOP_K) + token
    return items, token, dst_row


def _combine_kernel(x_ref, y0_ref, y1_ref, gate_ref, o_ref, ob_ref):
    x = x_ref[...] + (y0_ref[...] * gate_ref[:, 0:1] + y1_ref[...] * gate_ref[:, 1:2])
    o_ref[...] = x
    ob_ref[...] = x.astype(BF16)


def _combine(x, y, gate):
    t, d = x.shape
    tr = ROW_TILE
    row = pl.BlockSpec((tr, d), lambda i: (i, 0))
    return pl.pallas_call(
        _combine_kernel,
        grid=(t // tr,),
        in_specs=[row, row, pl.BlockSpec((tr, d), lambda i: (i + t // tr, 0)),
                  pl.BlockSpec((tr, LANES), lambda i: (i, 0))],
        out_specs=[row, row],
        out_shape=[jax.ShapeDtypeStruct((t, d), F32), jax.ShapeDtypeStruct((t, d), BF16)],
        compiler_params=_cp(("parallel",)),
        name="moe_combine",
    )(x, y, y, gate)


def _rope_tables(pos):
    half = RET_DK // 2
    inv = ROPE_BASE ** (-jnp.arange(half, dtype=F32) / half)
    ang = pos.astype(F32)[:, None] * inv[None, :]
    return jnp.cos(ang), jnp.sin(ang)


def kernel(x_prompt, x_sample, state_ret, state_gla, p_prompt, p_sample, g_mix, w_in, w_gla_up, b_gla,
           ret_norm_g, gla_norm_g, w_out, g_moe, w_rg, b_rg, w_re, b_re, w_gate, w_up, w_down, w_pg, w_pp,
           g_final):
    n_p, l_p, d = x_prompt.shape
    n_s, l_s, _ = x_sample.shape
    depth = g_mix.shape[0]
    t_p, t_s = n_p * l_p, n_s * l_s
    t = t_p + t_s

    log_g_py = [math.log1p(-(2.0 ** (-5.0 - h))) for h in range(RET_HEADS)]
    log_g = jnp.asarray(log_g_py, dtype=F32)
    cos_p, sin_p = _rope_tables(jnp.arange(l_p, dtype=jnp.int32))
    cos_s, sin_s = _rope_tables(PAST_LEN + jnp.arange(l_s, dtype=jnp.int32))
    cos_s, sin_s = jnp.tile(cos_s, (DEC_PAIR, 1)), jnp.tile(sin_s, (DEC_PAIR, 1))

    x_p = x_prompt.reshape(t_p, d)
    x_s = x_sample.reshape(t_s, d)
    x = None
    ret_p, ret_s, gla_p, gla_s = [], [], [], []
    for l in range(depth):
        w_in_t = jnp.swapaxes(w_in[l], 0, 1)
        w_ga = jnp.pad(w_in_t[N_MAIN:], ((0, LANES - GLA_RANK), (0, 0))).astype(BF16)
        w_gup = jnp.pad(w_gla_up[l], ((0, LANES - GLA_RANK), (0, 0))).astype(BF16)
        w_r = jnp.pad(jnp.concatenate([w_rg[l], w_re[l]], axis=1),
                      ((0, 0), (0, LANES - N_GROUPS - N_EXPERTS)))
        b_r = jnp.pad(jnp.concatenate([b_rg[l], b_re[l]]), (0, LANES - N_GROUPS - N_EXPERTS)).reshape(1, LANES)
        p = jnp.concatenate([p_prompt[l].reshape(t_p, -1), p_sample[l].reshape(t_s, -1)], axis=0)
        if x is not None:
            x_p, x_s = x[:t_p], x[t_p:]

        h = _norm_in(x_p, x_s, g_mix[l])
        proj = _in_proj(h, w_in_t, N_MAIN, t // 8, 512)
        log_a = _log_decay(h, w_ga, w_gup, b_gla[l])
        ro_p, sr_p = _ret_prompt(proj, log_g, cos_p, sin_p, ret_norm_g[l], n_p, l_p)
        go_p, sg_p = _gla_prompt(proj, log_a, gla_norm_g[l], n_p, l_p)
        ro_s, sr_s = _ret_sample(proj, t_p, log_g_py, cos_s, sin_s, ret_norm_g[l], state_ret[l], n_s, l_s)
        go_s, sg_s = _gla_sample(proj, t_p, log_a, gla_norm_g[l], state_gla[l], n_s, l_s)
        ret_p.append(sr_p)
        ret_s.append(sr_s)
        gla_p.append(sg_p)
        gla_s.append(sg_s)
        mix = _out_proj(ro_p, ro_s, go_p, go_s, w_out[l], 512)

        x, hm, idx, gate = _router(mix, x_p, x_s, g_moe[l], w_r, b_r)
        items, tok, dst = _moe_plan(idx[:, :TOP_K])
        y = _moe_experts(hm, w_gate[l], w_up[l], w_down[l], items, tok, dst, t * TOP_K)
        x, xb = _combine(x, y, gate)

        x = _ple(xb, x, p, w_pg[l], w_pp[l], t // 8, 512)

    y_prompt = _norm_out(x, g_final, 0, t_p).reshape(n_p, l_p, d)
    y_sample = _norm_out(x, g_final, t_p, t_s).reshape(n_s, l_s, d)
    return (y_prompt, y_sample,
            jnp.stack(ret_p).astype(state_ret.dtype), jnp.stack(ret_s).astype(state_ret.dtype),
            jnp.stack(gla_p).astype(state_gla.dtype), jnp.stack(gla_s).astype(state_gla.dtype))
```

```python
import functools
import math

import jax
import jax.numpy as jnp
from jax import lax
from jax.experimental import pallas as pl
from jax.experimental.pallas import tpu as pltpu

F32 = jnp.float32
BF16 = jnp.bfloat16

D_MODEL = 4096
RET_HEADS = 8
RET_DK = 256
RET_DV = 256
GLA_HEADS = 4
GLA_DK = 256
GLA_DV = 512
GLA_RANK = 16
GLA_GATE_TEMP = 16.0
ROPE_BASE = 10000.0
PAST_LEN = 16384
N_GROUPS = 4
EXPERTS_PER_GROUP = 8
N_EXPERTS = N_GROUPS * EXPERTS_PER_GROUP
TOP_K = 2
D_EXPERT = D_MODEL // 4
EPS = 1e-6

RET_W = RET_HEADS * RET_DK
GLA_KW = GLA_HEADS * GLA_DK
GLA_VW = GLA_HEADS * GLA_DV
N_MAIN = 4 * RET_W + 2 * GLA_KW + 2 * GLA_VW
COL_GQ = 4 * RET_W
COL_GK = COL_GQ + GLA_KW
COL_GV = COL_GK + GLA_KW
COL_GG = COL_GV + GLA_VW

LANES = 128
ROW_TILE = 256
LHS_TILE = 512
RET_CHUNK = 256
GLA_CHUNK = 128
GLA_SUB = 32
DEC_PAIR = 2
MOE_ROWS = 768
MOE_BLK = 128
MOE_TJ = 256
MOE_TN = 1024
MOE_DMA_UNROLL = 8
VMEM_LIMIT = 56 * 1024 * 1024


def _cp(semantics, vmem=VMEM_LIMIT):
    return pltpu.CompilerParams(dimension_semantics=semantics, vmem_limit_bytes=vmem)


def _sigmoid(x):
    return 1.0 / (1.0 + jnp.exp(-x))


def _silu(x):
    return x * _sigmoid(x)


def _dot(a, b):
    return jnp.dot(a, b, preferred_element_type=F32)


def _dot_nt(a, b):
    return lax.dot_general(a, b, (((1,), (1,)), ((), ())), preferred_element_type=F32)


def _dot_tn(a, b):
    return lax.dot_general(a, b, (((0,), (0,)), ((), ())), preferred_element_type=F32)


def _rms_norm(x, g):
    ms = jnp.mean(x * x, axis=-1, keepdims=True)
    return x * lax.rsqrt(ms + EPS) * g


def _two_group_specs(shape_tail, tile, n_p_tiles):
    zeros = (0,) * len(shape_tail)

    def p_map(*idx):
        return (jnp.minimum(idx[-1], n_p_tiles - 1),) + zeros

    def s_map(*idx):
        return (jnp.maximum(idx[-1] - n_p_tiles, 0),) + zeros

    return pl.BlockSpec((tile,) + shape_tail, p_map), pl.BlockSpec((tile,) + shape_tail, s_map)


def _log_sigmoid(z):
    return jnp.minimum(z, 0.0) - jnp.log(1.0 + jnp.exp(-jnp.abs(z)))


def _norm_in_kernel(n_p_tiles, xp_ref, xs_ref, g_ref, wga_ref, wup_ref, b_ref, h_ref, la_ref):
    i = pl.program_id(0)

    def emit(x):
        h = _rms_norm(x, g_ref[...]).astype(BF16)
        h_ref[...] = h
        ga = _dot_nt(h, wga_ref[...])
        z = _dot(ga.astype(BF16), wup_ref[...]) + b_ref[...]
        la_ref[...] = _log_sigmoid(z) * (1.0 / GLA_GATE_TEMP)

    @pl.when(i < n_p_tiles)
    def _():
        emit(xp_ref[...])

    @pl.when(i >= n_p_tiles)
    def _():
        emit(xs_ref[...])


def _norm_in(x_p, x_s, g, w_ga, w_up, b):
    (t_p, d), t_s = x_p.shape, x_s.shape[0]
    tr = ROW_TILE
    p_spec, s_spec = _two_group_specs((d,), tr, t_p // tr)
    return pl.pallas_call(
        functools.partial(_norm_in_kernel, t_p // tr),
        grid=((t_p + t_s) // tr,),
        in_specs=[p_spec, s_spec, pl.BlockSpec((1, d), lambda i: (0, 0)),
                  pl.BlockSpec((LANES, d), lambda i: (0, 0)),
                  pl.BlockSpec((LANES, GLA_KW), lambda i: (0, 0)),
                  pl.BlockSpec((1, GLA_KW), lambda i: (0, 0))],
        out_specs=[pl.BlockSpec((tr, d), lambda i: (i, 0)), pl.BlockSpec((tr, GLA_KW), lambda i: (i, 0))],
        out_shape=[jax.ShapeDtypeStruct((t_p + t_s, d), BF16), jax.ShapeDtypeStruct((t_p + t_s, GLA_KW), F32)],
        compiler_params=_cp(("arbitrary",)),
        name="norm_in",
    )(x_p, x_s, g.reshape(1, d), w_ga, w_up, b.reshape(1, GLA_KW))


def _norm_out_kernel(x_ref, g_ref, o_ref):
    o_ref[...] = _rms_norm(x_ref[...], g_ref[...])


def _norm_out(x, g, row0, n_rows):
    d = x.shape[1]
    tr = ROW_TILE
    blk0 = row0 // tr
    return pl.pallas_call(
        _norm_out_kernel,
        grid=(n_rows // tr,),
        in_specs=[pl.BlockSpec((tr, d), lambda i: (i + blk0, 0)), pl.BlockSpec((1, d), lambda i: (0, 0))],
        out_specs=pl.BlockSpec((tr, d), lambda i: (i, 0)),
        out_shape=jax.ShapeDtypeStruct((n_rows, d), F32),
        compiler_params=_cp(("parallel",)),
        name="norm_out",
    )(x, g.reshape(1, d))


def _in_proj_kernel(a_ref, wt_ref, o_ref, wb):
    @pl.when(pl.program_id(1) == 0)
    def _():
        wb[...] = wt_ref[...].astype(BF16)

    o_ref[...] = _dot_nt(a_ref[...], wb[...])


def _in_proj(h, w_t, n_cols, tm, tn):
    t, d = h.shape
    return pl.pallas_call(
        _in_proj_kernel,
        grid=(n_cols // tn, t // tm),
        in_specs=[pl.BlockSpec((tm, d), lambda j, i: (i, 0)), pl.BlockSpec((tn, d), lambda j, i: (j, 0))],
        out_specs=pl.BlockSpec((tm, tn), lambda j, i: (i, j)),
        out_shape=jax.ShapeDtypeStruct((t, n_cols), F32),
        scratch_shapes=[pltpu.VMEM((tn, d), BF16)],
        compiler_params=_cp(("arbitrary", "arbitrary")),
        name="in_proj",
    )(h, w_t)


def _out_proj_kernel(n_p_tiles, rp_ref, rs_ref, gp_ref, gs_ref, w_ref, o_ref, wb):
    i = pl.program_id(1)

    @pl.when(i == 0)
    def _():
        wb[...] = w_ref[...].astype(BF16)

    @pl.when(i < n_p_tiles)
    def _():
        o_ref[...] = _dot(rp_ref[...], wb[:RET_W, :]) + _dot(gp_ref[...], wb[RET_W:, :])

    @pl.when(i >= n_p_tiles)
    def _():
        o_ref[...] = _dot(rs_ref[...], wb[:RET_W, :]) + _dot(gs_ref[...], wb[RET_W:, :])


def _out_proj(ro_p, ro_s, go_p, go_s, w, tn):
    t_p, t_s = ro_p.shape[0], ro_s.shape[0]
    d_in, d_out = w.shape
    tm = LHS_TILE
    rp_spec, rs_spec = _two_group_specs((RET_W,), tm, t_p // tm)
    gp_spec, gs_spec = _two_group_specs((GLA_VW,), tm, t_p // tm)
    return pl.pallas_call(
        functools.partial(_out_proj_kernel, t_p // tm),
        grid=(d_out // tn, (t_p + t_s) // tm),
        in_specs=[rp_spec, rs_spec, gp_spec, gs_spec, pl.BlockSpec((d_in, tn), lambda j, i: (0, j))],
        out_specs=pl.BlockSpec((tm, tn), lambda j, i: (i, j)),
        out_shape=jax.ShapeDtypeStruct((t_p + t_s, d_out), F32),
        scratch_shapes=[pltpu.VMEM((d_in, tn), BF16)],
        compiler_params=_cp(("arbitrary", "arbitrary")),
        name="out_proj",
    )(ro_p, ro_s, go_p, go_s, w)


def _ple_kernel(a_ref, w_ref, p_ref, wp_ref, r_ref, o_ref, wb):
    @pl.when(pl.program_id(1) == 0)
    def _():
        wb[...] = w_ref[...].astype(BF16)

    gate = _sigmoid(_dot(a_ref[...], wb[...]))
    emb = _dot(p_ref[...].astype(BF16), wp_ref[...].astype(BF16))
    o_ref[...] = r_ref[...] + gate * emb


def _ple(xb, x, p, w_pg, w_pp, tm, tn):
    t, d = x.shape
    kp = p.shape[1]
    o_spec = pl.BlockSpec((tm, tn), lambda j, i: (i, j))
    return pl.pallas_call(
        _ple_kernel,
        grid=(d // tn, t // tm),
        in_specs=[pl.BlockSpec((tm, d), lambda j, i: (i, 0)), pl.BlockSpec((d, tn), lambda j, i: (0, j)),
                  pl.BlockSpec((tm, kp), lambda j, i: (i, 0)), pl.BlockSpec((kp, tn), lambda j, i: (0, j)),
                  o_spec],
        out_specs=o_spec,
        out_shape=jax.ShapeDtypeStruct((t, d), F32),
        scratch_shapes=[pltpu.VMEM((d, tn), BF16)],
        compiler_params=_cp(("arbitrary", "arbitrary")),
        name="ple",
    )(xb, w_pg, p, w_pp, x)


def _rotary(x, cos, sin):
    half = x.shape[-1] // 2
    x1, x2 = x[:, :half], x[:, half:]
    return jnp.concatenate([x1 * cos - x2 * sin, x1 * sin + x2 * cos], axis=-1)


def _group_norm_gate(o, gain, gate):
    mu = jnp.mean(o, axis=-1, keepdims=True)
    d = o - mu
    var = jnp.mean(d * d, axis=-1, keepdims=True)
    return d * lax.rsqrt(var + EPS) * gain * _silu(gate)


def _rms_gate(o, gain, gate):
    ms = jnp.mean(o * o, axis=-1, keepdims=True)
    return o * lax.rsqrt(ms + EPS) * gain * _silu(gate)


def _ret_prompt_kernel(log_g, q_ref, k_ref, v_ref, g_ref, cos_ref, sin_ref, gn_ref, o_ref, s_ref,
                       state, intra_tab, q_tab, k_tab):
    c = RET_CHUNK
    b = pl.program_id(0)
    i = pl.program_id(1)

    @pl.when((b == 0) & (i == 0))
    def _():
        row = lax.broadcasted_iota(jnp.int32, (c, RET_DK), 0).astype(F32)
        ii = lax.broadcasted_iota(jnp.int32, (c, c), 0)
        jj = lax.broadcasted_iota(jnp.int32, (c, c), 1)
        diff = (ii - jj).astype(F32)
        for h in range(RET_HEADS):
            intra_tab[h] = jnp.where(diff >= 0, jnp.exp(jnp.maximum(diff, 0.0) * log_g[h]), 0.0)
            q_tab[h] = jnp.exp((row + 1.0) * log_g[h])
            k_tab[h] = jnp.exp((c - 1.0 - row) * log_g[h])

    @pl.when(i == 0)
    def _():
        state[...] = jnp.zeros_like(state)

    cos, sin = cos_ref[...], sin_ref[...]
    for h in range(RET_HEADS):
        sl = slice(h * RET_DK, (h + 1) * RET_DK)
        q = _rotary(q_ref[:, sl], cos, sin)
        k = _rotary(k_ref[:, sl], cos, sin) * (RET_DK ** -0.5)
        v = v_ref[:, sl].astype(BF16)
        att = _dot_nt(q.astype(BF16), k.astype(BF16)) * intra_tab[h]
        s_old = state[h]
        o = _dot(att.astype(BF16), v) + _dot((q * q_tab[h]).astype(BF16), s_old.astype(BF16))
        state[h] = s_old * math.exp(c * log_g[h]) + _dot_tn((k * k_tab[h]).astype(BF16), v)
        o_ref[:, sl] = _group_norm_gate(o, gn_ref[h], g_ref[:, sl]).astype(o_ref.dtype)

    @pl.when(i == pl.num_programs(1) - 1)
    def _():
        s_ref[0] = state[...]


def _ret_prompt(proj, log_g, cos, sin, gn, batch, seq):
    c = RET_CHUNK
    nc = seq // c

    def col(group):
        return pl.BlockSpec((c, RET_W), lambda b, i: (b * nc + i, group))

    tab = pl.BlockSpec((c, RET_DK // 2), lambda b, i: (i, 0))
    return pl.pallas_call(
        functools.partial(_ret_prompt_kernel, log_g),
        grid=(batch, nc),
        in_specs=[col(0), col(1), col(2), col(3), tab, tab,
                  pl.BlockSpec((RET_HEADS, 1, RET_DV), lambda b, i: (0, 0, 0))],
        out_specs=[pl.BlockSpec((c, RET_W), lambda b, i: (b * nc + i, 0)),
                   pl.BlockSpec((1, RET_HEADS, RET_DK, RET_DV), lambda b, i: (b, 0, 0, 0))],
        out_shape=[jax.ShapeDtypeStruct((batch * seq, RET_W), BF16),
                   jax.ShapeDtypeStruct((batch, RET_HEADS, RET_DK, RET_DV), F32)],
        scratch_shapes=[pltpu.VMEM((RET_HEADS, RET_DK, RET_DV), F32),
                        pltpu.VMEM((RET_HEADS, c, c), F32),
                        pltpu.VMEM((RET_HEADS, c, RET_DK), F32),
                        pltpu.VMEM((RET_HEADS, c, RET_DK), F32)],
        compiler_params=_cp(("arbitrary", "arbitrary")),
        name="retention_prompt",
    )(proj, proj, proj, proj, cos, sin, gn.reshape(RET_HEADS, 1, RET_DV))


def _ret_sample_kernel(log_g, seq, q_ref, k_ref, v_ref, g_ref, cos_ref, sin_ref, gn_ref, s_in, o_ref, s_out):
    rows = DEC_PAIR * seq
    cos, sin = cos_ref[...], sin_ref[...]
    rid = lax.broadcasted_iota(jnp.int32, (rows, RET_DK), 0)
    pos = (rid % seq).astype(F32)
    batch_of_row = rid // seq
    ii = lax.broadcasted_iota(jnp.int32, (rows, rows), 0)
    jj = lax.broadcasted_iota(jnp.int32, (rows, rows), 1)
    visible = (ii // seq == jj // seq) & (ii >= jj)
    diff = jnp.maximum(ii - jj, 0).astype(F32)
    for h in range(RET_HEADS):
        lg = log_g[h]
        sl = slice(h * RET_DK, (h + 1) * RET_DK)
        q = _rotary(q_ref[:, sl], cos, sin)
        k = _rotary(k_ref[:, sl], cos, sin) * (RET_DK ** -0.5)
        v = v_ref[:, sl].astype(BF16)
        intra = jnp.where(visible, jnp.exp(diff * lg), 0.0)
        att = _dot_nt(q.astype(BF16), k.astype(BF16)) * intra
        o = _dot(att.astype(BF16), v)
        qd = (q * jnp.exp((pos + 1.0) * lg)).astype(BF16)
        kd = k * jnp.exp((seq - 1.0 - pos) * lg)
        for b in range(DEC_PAIR):
            mine = batch_of_row == b
            s_old = s_in[b, h]
            o = o + jnp.where(mine, _dot(qd, s_old.astype(BF16)), 0.0)
            s_out[b, h] = s_old * math.exp(seq * lg) + _dot_tn(jnp.where(mine, kd, 0.0).astype(BF16), v)
        o_ref[:, sl] = _group_norm_gate(o, gn_ref[h], g_ref[:, sl]).astype(o_ref.dtype)


def _ret_sample(proj, row0, log_g, cos, sin, gn, state, batch, seq):
    rows = DEC_PAIR * seq
    blk0 = row0 // rows

    def col(group):
        return pl.BlockSpec((rows, RET_W), lambda i: (i + blk0, group))

    tab = pl.BlockSpec((rows, RET_DK // 2), lambda i: (0, 0))
    st = pl.BlockSpec((DEC_PAIR, RET_HEADS, RET_DK, RET_DV), lambda i: (i, 0, 0, 0))
    return pl.pallas_call(
        functools.partial(_ret_sample_kernel, log_g, seq),
        grid=(batch // DEC_PAIR,),
        in_specs=[col(0), col(1), col(2), col(3), tab, tab,
                  pl.BlockSpec((RET_HEADS, 1, RET_DV), lambda i: (0, 0, 0)), st],
        out_specs=[pl.BlockSpec((rows, RET_W), lambda i: (i, 0)), st],
        out_shape=[jax.ShapeDtypeStruct((batch * seq, RET_W), BF16),
                   jax.ShapeDtypeStruct(state.shape, F32)],
        compiler_params=_cp(("parallel",)),
        name="retention_sample",
    )(proj, proj, proj, proj, cos, sin, gn.reshape(RET_HEADS, 1, RET_DV), state)


def _split3(x):
    hi = x.astype(BF16)
    r1 = x - hi.astype(F32)
    mid = r1.astype(BF16)
    lo = (r1 - mid.astype(F32)).astype(BF16)
    return hi, mid, lo


def _column_scale(row_vec, width):
    n = row_vec.shape[-1]
    t = jnp.transpose(jnp.broadcast_to(row_vec, (LANES, n)))
    return jnp.concatenate([t] * (width // LANES), axis=-1)


def _gla_prompt_kernel(q_ref, k_ref, v_ref, g_ref, la_ref, gn_ref, o_ref, s_ref, state):
    c, sub = GLA_CHUNK, GLA_SUB
    nsub = c // sub
    i = pl.program_id(1)

    @pl.when(i == 0)
    def _():
        state[...] = jnp.zeros_like(state)

    ii = lax.broadcasted_iota(jnp.int32, (c, c), 0)
    jj = lax.broadcasted_iota(jnp.int32, (c, c), 1)
    causal = ii >= jj
    tri = jnp.where(causal, 1.0, 0.0).astype(BF16)
    blk = lax.broadcasted_iota(jnp.int32, (c, GLA_DK), 0) // sub
    for h in range(GLA_HEADS):
        ks = slice(h * GLA_DK, (h + 1) * GLA_DK)
        vs = slice(h * GLA_DV, (h + 1) * GLA_DV)
        q = q_ref[:, ks]
        k = k_ref[:, ks] * (GLA_DK ** -0.5)
        v = v_ref[:, vs].astype(BF16)
        hi, mid, lo = _split3(la_ref[:, ks])
        b = _dot(tri, hi) + _dot(tri, mid) + _dot(tri, lo)
        mids = [b[s * sub + sub // 2 - 1: s * sub + sub // 2, :] for s in range(nsub)]
        ref_lvl = jnp.concatenate([jnp.broadcast_to(m, (sub, GLA_DK)) for m in mids], axis=0)
        qd = (q * jnp.exp(b - ref_lvl)).astype(BF16)
        kd = k * jnp.exp(ref_lvl - b)
        rows = []
        for s in range(nsub):
            scale = jnp.where(blk <= s, jnp.exp(jnp.minimum(mids[s] - ref_lvl, 0.0)), 0.0)
            rows.append(_dot_nt(qd[s * sub:(s + 1) * sub], (kd * scale).astype(BF16)))
        att = jnp.where(causal, jnp.concatenate(rows, axis=0), 0.0)
        s_old = state[h]
        o = _dot(att.astype(BF16), v) + _dot((q * jnp.exp(b)).astype(BF16), s_old.astype(BF16))
        b_last = b[c - 1:c, :]
        k_rem = (k * jnp.exp(b_last - b)).astype(BF16)
        state[h] = s_old * _column_scale(jnp.exp(b_last), GLA_DV) + _dot_tn(k_rem, v)
        o_ref[:, vs] = _rms_gate(o, gn_ref[h], g_ref[:, vs]).astype(o_ref.dtype)

    @pl.when(i == pl.num_programs(1) - 1)
    def _():
        s_ref[0] = state[...]


def _gla_prompt(proj, log_a, gn, batch, seq):
    c = GLA_CHUNK
    nc = seq // c

    def col(start, width):
        return pl.BlockSpec((c, width), lambda b, i: (b * nc + i, start // width))

    return pl.pallas_call(
        _gla_prompt_kernel,
        grid=(batch, nc),
        in_specs=[col(COL_GQ, GLA_KW), col(COL_GK, GLA_KW), col(COL_GV, GLA_VW), col(COL_GG, GLA_VW),
                  pl.BlockSpec((c, GLA_KW), lambda b, i: (b * nc + i, 0)),
                  pl.BlockSpec((GLA_HEADS, 1, GLA_DV), lambda b, i: (0, 0, 0))],
        out_specs=[pl.BlockSpec((c, GLA_VW), lambda b, i: (b * nc + i, 0)),
                   pl.BlockSpec((1, GLA_HEADS, GLA_DK, GLA_DV), lambda b, i: (b, 0, 0, 0))],
        out_shape=[jax.ShapeDtypeStruct((batch * seq, GLA_VW), BF16),
                   jax.ShapeDtypeStruct((batch, GLA_HEADS, GLA_DK, GLA_DV), F32)],
        scratch_shapes=[pltpu.VMEM((GLA_HEADS, GLA_DK, GLA_DV), F32)],
        compiler_params=_cp(("parallel", "arbitrary")),
        name="gla_prompt",
    )(proj, proj, proj, proj, log_a, gn.reshape(GLA_HEADS, 1, GLA_DV))


def _gla_sample_kernel(seq, q_ref, k_ref, v_ref, g_ref, la_ref, gn_ref, s_in, o_ref, s_out):
    rows = DEC_PAIR * seq
    rid = lax.broadcasted_iota(jnp.int32, (rows, GLA_DK), 0)
    pos = rid % seq
    batch_of_row = rid // seq
    ii = lax.broadcasted_iota(jnp.int32, (rows, rows), 0)
    jj = lax.broadcasted_iota(jnp.int32, (rows, rows), 1)
    visible = (ii // seq == jj // seq) & (ii >= jj)
    for h in range(GLA_HEADS):
        ks = slice(h * GLA_DK, (h + 1) * GLA_DK)
        vs = slice(h * GLA_DV, (h + 1) * GLA_DV)
        la = la_ref[:, ks]
        b = la
        for d in range(1, seq):
            b = b + jnp.where(pos >= d, pltpu.roll(la, d, axis=0), 0.0)
        q = q_ref[:, ks]
        k = k_ref[:, ks] * (GLA_DK ** -0.5)
        v = v_ref[:, vs].astype(BF16)
        qb = (q * jnp.exp(b)).astype(BF16)
        kb = (k * jnp.exp(-b)).astype(BF16)
        att = jnp.where(visible, _dot_nt(qb, kb), 0.0)
        o = _dot(att.astype(BF16), v)
        for bi in range(DEC_PAIR):
            mine = batch_of_row == bi
            last = bi * seq + seq - 1
            b_last = b[last:last + 1, :]
            s_old = s_in[bi, h]
            o = o + jnp.where(mine[:, :1], _dot(qb, s_old.astype(BF16)), 0.0)
            k_rem = jnp.where(mine, k * jnp.exp(b_last - b), 0.0).astype(BF16)
            s_out[bi, h] = s_old * _column_scale(jnp.exp(b_last), GLA_DV) + _dot_tn(k_rem, v)
        o_ref[:, vs] = _rms_gate(o, gn_ref[h], g_ref[:, vs]).astype(o_ref.dtype)


def _gla_sample(proj, row0, log_a, gn, state, batch, seq):
    rows = DEC_PAIR * seq
    blk0 = row0 // rows
    st = pl.BlockSpec((DEC_PAIR, GLA_HEADS, GLA_DK, GLA_DV), lambda i: (i, 0, 0, 0))
    return pl.pallas_call(
        functools.partial(_gla_sample_kernel, seq),
        grid=(batch // DEC_PAIR,),
        in_specs=[pl.BlockSpec((rows, GLA_KW), lambda i: (i + blk0, COL_GQ // GLA_KW)),
                  pl.BlockSpec((rows, GLA_KW), lambda i: (i + blk0, COL_GK // GLA_KW)),
                  pl.BlockSpec((rows, GLA_VW), lambda i: (i + blk0, COL_GV // GLA_VW)),
                  pl.BlockSpec((rows, GLA_VW), lambda i: (i + blk0, COL_GG // GLA_VW)),
                  pl.BlockSpec((rows, GLA_KW), lambda i: (i + blk0, 0)),
                  pl.BlockSpec((GLA_HEADS, 1, GLA_DV), lambda i: (0, 0, 0)), st],
        out_specs=[pl.BlockSpec((rows, GLA_VW), lambda i: (i, 0)), st],
        out_shape=[jax.ShapeDtypeStruct((batch * seq, GLA_VW), BF16),
                   jax.ShapeDtypeStruct(state.shape, F32)],
        compiler_params=_cp(("parallel",)),
        name="gla_sample",
    )(proj, proj, proj, proj, log_a, gn.reshape(GLA_HEADS, 1, GLA_DV), state)


def _router_kernel(n_p_tiles, m_ref, xp_ref, xs_ref, g_ref, wr_ref, br_ref, x_ref, hm_ref, idx_ref, gate_ref):
    tr = m_ref.shape[0]
    i = pl.program_id(0)

    @pl.when(i < n_p_tiles)
    def _():
        x_ref[...] = xp_ref[...] + m_ref[...]

    @pl.when(i >= n_p_tiles)
    def _():
        x_ref[...] = xs_ref[...] + m_ref[...]

    hm = _rms_norm(x_ref[...], g_ref[...])
    hm_ref[...] = hm
    h1, h2, _ = _split3(hm)
    w = wr_ref[...]
    w1 = w.astype(BF16)
    w2 = (w - w1.astype(F32)).astype(BF16)
    logits = _dot(h1, w1) + _dot(h1, w2) + _dot(h2, w1) + br_ref[...]
    lane = lax.broadcasted_iota(jnp.int32, (tr, LANES), 1).astype(F32)
    neg, far = -1e30, 1e9
    is_group = lane < N_GROUPS
    gl = jnp.where(is_group, logits, neg)
    gmax = jnp.max(gl, axis=-1, keepdims=True)
    gidx = jnp.min(jnp.where(gl == gmax, lane, far), axis=-1, keepdims=True)
    gsum = jnp.sum(jnp.where(is_group, jnp.exp(gl - gmax), 0.0), axis=-1, keepdims=True)
    g_p = 1.0 / gsum
    lo = N_GROUPS + EXPERTS_PER_GROUP * gidx
    in_sel = (lane >= lo) & (lane < lo + EXPERTS_PER_GROUP)
    el = jnp.where(in_sel, logits, neg)
    emax = jnp.max(el, axis=-1, keepdims=True)
    e1 = jnp.min(jnp.where(el == emax, lane, far), axis=-1, keepdims=True)
    esum = jnp.sum(jnp.where(in_sel, jnp.exp(el - emax), 0.0), axis=-1, keepdims=True)
    el2 = jnp.where(lane == e1, neg, el)
    m2 = jnp.max(el2, axis=-1, keepdims=True)
    e2 = jnp.min(jnp.where(el2 == m2, lane, far), axis=-1, keepdims=True)
    p1 = 1.0 / esum
    p2 = jnp.exp(m2 - emax) / esum
    den = p1 + p2
    idx_ref[...] = jnp.where(lane == 0, e1 - N_GROUPS, jnp.where(lane == 1, e2 - N_GROUPS, 0.0)).astype(jnp.int32)
    gate_ref[...] = jnp.where(lane == 0, g_p * p1 / den, jnp.where(lane == 1, g_p * p2 / den, 0.0))


def _router(m, x_p, x_s, g, w_r, b_r):
    t, d = m.shape
    t_p = x_p.shape[0]
    tr = ROW_TILE
    p_spec, s_spec = _two_group_specs((d,), tr, t_p // tr)
    row = pl.BlockSpec((tr, d), lambda i: (i, 0))
    lane_row = pl.BlockSpec((tr, LANES), lambda i: (i, 0))
    return pl.pallas_call(
        functools.partial(_router_kernel, t_p // tr),
        grid=(t // tr,),
        in_specs=[row, p_spec, s_spec, pl.BlockSpec((1, d), lambda i: (0, 0)),
                  pl.BlockSpec((d, LANES), lambda i: (0, 0)), pl.BlockSpec((1, LANES), lambda i: (0, 0))],
        out_specs=[row, row, lane_row, lane_row],
        out_shape=[jax.ShapeDtypeStruct((t, d), F32), jax.ShapeDtypeStruct((t, d), F32),
                   jax.ShapeDtypeStruct((t, LANES), jnp.int32), jax.ShapeDtypeStruct((t, LANES), F32)],
        compiler_params=_cp(("arbitrary",)),
        name="moe_router",
    )(m, x_p, x_s, g.reshape(1, d), w_r, b_r)


def _moe_kernel(n_used, item_e, item_start, item_n, tok, dst,
                hm_hbm, wg_ref, wu_ref, wd_ref, y_hbm,
                rows_in, rows_out, hid, sem_in, sem_out):
    del n_used, item_e
    i = pl.program_id(0)
    j = pl.program_id(1)
    n_items = pl.num_programs(0)
    n_steps = pl.num_programs(1)
    n_up = D_EXPERT // MOE_TJ
    n_dn = D_MODEL // MOE_TN
    n, s0 = item_n[i], item_start[i]
    nxt = jnp.minimum(i + 1, n_items - 1)
    n_next, s_next = jnp.where(i + 1 < n_items, item_n[nxt], 0), item_start[nxt]
    prv = jnp.maximum(i - 1, 0)
    n_prev, s_prev = jnp.where(i > 0, item_n[prv], 0), item_start[prv]
    nblk = (n + MOE_BLK - 1) // MOE_BLK

    def gather(s, r):
        return pltpu.make_async_copy(hm_hbm.at[pl.ds(tok[s + r], 1), :], rows_in.at[pl.ds(r, 1), :], sem_in)

    def scatter(s, r):
        return pltpu.make_async_copy(rows_out.at[pl.ds(r, 1), :], y_hbm.at[pl.ds(dst[s + r], 1), :], sem_out)

    def gathered_group(c):
        rows = pl.ds(pl.multiple_of(c * MOE_DMA_UNROLL, MOE_DMA_UNROLL), MOE_DMA_UNROLL)
        return pltpu.make_async_copy(hm_hbm.at[pl.ds(0, MOE_DMA_UNROLL), :], rows_in.at[rows, :], sem_in)

    def scattered_group(c):
        rows = pl.ds(pl.multiple_of(c * MOE_DMA_UNROLL, MOE_DMA_UNROLL), MOE_DMA_UNROLL)
        return pltpu.make_async_copy(rows_out.at[rows, :], y_hbm.at[pl.ds(0, MOE_DMA_UNROLL), :], sem_out)

    def for_rows(count, fn, group_fn=None):
        groups = count // MOE_DMA_UNROLL

        def group(c, carry):
            if group_fn is None:
                for u in range(MOE_DMA_UNROLL):
                    fn(c * MOE_DMA_UNROLL + u)
            else:
                group_fn(c)
            return carry
        lax.fori_loop(0, groups, group, 0)

        def single(r, carry):
            fn(r)
            return carry
        lax.fori_loop(groups * MOE_DMA_UNROLL, count, single, 0)

    def for_row_count(fn):
        for k in range(1, MOE_ROWS // MOE_BLK + 1):
            @pl.when(nblk == k)
            def _(k=k):
                fn(k * MOE_BLK)

    @pl.when((i == 0) & (j == 0))
    def _():
        rows_in[...] = jnp.zeros_like(rows_in)
        for_rows(n, lambda r: gather(s0, r).start())

    @pl.when(j == 0)
    def _():
        for_rows(n, lambda r: gather(s0, r).wait(), lambda c: gathered_group(c).wait())

    @pl.when(j < n_up)
    def _():
        def up(m):
            x = rows_in[0:m, :].astype(BF16)
            act = _silu(_dot(x, wg_ref[0].astype(BF16))) * _dot(x, wu_ref[0].astype(BF16))
            hid[j, 0:m, :] = act.astype(BF16)
        for_row_count(up)

    @pl.when(j == n_up)
    def _():
        for_rows(n_next, lambda r: gather(s_next, r).start())
        for_rows(n_prev, lambda r: scatter(s_prev, r).wait(), lambda c: scattered_group(c).wait())

    @pl.when(j >= n_up)
    def _():
        col = pl.multiple_of((j - n_up) * MOE_TN, MOE_TN)

        def down(m):
            act = jnp.concatenate([hid[u, 0:m, :] for u in range(n_up)], axis=-1)
            rows_out[0:m, pl.ds(col, MOE_TN)] = _dot(act, wd_ref[0].astype(BF16))
        for_row_count(down)

    @pl.when(j == n_steps - 1)
    def _():
        for_rows(n, lambda r: scatter(s0, r).start())

    @pl.when((j == n_steps - 1) & (i == n_items - 1))
    def _():
        for_rows(n, lambda r: scatter(s0, r).wait(), lambda c: scattered_group(c).wait())


def _moe_experts(hm, w_gate, w_up, w_down, items, tok, dst, n_assign):
    n_used, item_e, item_start, item_n = items
    n_up = D_EXPERT // MOE_TJ
    n_dn = D_MODEL // MOE_TN

    def up_map(i, j, nu, e, st, n, tok, dst):
        return (e[i], 0, jnp.minimum(j, n_up - 1))

    def dn_map(i, j, nu, e, st, n, tok, dst):
        return (e[i], 0, jnp.maximum(j - n_up, 0))

    return pl.pallas_call(
        _moe_kernel,
        grid_spec=pltpu.PrefetchScalarGridSpec(
            num_scalar_prefetch=6,
            grid=(n_used[0], n_up + n_dn),
            in_specs=[pl.BlockSpec(memory_space=pl.ANY),
                      pl.BlockSpec((1, D_MODEL, MOE_TJ), up_map),
                      pl.BlockSpec((1, D_MODEL, MOE_TJ), up_map),
                      pl.BlockSpec((1, D_EXPERT, MOE_TN), dn_map)],
            out_specs=pl.BlockSpec(memory_space=pl.ANY),
            scratch_shapes=[pltpu.VMEM((MOE_ROWS, D_MODEL), F32),
                            pltpu.VMEM((MOE_ROWS, D_MODEL), F32),
                            pltpu.VMEM((n_up, MOE_ROWS, MOE_TJ), BF16),
                            pltpu.SemaphoreType.DMA(()),
                            pltpu.SemaphoreType.DMA(())],
        ),
        out_shape=jax.ShapeDtypeStruct((n_assign, D_MODEL), F32),
        compiler_params=_cp(("arbitrary", "arbitrary")),
        name="moe_experts",
    )(n_used, item_e, item_start, item_n, tok, dst, hm, w_gate, w_up, w_down)


def _moe_plan(expert_idx):
    n_assign = expert_idx.size
    flat_e = expert_idx.reshape(n_assign)
    order = jnp.argsort(flat_e).astype(jnp.int32)
    counts = jnp.bincount(flat_e, length=N_EXPERTS).astype(jnp.int32)
    start = jnp.cumsum(counts) - counts
    per_e = (counts + MOE_ROWS - 1) // MOE_ROWS
    item_end = jnp.cumsum(per_e)
    n_items = n_assign // MOE_ROWS + N_EXPERTS
    ids = jnp.arange(n_items, dtype=jnp.int32)
    used = ids < item_end[-1]
    last = jnp.maximum(item_end[-1] - 1, 0)
    e_of = jnp.minimum(jnp.searchsorted(item_end, jnp.minimum(ids, last), side="right"), N_EXPERTS - 1).astype(jnp.int32)
    local = jnp.minimum(ids, last) - (item_end - per_e)[e_of]
    item_start = start[e_of] + local * MOE_ROWS
    item_n = jnp.where(used, jnp.clip(counts[e_of] - local * MOE_ROWS, 0, MOE_ROWS), 0)
    items = (item_end[-1:].astype(jnp.int32), e_of, item_start.astype(jnp.int32), item_n.astype(jnp.int32))
    token = order // TOP_K
    dst_row = (order % TOP_K) * (n_assign // TOP_K) + token
    return items, token, dst_row


def _combine_kernel(x_ref, y0_ref, y1_ref, gate_ref, o_ref, ob_ref):
    x = x_ref[...] + (y0_ref[...] * gate_ref[:, 0:1] + y1_ref[...] * gate_ref[:, 1:2])
    o_ref[...] = x
    ob_ref[...] = x.astype(BF16)


def _combine(x, y, gate):
    t, d = x.shape
    tr = ROW_TILE
    row = pl.BlockSpec((tr, d), lambda i: (i, 0))
    return pl.pallas_call(
        _combine_kernel,
        grid=(t // tr,),
        in_specs=[row, row, pl.BlockSpec((tr, d), lambda i: (i + t // tr, 0)),
                  pl.BlockSpec((tr, LANES), lambda i: (i, 0))],
        out_specs=[row, row],
        out_shape=[jax.ShapeDtypeStruct((t, d), F32), jax.ShapeDtypeStruct((t, d), BF16)],
        compiler_params=_cp(("parallel",)),
        name="moe_combine",
    )(x, y, y, gate)


def _rope_tables(pos):
    half = RET_DK // 2
    inv = ROPE_BASE ** (-jnp.arange(half, dtype=F32) / half)
    ang = pos.astype(F32)[:, None] * inv[None, :]
    return jnp.cos(ang), jnp.sin(ang)


def kernel(x_prompt, x_sample, state_ret, state_gla, p_prompt, p_sample, g_mix, w_in, w_gla_up, b_gla,
           ret_norm_g, gla_norm_g, w_out, g_moe, w_rg, b_rg, w_re, b_re, w_gate, w_up, w_down, w_pg, w_pp,
           g_final):
    n_p, l_p, d = x_prompt.shape
    n_s, l_s, _ = x_sample.shape
    depth = g_mix.shape[0]
    t_p, t_s = n_p * l_p, n_s * l_s
    t = t_p + t_s

    log_g_py = [math.log1p(-(2.0 ** (-5.0 - h))) for h in range(RET_HEADS)]
    cos_p, sin_p = _rope_tables(jnp.arange(l_p, dtype=jnp.int32))
    cos_s, sin_s = _rope_tables(PAST_LEN + jnp.arange(l_s, dtype=jnp.int32))
    cos_s, sin_s = jnp.tile(cos_s, (DEC_PAIR, 1)), jnp.tile(sin_s, (DEC_PAIR, 1))

    x_p = x_prompt.reshape(t_p, d)
    x_s = x_sample.reshape(t_s, d)
    x = None
    ret_p, ret_s, gla_p, gla_s = [], [], [], []
    for l in range(depth):
        w_in_t = jnp.swapaxes(w_in[l], 0, 1)
        w_ga = jnp.pad(w_in_t[N_MAIN:], ((0, LANES - GLA_RANK), (0, 0))).astype(BF16)
        w_gup = jnp.pad(w_gla_up[l], ((0, LANES - GLA_RANK), (0, 0))).astype(BF16)
        w_r = jnp.pad(jnp.concatenate([w_rg[l], w_re[l]], axis=1),
                      ((0, 0), (0, LANES - N_GROUPS - N_EXPERTS)))
        b_r = jnp.pad(jnp.concatenate([b_rg[l], b_re[l]]), (0, LANES - N_GROUPS - N_EXPERTS)).reshape(1, LANES)
        p = jnp.concatenate([p_prompt[l].reshape(t_p, -1), p_sample[l].reshape(t_s, -1)], axis=0)
        if x is not None:
            x_p, x_s = x[:t_p], x[t_p:]

        h, log_a = _norm_in(x_p, x_s, g_mix[l], w_ga, w_gup, b_gla[l])
        proj = _in_proj(h, w_in_t, N_MAIN, t // 8, 512)
        ro_p, sr_p = _ret_prompt(proj, log_g_py, cos_p, sin_p, ret_norm_g[l], n_p, l_p)
        go_p, sg_p = _gla_prompt(proj, log_a, gla_norm_g[l], n_p, l_p)
        ro_s, sr_s = _ret_sample(proj, t_p, log_g_py, cos_s, sin_s, ret_norm_g[l], state_ret[l], n_s, l_s)
        go_s, sg_s = _gla_sample(proj, t_p, log_a, gla_norm_g[l], state_gla[l], n_s, l_s)
        ret_p.append(sr_p)
        ret_s.append(sr_s)
        gla_p.append(sg_p)
        gla_s.append(sg_s)
        mix = _out_proj(ro_p, ro_s, go_p, go_s, w_out[l], 512)

        x, hm, idx, gate = _router(mix, x_p, x_s, g_moe[l], w_r, b_r)
        items, tok, dst = _moe_plan(idx[:, :TOP_K])
        y = _moe_experts(hm, w_gate[l], w_up[l], w_down[l], items, tok, dst, t * TOP_K)
        x, xb = _combine(x, y, gate)

        x = _ple(xb, x, p, w_pg[l], w_pp[l], t // 8, 512)

    y_prompt = _norm_out(x, g_final, 0, t_p).reshape(n_p, l_p, d)
    y_sample = _norm_out(x, g_final, t_p, t_s).reshape(n_s, l_s, d)
    return (y_prompt, y_sample,
            jnp.stack(ret_p).astype(state_ret.dtype), jnp.stack(ret_s).astype(state_ret.dtype),
            jnp.stack(gla_p).astype(state_gla.dtype), jnp.stack(gla_s).astype(state_gla.dtype))
```

```python
import functools
import math

import jax
import jax.numpy as jnp
from jax import lax
from jax.experimental import pallas as pl
from jax.experimental.pallas import tpu as pltpu

F32 = jnp.float32
BF16 = jnp.bfloat16

D_MODEL = 4096
RET_HEADS = 8
RET_DK = 256
RET_DV = 256
GLA_HEADS = 4
GLA_DK = 256
GLA_DV = 512
GLA_RANK = 16
GLA_GATE_TEMP = 16.0
ROPE_BASE = 10000.0
PAST_LEN = 16384
N_GROUPS = 4
EXPERTS_PER_GROUP = 8
N_EXPERTS = N_GROUPS * EXPERTS_PER_GROUP
TOP_K = 2
D_EXPERT = D_MODEL // 4
EPS = 1e-6

RET_W = RET_HEADS * RET_DK
GLA_KW = GLA_HEADS * GLA_DK
GLA_VW = GLA_HEADS * GLA_DV
N_MAIN = 4 * RET_W + 2 * GLA_KW + 2 * GLA_VW
COL_GQ = 4 * RET_W
COL_GK = COL_GQ + GLA_KW
COL_GV = COL_GK + GLA_KW
COL_GG = COL_GV + GLA_VW

LANES = 128
ROW_TILE = 256
LHS_TILE = 512
RET_CHUNK = 256
GLA_CHUNK = 128
GLA_SUB = 32
DEC_PAIR = 4
MOE_ROWS = 768
MOE_BLK = 128
MOE_TJ = 256
MOE_TN = 1024
MOE_DMA_UNROLL = 8
VMEM_LIMIT = 56 * 1024 * 1024


def _cp(semantics, vmem=VMEM_LIMIT):
    return pltpu.CompilerParams(dimension_semantics=semantics, vmem_limit_bytes=vmem)


def _sigmoid(x):
    return 1.0 / (1.0 + jnp.exp(-x))


def _silu(x):
    return x * _sigmoid(x)


def _dot(a, b):
    return jnp.dot(a, b, preferred_element_type=F32)


def _dot_nt(a, b):
    return lax.dot_general(a, b, (((1,), (1,)), ((), ())), preferred_element_type=F32)


def _dot_tn(a, b):
    return lax.dot_general(a, b, (((0,), (0,)), ((), ())), preferred_element_type=F32)


def _rms_norm(x, g):
    ms = jnp.mean(x * x, axis=-1, keepdims=True)
    return x * lax.rsqrt(ms + EPS) * g


def _two_group_specs(shape_tail, tile, n_p_tiles):
    zeros = (0,) * len(shape_tail)

    def p_map(*idx):
        return (jnp.minimum(idx[-1], n_p_tiles - 1),) + zeros

    def s_map(*idx):
        return (jnp.maximum(idx[-1] - n_p_tiles, 0),) + zeros

    return pl.BlockSpec((tile,) + shape_tail, p_map), pl.BlockSpec((tile,) + shape_tail, s_map)


def _log_sigmoid(z):
    return jnp.minimum(z, 0.0) - jnp.log(1.0 + jnp.exp(-jnp.abs(z)))


def _norm_in_kernel(n_p_tiles, xp_ref, xs_ref, g_ref, wga_ref, wup_ref, b_ref, h_ref, la_ref):
    i = pl.program_id(0)

    def emit(x):
        h = _rms_norm(x, g_ref[...]).astype(BF16)
        h_ref[...] = h
        ga = _dot_nt(h, wga_ref[...])
        z = _dot(ga.astype(BF16), wup_ref[...]) + b_ref[...]
        la_ref[...] = _log_sigmoid(z) * (1.0 / GLA_GATE_TEMP)

    @pl.when(i < n_p_tiles)
    def _():
        emit(xp_ref[...])

    @pl.when(i >= n_p_tiles)
    def _():
        emit(xs_ref[...])


def _norm_in(x_p, x_s, g, w_ga, w_up, b):
    (t_p, d), t_s = x_p.shape, x_s.shape[0]
    tr = ROW_TILE
    p_spec, s_spec = _two_group_specs((d,), tr, t_p // tr)
    return pl.pallas_call(
        functools.partial(_norm_in_kernel, t_p // tr),
        grid=((t_p + t_s) // tr,),
        in_specs=[p_spec, s_spec, pl.BlockSpec((1, d), lambda i: (0, 0)),
                  pl.BlockSpec((LANES, d), lambda i: (0, 0)),
                  pl.BlockSpec((LANES, GLA_KW), lambda i: (0, 0)),
                  pl.BlockSpec((1, GLA_KW), lambda i: (0, 0))],
        out_specs=[pl.BlockSpec((tr, d), lambda i: (i, 0)), pl.BlockSpec((tr, GLA_KW), lambda i: (i, 0))],
        out_shape=[jax.ShapeDtypeStruct((t_p + t_s, d), BF16), jax.ShapeDtypeStruct((t_p + t_s, GLA_KW), F32)],
        compiler_params=_cp(("arbitrary",)),
        name="norm_in",
    )(x_p, x_s, g.reshape(1, d), w_ga, w_up, b.reshape(1, GLA_KW))


def _norm_out_kernel(x_ref, g_ref, o_ref):
    o_ref[...] = _rms_norm(x_ref[...], g_ref[...])


def _norm_out(x, g, row0, n_rows):
    d = x.shape[1]
    tr = ROW_TILE
    blk0 = row0 // tr
    return pl.pallas_call(
        _norm_out_kernel,
        grid=(n_rows // tr,),
        in_specs=[pl.BlockSpec((tr, d), lambda i: (i + blk0, 0)), pl.BlockSpec((1, d), lambda i: (0, 0))],
        out_specs=pl.BlockSpec((tr, d), lambda i: (i, 0)),
        out_shape=jax.ShapeDtypeStruct((n_rows, d), F32),
        compiler_params=_cp(("parallel",)),
        name="norm_out",
    )(x, g.reshape(1, d))


def _in_proj_kernel(a_ref, wt_ref, o_ref):
    o_ref[...] = _dot_nt(a_ref[...], wt_ref[...].astype(BF16))


def _in_proj(h, w_t, n_cols, tm, tn):
    t, d = h.shape
    return pl.pallas_call(
        _in_proj_kernel,
        grid=(n_cols // tn, t // tm),
        in_specs=[pl.BlockSpec((tm, d), lambda j, i: (i, 0)), pl.BlockSpec((tn, d), lambda j, i: (j, 0))],
        out_specs=pl.BlockSpec((tm, tn), lambda j, i: (i, j)),
        out_shape=jax.ShapeDtypeStruct((t, n_cols), F32),
        compiler_params=_cp(("parallel", "parallel")),
        name="in_proj",
    )(h, w_t)


def _out_proj_kernel(n_p_tiles, rp_ref, rs_ref, gp_ref, gs_ref, w_ref, o_ref):
    i = pl.program_id(1)

    def emit(r_ref, g_ref):
        o_ref[...] = (_dot(r_ref[...], w_ref[:RET_W, :].astype(BF16))
                      + _dot(g_ref[...], w_ref[RET_W:, :].astype(BF16)))

    @pl.when(i < n_p_tiles)
    def _():
        emit(rp_ref, gp_ref)

    @pl.when(i >= n_p_tiles)
    def _():
        emit(rs_ref, gs_ref)


def _out_proj(ro_p, ro_s, go_p, go_s, w, tn):
    t_p, t_s = ro_p.shape[0], ro_s.shape[0]
    d_in, d_out = w.shape
    tm = LHS_TILE
    rp_spec, rs_spec = _two_group_specs((RET_W,), tm, t_p // tm)
    gp_spec, gs_spec = _two_group_specs((GLA_VW,), tm, t_p // tm)
    return pl.pallas_call(
        functools.partial(_out_proj_kernel, t_p // tm),
        grid=(d_out // tn, (t_p + t_s) // tm),
        in_specs=[rp_spec, rs_spec, gp_spec, gs_spec, pl.BlockSpec((d_in, tn), lambda j, i: (0, j))],
        out_specs=pl.BlockSpec((tm, tn), lambda j, i: (i, j)),
        out_shape=jax.ShapeDtypeStruct((t_p + t_s, d_out), F32),
        compiler_params=_cp(("arbitrary", "arbitrary")),
        name="out_proj",
    )(ro_p, ro_s, go_p, go_s, w)


def _ple_kernel(a_ref, w_ref, p_ref, wp_ref, r_ref, o_ref):
    gate = _sigmoid(_dot(a_ref[...], w_ref[...].astype(BF16)))
    emb = _dot(p_ref[...].astype(BF16), wp_ref[...].astype(BF16))
    o_ref[...] = r_ref[...] + gate * emb


def _ple(xb, x, p, w_pg, w_pp, tm, tn):
    t, d = x.shape
    kp = p.shape[1]
    o_spec = pl.BlockSpec((tm, tn), lambda j, i: (i, j))
    return pl.pallas_call(
        _ple_kernel,
        grid=(d // tn, t // tm),
        in_specs=[pl.BlockSpec((tm, d), lambda j, i: (i, 0)), pl.BlockSpec((d, tn), lambda j, i: (0, j)),
                  pl.BlockSpec((tm, kp), lambda j, i: (i, 0)), pl.BlockSpec((kp, tn), lambda j, i: (0, j)),
                  o_spec],
        out_specs=o_spec,
        out_shape=jax.ShapeDtypeStruct((t, d), F32),
        compiler_params=_cp(("parallel", "parallel")),
        name="ple",
    )(xb, w_pg, p, w_pp, x)


def _rotary(x, cos, sin):
    half = x.shape[-1] // 2
    x1, x2 = x[:, :half], x[:, half:]
    return jnp.concatenate([x1 * cos - x2 * sin, x1 * sin + x2 * cos], axis=-1)


def _group_norm_gate(o, gain, gate):
    mu = jnp.mean(o, axis=-1, keepdims=True)
    d = o - mu
    var = jnp.mean(d * d, axis=-1, keepdims=True)
    return d * lax.rsqrt(var + EPS) * gain * _silu(gate)


def _rms_gate(o, gain, gate):
    ms = jnp.mean(o * o, axis=-1, keepdims=True)
    return o * lax.rsqrt(ms + EPS) * gain * _silu(gate)


def _ret_prompt_kernel(log_g, q_ref, k_ref, v_ref, g_ref, cos_ref, sin_ref, gn_ref, o_ref, s_ref,
                       state, intra_tab, q_tab, k_tab):
    c = RET_CHUNK
    b = pl.program_id(0)
    i = pl.program_id(1)

    @pl.when((b == 0) & (i == 0))
    def _():
        row = lax.broadcasted_iota(jnp.int32, (c, RET_DK), 0).astype(F32)
        ii = lax.broadcasted_iota(jnp.int32, (c, c), 0)
        jj = lax.broadcasted_iota(jnp.int32, (c, c), 1)
        diff = (ii - jj).astype(F32)
        for h in range(RET_HEADS):
            intra_tab[h] = jnp.where(diff >= 0, jnp.exp(jnp.maximum(diff, 0.0) * log_g[h]), 0.0)
            q_tab[h] = jnp.exp((row + 1.0) * log_g[h])
            k_tab[h] = jnp.exp((c - 1.0 - row) * log_g[h])

    @pl.when(i == 0)
    def _():
        state[...] = jnp.zeros_like(state)

    cos, sin = cos_ref[...], sin_ref[...]
    for h in range(RET_HEADS):
        sl = slice(h * RET_DK, (h + 1) * RET_DK)
        q = _rotary(q_ref[:, sl], cos, sin)
        k = _rotary(k_ref[:, sl], cos, sin) * (RET_DK ** -0.5)
        v = v_ref[:, sl].astype(BF16)
        att = _dot_nt(q.astype(BF16), k.astype(BF16)) * intra_tab[h]
        s_old = state[h]
        o = _dot(att.astype(BF16), v) + _dot((q * q_tab[h]).astype(BF16), s_old.astype(BF16))
        state[h] = s_old * math.exp(c * log_g[h]) + _dot_tn((k * k_tab[h]).astype(BF16), v)
        o_ref[:, sl] = _group_norm_gate(o, gn_ref[h], g_ref[:, sl]).astype(o_ref.dtype)

    @pl.when(i == pl.num_programs(1) - 1)
    def _():
        s_ref[0] = state[...]


def _ret_prompt(proj, log_g, cos, sin, gn, batch, seq):
    c = RET_CHUNK
    nc = seq // c

    def col(group):
        return pl.BlockSpec((c, RET_W), lambda b, i: (b * nc + i, group))

    tab = pl.BlockSpec((c, RET_DK // 2), lambda b, i: (i, 0))
    return pl.pallas_call(
        functools.partial(_ret_prompt_kernel, log_g),
        grid=(batch, nc),
        in_specs=[col(0), col(1), col(2), col(3), tab, tab,
                  pl.BlockSpec((RET_HEADS, 1, RET_DV), lambda b, i: (0, 0, 0))],
        out_specs=[pl.BlockSpec((c, RET_W), lambda b, i: (b * nc + i, 0)),
                   pl.BlockSpec((1, RET_HEADS, RET_DK, RET_DV), lambda b, i: (b, 0, 0, 0))],
        out_shape=[jax.ShapeDtypeStruct((batch * seq, RET_W), BF16),
                   jax.ShapeDtypeStruct((batch, RET_HEADS, RET_DK, RET_DV), F32)],
        scratch_shapes=[pltpu.VMEM((RET_HEADS, RET_DK, RET_DV), F32),
                        pltpu.VMEM((RET_HEADS, c, c), F32),
                        pltpu.VMEM((RET_HEADS, c, RET_DK), F32),
                        pltpu.VMEM((RET_HEADS, c, RET_DK), F32)],
        compiler_params=_cp(("arbitrary", "arbitrary")),
        name="retention_prompt",
    )(proj, proj, proj, proj, cos, sin, gn.reshape(RET_HEADS, 1, RET_DV))


def _ret_sample_kernel(log_g, seq, q_ref, k_ref, v_ref, g_ref, cos_ref, sin_ref, gn_ref, s_in, o_ref, s_out):
    rows = DEC_PAIR * seq
    cos, sin = cos_ref[...], sin_ref[...]
    rid = lax.broadcasted_iota(jnp.int32, (rows, RET_DK), 0)
    pos = (rid % seq).astype(F32)
    batch_of_row = rid // seq
    ii = lax.broadcasted_iota(jnp.int32, (rows, rows), 0)
    jj = lax.broadcasted_iota(jnp.int32, (rows, rows), 1)
    visible = (ii // seq == jj // seq) & (ii >= jj)
    diff = jnp.maximum(ii - jj, 0).astype(F32)
    for h in range(RET_HEADS):
        lg = log_g[h]
        sl = slice(h * RET_DK, (h + 1) * RET_DK)
        q = _rotary(q_ref[:, sl], cos, sin)
        k = _rotary(k_ref[:, sl], cos, sin) * (RET_DK ** -0.5)
        v = v_ref[:, sl].astype(BF16)
        intra = jnp.where(visible, jnp.exp(diff * lg), 0.0)
        att = _dot_nt(q.astype(BF16), k.astype(BF16)) * intra
        o = _dot(att.astype(BF16), v)
        qd = (q * jnp.exp((pos + 1.0) * lg)).astype(BF16)
        kd = k * jnp.exp((seq - 1.0 - pos) * lg)
        for b in range(DEC_PAIR):
            mine = batch_of_row == b
            s_old = s_in[b, h]
            o = o + jnp.where(mine, _dot(qd, s_old.astype(BF16)), 0.0)
            s_out[b, h] = s_old * math.exp(seq * lg) + _dot_tn(jnp.where(mine, kd, 0.0).astype(BF16), v)
        o_ref[:, sl] = _group_norm_gate(o, gn_ref[h], g_ref[:, sl]).astype(o_ref.dtype)


def _ret_sample(proj, row0, log_g, cos, sin, gn, state, batch, seq):
    rows = DEC_PAIR * seq
    blk0 = row0 // rows

    def col(group):
        return pl.BlockSpec((rows, RET_W), lambda i: (i + blk0, group))

    tab = pl.BlockSpec((rows, RET_DK // 2), lambda i: (0, 0))
    st = pl.BlockSpec((DEC_PAIR, RET_HEADS, RET_DK, RET_DV), lambda i: (i, 0, 0, 0))
    return pl.pallas_call(
        functools.partial(_ret_sample_kernel, log_g, seq),
        grid=(batch // DEC_PAIR,),
        in_specs=[col(0), col(1), col(2), col(3), tab, tab,
                  pl.BlockSpec((RET_HEADS, 1, RET_DV), lambda i: (0, 0, 0)), st],
        out_specs=[pl.BlockSpec((rows, RET_W), lambda i: (i, 0)), st],
        out_shape=[jax.ShapeDtypeStruct((batch * seq, RET_W), BF16),
                   jax.ShapeDtypeStruct(state.shape, F32)],
        compiler_params=_cp(("parallel",)),
        name="retention_sample",
    )(proj, proj, proj, proj, cos, sin, gn.reshape(RET_HEADS, 1, RET_DV), state)


def _split3(x):
    hi = x.astype(BF16)
    r1 = x - hi.astype(F32)
    mid = r1.astype(BF16)
    lo = (r1 - mid.astype(F32)).astype(BF16)
    return hi, mid, lo


def _column_scale(row_vec, width):
    n = row_vec.shape[-1]
    t = jnp.transpose(jnp.broadcast_to(row_vec, (LANES, n)))
    return jnp.concatenate([t] * (width // LANES), axis=-1)


def _gla_prompt_kernel(q_ref, k_ref, v_ref, g_ref, la_ref, gn_ref, o_ref, s_ref, state):
    c, sub = GLA_CHUNK, GLA_SUB
    nsub = c // sub
    i = pl.program_id(1)

    @pl.when(i == 0)
    def _():
        state[...] = jnp.zeros_like(state)

    ii = lax.broadcasted_iota(jnp.int32, (c, c), 0)
    jj = lax.broadcasted_iota(jnp.int32, (c, c), 1)
    causal = ii >= jj
    tri = jnp.where(causal, 1.0, 0.0).astype(BF16)
    blk = lax.broadcasted_iota(jnp.int32, (c, GLA_DK), 0) // sub
    for h in range(GLA_HEADS):
        ks = slice(h * GLA_DK, (h + 1) * GLA_DK)
        vs = slice(h * GLA_DV, (h + 1) * GLA_DV)
        q = q_ref[:, ks]
        k = k_ref[:, ks] * (GLA_DK ** -0.5)
        v = v_ref[:, vs].astype(BF16)
        hi, mid, lo = _split3(la_ref[:, ks])
        b = _dot(tri, hi) + _dot(tri, mid) + _dot(tri, lo)
        mids = [b[s * sub + sub // 2 - 1: s * sub + sub // 2, :] for s in range(nsub)]
        ref_lvl = jnp.concatenate([jnp.broadcast_to(m, (sub, GLA_DK)) for m in mids], axis=0)
        qd = (q * jnp.exp(b - ref_lvl)).astype(BF16)
        kd = k * jnp.exp(ref_lvl - b)
        rows = []
        for s in range(nsub):
            scale = jnp.where(blk <= s, jnp.exp(jnp.minimum(mids[s] - ref_lvl, 0.0)), 0.0)
            rows.append(_dot_nt(qd[s * sub:(s + 1) * sub], (kd * scale).astype(BF16)))
        att = jnp.where(causal, jnp.concatenate(rows, axis=0), 0.0)
        s_old = state[h]
        o = _dot(att.astype(BF16), v) + _dot((q * jnp.exp(b)).astype(BF16), s_old.astype(BF16))
        b_last = b[c - 1:c, :]
        k_rem = (k * jnp.exp(b_last - b)).astype(BF16)
        state[h] = s_old * _column_scale(jnp.exp(b_last), GLA_DV) + _dot_tn(k_rem, v)
        o_ref[:, vs] = _rms_gate(o, gn_ref[h], g_ref[:, vs]).astype(o_ref.dtype)

    @pl.when(i == pl.num_programs(1) - 1)
    def _():
        s_ref[0] = state[...]


def _gla_prompt(proj, log_a, gn, batch, seq):
    c = GLA_CHUNK
    nc = seq // c

    def col(start, width):
        return pl.BlockSpec((c, width), lambda b, i: (b * nc + i, start // width))

    return pl.pallas_call(
        _gla_prompt_kernel,
        grid=(batch, nc),
        in_specs=[col(COL_GQ, GLA_KW), col(COL_GK, GLA_KW), col(COL_GV, GLA_VW), col(COL_GG, GLA_VW),
                  pl.BlockSpec((c, GLA_KW), lambda b, i: (b * nc + i, 0)),
                  pl.BlockSpec((GLA_HEADS, 1, GLA_DV), lambda b, i: (0, 0, 0))],
        out_specs=[pl.BlockSpec((c, GLA_VW), lambda b, i: (b * nc + i, 0)),
                   pl.BlockSpec((1, GLA_HEADS, GLA_DK, GLA_DV), lambda b, i: (b, 0, 0, 0))],
        out_shape=[jax.ShapeDtypeStruct((batch * seq, GLA_VW), BF16),
                   jax.ShapeDtypeStruct((batch, GLA_HEADS, GLA_DK, GLA_DV), F32)],
        scratch_shapes=[pltpu.VMEM((GLA_HEADS, GLA_DK, GLA_DV), F32)],
        compiler_params=_cp(("parallel", "arbitrary")),
        name="gla_prompt",
    )(proj, proj, proj, proj, log_a, gn.reshape(GLA_HEADS, 1, GLA_DV))


def _gla_sample_kernel(seq, q_ref, k_ref, v_ref, g_ref, la_ref, gn_ref, s_in, o_ref, s_out):
    rows = DEC_PAIR * seq
    rid = lax.broadcasted_iota(jnp.int32, (rows, GLA_DK), 0)
    pos = rid % seq
    batch_of_row = rid // seq
    ii = lax.broadcasted_iota(jnp.int32, (rows, rows), 0)
    jj = lax.broadcasted_iota(jnp.int32, (rows, rows), 1)
    visible = (ii // seq == jj // seq) & (ii >= jj)
    for h in range(GLA_HEADS):
        ks = slice(h * GLA_DK, (h + 1) * GLA_DK)
        vs = slice(h * GLA_DV, (h + 1) * GLA_DV)
        la = la_ref[:, ks]
        b = la
        for d in range(1, seq):
            b = b + jnp.where(pos >= d, pltpu.roll(la, d, axis=0), 0.0)
        q = q_ref[:, ks]
        k = k_ref[:, ks] * (GLA_DK ** -0.5)
        v = v_ref[:, vs].astype(BF16)
        qb = (q * jnp.exp(b)).astype(BF16)
        kb = (k * jnp.exp(-b)).astype(BF16)
        att = jnp.where(visible, _dot_nt(qb, kb), 0.0)
        o = _dot(att.astype(BF16), v)
        for bi in range(DEC_PAIR):
            mine = batch_of_row == bi
            last = bi * seq + seq - 1
            b_last = b[last:last + 1, :]
            s_old = s_in[bi, h]
            o = o + jnp.where(mine[:, :1], _dot(qb, s_old.astype(BF16)), 0.0)
            k_rem = jnp.where(mine, k * jnp.exp(b_last - b), 0.0).astype(BF16)
            s_out[bi, h] = s_old * _column_scale(jnp.exp(b_last), GLA_DV) + _dot_tn(k_rem, v)
        o_ref[:, vs] = _rms_gate(o, gn_ref[h], g_ref[:, vs]).astype(o_ref.dtype)


def _gla_sample(proj, row0, log_a, gn, state, batch, seq):
    rows = DEC_PAIR * seq
    blk0 = row0 // rows
    st = pl.BlockSpec((DEC_PAIR, GLA_HEADS, GLA_DK, GLA_DV), lambda i: (i, 0, 0, 0))
    return pl.pallas_call(
        functools.partial(_gla_sample_kernel, seq),
        grid=(batch // DEC_PAIR,),
        in_specs=[pl.BlockSpec((rows, GLA_KW), lambda i: (i + blk0, COL_GQ // GLA_KW)),
                  pl.BlockSpec((rows, GLA_KW), lambda i: (i + blk0, COL_GK // GLA_KW)),
                  pl.BlockSpec((rows, GLA_VW), lambda i: (i + blk0, COL_GV // GLA_VW)),
                  pl.BlockSpec((rows, GLA_VW), lambda i: (i + blk0, COL_GG // GLA_VW)),
                  pl.BlockSpec((rows, GLA_KW), lambda i: (i + blk0, 0)),
                  pl.BlockSpec((GLA_HEADS, 1, GLA_DV), lambda i: (0, 0, 0)), st],
        out_specs=[pl.BlockSpec((rows, GLA_VW), lambda i: (i, 0)), st],
        out_shape=[jax.ShapeDtypeStruct((batch * seq, GLA_VW), BF16),
                   jax.ShapeDtypeStruct(state.shape, F32)],
        compiler_params=_cp(("parallel",)),
        name="gla_sample",
    )(proj, proj, proj, proj, log_a, gn.reshape(GLA_HEADS, 1, GLA_DV), state)


def _router_kernel(n_p_tiles, m_ref, xp_ref, xs_ref, g_ref, wr_ref, br_ref, x_ref, hm_ref, idx_ref, gate_ref):
    tr = m_ref.shape[0]
    i = pl.program_id(0)

    @pl.when(i < n_p_tiles)
    def _():
        x_ref[...] = xp_ref[...] + m_ref[...]

    @pl.when(i >= n_p_tiles)
    def _():
        x_ref[...] = xs_ref[...] + m_ref[...]

    hm = _rms_norm(x_ref[...], g_ref[...])
    hm_ref[...] = hm
    h1, h2, _ = _split3(hm)
    w = wr_ref[...]
    w1 = w.astype(BF16)
    w2 = (w - w1.astype(F32)).astype(BF16)
    logits = _dot(h1, w1) + _dot(h1, w2) + _dot(h2, w1) + br_ref[...]
    lane = lax.broadcasted_iota(jnp.int32, (tr, LANES), 1).astype(F32)
    neg, far = -1e30, 1e9
    is_group = lane < N_GROUPS
    gl = jnp.where(is_group, logits, neg)
    gmax = jnp.max(gl, axis=-1, keepdims=True)
    gidx = jnp.min(jnp.where(gl == gmax, lane, far), axis=-1, keepdims=True)
    gsum = jnp.sum(jnp.where(is_group, jnp.exp(gl - gmax), 0.0), axis=-1, keepdims=True)
    g_p = 1.0 / gsum
    lo = N_GROUPS + EXPERTS_PER_GROUP * gidx
    in_sel = (lane >= lo) & (lane < lo + EXPERTS_PER_GROUP)
    el = jnp.where(in_sel, logits, neg)
    emax = jnp.max(el, axis=-1, keepdims=True)
    e1 = jnp.min(jnp.where(el == emax, lane, far), axis=-1, keepdims=True)
    esum = jnp.sum(jnp.where(in_sel, jnp.exp(el - emax), 0.0), axis=-1, keepdims=True)
    el2 = jnp.where(lane == e1, neg, el)
    m2 = jnp.max(el2, axis=-1, keepdims=True)
    e2 = jnp.min(jnp.where(el2 == m2, lane, far), axis=-1, keepdims=True)
    p1 = 1.0 / esum
    p2 = jnp.exp(m2 - emax) / esum
    den = p1 + p2
    idx_ref[...] = jnp.where(lane == 0, e1 - N_GROUPS, jnp.where(lane == 1, e2 - N_GROUPS, 0.0)).astype(jnp.int32)
    gate_ref[...] = jnp.where(lane == 0, g_p * p1 / den, jnp.where(lane == 1, g_p * p2 / den, 0.0))


def _router(m, x_p, x_s, g, w_r, b_r):
    t, d = m.shape
    t_p = x_p.shape[0]
    tr = ROW_TILE
    p_spec, s_spec = _two_group_specs((d,), tr, t_p // tr)
    row = pl.BlockSpec((tr, d), lambda i: (i, 0))
    lane_row = pl.BlockSpec((tr, LANES), lambda i: (i, 0))
    return pl.pallas_call(
        functools.partial(_router_kernel, t_p // tr),
        grid=(t // tr,),
        in_specs=[row, p_spec, s_spec, pl.BlockSpec((1, d), lambda i: (0, 0)),
                  pl.BlockSpec((d, LANES), lambda i: (0, 0)), pl.BlockSpec((1, LANES), lambda i: (0, 0))],
        out_specs=[row, row, lane_row, lane_row],
        out_shape=[jax.ShapeDtypeStruct((t, d), F32), jax.ShapeDtypeStruct((t, d), F32),
                   jax.ShapeDtypeStruct((t, LANES), jnp.int32), jax.ShapeDtypeStruct((t, LANES), F32)],
        compiler_params=_cp(("arbitrary",)),
        name="moe_router",
    )(m, x_p, x_s, g.reshape(1, d), w_r, b_r)


def _moe_kernel(n_used, item_e, item_start, item_n, tok, dst,
                hm_hbm, wg_ref, wu_ref, wd_ref, y_hbm,
                rows_in, rows_out, hid, sem_in, sem_out):
    del n_used, item_e
    i = pl.program_id(0)
    j = pl.program_id(1)
    n_items = pl.num_programs(0)
    n_steps = pl.num_programs(1)
    n_up = D_EXPERT // MOE_TJ
    n_dn = D_MODEL // MOE_TN
    n, s0 = item_n[i], item_start[i]
    nxt = jnp.minimum(i + 1, n_items - 1)
    n_next, s_next = jnp.where(i + 1 < n_items, item_n[nxt], 0), item_start[nxt]
    prv = jnp.maximum(i - 1, 0)
    n_prev, s_prev = jnp.where(i > 0, item_n[prv], 0), item_start[prv]
    nblk = (n + MOE_BLK - 1) // MOE_BLK

    def gather(s, r):
        return pltpu.make_async_copy(hm_hbm.at[pl.ds(tok[s + r], 1), :], rows_in.at[pl.ds(r, 1), :], sem_in)

    def scatter(s, r):
        return pltpu.make_async_copy(rows_out.at[pl.ds(r, 1), :], y_hbm.at[pl.ds(dst[s + r], 1), :], sem_out)

    def gathered_group(c):
        rows = pl.ds(pl.multiple_of(c * MOE_DMA_UNROLL, MOE_DMA_UNROLL), MOE_DMA_UNROLL)
        return pltpu.make_async_copy(hm_hbm.at[pl.ds(0, MOE_DMA_UNROLL), :], rows_in.at[rows, :], sem_in)

    def scattered_group(c):
        rows = pl.ds(pl.multiple_of(c * MOE_DMA_UNROLL, MOE_DMA_UNROLL), MOE_DMA_UNROLL)
        return pltpu.make_async_copy(rows_out.at[rows, :], y_hbm.at[pl.ds(0, MOE_DMA_UNROLL), :], sem_out)

    def for_rows(count, fn, group_fn=None):
        groups = count // MOE_DMA_UNROLL

        def group(c, carry):
            if group_fn is None:
                for u in range(MOE_DMA_UNROLL):
                    fn(c * MOE_DMA_UNROLL + u)
            else:
                group_fn(c)
            return carry
        lax.fori_loop(0, groups, group, 0)

        def single(r, carry):
            fn(r)
            return carry
        lax.fori_loop(groups * MOE_DMA_UNROLL, count, single, 0)

    def for_row_count(fn):
        for k in range(1, MOE_ROWS // MOE_BLK + 1):
            @pl.when(nblk == k)
            def _(k=k):
                fn(k * MOE_BLK)

    @pl.when((i == 0) & (j == 0))
    def _():
        rows_in[...] = jnp.zeros_like(rows_in)
        for_rows(n, lambda r: gather(s0, r).start())

    @pl.when(j == 0)
    def _():
        for_rows(n, lambda r: gather(s0, r).wait(), lambda c: gathered_group(c).wait())

    @pl.when(j < n_up)
    def _():
        def up(m):
            x = rows_in[0:m, :].astype(BF16)
            act = _silu(_dot(x, wg_ref[0].astype(BF16))) * _dot(x, wu_ref[0].astype(BF16))
            hid[j, 0:m, :] = act.astype(BF16)
        for_row_count(up)

    @pl.when(j == n_up)
    def _():
        for_rows(n_next, lambda r: gather(s_next, r).start())
        for_rows(n_prev, lambda r: scatter(s_prev, r).wait(), lambda c: scattered_group(c).wait())

    @pl.when(j >= n_up)
    def _():
        col = pl.multiple_of((j - n_up) * MOE_TN, MOE_TN)

        def down(m):
            act = jnp.concatenate([hid[u, 0:m, :] for u in range(n_up)], axis=-1)
            rows_out[0:m, pl.ds(col, MOE_TN)] = _dot(act, wd_ref[0].astype(BF16))
        for_row_count(down)

    @pl.when(j == n_steps - 1)
    def _():
        for_rows(n, lambda r: scatter(s0, r).start())

    @pl.when((j == n_steps - 1) & (i == n_items - 1))
    def _():
        for_rows(n, lambda r: scatter(s0, r).wait(), lambda c: scattered_group(c).wait())


def _moe_experts(hm, w_gate, w_up, w_down, items, tok, dst, n_assign):
    n_used, item_e, item_start, item_n = items
    n_up = D_EXPERT // MOE_TJ
    n_dn = D_MODEL // MOE_TN

    def up_map(i, j, nu, e, st, n, tok, dst):
        return (e[i], 0, jnp.minimum(j, n_up - 1))

    def dn_map(i, j, nu, e, st, n, tok, dst):
        return (e[i], 0, jnp.maximum(j - n_up, 0))

    return pl.pallas_call(
        _moe_kernel,
        grid_spec=pltpu.PrefetchScalarGridSpec(
            num_scalar_prefetch=6,
            grid=(n_used[0], n_up + n_dn),
            in_specs=[pl.BlockSpec(memory_space=pl.ANY),
                      pl.BlockSpec((1, D_MODEL, MOE_TJ), up_map),
                      pl.BlockSpec((1, D_MODEL, MOE_TJ), up_map),
                      pl.BlockSpec((1, D_EXPERT, MOE_TN), dn_map)],
            out_specs=pl.BlockSpec(memory_space=pl.ANY),
            scratch_shapes=[pltpu.VMEM((MOE_ROWS, D_MODEL), F32),
                            pltpu.VMEM((MOE_ROWS, D_MODEL), F32),
                            pltpu.VMEM((n_up, MOE_ROWS, MOE_TJ), BF16),
                            pltpu.SemaphoreType.DMA(()),
                            pltpu.SemaphoreType.DMA(())],
        ),
        out_shape=jax.ShapeDtypeStruct((n_assign, D_MODEL), F32),
        compiler_params=_cp(("arbitrary", "arbitrary")),
        name="moe_experts",
    )(n_used, item_e, item_start, item_n, tok, dst, hm, w_gate, w_up, w_down)


def _moe_plan(expert_idx):
    n_assign = expert_idx.size
    flat_e = expert_idx.reshape(n_assign)
    order = jnp.argsort(flat_e).astype(jnp.int32)
    counts = jnp.bincount(flat_e, length=N_EXPERTS).astype(jnp.int32)
    start = jnp.cumsum(counts) - counts
    per_e = (counts + MOE_ROWS - 1) // MOE_ROWS
    item_end = jnp.cumsum(per_e)
    n_items = n_assign // MOE_ROWS + N_EXPERTS
    ids = jnp.arange(n_items, dtype=jnp.int32)
    used = ids < item_end[-1]
    last = jnp.maximum(item_end[-1] - 1, 0)
    e_of = jnp.minimum(jnp.searchsorted(item_end, jnp.minimum(ids, last), side="right"), N_EXPERTS - 1).astype(jnp.int32)
    local = jnp.minimum(ids, last) - (item_end - per_e)[e_of]
    item_start = start[e_of] + local * MOE_ROWS
    item_n = jnp.where(used, jnp.clip(counts[e_of] - local * MOE_ROWS, 0, MOE_ROWS), 0)
    items = (item_end[-1:].astype(jnp.int32), e_of, item_start.astype(jnp.int32), item_n.astype(jnp.int32))
    token = order // TOP_K
    dst_row = (order % TOP_K) * (n_assign // TOP_K) + token
    return items, token, dst_row


def _combine_kernel(x_ref, y0_ref, y1_ref, gate_ref, o_ref, ob_ref):
    x = x_ref[...] + (y0_ref[...] * gate_ref[:, 0:1] + y1_ref[...] * gate_ref[:, 1:2])
    o_ref[...] = x
    ob_ref[...] = x.astype(BF16)


def _combine(x, y, gate):
    t, d = x.shape
    tr = ROW_TILE
    row = pl.BlockSpec((tr, d), lambda i: (i, 0))
    return pl.pallas_call(
        _combine_kernel,
        grid=(t // tr,),
        in_specs=[row, row, pl.BlockSpec((tr, d), lambda i: (i + t // tr, 0)),
                  pl.BlockSpec((tr, LANES), lambda i: (i, 0))],
        out_specs=[row, row],
        out_shape=[jax.ShapeDtypeStruct((t, d), F32), jax.ShapeDtypeStruct((t, d), BF16)],
        compiler_params=_cp(("parallel",)),
        name="moe_combine",
    )(x, y, y, gate)


def _rope_tables(pos):
    half = RET_DK // 2
    inv = ROPE_BASE ** (-jnp.arange(half, dtype=F32) / half)
    ang = pos.astype(F32)[:, None] * inv[None, :]
    return jnp.cos(ang), jnp.sin(ang)


def kernel(x_prompt, x_sample, state_ret, state_gla, p_prompt, p_sample, g_mix, w_in, w_gla_up, b_gla,
           ret_norm_g, gla_norm_g, w_out, g_moe, w_rg, b_rg, w_re, b_re, w_gate, w_up, w_down, w_pg, w_pp,
           g_final):
    n_p, l_p, d = x_prompt.shape
    n_s, l_s, _ = x_sample.shape
    depth = g_mix.shape[0]
    t_p, t_s = n_p * l_p, n_s * l_s
    t = t_p + t_s

    log_g_py = [math.log1p(-(2.0 ** (-5.0 - h))) for h in range(RET_HEADS)]
    cos_p, sin_p = _rope_tables(jnp.arange(l_p, dtype=jnp.int32))
    cos_s, sin_s = _rope_tables(PAST_LEN + jnp.arange(l_s, dtype=jnp.int32))
    cos_s, sin_s = jnp.tile(cos_s, (DEC_PAIR, 1)), jnp.tile(sin_s, (DEC_PAIR, 1))

    x_p = x_prompt.reshape(t_p, d)
    x_s = x_sample.reshape(t_s, d)
    x = None
    ret_p, ret_s, gla_p, gla_s = [], [], [], []
    for l in range(depth):
        w_in_t = jnp.swapaxes(w_in[l], 0, 1)
        w_ga = jnp.pad(w_in_t[N_MAIN:], ((0, LANES - GLA_RANK), (0, 0))).astype(BF16)
        w_gup = jnp.pad(w_gla_up[l], ((0, LANES - GLA_RANK), (0, 0))).astype(BF16)
        w_r = jnp.pad(jnp.concatenate([w_rg[l], w_re[l]], axis=1),
                      ((0, 0), (0, LANES - N_GROUPS - N_EXPERTS)))
        b_r = jnp.pad(jnp.concatenate([b_rg[l], b_re[l]]), (0, LANES - N_GROUPS - N_EXPERTS)).reshape(1, LANES)
        p = jnp.concatenate([p_prompt[l].reshape(t_p, -1), p_sample[l].reshape(t_s, -1)], axis=0)
        if x is not None:
            x_p, x_s = x[:t_p], x[t_p:]

        h, log_a = _norm_in(x_p, x_s, g_mix[l], w_ga, w_gup, b_gla[l])
        proj = _in_proj(h, w_in_t, N_MAIN, t // 16, 1024)
        ro_p, sr_p = _ret_prompt(proj, log_g_py, cos_p, sin_p, ret_norm_g[l], n_p, l_p)
        go_p, sg_p = _gla_prompt(proj, log_a, gla_norm_g[l], n_p, l_p)
        ro_s, sr_s = _ret_sample(proj, t_p, log_g_py, cos_s, sin_s, ret_norm_g[l], state_ret[l], n_s, l_s)
        go_s, sg_s = _gla_sample(proj, t_p, log_a, gla_norm_g[l], state_gla[l], n_s, l_s)
        ret_p.append(sr_p)
        ret_s.append(sr_s)
        gla_p.append(sg_p)
        gla_s.append(sg_s)
        mix = _out_proj(ro_p, ro_s, go_p, go_s, w_out[l], 1024)

        x, hm, idx, gate = _router(mix, x_p, x_s, g_moe[l], w_r, b_r)
        items, tok, dst = _moe_plan(idx[:, :TOP_K])
        y = _moe_experts(hm, w_gate[l], w_up[l], w_down[l], items, tok, dst, t * TOP_K)
        x, xb = _combine(x, y, gate)

        x = _ple(xb, x, p, w_pg[l], w_pp[l], t // 8, 512)

    y_prompt = _norm_out(x, g_final, 0, t_p).reshape(n_p, l_p, d)
    y_sample = _norm_out(x, g_final, t_p, t_s).reshape(n_s, l_s, d)
    return (y_prompt, y_sample,
            jnp.stack(ret_p).astype(state_ret.dtype), jnp.stack(ret_s).astype(state_ret.dtype),
            jnp.stack(gla_p).astype(state_gla.dtype), jnp.stack(gla_s).astype(state_gla.dtype))
```

```python
import functools
import math

import jax
import jax.numpy as jnp
from jax import lax
from jax.experimental import pallas as pl
from jax.experimental.pallas import tpu as pltpu

F32 = jnp.float32
BF16 = jnp.bfloat16

D_MODEL = 4096
RET_HEADS = 8
RET_DK = 256
RET_DV = 256
GLA_HEADS = 4
GLA_DK = 256
GLA_DV = 512
GLA_RANK = 16
GLA_GATE_TEMP = 16.0
ROPE_BASE = 10000.0
PAST_LEN = 16384
N_GROUPS = 4
EXPERTS_PER_GROUP = 8
N_EXPERTS = N_GROUPS * EXPERTS_PER_GROUP
TOP_K = 2
D_EXPERT = D_MODEL // 4
EPS = 1e-6

RET_W = RET_HEADS * RET_DK
GLA_KW = GLA_HEADS * GLA_DK
GLA_VW = GLA_HEADS * GLA_DV
N_MAIN = 4 * RET_W + 2 * GLA_KW + 2 * GLA_VW
COL_GQ = 4 * RET_W
COL_GK = COL_GQ + GLA_KW
COL_GV = COL_GK + GLA_KW
COL_GG = COL_GV + GLA_VW

LANES = 128
ROW_TILE = 256
LHS_TILE = 512
RET_CHUNK = 256
GLA_CHUNK = 128
GLA_SUB = 32
DEC_PAIR = 4
MOE_ROWS = 768
MOE_BLK = 128
MOE_TJ = 256
MOE_TN = 1024
MOE_DMA_UNROLL = 8
VMEM_LIMIT = 56 * 1024 * 1024


def _cp(semantics, vmem=VMEM_LIMIT):
    return pltpu.CompilerParams(dimension_semantics=semantics, vmem_limit_bytes=vmem)


def _sigmoid(x):
    return 1.0 / (1.0 + jnp.exp(-x))


def _silu(x):
    return x * _sigmoid(x)


def _dot(a, b):
    return jnp.dot(a, b, preferred_element_type=F32)


def _dot_nt(a, b):
    return lax.dot_general(a, b, (((1,), (1,)), ((), ())), preferred_element_type=F32)


def _dot_tn(a, b):
    return lax.dot_general(a, b, (((0,), (0,)), ((), ())), preferred_element_type=F32)


def _rms_norm(x, g):
    ms = jnp.mean(x * x, axis=-1, keepdims=True)
    return x * lax.rsqrt(ms + EPS) * g


def _two_group_specs(shape_tail, tile, n_p_tiles):
    zeros = (0,) * len(shape_tail)

    def p_map(*idx):
        return (jnp.minimum(idx[-1], n_p_tiles - 1),) + zeros

    def s_map(*idx):
        return (jnp.maximum(idx[-1] - n_p_tiles, 0),) + zeros

    return pl.BlockSpec((tile,) + shape_tail, p_map), pl.BlockSpec((tile,) + shape_tail, s_map)


def _log_sigmoid(z):
    return jnp.minimum(z, 0.0) - jnp.log(1.0 + jnp.exp(-jnp.abs(z)))


def _norm_in_kernel(n_p_tiles, xp_ref, xs_ref, g_ref, wga_ref, wup_ref, b_ref, h_ref, la_ref):
    i = pl.program_id(0)

    def emit(x):
        h = _rms_norm(x, g_ref[...]).astype(BF16)
        h_ref[...] = h
        ga = _dot_nt(h, wga_ref[...])
        z = _dot(ga.astype(BF16), wup_ref[...]) + b_ref[...]
        la_ref[...] = _log_sigmoid(z) * (1.0 / GLA_GATE_TEMP)

    @pl.when(i < n_p_tiles)
    def _():
        emit(xp_ref[...])

    @pl.when(i >= n_p_tiles)
    def _():
        emit(xs_ref[...])


def _norm_in(x_p, x_s, g, w_ga, w_up, b):
    (t_p, d), t_s = x_p.shape, x_s.shape[0]
    tr = ROW_TILE
    p_spec, s_spec = _two_group_specs((d,), tr, t_p // tr)
    return pl.pallas_call(
        functools.partial(_norm_in_kernel, t_p // tr),
        grid=((t_p + t_s) // tr,),
        in_specs=[p_spec, s_spec, pl.BlockSpec((1, d), lambda i: (0, 0)),
                  pl.BlockSpec((LANES, d), lambda i: (0, 0)),
                  pl.BlockSpec((LANES, GLA_KW), lambda i: (0, 0)),
                  pl.BlockSpec((1, GLA_KW), lambda i: (0, 0))],
        out_specs=[pl.BlockSpec((tr, d), lambda i: (i, 0)), pl.BlockSpec((tr, GLA_KW), lambda i: (i, 0))],
        out_shape=[jax.ShapeDtypeStruct((t_p + t_s, d), BF16), jax.ShapeDtypeStruct((t_p + t_s, GLA_KW), F32)],
        compiler_params=_cp(("arbitrary",)),
        name="norm_in",
    )(x_p, x_s, g.reshape(1, d), w_ga, w_up, b.reshape(1, GLA_KW))


def _norm_out_kernel(x_ref, g_ref, o_ref):
    o_ref[...] = _rms_norm(x_ref[...], g_ref[...])


def _norm_out(x, g, row0, n_rows):
    d = x.shape[1]
    tr = ROW_TILE
    blk0 = row0 // tr
    return pl.pallas_call(
        _norm_out_kernel,
        grid=(n_rows // tr,),
        in_specs=[pl.BlockSpec((tr, d), lambda i: (i + blk0, 0)), pl.BlockSpec((1, d), lambda i: (0, 0))],
        out_specs=pl.BlockSpec((tr, d), lambda i: (i, 0)),
        out_shape=jax.ShapeDtypeStruct((n_rows, d), F32),
        compiler_params=_cp(("parallel",)),
        name="norm_out",
    )(x, g.reshape(1, d))


def _in_proj_kernel(a_ref, wt_ref, o_ref):
    o_ref[...] = _dot_nt(a_ref[...], wt_ref[...].astype(BF16))


def _in_proj(h, w_t, n_cols, tm, tn):
    t, d = h.shape
    return pl.pallas_call(
        _in_proj_kernel,
        grid=(n_cols // tn, t // tm),
        in_specs=[pl.BlockSpec((tm, d), lambda j, i: (i, 0)), pl.BlockSpec((tn, d), lambda j, i: (j, 0))],
        out_specs=pl.BlockSpec((tm, tn), lambda j, i: (i, j)),
        out_shape=jax.ShapeDtypeStruct((t, n_cols), F32),
        compiler_params=_cp(("parallel", "parallel")),
        name="in_proj",
    )(h, w_t)


def _out_proj_kernel(n_p_tiles, rp_ref, rs_ref, gp_ref, gs_ref, w_ref, o_ref):
    i = pl.program_id(1)

    def emit(r_ref, g_ref):
        o_ref[...] = (_dot(r_ref[...], w_ref[:RET_W, :].astype(BF16))
                      + _dot(g_ref[...], w_ref[RET_W:, :].astype(BF16)))

    @pl.when(i < n_p_tiles)
    def _():
        emit(rp_ref, gp_ref)

    @pl.when(i >= n_p_tiles)
    def _():
        emit(rs_ref, gs_ref)


def _out_proj(ro_p, ro_s, go_p, go_s, w, tn):
    t_p, t_s = ro_p.shape[0], ro_s.shape[0]
    d_in, d_out = w.shape
    tm = LHS_TILE
    rp_spec, rs_spec = _two_group_specs((RET_W,), tm, t_p // tm)
    gp_spec, gs_spec = _two_group_specs((GLA_VW,), tm, t_p // tm)
    return pl.pallas_call(
        functools.partial(_out_proj_kernel, t_p // tm),
        grid=(d_out // tn, (t_p + t_s) // tm),
        in_specs=[rp_spec, rs_spec, gp_spec, gs_spec, pl.BlockSpec((d_in, tn), lambda j, i: (0, j))],
        out_specs=pl.BlockSpec((tm, tn), lambda j, i: (i, j)),
        out_shape=jax.ShapeDtypeStruct((t_p + t_s, d_out), F32),
        compiler_params=_cp(("arbitrary", "arbitrary")),
        name="out_proj",
    )(ro_p, ro_s, go_p, go_s, w)


def _ple_kernel(a_ref, w_ref, p_ref, wp_ref, r_ref, o_ref):
    gate = _sigmoid(_dot(a_ref[...], w_ref[...].astype(BF16)))
    emb = _dot(p_ref[...].astype(BF16), wp_ref[...].astype(BF16))
    o_ref[...] = r_ref[...] + gate * emb


def _ple(xb, x, p, w_pg, w_pp, tm, tn):
    t, d = x.shape
    kp = p.shape[1]
    o_spec = pl.BlockSpec((tm, tn), lambda j, i: (i, j))
    return pl.pallas_call(
        _ple_kernel,
        grid=(d // tn, t // tm),
        in_specs=[pl.BlockSpec((tm, d), lambda j, i: (i, 0)), pl.BlockSpec((d, tn), lambda j, i: (0, j)),
                  pl.BlockSpec((tm, kp), lambda j, i: (i, 0)), pl.BlockSpec((kp, tn), lambda j, i: (0, j)),
                  o_spec],
        out_specs=o_spec,
        out_shape=jax.ShapeDtypeStruct((t, d), F32),
        compiler_params=_cp(("parallel", "parallel")),
        name="ple",
    )(xb, w_pg, p, w_pp, x)


def _rotary(x, cos, sin):
    half = x.shape[-1] // 2
    x1, x2 = x[:, :half], x[:, half:]
    return jnp.concatenate([x1 * cos - x2 * sin, x1 * sin + x2 * cos], axis=-1)


def _group_norm_gate(o, gain, gate):
    mu = jnp.mean(o, axis=-1, keepdims=True)
    d = o - mu
    var = jnp.mean(d * d, axis=-1, keepdims=True)
    return d * lax.rsqrt(var + EPS) * gain * _silu(gate)


def _rms_gate(o, gain, gate):
    ms = jnp.mean(o * o, axis=-1, keepdims=True)
    return o * lax.rsqrt(ms + EPS) * gain * _silu(gate)


def _ret_prompt_kernel(log_g, q_ref, k_ref, v_ref, g_ref, cos_ref, sin_ref, gn_ref, o_ref, s_ref,
                       state, intra_tab, q_tab, k_tab):
    c = RET_CHUNK
    b = pl.program_id(0)
    i = pl.program_id(1)

    @pl.when((b == 0) & (i == 0))
    def _():
        row = lax.broadcasted_iota(jnp.int32, (c, RET_DK), 0).astype(F32)
        ii = lax.broadcasted_iota(jnp.int32, (c, c), 0)
        jj = lax.broadcasted_iota(jnp.int32, (c, c), 1)
        diff = (ii - jj).astype(F32)
        for h in range(RET_HEADS):
            intra_tab[h] = jnp.where(diff >= 0, jnp.exp(jnp.maximum(diff, 0.0) * log_g[h]), 0.0)
            q_tab[h] = jnp.exp((row + 1.0) * log_g[h])
            k_tab[h] = jnp.exp((c - 1.0 - row) * log_g[h])

    @pl.when(i == 0)
    def _():
        state[...] = jnp.zeros_like(state)

    cos, sin = cos_ref[...], sin_ref[...]
    for h in range(RET_HEADS):
        sl = slice(h * RET_DK, (h + 1) * RET_DK)
        q = _rotary(q_ref[:, sl], cos, sin)
        k = _rotary(k_ref[:, sl], cos, sin) * (RET_DK ** -0.5)
        v = v_ref[:, sl].astype(BF16)
        att = _dot_nt(q.astype(BF16), k.astype(BF16)) * intra_tab[h]
        s_old = state[h]
        o = _dot(att.astype(BF16), v) + _dot((q * q_tab[h]).astype(BF16), s_old.astype(BF16))
        state[h] = s_old * math.exp(c * log_g[h]) + _dot_tn((k * k_tab[h]).astype(BF16), v)
        o_ref[:, sl] = _group_norm_gate(o, gn_ref[h], g_ref[:, sl]).astype(o_ref.dtype)

    @pl.when(i == pl.num_programs(1) - 1)
    def _():
        s_ref[0] = state[...]


def _ret_prompt(proj, log_g, cos, sin, gn, batch, seq):
    c = RET_CHUNK
    nc = seq // c

    def col(group):
        return pl.BlockSpec((c, RET_W), lambda b, i: (b * nc + i, group))

    tab = pl.BlockSpec((c, RET_DK // 2), lambda b, i: (i, 0))
    return pl.pallas_call(
        functools.partial(_ret_prompt_kernel, log_g),
        grid=(batch, nc),
        in_specs=[col(0), col(1), col(2), col(3), tab, tab,
                  pl.BlockSpec((RET_HEADS, 1, RET_DV), lambda b, i: (0, 0, 0))],
        out_specs=[pl.BlockSpec((c, RET_W), lambda b, i: (b * nc + i, 0)),
                   pl.BlockSpec((1, RET_HEADS, RET_DK, RET_DV), lambda b, i: (b, 0, 0, 0))],
        out_shape=[jax.ShapeDtypeStruct((batch * seq, RET_W), BF16),
                   jax.ShapeDtypeStruct((batch, RET_HEADS, RET_DK, RET_DV), F32)],
        scratch_shapes=[pltpu.VMEM((RET_HEADS, RET_DK, RET_DV), F32),
                        pltpu.VMEM((RET_HEADS, c, c), F32),
                        pltpu.VMEM((RET_HEADS, c, RET_DK), F32),
                        pltpu.VMEM((RET_HEADS, c, RET_DK), F32)],
        compiler_params=_cp(("arbitrary", "arbitrary")),
        name="retention_prompt",
    )(proj, proj, proj, proj, cos, sin, gn.reshape(RET_HEADS, 1, RET_DV))


def _ret_sample_kernel(log_g, seq, q_ref, k_ref, v_ref, g_ref, cos_ref, sin_ref, gn_ref, s_in, o_ref, s_out):
    rows = DEC_PAIR * seq
    cos, sin = cos_ref[...], sin_ref[...]
    rid = lax.broadcasted_iota(jnp.int32, (rows, RET_DK), 0)
    pos = (rid % seq).astype(F32)
    batch_of_row = rid // seq
    ii = lax.broadcasted_iota(jnp.int32, (rows, rows), 0)
    jj = lax.broadcasted_iota(jnp.int32, (rows, rows), 1)
    visible = (ii // seq == jj // seq) & (ii >= jj)
    diff = jnp.maximum(ii - jj, 0).astype(F32)
    for h in range(RET_HEADS):
        lg = log_g[h]
        sl = slice(h * RET_DK, (h + 1) * RET_DK)
        q = _rotary(q_ref[:, sl], cos, sin)
        k = _rotary(k_ref[:, sl], cos, sin) * (RET_DK ** -0.5)
        v = v_ref[:, sl].astype(BF16)
        intra = jnp.where(visible, jnp.exp(diff * lg), 0.0)
        att = _dot_nt(q.astype(BF16), k.astype(BF16)) * intra
        o = _dot(att.astype(BF16), v)
        qd = (q * jnp.exp((pos + 1.0) * lg)).astype(BF16)
        kd = k * jnp.exp((seq - 1.0 - pos) * lg)
        for b in range(DEC_PAIR):
            mine = batch_of_row == b
            s_old = s_in[b, h]
            o = o + jnp.where(mine, _dot(qd, s_old.astype(BF16)), 0.0)
            s_out[b, h] = s_old * math.exp(seq * lg) + _dot_tn(jnp.where(mine, kd, 0.0).astype(BF16), v)
        o_ref[:, sl] = _group_norm_gate(o, gn_ref[h], g_ref[:, sl]).astype(o_ref.dtype)


def _ret_sample(proj, row0, log_g, cos, sin, gn, state, batch, seq):
    rows = DEC_PAIR * seq
    blk0 = row0 // rows

    def col(group):
        return pl.BlockSpec((rows, RET_W), lambda i: (i + blk0, group))

    tab = pl.BlockSpec((rows, RET_DK // 2), lambda i: (0, 0))
    st = pl.BlockSpec((DEC_PAIR, RET_HEADS, RET_DK, RET_DV), lambda i: (i, 0, 0, 0))
    return pl.pallas_call(
        functools.partial(_ret_sample_kernel, log_g, seq),
        grid=(batch // DEC_PAIR,),
        in_specs=[col(0), col(1), col(2), col(3), tab, tab,
                  pl.BlockSpec((RET_HEADS, 1, RET_DV), lambda i: (0, 0, 0)), st],
        out_specs=[pl.BlockSpec((rows, RET_W), lambda i: (i, 0)), st],
        out_shape=[jax.ShapeDtypeStruct((batch * seq, RET_W), BF16),
                   jax.ShapeDtypeStruct(state.shape, F32)],
        compiler_params=_cp(("parallel",)),
        name="retention_sample",
    )(proj, proj, proj, proj, cos, sin, gn.reshape(RET_HEADS, 1, RET_DV), state)


def _split3(x):
    hi = x.astype(BF16)
    r1 = x - hi.astype(F32)
    mid = r1.astype(BF16)
    lo = (r1 - mid.astype(F32)).astype(BF16)
    return hi, mid, lo


def _column_scale(row_vec, width):
    n = row_vec.shape[-1]
    t = jnp.transpose(jnp.broadcast_to(row_vec, (LANES, n)))
    return jnp.concatenate([t] * (width // LANES), axis=-1)


def _gla_prompt_kernel(q_ref, k_ref, v_ref, g_ref, la_ref, gn_ref, o_ref, s_ref, state):
    c, sub = GLA_CHUNK, GLA_SUB
    nsub = c // sub
    i = pl.program_id(1)

    @pl.when(i == 0)
    def _():
        state[...] = jnp.zeros_like(state)

    ii = lax.broadcasted_iota(jnp.int32, (c, c), 0)
    jj = lax.broadcasted_iota(jnp.int32, (c, c), 1)
    causal = ii >= jj
    tri = jnp.where(causal, 1.0, 0.0).astype(BF16)
    blk = lax.broadcasted_iota(jnp.int32, (c, GLA_DK), 0) // sub
    for h in range(GLA_HEADS):
        ks = slice(h * GLA_DK, (h + 1) * GLA_DK)
        vs = slice(h * GLA_DV, (h + 1) * GLA_DV)
        q = q_ref[:, ks]
        k = k_ref[:, ks] * (GLA_DK ** -0.5)
        v = v_ref[:, vs].astype(BF16)
        hi, mid, lo = _split3(la_ref[:, ks])
        b = _dot(tri, hi) + _dot(tri, mid) + _dot(tri, lo)
        mids = [b[s * sub + sub // 2 - 1: s * sub + sub // 2, :] for s in range(nsub)]
        ref_lvl = jnp.concatenate([jnp.broadcast_to(m, (sub, GLA_DK)) for m in mids], axis=0)
        qd = (q * jnp.exp(b - ref_lvl)).astype(BF16)
        kd = k * jnp.exp(ref_lvl - b)
        rows = []
        for s in range(nsub):
            scale = jnp.where(blk <= s, jnp.exp(jnp.minimum(mids[s] - ref_lvl, 0.0)), 0.0)
            rows.append(_dot_nt(qd[s * sub:(s + 1) * sub], (kd * scale).astype(BF16)))
        att = jnp.where(causal, jnp.concatenate(rows, axis=0), 0.0)
        s_old = state[h]
        o = _dot(att.astype(BF16), v) + _dot((q * jnp.exp(b)).astype(BF16), s_old.astype(BF16))
        b_last = b[c - 1:c, :]
        k_rem = (k * jnp.exp(b_last - b)).astype(BF16)
        state[h] = s_old * _column_scale(jnp.exp(b_last), GLA_DV) + _dot_tn(k_rem, v)
        o_ref[:, vs] = _rms_gate(o, gn_ref[h], g_ref[:, vs]).astype(o_ref.dtype)

    @pl.when(i == pl.num_programs(1) - 1)
    def _():
        s_ref[0] = state[...]


def _gla_prompt(proj, log_a, gn, batch, seq):
    c = GLA_CHUNK
    nc = seq // c

    def col(start, width):
        return pl.BlockSpec((c, width), lambda b, i: (b * nc + i, start // width))

    return pl.pallas_call(
        _gla_prompt_kernel,
        grid=(batch, nc),
        in_specs=[col(COL_GQ, GLA_KW), col(COL_GK, GLA_KW), col(COL_GV, GLA_VW), col(COL_GG, GLA_VW),
                  pl.BlockSpec((c, GLA_KW), lambda b, i: (b * nc + i, 0)),
                  pl.BlockSpec((GLA_HEADS, 1, GLA_DV), lambda b, i: (0, 0, 0))],
        out_specs=[pl.BlockSpec((c, GLA_VW), lambda b, i: (b * nc + i, 0)),
                   pl.BlockSpec((1, GLA_HEADS, GLA_DK, GLA_DV), lambda b, i: (b, 0, 0, 0))],
        out_shape=[jax.ShapeDtypeStruct((batch * seq, GLA_VW), BF16),
                   jax.ShapeDtypeStruct((batch, GLA_HEADS, GLA_DK, GLA_DV), F32)],
        scratch_shapes=[pltpu.VMEM((GLA_HEADS, GLA_DK, GLA_DV), F32)],
        compiler_params=_cp(("parallel", "arbitrary")),
        name="gla_prompt",
    )(proj, proj, proj, proj, log_a, gn.reshape(GLA_HEADS, 1, GLA_DV))


def _gla_sample_kernel(seq, q_ref, k_ref, v_ref, g_ref, la_ref, gn_ref, s_in, o_ref, s_out):
    rows = DEC_PAIR * seq
    rid = lax.broadcasted_iota(jnp.int32, (rows, GLA_DK), 0)
    pos = rid % seq
    batch_of_row = rid // seq
    ii = lax.broadcasted_iota(jnp.int32, (rows, rows), 0)
    jj = lax.broadcasted_iota(jnp.int32, (rows, rows), 1)
    visible = (ii // seq == jj // seq) & (ii >= jj)
    for h in range(GLA_HEADS):
        ks = slice(h * GLA_DK, (h + 1) * GLA_DK)
        vs = slice(h * GLA_DV, (h + 1) * GLA_DV)
        la = la_ref[:, ks]
        b = la
        for d in range(1, seq):
            b = b + jnp.where(pos >= d, pltpu.roll(la, d, axis=0), 0.0)
        q = q_ref[:, ks]
        k = k_ref[:, ks] * (GLA_DK ** -0.5)
        v = v_ref[:, vs].astype(BF16)
        qb = (q * jnp.exp(b)).astype(BF16)
        kb = (k * jnp.exp(-b)).astype(BF16)
        att = jnp.where(visible, _dot_nt(qb, kb), 0.0)
        o = _dot(att.astype(BF16), v)
        for bi in range(DEC_PAIR):
            mine = batch_of_row == bi
            last = bi * seq + seq - 1
            b_last = b[last:last + 1, :]
            s_old = s_in[bi, h]
            o = o + jnp.where(mine[:, :1], _dot(qb, s_old.astype(BF16)), 0.0)
            k_rem = jnp.where(mine, k * jnp.exp(b_last - b), 0.0).astype(BF16)
            s_out[bi, h] = s_old * _column_scale(jnp.exp(b_last), GLA_DV) + _dot_tn(k_rem, v)
        o_ref[:, vs] = _rms_gate(o, gn_ref[h], g_ref[:, vs]).astype(o_ref.dtype)


def _gla_sample(proj, row0, log_a, gn, state, batch, seq):
    rows = DEC_PAIR * seq
    blk0 = row0 // rows
    st = pl.BlockSpec((DEC_PAIR, GLA_HEADS, GLA_DK, GLA_DV), lambda i: (i, 0, 0, 0))
    return pl.pallas_call(
        functools.partial(_gla_sample_kernel, seq),
        grid=(batch // DEC_PAIR,),
        in_specs=[pl.BlockSpec((rows, GLA_KW), lambda i: (i + blk0, COL_GQ // GLA_KW)),
                  pl.BlockSpec((rows, GLA_KW), lambda i: (i + blk0, COL_GK // GLA_KW)),
                  pl.BlockSpec((rows, GLA_VW), lambda i: (i + blk0, COL_GV // GLA_VW)),
                  pl.BlockSpec((rows, GLA_VW), lambda i: (i + blk0, COL_GG // GLA_VW)),
                  pl.BlockSpec((rows, GLA_KW), lambda i: (i + blk0, 0)),
                  pl.BlockSpec((GLA_HEADS, 1, GLA_DV), lambda i: (0, 0, 0)), st],
        out_specs=[pl.BlockSpec((rows, GLA_VW), lambda i: (i, 0)), st],
        out_shape=[jax.ShapeDtypeStruct((batch * seq, GLA_VW), BF16),
                   jax.ShapeDtypeStruct(state.shape, F32)],
        compiler_params=_cp(("parallel",)),
        name="gla_sample",
    )(proj, proj, proj, proj, log_a, gn.reshape(GLA_HEADS, 1, GLA_DV), state)


def _router_kernel(n_p_tiles, m_ref, xp_ref, xs_ref, g_ref, wr_ref, br_ref, x_ref, hm_ref, idx_ref, gate_ref):
    tr = m_ref.shape[0]
    i = pl.program_id(0)

    @pl.when(i < n_p_tiles)
    def _():
        x_ref[...] = xp_ref[...] + m_ref[...]

    @pl.when(i >= n_p_tiles)
    def _():
        x_ref[...] = xs_ref[...] + m_ref[...]

    hm = _rms_norm(x_ref[...], g_ref[...])
    hm_ref[...] = hm
    h1, h2, _ = _split3(hm)
    w = wr_ref[...]
    w1 = w.astype(BF16)
    w2 = (w - w1.astype(F32)).astype(BF16)
    logits = _dot(h1, w1) + _dot(h1, w2) + _dot(h2, w1) + br_ref[...]
    lane = lax.broadcasted_iota(jnp.int32, (tr, LANES), 1).astype(F32)
    neg, far = -1e30, 1e9
    is_group = lane < N_GROUPS
    gl = jnp.where(is_group, logits, neg)
    gmax = jnp.max(gl, axis=-1, keepdims=True)
    gidx = jnp.min(jnp.where(gl == gmax, lane, far), axis=-1, keepdims=True)
    gsum = jnp.sum(jnp.where(is_group, jnp.exp(gl - gmax), 0.0), axis=-1, keepdims=True)
    g_p = 1.0 / gsum
    lo = N_GROUPS + EXPERTS_PER_GROUP * gidx
    in_sel = (lane >= lo) & (lane < lo + EXPERTS_PER_GROUP)
    el = jnp.where(in_sel, logits, neg)
    emax = jnp.max(el, axis=-1, keepdims=True)
    e1 = jnp.min(jnp.where(el == emax, lane, far), axis=-1, keepdims=True)
    esum = jnp.sum(jnp.where(in_sel, jnp.exp(el - emax), 0.0), axis=-1, keepdims=True)
    el2 = jnp.where(lane == e1, neg, el)
    m2 = jnp.max(el2, axis=-1, keepdims=True)
    e2 = jnp.min(jnp.where(el2 == m2, lane, far), axis=-1, keepdims=True)
    p1 = 1.0 / esum
    p2 = jnp.exp(m2 - emax) / esum
    den = p1 + p2
    idx_ref[...] = jnp.where(lane == 0, e1 - N_GROUPS, jnp.where(lane == 1, e2 - N_GROUPS, 0.0)).astype(jnp.int32)
    gate_ref[...] = jnp.where(lane == 0, g_p * p1 / den, jnp.where(lane == 1, g_p * p2 / den, 0.0))


def _router(m, x_p, x_s, g, w_r, b_r):
    t, d = m.shape
    t_p = x_p.shape[0]
    tr = ROW_TILE
    p_spec, s_spec = _two_group_specs((d,), tr, t_p // tr)
    row = pl.BlockSpec((tr, d), lambda i: (i, 0))
    lane_row = pl.BlockSpec((tr, LANES), lambda i: (i, 0))
    return pl.pallas_call(
        functools.partial(_router_kernel, t_p // tr),
        grid=(t // tr,),
        in_specs=[row, p_spec, s_spec, pl.BlockSpec((1, d), lambda i: (0, 0)),
                  pl.BlockSpec((d, LANES), lambda i: (0, 0)), pl.BlockSpec((1, LANES), lambda i: (0, 0))],
        out_specs=[row, row, lane_row, lane_row],
        out_shape=[jax.ShapeDtypeStruct((t, d), F32), jax.ShapeDtypeStruct((t, d), F32),
                   jax.ShapeDtypeStruct((t, LANES), jnp.int32), jax.ShapeDtypeStruct((t, LANES), F32)],
        compiler_params=_cp(("arbitrary",)),
        name="moe_router",
    )(m, x_p, x_s, g.reshape(1, d), w_r, b_r)


def _moe_kernel(n_used, item_e, item_start, item_n, tok,
                hm_hbm, wg_ref, wu_ref, wd_ref, y_hbm,
                rows_in, rows_out, hid, sem_in, sem_out):
    del n_used, item_e
    i = pl.program_id(0)
    j = pl.program_id(1)
    n_items = pl.num_programs(0)
    n_steps = pl.num_programs(1)
    n_up = D_EXPERT // MOE_TJ
    max_blk = MOE_ROWS // MOE_BLK
    n, s0 = item_n[i], item_start[i]
    nxt = jnp.minimum(i + 1, n_items - 1)
    n_next, s_next = jnp.where(i + 1 < n_items, item_n[nxt], 0), item_start[nxt]
    prv = jnp.maximum(i - 1, 0)
    n_prev, s_prev = jnp.where(i > 0, item_n[prv], 0), item_start[prv]
    nblk = (n + MOE_BLK - 1) // MOE_BLK
    nblk_prev = (n_prev + MOE_BLK - 1) // MOE_BLK

    def gather(s, r):
        return pltpu.make_async_copy(hm_hbm.at[pl.ds(tok[s + r], 1), :], rows_in.at[pl.ds(r, 1), :], sem_in)

    def gathered_group(c):
        rows = pl.ds(pl.multiple_of(c * MOE_DMA_UNROLL, MOE_DMA_UNROLL), MOE_DMA_UNROLL)
        return pltpu.make_async_copy(hm_hbm.at[pl.ds(0, MOE_DMA_UNROLL), :], rows_in.at[rows, :], sem_in)

    def put_block(s, k):
        dst_rows = pl.ds(pl.multiple_of(s + k * MOE_BLK, 8), MOE_BLK)
        return pltpu.make_async_copy(rows_out.at[pl.ds(k * MOE_BLK, MOE_BLK), :], y_hbm.at[dst_rows, :], sem_out)

    def for_blocks(count, fn):
        for k in range(max_blk):
            @pl.when(k < count)
            def _(k=k):
                fn(k)

    def for_rows(count, fn, group_fn=None):
        groups = count // MOE_DMA_UNROLL

        def group(c, carry):
            if group_fn is None:
                for u in range(MOE_DMA_UNROLL):
                    fn(c * MOE_DMA_UNROLL + u)
            else:
                group_fn(c)
            return carry
        lax.fori_loop(0, groups, group, 0)

        def single(r, carry):
            fn(r)
            return carry
        lax.fori_loop(groups * MOE_DMA_UNROLL, count, single, 0)

    def for_row_count(fn):
        for k in range(1, max_blk + 1):
            @pl.when(nblk == k)
            def _(k=k):
                fn(k * MOE_BLK)

    @pl.when((i == 0) & (j == 0))
    def _():
        rows_in[...] = jnp.zeros_like(rows_in)
        for_rows(n, lambda r: gather(s0, r).start())

    @pl.when(j == 0)
    def _():
        for_rows(n, lambda r: gather(s0, r).wait(), lambda c: gathered_group(c).wait())

    @pl.when(j < n_up)
    def _():
        def up(m):
            x = rows_in[0:m, :].astype(BF16)
            act = _silu(_dot(x, wg_ref[0].astype(BF16))) * _dot(x, wu_ref[0].astype(BF16))
            hid[j, 0:m, :] = act.astype(BF16)
        for_row_count(up)

    @pl.when(j == n_up)
    def _():
        for_rows(n_next, lambda r: gather(s_next, r).start())
        for_blocks(nblk_prev, lambda k: put_block(s_prev, k).wait())

    @pl.when(j >= n_up)
    def _():
        col = pl.multiple_of((j - n_up) * MOE_TN, MOE_TN)

        def down(m):
            act = jnp.concatenate([hid[u, 0:m, :] for u in range(n_up)], axis=-1)
            rows_out[0:m, pl.ds(col, MOE_TN)] = _dot(act, wd_ref[0].astype(BF16))
        for_row_count(down)

    @pl.when(j == n_steps - 1)
    def _():
        for_blocks(nblk, lambda k: put_block(s0, k).start())

    @pl.when((j == n_steps - 1) & (i == n_items - 1))
    def _():
        for_blocks(nblk, lambda k: put_block(s0, k).wait())


def _moe_experts(hm, w_gate, w_up, w_down, items, tok):
    n_used, item_e, item_start, item_n = items
    n_up = D_EXPERT // MOE_TJ
    n_dn = D_MODEL // MOE_TN

    def up_map(i, j, nu, e, st, n, tok):
        return (e[i], 0, jnp.minimum(j, n_up - 1))

    def dn_map(i, j, nu, e, st, n, tok):
        return (e[i], 0, jnp.maximum(j - n_up, 0))

    return pl.pallas_call(
        _moe_kernel,
        grid_spec=pltpu.PrefetchScalarGridSpec(
            num_scalar_prefetch=5,
            grid=(n_used[0], n_up + n_dn),
            in_specs=[pl.BlockSpec(memory_space=pl.ANY),
                      pl.BlockSpec((1, D_MODEL, MOE_TJ), up_map),
                      pl.BlockSpec((1, D_MODEL, MOE_TJ), up_map),
                      pl.BlockSpec((1, D_EXPERT, MOE_TN), dn_map)],
            out_specs=pl.BlockSpec(memory_space=pl.ANY),
            scratch_shapes=[pltpu.VMEM((MOE_ROWS, D_MODEL), F32),
                            pltpu.VMEM((MOE_ROWS, D_MODEL), F32),
                            pltpu.VMEM((n_up, MOE_ROWS, MOE_TJ), BF16),
                            pltpu.SemaphoreType.DMA(()),
                            pltpu.SemaphoreType.DMA(())],
        ),
        out_shape=jax.ShapeDtypeStruct((tok.shape[0], D_MODEL), F32),
        compiler_params=_cp(("arbitrary", "arbitrary")),
        name="moe_experts",
    )(n_used, item_e, item_start, item_n, tok, hm, w_gate, w_up, w_down)


def _moe_plan(expert_idx):
    n_assign = expert_idx.size
    flat_e = expert_idx.reshape(n_assign)
    order = jnp.argsort(flat_e).astype(jnp.int32)
    counts = jnp.bincount(flat_e, length=N_EXPERTS).astype(jnp.int32)
    seg = (counts + 7) // 8 * 8
    start = jnp.cumsum(seg) - seg
    sorted_e = flat_e[order]
    row_of_sorted = start[sorted_e] + jnp.arange(n_assign, dtype=jnp.int32) - (jnp.cumsum(counts) - counts)[sorted_e]
    n_rows = n_assign + 8 * N_EXPERTS + MOE_ROWS
    token_of_row = jnp.zeros((n_rows,), jnp.int32).at[row_of_sorted].set(order // TOP_K)
    row_of_assign = jnp.zeros((n_assign,), jnp.int32).at[order].set(row_of_sorted)
    per_e = (counts + MOE_ROWS - 1) // MOE_ROWS
    item_end = jnp.cumsum(per_e)
    n_items = n_assign // MOE_ROWS + N_EXPERTS
    ids = jnp.arange(n_items, dtype=jnp.int32)
    used = ids < item_end[-1]
    last = jnp.maximum(item_end[-1] - 1, 0)
    e_of = jnp.minimum(jnp.searchsorted(item_end, jnp.minimum(ids, last), side="right"), N_EXPERTS - 1).astype(jnp.int32)
    local = jnp.minimum(ids, last) - (item_end - per_e)[e_of]
    item_start = start[e_of] + local * MOE_ROWS
    item_n = jnp.where(used, jnp.clip(counts[e_of] - local * MOE_ROWS, 0, MOE_ROWS), 0)
    items = (item_end[-1:].astype(jnp.int32), e_of, item_start.astype(jnp.int32), item_n.astype(jnp.int32))
    return items, token_of_row, row_of_assign


def _combine_kernel(row_of, x_ref, gate_ref, y_hbm, o_ref, ob_ref, ybuf, sems):
    tr = x_ref.shape[0]
    i = pl.program_id(0)
    slot = i % 2

    def fetch(tile, buf, k, r):
        src = row_of[(tile * tr + r) * TOP_K + k]
        return pltpu.make_async_copy(y_hbm.at[pl.ds(src, 1), :], ybuf.at[buf, k, pl.ds(r, 1), :], sems.at[buf])

    def fetched_group(buf, k, c):
        rows = pl.ds(pl.multiple_of(c * MOE_DMA_UNROLL, MOE_DMA_UNROLL), MOE_DMA_UNROLL)
        return pltpu.make_async_copy(y_hbm.at[pl.ds(0, MOE_DMA_UNROLL), :], ybuf.at[buf, k, rows, :], sems.at[buf])

    def start_tile(tile, buf):
        def group(c, carry):
            for u in range(MOE_DMA_UNROLL):
                for k in range(TOP_K):
                    fetch(tile, buf, k, c * MOE_DMA_UNROLL + u).start()
            return carry
        lax.fori_loop(0, tr // MOE_DMA_UNROLL, group, 0)

    @pl.when(i == 0)
    def _():
        start_tile(0, 0)

    @pl.when(i + 1 < pl.num_programs(0))
    def _():
        start_tile(i + 1, 1 - slot)

    def wait_group(c, carry):
        for k in range(TOP_K):
            fetched_group(slot, k, c).wait()
        return carry
    lax.fori_loop(0, tr // MOE_DMA_UNROLL, wait_group, 0)

    x = x_ref[...] + (ybuf[slot, 0] * gate_ref[:, 0:1] + ybuf[slot, 1] * gate_ref[:, 1:2])
    o_ref[...] = x
    ob_ref[...] = x.astype(BF16)


def _combine(x, y, row_of_assign, gate):
    t, d = x.shape
    tr = ROW_TILE
    row = pl.BlockSpec((tr, d), lambda i, rows: (i, 0))
    return pl.pallas_call(
        _combine_kernel,
        grid_spec=pltpu.PrefetchScalarGridSpec(
            num_scalar_prefetch=1,
            grid=(t // tr,),
            in_specs=[row, pl.BlockSpec((tr, LANES), lambda i, rows: (i, 0)), pl.BlockSpec(memory_space=pl.ANY)],
            out_specs=[row, row],
            scratch_shapes=[pltpu.VMEM((2, TOP_K, tr, d), F32), pltpu.SemaphoreType.DMA((2,))],
        ),
        out_shape=[jax.ShapeDtypeStruct((t, d), F32), jax.ShapeDtypeStruct((t, d), BF16)],
        compiler_params=_cp(("arbitrary",)),
        name="moe_combine",
    )(row_of_assign, x, gate, y)


def _rope_tables(pos):
    half = RET_DK // 2
    inv = ROPE_BASE ** (-jnp.arange(half, dtype=F32) / half)
    ang = pos.astype(F32)[:, None] * inv[None, :]
    return jnp.cos(ang), jnp.sin(ang)


def kernel(x_prompt, x_sample, state_ret, state_gla, p_prompt, p_sample, g_mix, w_in, w_gla_up, b_gla,
           ret_norm_g, gla_norm_g, w_out, g_moe, w_rg, b_rg, w_re, b_re, w_gate, w_up, w_down, w_pg, w_pp,
           g_final):
    n_p, l_p, d = x_prompt.shape
    n_s, l_s, _ = x_sample.shape
    depth = g_mix.shape[0]
    t_p, t_s = n_p * l_p, n_s * l_s
    t = t_p + t_s

    log_g_py = [math.log1p(-(2.0 ** (-5.0 - h))) for h in range(RET_HEADS)]
    cos_p, sin_p = _rope_tables(jnp.arange(l_p, dtype=jnp.int32))
    cos_s, sin_s = _rope_tables(PAST_LEN + jnp.arange(l_s, dtype=jnp.int32))
    cos_s, sin_s = jnp.tile(cos_s, (DEC_PAIR, 1)), jnp.tile(sin_s, (DEC_PAIR, 1))

    x_p = x_prompt.reshape(t_p, d)
    x_s = x_sample.reshape(t_s, d)
    x = None
    ret_p, ret_s, gla_p, gla_s = [], [], [], []
    for l in range(depth):
        w_in_t = jnp.swapaxes(w_in[l], 0, 1)
        w_ga = jnp.pad(w_in_t[N_MAIN:], ((0, LANES - GLA_RANK), (0, 0))).astype(BF16)
        w_gup = jnp.pad(w_gla_up[l], ((0, LANES - GLA_RANK), (0, 0))).astype(BF16)
        w_r = jnp.pad(jnp.concatenate([w_rg[l], w_re[l]], axis=1),
                      ((0, 0), (0, LANES - N_GROUPS - N_EXPERTS)))
        b_r = jnp.pad(jnp.concatenate([b_rg[l], b_re[l]]), (0, LANES - N_GROUPS - N_EXPERTS)).reshape(1, LANES)
        p = jnp.concatenate([p_prompt[l].reshape(t_p, -1), p_sample[l].reshape(t_s, -1)], axis=0)
        if x is not None:
            x_p, x_s = x[:t_p], x[t_p:]

        h, log_a = _norm_in(x_p, x_s, g_mix[l], w_ga, w_gup, b_gla[l])
        proj = _in_proj(h, w_in_t, N_MAIN, t // 16, 1024)
        ro_p, sr_p = _ret_prompt(proj, log_g_py, cos_p, sin_p, ret_norm_g[l], n_p, l_p)
        go_p, sg_p = _gla_prompt(proj, log_a, gla_norm_g[l], n_p, l_p)
        ro_s, sr_s = _ret_sample(proj, t_p, log_g_py, cos_s, sin_s, ret_norm_g[l], state_ret[l], n_s, l_s)
        go_s, sg_s = _gla_sample(proj, t_p, log_a, gla_norm_g[l], state_gla[l], n_s, l_s)
        ret_p.append(sr_p)
        ret_s.append(sr_s)
        gla_p.append(sg_p)
        gla_s.append(sg_s)
        mix = _out_proj(ro_p, ro_s, go_p, go_s, w_out[l], 1024)

        x, hm, idx, gate = _router(mix, x_p, x_s, g_moe[l], w_r, b_r)
        items, token_of_row, row_of_assign = _moe_plan(idx[:, :TOP_K])
        y = _moe_experts(hm, w_gate[l], w_up[l], w_down[l], items, token_of_row)
        x, xb = _combine(x, y, row_of_assign, gate)

        x = _ple(xb, x, p, w_pg[l], w_pp[l], t // 8, 512)

    y_prompt = _norm_out(x, g_final, 0, t_p).reshape(n_p, l_p, d)
    y_sample = _norm_out(x, g_final, t_p, t_s).reshape(n_s, l_s, d)
    return (y_prompt, y_sample,
            jnp.stack(ret_p).astype(state_ret.dtype), jnp.stack(ret_s).astype(state_ret.dtype),
            jnp.stack(gla_p).astype(state_gla.dtype), jnp.stack(gla_s).astype(state_gla.dtype))
```

```python
import functools
import math

import jax
import jax.numpy as jnp
from jax import lax
from jax.experimental import pallas as pl
from jax.experimental.pallas import tpu as pltpu

F32 = jnp.float32
BF16 = jnp.bfloat16

D_MODEL = 4096
RET_HEADS = 8
RET_DK = 256
RET_DV = 256
GLA_HEADS = 4
GLA_DK = 256
GLA_DV = 512
GLA_RANK = 16
GLA_GATE_TEMP = 16.0
ROPE_BASE = 10000.0
PAST_LEN = 16384
N_GROUPS = 4
EXPERTS_PER_GROUP = 8
N_EXPERTS = N_GROUPS * EXPERTS_PER_GROUP
TOP_K = 2
D_EXPERT = D_MODEL // 4
EPS = 1e-6

RET_W = RET_HEADS * RET_DK
GLA_KW = GLA_HEADS * GLA_DK
GLA_VW = GLA_HEADS * GLA_DV
N_MAIN = 4 * RET_W + 2 * GLA_KW + 2 * GLA_VW
COL_GQ = 4 * RET_W
COL_GK = COL_GQ + GLA_KW
COL_GV = COL_GK + GLA_KW
COL_GG = COL_GV + GLA_VW

LANES = 128
ROW_TILE = 256
LHS_TILE = 512
RET_CHUNK = 256
GLA_CHUNK = 128
GLA_SUB = 32
DEC_PAIR = 4
MOE_ROWS = 640
MOE_BLK = 128
MOE_TK = 1024
MOE_TN = 1024
MOE_DMA_UNROLL = 8
VMEM_LIMIT = 56 * 1024 * 1024


def _cp(semantics, vmem=VMEM_LIMIT):
    return pltpu.CompilerParams(dimension_semantics=semantics, vmem_limit_bytes=vmem)


def _sigmoid(x):
    return 1.0 / (1.0 + jnp.exp(-x))


def _silu(x):
    return x * _sigmoid(x)


def _dot(a, b):
    return jnp.dot(a, b, preferred_element_type=F32)


def _dot_nt(a, b):
    return lax.dot_general(a, b, (((1,), (1,)), ((), ())), preferred_element_type=F32)


def _dot_tn(a, b):
    return lax.dot_general(a, b, (((0,), (0,)), ((), ())), preferred_element_type=F32)


def _rms_norm(x, g):
    ms = jnp.mean(x * x, axis=-1, keepdims=True)
    return x * lax.rsqrt(ms + EPS) * g


def _two_group_specs(shape_tail, tile, n_p_tiles):
    zeros = (0,) * len(shape_tail)

    def p_map(*idx):
        return (jnp.minimum(idx[-1], n_p_tiles - 1),) + zeros

    def s_map(*idx):
        return (jnp.maximum(idx[-1] - n_p_tiles, 0),) + zeros

    return pl.BlockSpec((tile,) + shape_tail, p_map), pl.BlockSpec((tile,) + shape_tail, s_map)


def _log_sigmoid(z):
    return jnp.minimum(z, 0.0) - jnp.log(1.0 + jnp.exp(-jnp.abs(z)))


def _norm_in_kernel(n_p_tiles, xp_ref, xs_ref, g_ref, wga_ref, wup_ref, b_ref, h_ref, la_ref):
    i = pl.program_id(0)

    def emit(x):
        h = _rms_norm(x, g_ref[...]).astype(BF16)
        h_ref[...] = h
        ga = _dot_nt(h, wga_ref[...])
        z = _dot(ga.astype(BF16), wup_ref[...]) + b_ref[...]
        la_ref[...] = _log_sigmoid(z) * (1.0 / GLA_GATE_TEMP)

    @pl.when(i < n_p_tiles)
    def _():
        emit(xp_ref[...])

    @pl.when(i >= n_p_tiles)
    def _():
        emit(xs_ref[...])


def _norm_in(x_p, x_s, g, w_ga, w_up, b):
    (t_p, d), t_s = x_p.shape, x_s.shape[0]
    tr = ROW_TILE
    p_spec, s_spec = _two_group_specs((d,), tr, t_p // tr)
    return pl.pallas_call(
        functools.partial(_norm_in_kernel, t_p // tr),
        grid=((t_p + t_s) // tr,),
        in_specs=[p_spec, s_spec, pl.BlockSpec((1, d), lambda i: (0, 0)),
                  pl.BlockSpec((LANES, d), lambda i: (0, 0)),
                  pl.BlockSpec((LANES, GLA_KW), lambda i: (0, 0)),
                  pl.BlockSpec((1, GLA_KW), lambda i: (0, 0))],
        out_specs=[pl.BlockSpec((tr, d), lambda i: (i, 0)), pl.BlockSpec((tr, GLA_KW), lambda i: (i, 0))],
        out_shape=[jax.ShapeDtypeStruct((t_p + t_s, d), BF16), jax.ShapeDtypeStruct((t_p + t_s, GLA_KW), F32)],
        compiler_params=_cp(("arbitrary",)),
        name="norm_in",
    )(x_p, x_s, g.reshape(1, d), w_ga, w_up, b.reshape(1, GLA_KW))


def _norm_out_kernel(x_ref, g_ref, o_ref):
    o_ref[...] = _rms_norm(x_ref[...], g_ref[...])


def _norm_out(x, g, row0, n_rows):
    d = x.shape[1]
    tr = ROW_TILE
    blk0 = row0 // tr
    return pl.pallas_call(
        _norm_out_kernel,
        grid=(n_rows // tr,),
        in_specs=[pl.BlockSpec((tr, d), lambda i: (i + blk0, 0)), pl.BlockSpec((1, d), lambda i: (0, 0))],
        out_specs=pl.BlockSpec((tr, d), lambda i: (i, 0)),
        out_shape=jax.ShapeDtypeStruct((n_rows, d), F32),
        compiler_params=_cp(("parallel",)),
        name="norm_out",
    )(x, g.reshape(1, d))


def _in_proj_kernel(a_ref, wt_ref, o_ref):
    o_ref[...] = _dot_nt(a_ref[...], wt_ref[...].astype(BF16))


def _in_proj(h, w_t, n_cols, tm, tn):
    t, d = h.shape
    return pl.pallas_call(
        _in_proj_kernel,
        grid=(n_cols // tn, t // tm),
        in_specs=[pl.BlockSpec((tm, d), lambda j, i: (i, 0)), pl.BlockSpec((tn, d), lambda j, i: (j, 0))],
        out_specs=pl.BlockSpec((tm, tn), lambda j, i: (i, j)),
        out_shape=jax.ShapeDtypeStruct((t, n_cols), F32),
        compiler_params=_cp(("parallel", "parallel")),
        name="in_proj",
    )(h, w_t)


def _out_proj_kernel(n_p_tiles, rp_ref, rs_ref, gp_ref, gs_ref, w_ref, o_ref):
    i = pl.program_id(1)

    def emit(r_ref, g_ref):
        o_ref[...] = (_dot(r_ref[...], w_ref[:RET_W, :].astype(BF16))
                      + _dot(g_ref[...], w_ref[RET_W:, :].astype(BF16)))

    @pl.when(i < n_p_tiles)
    def _():
        emit(rp_ref, gp_ref)

    @pl.when(i >= n_p_tiles)
    def _():
        emit(rs_ref, gs_ref)


def _out_proj(ro_p, ro_s, go_p, go_s, w, tn):
    t_p, t_s = ro_p.shape[0], ro_s.shape[0]
    d_in, d_out = w.shape
    tm = LHS_TILE
    rp_spec, rs_spec = _two_group_specs((RET_W,), tm, t_p // tm)
    gp_spec, gs_spec = _two_group_specs((GLA_VW,), tm, t_p // tm)
    return pl.pallas_call(
        functools.partial(_out_proj_kernel, t_p // tm),
        grid=(d_out // tn, (t_p + t_s) // tm),
        in_specs=[rp_spec, rs_spec, gp_spec, gs_spec, pl.BlockSpec((d_in, tn), lambda j, i: (0, j))],
        out_specs=pl.BlockSpec((tm, tn), lambda j, i: (i, j)),
        out_shape=jax.ShapeDtypeStruct((t_p + t_s, d_out), F32),
        compiler_params=_cp(("arbitrary", "arbitrary")),
        name="out_proj",
    )(ro_p, ro_s, go_p, go_s, w)


def _ple_kernel(a_ref, w_ref, p_ref, wp_ref, r_ref, o_ref):
    gate = _sigmoid(_dot(a_ref[...], w_ref[...].astype(BF16)))
    emb = _dot(p_ref[...].astype(BF16), wp_ref[...].astype(BF16))
    o_ref[...] = r_ref[...] + gate * emb


def _ple(xb, x, p, w_pg, w_pp, tm, tn):
    t, d = x.shape
    kp = p.shape[1]
    o_spec = pl.BlockSpec((tm, tn), lambda j, i: (i, j))
    return pl.pallas_call(
        _ple_kernel,
        grid=(d // tn, t // tm),
        in_specs=[pl.BlockSpec((tm, d), lambda j, i: (i, 0)), pl.BlockSpec((d, tn), lambda j, i: (0, j)),
                  pl.BlockSpec((tm, kp), lambda j, i: (i, 0)), pl.BlockSpec((kp, tn), lambda j, i: (0, j)),
                  o_spec],
        out_specs=o_spec,
        out_shape=jax.ShapeDtypeStruct((t, d), F32),
        compiler_params=_cp(("parallel", "parallel")),
        name="ple",
    )(xb, w_pg, p, w_pp, x)


def _rotary(x, cos, sin):
    half = x.shape[-1] // 2
    x1, x2 = x[:, :half], x[:, half:]
    return jnp.concatenate([x1 * cos - x2 * sin, x1 * sin + x2 * cos], axis=-1)


def _group_norm_gate(o, gain, gate):
    mu = jnp.mean(o, axis=-1, keepdims=True)
    d = o - mu
    var = jnp.mean(d * d, axis=-1, keepdims=True)
    return d * lax.rsqrt(var + EPS) * gain * _silu(gate)


def _rms_gate(o, gain, gate):
    ms = jnp.mean(o * o, axis=-1, keepdims=True)
    return o * lax.rsqrt(ms + EPS) * gain * _silu(gate)


def _ret_prompt_kernel(log_g, q_ref, k_ref, v_ref, g_ref, cos_ref, sin_ref, gn_ref, o_ref, s_ref,
                       state, intra_tab, q_tab, k_tab):
    c = RET_CHUNK
    b = pl.program_id(0)
    i = pl.program_id(1)

    @pl.when((b == 0) & (i == 0))
    def _():
        row = lax.broadcasted_iota(jnp.int32, (c, RET_DK), 0).astype(F32)
        ii = lax.broadcasted_iota(jnp.int32, (c, c), 0)
        jj = lax.broadcasted_iota(jnp.int32, (c, c), 1)
        diff = (ii - jj).astype(F32)
        for h in range(RET_HEADS):
            intra_tab[h] = jnp.where(diff >= 0, jnp.exp(jnp.maximum(diff, 0.0) * log_g[h]), 0.0)
            q_tab[h] = jnp.exp((row + 1.0) * log_g[h])
            k_tab[h] = jnp.exp((c - 1.0 - row) * log_g[h])

    @pl.when(i == 0)
    def _():
        state[...] = jnp.zeros_like(state)

    cos, sin = cos_ref[...], sin_ref[...]
    for h in range(RET_HEADS):
        sl = slice(h * RET_DK, (h + 1) * RET_DK)
        q = _rotary(q_ref[:, sl], cos, sin)
        k = _rotary(k_ref[:, sl], cos, sin) * (RET_DK ** -0.5)
        v = v_ref[:, sl].astype(BF16)
        att = _dot_nt(q.astype(BF16), k.astype(BF16)) * intra_tab[h]
        s_old = state[h]
        o = _dot(att.astype(BF16), v) + _dot((q * q_tab[h]).astype(BF16), s_old.astype(BF16))
        state[h] = s_old * math.exp(c * log_g[h]) + _dot_tn((k * k_tab[h]).astype(BF16), v)
        o_ref[:, sl] = _group_norm_gate(o, gn_ref[h], g_ref[:, sl]).astype(o_ref.dtype)

    @pl.when(i == pl.num_programs(1) - 1)
    def _():
        s_ref[0] = state[...]


def _ret_prompt(proj, log_g, cos, sin, gn, batch, seq):
    c = RET_CHUNK
    nc = seq // c

    def col(group):
        return pl.BlockSpec((c, RET_W), lambda b, i: (b * nc + i, group))

    tab = pl.BlockSpec((c, RET_DK // 2), lambda b, i: (i, 0))
    return pl.pallas_call(
        functools.partial(_ret_prompt_kernel, log_g),
        grid=(batch, nc),
        in_specs=[col(0), col(1), col(2), col(3), tab, tab,
                  pl.BlockSpec((RET_HEADS, 1, RET_DV), lambda b, i: (0, 0, 0))],
        out_specs=[pl.BlockSpec((c, RET_W), lambda b, i: (b * nc + i, 0)),
                   pl.BlockSpec((1, RET_HEADS, RET_DK, RET_DV), lambda b, i: (b, 0, 0, 0))],
        out_shape=[jax.ShapeDtypeStruct((batch * seq, RET_W), BF16),
                   jax.ShapeDtypeStruct((batch, RET_HEADS, RET_DK, RET_DV), F32)],
        scratch_shapes=[pltpu.VMEM((RET_HEADS, RET_DK, RET_DV), F32),
                        pltpu.VMEM((RET_HEADS, c, c), F32),
                        pltpu.VMEM((RET_HEADS, c, RET_DK), F32),
                        pltpu.VMEM((RET_HEADS, c, RET_DK), F32)],
        compiler_params=_cp(("arbitrary", "arbitrary")),
        name="retention_prompt",
    )(proj, proj, proj, proj, cos, sin, gn.reshape(RET_HEADS, 1, RET_DV))


def _ret_sample_kernel(log_g, seq, q_ref, k_ref, v_ref, g_ref, cos_ref, sin_ref, gn_ref, s_in, o_ref, s_out):
    rows = DEC_PAIR * seq
    cos, sin = cos_ref[...], sin_ref[...]
    rid = lax.broadcasted_iota(jnp.int32, (rows, RET_DK), 0)
    pos = (rid % seq).astype(F32)
    batch_of_row = rid // seq
    ii = lax.broadcasted_iota(jnp.int32, (rows, rows), 0)
    jj = lax.broadcasted_iota(jnp.int32, (rows, rows), 1)
    visible = (ii // seq == jj // seq) & (ii >= jj)
    diff = jnp.maximum(ii - jj, 0).astype(F32)
    for h in range(RET_HEADS):
        lg = log_g[h]
        sl = slice(h * RET_DK, (h + 1) * RET_DK)
        q = _rotary(q_ref[:, sl], cos, sin)
        k = _rotary(k_ref[:, sl], cos, sin) * (RET_DK ** -0.5)
        v = v_ref[:, sl].astype(BF16)
        intra = jnp.where(visible, jnp.exp(diff * lg), 0.0)
        att = _dot_nt(q.astype(BF16), k.astype(BF16)) * intra
        o = _dot(att.astype(BF16), v)
        qd = (q * jnp.exp((pos + 1.0) * lg)).astype(BF16)
        kd = k * jnp.exp((seq - 1.0 - pos) * lg)
        for b in range(DEC_PAIR):
            mine = batch_of_row == b
            s_old = s_in[b, h]
            o = o + jnp.where(mine, _dot(qd, s_old.astype(BF16)), 0.0)
            s_out[b, h] = s_old * math.exp(seq * lg) + _dot_tn(jnp.where(mine, kd, 0.0).astype(BF16), v)
        o_ref[:, sl] = _group_norm_gate(o, gn_ref[h], g_ref[:, sl]).astype(o_ref.dtype)


def _ret_sample(proj, row0, log_g, cos, sin, gn, state, batch, seq):
    rows = DEC_PAIR * seq
    blk0 = row0 // rows

    def col(group):
        return pl.BlockSpec((rows, RET_W), lambda i: (i + blk0, group))

    tab = pl.BlockSpec((rows, RET_DK // 2), lambda i: (0, 0))
    st = pl.BlockSpec((DEC_PAIR, RET_HEADS, RET_DK, RET_DV), lambda i: (i, 0, 0, 0))
    return pl.pallas_call(
        functools.partial(_ret_sample_kernel, log_g, seq),
        grid=(batch // DEC_PAIR,),
        in_specs=[col(0), col(1), col(2), col(3), tab, tab,
                  pl.BlockSpec((RET_HEADS, 1, RET_DV), lambda i: (0, 0, 0)), st],
        out_specs=[pl.BlockSpec((rows, RET_W), lambda i: (i, 0)), st],
        out_shape=[jax.ShapeDtypeStruct((batch * seq, RET_W), BF16),
                   jax.ShapeDtypeStruct(state.shape, F32)],
        compiler_params=_cp(("parallel",)),
        name="retention_sample",
    )(proj, proj, proj, proj, cos, sin, gn.reshape(RET_HEADS, 1, RET_DV), state)


def _split3(x):
    hi = x.astype(BF16)
    r1 = x - hi.astype(F32)
    mid = r1.astype(BF16)
    lo = (r1 - mid.astype(F32)).astype(BF16)
    return hi, mid, lo


def _column_scale(row_vec, width):
    n = row_vec.shape[-1]
    t = jnp.transpose(jnp.broadcast_to(row_vec, (LANES, n)))
    return jnp.concatenate([t] * (width // LANES), axis=-1)


def _gla_prompt_kernel(q_ref, k_ref, v_ref, g_ref, la_ref, gn_ref, o_ref, s_ref, state):
    c, sub = GLA_CHUNK, GLA_SUB
    nsub = c // sub
    i = pl.program_id(1)

    @pl.when(i == 0)
    def _():
        state[...] = jnp.zeros_like(state)

    ii = lax.broadcasted_iota(jnp.int32, (c, c), 0)
    jj = lax.broadcasted_iota(jnp.int32, (c, c), 1)
    causal = ii >= jj
    tri = jnp.where(causal, 1.0, 0.0).astype(BF16)
    blk = lax.broadcasted_iota(jnp.int32, (c, GLA_DK), 0) // sub
    for h in range(GLA_HEADS):
        ks = slice(h * GLA_DK, (h + 1) * GLA_DK)
        vs = slice(h * GLA_DV, (h + 1) * GLA_DV)
        q = q_ref[:, ks]
        k = k_ref[:, ks] * (GLA_DK ** -0.5)
        v = v_ref[:, vs].astype(BF16)
        hi, mid, lo = _split3(la_ref[:, ks])
        b = _dot(tri, hi) + _dot(tri, mid) + _dot(tri, lo)
        mids = [b[s * sub + sub // 2 - 1: s * sub + sub // 2, :] for s in range(nsub)]
        ref_lvl = jnp.concatenate([jnp.broadcast_to(m, (sub, GLA_DK)) for m in mids], axis=0)
        qd = (q * jnp.exp(b - ref_lvl)).astype(BF16)
        kd = k * jnp.exp(ref_lvl - b)
        rows = []
        for s in range(nsub):
            scale = jnp.where(blk <= s, jnp.exp(jnp.minimum(mids[s] - ref_lvl, 0.0)), 0.0)
            rows.append(_dot_nt(qd[s * sub:(s + 1) * sub], (kd * scale).astype(BF16)))
        att = jnp.where(causal, jnp.concatenate(rows, axis=0), 0.0)
        s_old = state[h]
        o = _dot(att.astype(BF16), v) + _dot((q * jnp.exp(b)).astype(BF16), s_old.astype(BF16))
        b_last = b[c - 1:c, :]
        k_rem = (k * jnp.exp(b_last - b)).astype(BF16)
        state[h] = s_old * _column_scale(jnp.exp(b_last), GLA_DV) + _dot_tn(k_rem, v)
        o_ref[:, vs] = _rms_gate(o, gn_ref[h], g_ref[:, vs]).astype(o_ref.dtype)

    @pl.when(i == pl.num_programs(1) - 1)
    def _():
        s_ref[0] = state[...]


def _gla_prompt(proj, log_a, gn, batch, seq):
    c = GLA_CHUNK
    nc = seq // c

    def col(start, width):
        return pl.BlockSpec((c, width), lambda b, i: (b * nc + i, start // width))

    return pl.pallas_call(
        _gla_prompt_kernel,
        grid=(batch, nc),
        in_specs=[col(COL_GQ, GLA_KW), col(COL_GK, GLA_KW), col(COL_GV, GLA_VW), col(COL_GG, GLA_VW),
                  pl.BlockSpec((c, GLA_KW), lambda b, i: (b * nc + i, 0)),
                  pl.BlockSpec((GLA_HEADS, 1, GLA_DV), lambda b, i: (0, 0, 0))],
        out_specs=[pl.BlockSpec((c, GLA_VW), lambda b, i: (b * nc + i, 0)),
                   pl.BlockSpec((1, GLA_HEADS, GLA_DK, GLA_DV), lambda b, i: (b, 0, 0, 0))],
        out_shape=[jax.ShapeDtypeStruct((batch * seq, GLA_VW), BF16),
                   jax.ShapeDtypeStruct((batch, GLA_HEADS, GLA_DK, GLA_DV), F32)],
        scratch_shapes=[pltpu.VMEM((GLA_HEADS, GLA_DK, GLA_DV), F32)],
        compiler_params=_cp(("parallel", "arbitrary")),
        name="gla_prompt",
    )(proj, proj, proj, proj, log_a, gn.reshape(GLA_HEADS, 1, GLA_DV))


def _gla_sample_kernel(seq, q_ref, k_ref, v_ref, g_ref, la_ref, gn_ref, s_in, o_ref, s_out):
    rows = DEC_PAIR * seq
    rid = lax.broadcasted_iota(jnp.int32, (rows, GLA_DK), 0)
    pos = rid % seq
    batch_of_row = rid // seq
    ii = lax.broadcasted_iota(jnp.int32, (rows, rows), 0)
    jj = lax.broadcasted_iota(jnp.int32, (rows, rows), 1)
    visible = (ii // seq == jj // seq) & (ii >= jj)
    for h in range(GLA_HEADS):
        ks = slice(h * GLA_DK, (h + 1) * GLA_DK)
        vs = slice(h * GLA_DV, (h + 1) * GLA_DV)
        la = la_ref[:, ks]
        b = la
        for d in range(1, seq):
            b = b + jnp.where(pos >= d, pltpu.roll(la, d, axis=0), 0.0)
        q = q_ref[:, ks]
        k = k_ref[:, ks] * (GLA_DK ** -0.5)
        v = v_ref[:, vs].astype(BF16)
        qb = (q * jnp.exp(b)).astype(BF16)
        kb = (k * jnp.exp(-b)).astype(BF16)
        att = jnp.where(visible, _dot_nt(qb, kb), 0.0)
        o = _dot(att.astype(BF16), v)
        for bi in range(DEC_PAIR):
            mine = batch_of_row == bi
            last = bi * seq + seq - 1
            b_last = b[last:last + 1, :]
            s_old = s_in[bi, h]
            o = o + jnp.where(mine[:, :1], _dot(qb, s_old.astype(BF16)), 0.0)
            k_rem = jnp.where(mine, k * jnp.exp(b_last - b), 0.0).astype(BF16)
            s_out[bi, h] = s_old * _column_scale(jnp.exp(b_last), GLA_DV) + _dot_tn(k_rem, v)
        o_ref[:, vs] = _rms_gate(o, gn_ref[h], g_ref[:, vs]).astype(o_ref.dtype)


def _gla_sample(proj, row0, log_a, gn, state, batch, seq):
    rows = DEC_PAIR * seq
    blk0 = row0 // rows
    st = pl.BlockSpec((DEC_PAIR, GLA_HEADS, GLA_DK, GLA_DV), lambda i: (i, 0, 0, 0))
    return pl.pallas_call(
        functools.partial(_gla_sample_kernel, seq),
        grid=(batch // DEC_PAIR,),
        in_specs=[pl.BlockSpec((rows, GLA_KW), lambda i: (i + blk0, COL_GQ // GLA_KW)),
                  pl.BlockSpec((rows, GLA_KW), lambda i: (i + blk0, COL_GK // GLA_KW)),
                  pl.BlockSpec((rows, GLA_VW), lambda i: (i + blk0, COL_GV // GLA_VW)),
                  pl.BlockSpec((rows, GLA_VW), lambda i: (i + blk0, COL_GG // GLA_VW)),
                  pl.BlockSpec((rows, GLA_KW), lambda i: (i + blk0, 0)),
                  pl.BlockSpec((GLA_HEADS, 1, GLA_DV), lambda i: (0, 0, 0)), st],
        out_specs=[pl.BlockSpec((rows, GLA_VW), lambda i: (i, 0)), st],
        out_shape=[jax.ShapeDtypeStruct((batch * seq, GLA_VW), BF16),
                   jax.ShapeDtypeStruct(state.shape, F32)],
        compiler_params=_cp(("parallel",)),
        name="gla_sample",
    )(proj, proj, proj, proj, log_a, gn.reshape(GLA_HEADS, 1, GLA_DV), state)


def _router_kernel(n_p_tiles, m_ref, xp_ref, xs_ref, g_ref, wr_ref, br_ref, x_ref, hm_ref, idx_ref, gate_ref):
    tr = m_ref.shape[0]
    i = pl.program_id(0)

    @pl.when(i < n_p_tiles)
    def _():
        x_ref[...] = xp_ref[...] + m_ref[...]

    @pl.when(i >= n_p_tiles)
    def _():
        x_ref[...] = xs_ref[...] + m_ref[...]

    hm = _rms_norm(x_ref[...], g_ref[...])
    hm_ref[...] = hm
    h1, h2, _ = _split3(hm)
    w = wr_ref[...]
    w1 = w.astype(BF16)
    w2 = (w - w1.astype(F32)).astype(BF16)
    logits = _dot(h1, w1) + _dot(h1, w2) + _dot(h2, w1) + br_ref[...]
    lane = lax.broadcasted_iota(jnp.int32, (tr, LANES), 1).astype(F32)
    neg, far = -1e30, 1e9
    is_group = lane < N_GROUPS
    gl = jnp.where(is_group, logits, neg)
    gmax = jnp.max(gl, axis=-1, keepdims=True)
    gidx = jnp.min(jnp.where(gl == gmax, lane, far), axis=-1, keepdims=True)
    gsum = jnp.sum(jnp.where(is_group, jnp.exp(gl - gmax), 0.0), axis=-1, keepdims=True)
    g_p = 1.0 / gsum
    lo = N_GROUPS + EXPERTS_PER_GROUP * gidx
    in_sel = (lane >= lo) & (lane < lo + EXPERTS_PER_GROUP)
    el = jnp.where(in_sel, logits, neg)
    emax = jnp.max(el, axis=-1, keepdims=True)
    e1 = jnp.min(jnp.where(el == emax, lane, far), axis=-1, keepdims=True)
    esum = jnp.sum(jnp.where(in_sel, jnp.exp(el - emax), 0.0), axis=-1, keepdims=True)
    el2 = jnp.where(lane == e1, neg, el)
    m2 = jnp.max(el2, axis=-1, keepdims=True)
    e2 = jnp.min(jnp.where(el2 == m2, lane, far), axis=-1, keepdims=True)
    p1 = 1.0 / esum
    p2 = jnp.exp(m2 - emax) / esum
    den = p1 + p2
    idx_ref[...] = jnp.where(lane == 0, e1 - N_GROUPS, jnp.where(lane == 1, e2 - N_GROUPS, 0.0)).astype(jnp.int32)
    gate_ref[...] = jnp.where(lane == 0, g_p * p1 / den, jnp.where(lane == 1, g_p * p2 / den, 0.0))


def _router(m, x_p, x_s, g, w_r, b_r):
    t, d = m.shape
    t_p = x_p.shape[0]
    tr = ROW_TILE
    p_spec, s_spec = _two_group_specs((d,), tr, t_p // tr)
    row = pl.BlockSpec((tr, d), lambda i: (i, 0))
    lane_row = pl.BlockSpec((tr, LANES), lambda i: (i, 0))
    return pl.pallas_call(
        functools.partial(_router_kernel, t_p // tr),
        grid=(t // tr,),
        in_specs=[row, p_spec, s_spec, pl.BlockSpec((1, d), lambda i: (0, 0)),
                  pl.BlockSpec((d, LANES), lambda i: (0, 0)), pl.BlockSpec((1, LANES), lambda i: (0, 0))],
        out_specs=[row, row, lane_row, lane_row],
        out_shape=[jax.ShapeDtypeStruct((t, d), F32), jax.ShapeDtypeStruct((t, d), F32),
                   jax.ShapeDtypeStruct((t, LANES), jnp.int32), jax.ShapeDtypeStruct((t, LANES), F32)],
        compiler_params=_cp(("arbitrary",)),
        name="moe_router",
    )(m, x_p, x_s, g.reshape(1, d), w_r, b_r)


def _moe_kernel(n_used, item_e, item_start, item_n, tok, dst,
                hm_hbm, wg_ref, wu_ref, wd_ref, y_hbm,
                rows_in, rows_out, hid, gate_acc, up_acc, sem_in, sem_out):
    del n_used, item_e
    i = pl.program_id(0)
    j = pl.program_id(1)
    n_items = pl.num_programs(0)
    n_steps = pl.num_programs(1)
    n_up = D_MODEL // MOE_TK
    max_blk = MOE_ROWS // MOE_BLK
    n, s0 = item_n[i], item_start[i]
    nxt = jnp.minimum(i + 1, n_items - 1)
    n_next, s_next = jnp.where(i + 1 < n_items, item_n[nxt], 0), item_start[nxt]
    prv = jnp.maximum(i - 1, 0)
    n_prev, s_prev = jnp.where(i > 0, item_n[prv], 0), item_start[prv]
    nblk = (n + MOE_BLK - 1) // MOE_BLK

    def gather(s, r):
        return pltpu.make_async_copy(hm_hbm.at[pl.ds(tok[s + r], 1), :], rows_in.at[pl.ds(r, 1), :], sem_in)

    def scatter(s, r):
        return pltpu.make_async_copy(rows_out.at[pl.ds(r, 1), :], y_hbm.at[pl.ds(dst[s + r], 1), :], sem_out)

    def gathered_group(c):
        rows = pl.ds(pl.multiple_of(c * MOE_DMA_UNROLL, MOE_DMA_UNROLL), MOE_DMA_UNROLL)
        return pltpu.make_async_copy(hm_hbm.at[pl.ds(0, MOE_DMA_UNROLL), :], rows_in.at[rows, :], sem_in)

    def scattered_group(c):
        rows = pl.ds(pl.multiple_of(c * MOE_DMA_UNROLL, MOE_DMA_UNROLL), MOE_DMA_UNROLL)
        return pltpu.make_async_copy(rows_out.at[rows, :], y_hbm.at[pl.ds(0, MOE_DMA_UNROLL), :], sem_out)

    def for_rows(count, fn, group_fn=None):
        groups = count // MOE_DMA_UNROLL

        def group(c, carry):
            if group_fn is None:
                for u in range(MOE_DMA_UNROLL):
                    fn(c * MOE_DMA_UNROLL + u)
            else:
                group_fn(c)
            return carry
        lax.fori_loop(0, groups, group, 0)

        def single(r, carry):
            fn(r)
            return carry
        lax.fori_loop(groups * MOE_DMA_UNROLL, count, single, 0)

    def for_row_count(fn):
        for k in range(1, max_blk + 1):
            @pl.when(nblk == k)
            def _(k=k):
                fn(k * MOE_BLK)

    @pl.when((i == 0) & (j == 0))
    def _():
        rows_in[...] = jnp.zeros_like(rows_in)
        gate_acc[...] = jnp.zeros_like(gate_acc)
        up_acc[...] = jnp.zeros_like(up_acc)
        for_rows(n, lambda r: gather(s0, r).start())

    @pl.when(j == 0)
    def _():
        for_rows(n, lambda r: gather(s0, r).wait(), lambda c: gathered_group(c).wait())

    @pl.when(j < n_up)
    def _():
        kcol = pl.multiple_of(j * MOE_TK, MOE_TK)

        def up(m):
            x = rows_in[0:m, pl.ds(kcol, MOE_TK)].astype(BF16)
            gate_acc[0:m, :] = jnp.where(j > 0, gate_acc[0:m, :], 0.0) + _dot(x, wg_ref[0].astype(BF16))
            up_acc[0:m, :] = jnp.where(j > 0, up_acc[0:m, :], 0.0) + _dot(x, wu_ref[0].astype(BF16))

            @pl.when(j == n_up - 1)
            def _():
                hid[0:m, :] = (_silu(gate_acc[0:m, :]) * up_acc[0:m, :]).astype(BF16)
        for_row_count(up)

    @pl.when(j == n_up)
    def _():
        for_rows(n_next, lambda r: gather(s_next, r).start())
        for_rows(n_prev, lambda r: scatter(s_prev, r).wait(), lambda c: scattered_group(c).wait())

    @pl.when(j >= n_up)
    def _():
        col = pl.multiple_of((j - n_up) * MOE_TN, MOE_TN)

        def down(m):
            rows_out[0:m, pl.ds(col, MOE_TN)] = _dot(hid[0:m, :], wd_ref[0].astype(BF16))
        for_row_count(down)

    @pl.when(j == n_steps - 1)
    def _():
        for_rows(n, lambda r: scatter(s0, r).start())

    @pl.when((j == n_steps - 1) & (i == n_items - 1))
    def _():
        for_rows(n, lambda r: scatter(s0, r).wait(), lambda c: scattered_group(c).wait())


def _moe_experts(hm, w_gate, w_up, w_down, items, tok, dst, n_assign):
    n_used, item_e, item_start, item_n = items
    n_up = D_MODEL // MOE_TK
    n_dn = D_MODEL // MOE_TN

    def up_map(i, j, nu, e, st, n, tok, dst):
        return (e[i], jnp.minimum(j, n_up - 1), 0)

    def dn_map(i, j, nu, e, st, n, tok, dst):
        return (e[i], 0, jnp.maximum(j - n_up, 0))

    return pl.pallas_call(
        _moe_kernel,
        grid_spec=pltpu.PrefetchScalarGridSpec(
            num_scalar_prefetch=6,
            grid=(n_used[0], n_up + n_dn),
            in_specs=[pl.BlockSpec(memory_space=pl.ANY),
                      pl.BlockSpec((1, MOE_TK, D_EXPERT), up_map),
                      pl.BlockSpec((1, MOE_TK, D_EXPERT), up_map),
                      pl.BlockSpec((1, D_EXPERT, MOE_TN), dn_map)],
            out_specs=pl.BlockSpec(memory_space=pl.ANY),
            scratch_shapes=[pltpu.VMEM((MOE_ROWS, D_MODEL), F32),
                            pltpu.VMEM((MOE_ROWS, D_MODEL), F32),
                            pltpu.VMEM((MOE_ROWS, D_EXPERT), BF16),
                            pltpu.VMEM((MOE_ROWS, D_EXPERT), F32),
                            pltpu.VMEM((MOE_ROWS, D_EXPERT), F32),
                            pltpu.SemaphoreType.DMA(()),
                            pltpu.SemaphoreType.DMA(())],
        ),
        out_shape=jax.ShapeDtypeStruct((n_assign, D_MODEL), F32),
        compiler_params=_cp(("arbitrary", "arbitrary")),
        name="moe_experts",
    )(n_used, item_e, item_start, item_n, tok, dst, hm, w_gate, w_up, w_down)


def _moe_plan(expert_idx):
    n_assign = expert_idx.size
    flat_e = expert_idx.reshape(n_assign)
    order = jnp.argsort(flat_e).astype(jnp.int32)
    counts = jnp.bincount(flat_e, length=N_EXPERTS).astype(jnp.int32)
    start = jnp.cumsum(counts) - counts
    per_e = (counts + MOE_ROWS - 1) // MOE_ROWS
    item_end = jnp.cumsum(per_e)
    n_items = n_assign // MOE_ROWS + N_EXPERTS
    ids = jnp.arange(n_items, dtype=jnp.int32)
    used = ids < item_end[-1]
    last = jnp.maximum(item_end[-1] - 1, 0)
    e_of = jnp.minimum(jnp.searchsorted(item_end, jnp.minimum(ids, last), side="right"), N_EXPERTS - 1).astype(jnp.int32)
    local = jnp.minimum(ids, last) - (item_end - per_e)[e_of]
    item_start = start[e_of] + local * MOE_ROWS
    item_n = jnp.where(used, jnp.clip(counts[e_of] - local * MOE_ROWS, 0, MOE_ROWS), 0)
    items = (item_end[-1:].astype(jnp.int32), e_of, item_start.astype(jnp.int32), item_n.astype(jnp.int32))
    token = order // TOP_K
    dst_row = (order % TOP_K) * (n_assign // TOP_K) + token
    return items, token, dst_row


def _combine_kernel(x_ref, y0_ref, y1_ref, gate_ref, o_ref, ob_ref):
    x = x_ref[...] + (y0_ref[...] * gate_ref[:, 0:1] + y1_ref[...] * gate_ref[:, 1:2])
    o_ref[...] = x
    ob_ref[...] = x.astype(BF16)


def _combine(x, y, gate):
    t, d = x.shape
    tr = ROW_TILE
    row = pl.BlockSpec((tr, d), lambda i: (i, 0))
    return pl.pallas_call(
        _combine_kernel,
        grid=(t // tr,),
        in_specs=[row, row, pl.BlockSpec((tr, d), lambda i: (i + t // tr, 0)),
                  pl.BlockSpec((tr, LANES), lambda i: (i, 0))],
        out_specs=[row, row],
        out_shape=[jax.ShapeDtypeStruct((t, d), F32), jax.ShapeDtypeStruct((t, d), BF16)],
        compiler_params=_cp(("parallel",)),
        name="moe_combine",
    )(x, y, y, gate)


def _rope_tables(pos):
    half = RET_DK // 2
    inv = ROPE_BASE ** (-jnp.arange(half, dtype=F32) / half)
    ang = pos.astype(F32)[:, None] * inv[None, :]
    return jnp.cos(ang), jnp.sin(ang)


def kernel(x_prompt, x_sample, state_ret, state_gla, p_prompt, p_sample, g_mix, w_in, w_gla_up, b_gla,
           ret_norm_g, gla_norm_g, w_out, g_moe, w_rg, b_rg, w_re, b_re, w_gate, w_up, w_down, w_pg, w_pp,
           g_final):
    n_p, l_p, d = x_prompt.shape
    n_s, l_s, _ = x_sample.shape
    depth = g_mix.shape[0]
    t_p, t_s = n_p * l_p, n_s * l_s
    t = t_p + t_s

    log_g_py = [math.log1p(-(2.0 ** (-5.0 - h))) for h in range(RET_HEADS)]
    cos_p, sin_p = _rope_tables(jnp.arange(l_p, dtype=jnp.int32))
    cos_s, sin_s = _rope_tables(PAST_LEN + jnp.arange(l_s, dtype=jnp.int32))
    cos_s, sin_s = jnp.tile(cos_s, (DEC_PAIR, 1)), jnp.tile(sin_s, (DEC_PAIR, 1))

    x_p = x_prompt.reshape(t_p, d)
    x_s = x_sample.reshape(t_s, d)
    x = None
    ret_p, ret_s, gla_p, gla_s = [], [], [], []
    for l in range(depth):
        w_in_t = jnp.swapaxes(w_in[l], 0, 1)
        w_ga = jnp.pad(w_in_t[N_MAIN:], ((0, LANES - GLA_RANK), (0, 0))).astype(BF16)
        w_gup = jnp.pad(w_gla_up[l], ((0, LANES - GLA_RANK), (0, 0))).astype(BF16)
        w_r = jnp.pad(jnp.concatenate([w_rg[l], w_re[l]], axis=1),
                      ((0, 0), (0, LANES - N_GROUPS - N_EXPERTS)))
        b_r = jnp.pad(jnp.concatenate([b_rg[l], b_re[l]]), (0, LANES - N_GROUPS - N_EXPERTS)).reshape(1, LANES)
        p = jnp.concatenate([p_prompt[l].reshape(t_p, -1), p_sample[l].reshape(t_s, -1)], axis=0)
        if x is not None:
            x_p, x_s = x[:t_p], x[t_p:]

        h, log_a = _norm_in(x_p, x_s, g_mix[l], w_ga, w_gup, b_gla[l])
        proj = _in_proj(h, w_in_t, N_MAIN, t // 16, 1024)
        ro_p, sr_p = _ret_prompt(proj, log_g_py, cos_p, sin_p, ret_norm_g[l], n_p, l_p)
        go_p, sg_p = _gla_prompt(proj, log_a, gla_norm_g[l], n_p, l_p)
        ro_s, sr_s = _ret_sample(proj, t_p, log_g_py, cos_s, sin_s, ret_norm_g[l], state_ret[l], n_s, l_s)
        go_s, sg_s = _gla_sample(proj, t_p, log_a, gla_norm_g[l], state_gla[l], n_s, l_s)
        ret_p.append(sr_p)
        ret_s.append(sr_s)
        gla_p.append(sg_p)
        gla_s.append(sg_s)
        mix = _out_proj(ro_p, ro_s, go_p, go_s, w_out[l], 1024)

        x, hm, idx, gate = _router(mix, x_p, x_s, g_moe[l], w_r, b_r)
        items, tok, dst = _moe_plan(idx[:, :TOP_K])
        y = _moe_experts(hm, w_gate[l], w_up[l], w_down[l], items, tok, dst, t * TOP_K)
        x, xb = _combine(x, y, gate)

        x = _ple(xb, x, p, w_pg[l], w_pp[l], t // 8, 512)

    y_prompt = _norm_out(x, g_final, 0, t_p).reshape(n_p, l_p, d)
    y_sample = _norm_out(x, g_final, t_p, t_s).reshape(n_s, l_s, d)
    return (y_prompt, y_sample,
            jnp.stack(ret_p).astype(state_ret.dtype), jnp.stack(ret_s).astype(state_ret.dtype),
            jnp.stack(gla_p).astype(state_gla.dtype), jnp.stack(gla_s).astype(state_gla.dtype))
```

```python
import functools
import math

import jax
import jax.numpy as jnp
from jax import lax
from jax.experimental import pallas as pl
from jax.experimental.pallas import tpu as pltpu

F32 = jnp.float32
BF16 = jnp.bfloat16

D_MODEL = 4096
RET_HEADS = 8
RET_DK = 256
RET_DV = 256
GLA_HEADS = 4
GLA_DK = 256
GLA_DV = 512
GLA_RANK = 16
GLA_GATE_TEMP = 16.0
ROPE_BASE = 10000.0
PAST_LEN = 16384
N_GROUPS = 4
EXPERTS_PER_GROUP = 8
N_EXPERTS = N_GROUPS * EXPERTS_PER_GROUP
TOP_K = 2
D_EXPERT = D_MODEL // 4
EPS = 1e-6

RET_W = RET_HEADS * RET_DK
GLA_KW = GLA_HEADS * GLA_DK
GLA_VW = GLA_HEADS * GLA_DV
N_MAIN = 4 * RET_W + 2 * GLA_KW + 2 * GLA_VW
COL_GQ = 4 * RET_W
COL_GK = COL_GQ + GLA_KW
COL_GV = COL_GK + GLA_KW
COL_GG = COL_GV + GLA_VW

LANES = 128
ROW_TILE = 256
LHS_TILE = 512
RET_CHUNK = 256
GLA_CHUNK = 128
GLA_SUB = 32
DEC_PAIR = 4
MOE_ROWS = 768
MOE_BLK = 128
MOE_TJ = 256
MOE_TN = 1024
MOE_DMA_UNROLL = 8
VMEM_LIMIT = 56 * 1024 * 1024


def _cp(semantics, vmem=VMEM_LIMIT):
    return pltpu.CompilerParams(dimension_semantics=semantics, vmem_limit_bytes=vmem)


def _sigmoid(x):
    return 1.0 / (1.0 + jnp.exp(-x))


def _silu(x):
    return x * _sigmoid(x)


def _dot(a, b):
    return jnp.dot(a, b, preferred_element_type=F32)


def _dot_nt(a, b):
    return lax.dot_general(a, b, (((1,), (1,)), ((), ())), preferred_element_type=F32)


def _dot_tn(a, b):
    return lax.dot_general(a, b, (((0,), (0,)), ((), ())), preferred_element_type=F32)


def _rms_norm(x, g):
    ms = jnp.mean(x * x, axis=-1, keepdims=True)
    return x * lax.rsqrt(ms + EPS) * g


_HIGH_HALF = 0xFFFF0000


def _pack_bf16_pairs(x):
    half = x.shape[-1] // 2
    bits = lax.bitcast_convert_type(x.astype(BF16).astype(F32), jnp.uint32)
    return (bits[:, :half] >> 16) | (bits[:, half:] & jnp.uint32(_HIGH_HALF))


def _unpack_bf16_pairs(w):
    lo = lax.bitcast_convert_type(w << 16, F32)
    hi = lax.bitcast_convert_type(w & jnp.uint32(_HIGH_HALF), F32)
    return jnp.concatenate([lo, hi], axis=-1).astype(BF16)


def _two_group_specs(shape_tail, tile, n_p_tiles):
    zeros = (0,) * len(shape_tail)

    def p_map(*idx):
        return (jnp.minimum(idx[-1], n_p_tiles - 1),) + zeros

    def s_map(*idx):
        return (jnp.maximum(idx[-1] - n_p_tiles, 0),) + zeros

    return pl.BlockSpec((tile,) + shape_tail, p_map), pl.BlockSpec((tile,) + shape_tail, s_map)


def _log_sigmoid(z):
    return jnp.minimum(z, 0.0) - jnp.log(1.0 + jnp.exp(-jnp.abs(z)))


def _norm_in_kernel(n_p_tiles, xp_ref, xs_ref, g_ref, wga_ref, wup_ref, b_ref, h_ref, la_ref):
    i = pl.program_id(0)

    def emit(x):
        h = _rms_norm(x, g_ref[...]).astype(BF16)
        h_ref[...] = h
        ga = _dot_nt(h, wga_ref[...])
        z = _dot(ga.astype(BF16), wup_ref[...]) + b_ref[...]
        la_ref[...] = _log_sigmoid(z) * (1.0 / GLA_GATE_TEMP)

    @pl.when(i < n_p_tiles)
    def _():
        emit(xp_ref[...])

    @pl.when(i >= n_p_tiles)
    def _():
        emit(xs_ref[...])


def _norm_in(x_p, x_s, g, w_ga, w_up, b):
    (t_p, d), t_s = x_p.shape, x_s.shape[0]
    tr = ROW_TILE
    p_spec, s_spec = _two_group_specs((d,), tr, t_p // tr)
    return pl.pallas_call(
        functools.partial(_norm_in_kernel, t_p // tr),
        grid=((t_p + t_s) // tr,),
        in_specs=[p_spec, s_spec, pl.BlockSpec((1, d), lambda i: (0, 0)),
                  pl.BlockSpec((LANES, d), lambda i: (0, 0)),
                  pl.BlockSpec((LANES, GLA_KW), lambda i: (0, 0)),
                  pl.BlockSpec((1, GLA_KW), lambda i: (0, 0))],
        out_specs=[pl.BlockSpec((tr, d), lambda i: (i, 0)), pl.BlockSpec((tr, GLA_KW), lambda i: (i, 0))],
        out_shape=[jax.ShapeDtypeStruct((t_p + t_s, d), BF16), jax.ShapeDtypeStruct((t_p + t_s, GLA_KW), F32)],
        compiler_params=_cp(("arbitrary",)),
        name="norm_in",
    )(x_p, x_s, g.reshape(1, d), w_ga, w_up, b.reshape(1, GLA_KW))


def _norm_out_kernel(x_ref, g_ref, o_ref):
    o_ref[...] = _rms_norm(x_ref[...], g_ref[...])


def _norm_out(x, g, row0, n_rows):
    d = x.shape[1]
    tr = ROW_TILE
    blk0 = row0 // tr
    return pl.pallas_call(
        _norm_out_kernel,
        grid=(n_rows // tr,),
        in_specs=[pl.BlockSpec((tr, d), lambda i: (i + blk0, 0)), pl.BlockSpec((1, d), lambda i: (0, 0))],
        out_specs=pl.BlockSpec((tr, d), lambda i: (i, 0)),
        out_shape=jax.ShapeDtypeStruct((n_rows, d), F32),
        compiler_params=_cp(("parallel",)),
        name="norm_out",
    )(x, g.reshape(1, d))


def _in_proj_kernel(a_ref, wt_ref, o_ref):
    o_ref[...] = _dot_nt(a_ref[...], wt_ref[...].astype(BF16))


def _in_proj(h, w_t, n_cols, tm, tn):
    t, d = h.shape
    return pl.pallas_call(
        _in_proj_kernel,
        grid=(n_cols // tn, t // tm),
        in_specs=[pl.BlockSpec((tm, d), lambda j, i: (i, 0)), pl.BlockSpec((tn, d), lambda j, i: (j, 0))],
        out_specs=pl.BlockSpec((tm, tn), lambda j, i: (i, j)),
        out_shape=jax.ShapeDtypeStruct((t, n_cols), F32),
        compiler_params=_cp(("parallel", "parallel")),
        name="in_proj",
    )(h, w_t)


def _out_proj_kernel(n_p_tiles, rp_ref, rs_ref, gp_ref, gs_ref, w_ref, o_ref):
    i = pl.program_id(1)

    def emit(r_ref, g_ref):
        o_ref[...] = (_dot(r_ref[...], w_ref[:RET_W, :].astype(BF16))
                      + _dot(g_ref[...], w_ref[RET_W:, :].astype(BF16)))

    @pl.when(i < n_p_tiles)
    def _():
        emit(rp_ref, gp_ref)

    @pl.when(i >= n_p_tiles)
    def _():
        emit(rs_ref, gs_ref)


def _out_proj(ro_p, ro_s, go_p, go_s, w, tn):
    t_p, t_s = ro_p.shape[0], ro_s.shape[0]
    d_in, d_out = w.shape
    tm = LHS_TILE
    rp_spec, rs_spec = _two_group_specs((RET_W,), tm, t_p // tm)
    gp_spec, gs_spec = _two_group_specs((GLA_VW,), tm, t_p // tm)
    return pl.pallas_call(
        functools.partial(_out_proj_kernel, t_p // tm),
        grid=(d_out // tn, (t_p + t_s) // tm),
        in_specs=[rp_spec, rs_spec, gp_spec, gs_spec, pl.BlockSpec((d_in, tn), lambda j, i: (0, j))],
        out_specs=pl.BlockSpec((tm, tn), lambda j, i: (i, j)),
        out_shape=jax.ShapeDtypeStruct((t_p + t_s, d_out), F32),
        compiler_params=_cp(("arbitrary", "arbitrary")),
        name="out_proj",
    )(ro_p, ro_s, go_p, go_s, w)


def _ple_kernel(a_ref, w_ref, p_ref, wp_ref, r_ref, o_ref):
    gate = _sigmoid(_dot(a_ref[...], w_ref[...].astype(BF16)))
    emb = _dot(p_ref[...].astype(BF16), wp_ref[...].astype(BF16))
    o_ref[...] = r_ref[...] + gate * emb


def _ple(xb, x, p, w_pg, w_pp, tm, tn):
    t, d = x.shape
    kp = p.shape[1]
    o_spec = pl.BlockSpec((tm, tn), lambda j, i: (i, j))
    return pl.pallas_call(
        _ple_kernel,
        grid=(d // tn, t // tm),
        in_specs=[pl.BlockSpec((tm, d), lambda j, i: (i, 0)), pl.BlockSpec((d, tn), lambda j, i: (0, j)),
                  pl.BlockSpec((tm, kp), lambda j, i: (i, 0)), pl.BlockSpec((kp, tn), lambda j, i: (0, j)),
                  o_spec],
        out_specs=o_spec,
        out_shape=jax.ShapeDtypeStruct((t, d), F32),
        compiler_params=_cp(("parallel", "parallel")),
        name="ple",
    )(xb, w_pg, p, w_pp, x)


def _rotary(x, cos, sin):
    half = x.shape[-1] // 2
    x1, x2 = x[:, :half], x[:, half:]
    return jnp.concatenate([x1 * cos - x2 * sin, x1 * sin + x2 * cos], axis=-1)


def _group_norm_gate(o, gain, gate):
    mu = jnp.mean(o, axis=-1, keepdims=True)
    d = o - mu
    var = jnp.mean(d * d, axis=-1, keepdims=True)
    return d * lax.rsqrt(var + EPS) * gain * _silu(gate)


def _rms_gate(o, gain, gate):
    ms = jnp.mean(o * o, axis=-1, keepdims=True)
    return o * lax.rsqrt(ms + EPS) * gain * _silu(gate)


def _ret_prompt_kernel(log_g, q_ref, k_ref, v_ref, g_ref, cos_ref, sin_ref, gn_ref, o_ref, s_ref,
                       state, intra_tab, q_tab, k_tab):
    c = RET_CHUNK
    b = pl.program_id(0)
    i = pl.program_id(1)

    @pl.when((b == 0) & (i == 0))
    def _():
        row = lax.broadcasted_iota(jnp.int32, (c, RET_DK), 0).astype(F32)
        ii = lax.broadcasted_iota(jnp.int32, (c, c), 0)
        jj = lax.broadcasted_iota(jnp.int32, (c, c), 1)
        diff = (ii - jj).astype(F32)
        for h in range(RET_HEADS):
            intra_tab[h] = jnp.where(diff >= 0, jnp.exp(jnp.maximum(diff, 0.0) * log_g[h]), 0.0)
            q_tab[h] = jnp.exp((row + 1.0) * log_g[h])
            k_tab[h] = jnp.exp((c - 1.0 - row) * log_g[h])

    @pl.when(i == 0)
    def _():
        state[...] = jnp.zeros_like(state)

    cos, sin = cos_ref[...], sin_ref[...]
    for h in range(RET_HEADS):
        sl = slice(h * RET_DK, (h + 1) * RET_DK)
        q = _rotary(q_ref[:, sl], cos, sin)
        k = _rotary(k_ref[:, sl], cos, sin) * (RET_DK ** -0.5)
        v = v_ref[:, sl].astype(BF16)
        att = _dot_nt(q.astype(BF16), k.astype(BF16)) * intra_tab[h]
        s_old = state[h]
        o = _dot(att.astype(BF16), v) + _dot((q * q_tab[h]).astype(BF16), s_old.astype(BF16))
        state[h] = s_old * math.exp(c * log_g[h]) + _dot_tn((k * k_tab[h]).astype(BF16), v)
        o_ref[:, sl] = _group_norm_gate(o, gn_ref[h], g_ref[:, sl]).astype(o_ref.dtype)

    @pl.when(i == pl.num_programs(1) - 1)
    def _():
        s_ref[0] = state[...]


def _ret_prompt(proj, log_g, cos, sin, gn, batch, seq):
    c = RET_CHUNK
    nc = seq // c

    def col(group):
        return pl.BlockSpec((c, RET_W), lambda b, i: (b * nc + i, group))

    tab = pl.BlockSpec((c, RET_DK // 2), lambda b, i: (i, 0))
    return pl.pallas_call(
        functools.partial(_ret_prompt_kernel, log_g),
        grid=(batch, nc),
        in_specs=[col(0), col(1), col(2), col(3), tab, tab,
                  pl.BlockSpec((RET_HEADS, 1, RET_DV), lambda b, i: (0, 0, 0))],
        out_specs=[pl.BlockSpec((c, RET_W), lambda b, i: (b * nc + i, 0)),
                   pl.BlockSpec((1, RET_HEADS, RET_DK, RET_DV), lambda b, i: (b, 0, 0, 0))],
        out_shape=[jax.ShapeDtypeStruct((batch * seq, RET_W), BF16),
                   jax.ShapeDtypeStruct((batch, RET_HEADS, RET_DK, RET_DV), F32)],
        scratch_shapes=[pltpu.VMEM((RET_HEADS, RET_DK, RET_DV), F32),
                        pltpu.VMEM((RET_HEADS, c, c), F32),
                        pltpu.VMEM((RET_HEADS, c, RET_DK), F32),
                        pltpu.VMEM((RET_HEADS, c, RET_DK), F32)],
        compiler_params=_cp(("arbitrary", "arbitrary")),
        name="retention_prompt",
    )(proj, proj, proj, proj, cos, sin, gn.reshape(RET_HEADS, 1, RET_DV))


def _ret_sample_kernel(log_g, seq, q_ref, k_ref, v_ref, g_ref, cos_ref, sin_ref, gn_ref, s_in, o_ref, s_out):
    rows = DEC_PAIR * seq
    cos, sin = cos_ref[...], sin_ref[...]
    rid = lax.broadcasted_iota(jnp.int32, (rows, RET_DK), 0)
    pos = (rid % seq).astype(F32)
    batch_of_row = rid // seq
    ii = lax.broadcasted_iota(jnp.int32, (rows, rows), 0)
    jj = lax.broadcasted_iota(jnp.int32, (rows, rows), 1)
    visible = (ii // seq == jj // seq) & (ii >= jj)
    diff = jnp.maximum(ii - jj, 0).astype(F32)
    for h in range(RET_HEADS):
        lg = log_g[h]
        sl = slice(h * RET_DK, (h + 1) * RET_DK)
        q = _rotary(q_ref[:, sl], cos, sin)
        k = _rotary(k_ref[:, sl], cos, sin) * (RET_DK ** -0.5)
        v = v_ref[:, sl].astype(BF16)
        intra = jnp.where(visible, jnp.exp(diff * lg), 0.0)
        att = _dot_nt(q.astype(BF16), k.astype(BF16)) * intra
        o = _dot(att.astype(BF16), v)
        qd = (q * jnp.exp((pos + 1.0) * lg)).astype(BF16)
        kd = k * jnp.exp((seq - 1.0 - pos) * lg)
        for b in range(DEC_PAIR):
            mine = batch_of_row == b
            s_old = s_in[b, h]
            o = o + jnp.where(mine, _dot(qd, s_old.astype(BF16)), 0.0)
            s_out[b, h] = s_old * math.exp(seq * lg) + _dot_tn(jnp.where(mine, kd, 0.0).astype(BF16), v)
        o_ref[:, sl] = _group_norm_gate(o, gn_ref[h], g_ref[:, sl]).astype(o_ref.dtype)


def _ret_sample(proj, row0, log_g, cos, sin, gn, state, batch, seq):
    rows = DEC_PAIR * seq
    blk0 = row0 // rows

    def col(group):
        return pl.BlockSpec((rows, RET_W), lambda i: (i + blk0, group))

    tab = pl.BlockSpec((rows, RET_DK // 2), lambda i: (0, 0))
    st = pl.BlockSpec((DEC_PAIR, RET_HEADS, RET_DK, RET_DV), lambda i: (i, 0, 0, 0))
    return pl.pallas_call(
        functools.partial(_ret_sample_kernel, log_g, seq),
        grid=(batch // DEC_PAIR,),
        in_specs=[col(0), col(1), col(2), col(3), tab, tab,
                  pl.BlockSpec((RET_HEADS, 1, RET_DV), lambda i: (0, 0, 0)), st],
        out_specs=[pl.BlockSpec((rows, RET_W), lambda i: (i, 0)), st],
        out_shape=[jax.ShapeDtypeStruct((batch * seq, RET_W), BF16),
                   jax.ShapeDtypeStruct(state.shape, F32)],
        compiler_params=_cp(("parallel",)),
        name="retention_sample",
    )(proj, proj, proj, proj, cos, sin, gn.reshape(RET_HEADS, 1, RET_DV), state)


def _split3(x):
    hi = x.astype(BF16)
    r1 = x - hi.astype(F32)
    mid = r1.astype(BF16)
    lo = (r1 - mid.astype(F32)).astype(BF16)
    return hi, mid, lo


def _column_scale(row_vec, width):
    n = row_vec.shape[-1]
    t = jnp.transpose(jnp.broadcast_to(row_vec, (LANES, n)))
    return jnp.concatenate([t] * (width // LANES), axis=-1)


def _gla_prompt_kernel(q_ref, k_ref, v_ref, g_ref, la_ref, gn_ref, o_ref, s_ref, state):
    c, sub = GLA_CHUNK, GLA_SUB
    nsub = c // sub
    i = pl.program_id(1)

    @pl.when(i == 0)
    def _():
        state[...] = jnp.zeros_like(state)

    ii = lax.broadcasted_iota(jnp.int32, (c, c), 0)
    jj = lax.broadcasted_iota(jnp.int32, (c, c), 1)
    causal = ii >= jj
    tri = jnp.where(causal, 1.0, 0.0).astype(BF16)
    blk = lax.broadcasted_iota(jnp.int32, (c, GLA_DK), 0) // sub
    for h in range(GLA_HEADS):
        ks = slice(h * GLA_DK, (h + 1) * GLA_DK)
        vs = slice(h * GLA_DV, (h + 1) * GLA_DV)
        q = q_ref[:, ks]
        k = k_ref[:, ks] * (GLA_DK ** -0.5)
        v = v_ref[:, vs].astype(BF16)
        hi, mid, lo = _split3(la_ref[:, ks])
        b = _dot(tri, hi) + _dot(tri, mid) + _dot(tri, lo)
        mids = [b[s * sub + sub // 2 - 1: s * sub + sub // 2, :] for s in range(nsub)]
        ref_lvl = jnp.concatenate([jnp.broadcast_to(m, (sub, GLA_DK)) for m in mids], axis=0)
        qd = (q * jnp.exp(b - ref_lvl)).astype(BF16)
        kd = k * jnp.exp(ref_lvl - b)
        rows = []
        for s in range(nsub):
            scale = jnp.where(blk <= s, jnp.exp(jnp.minimum(mids[s] - ref_lvl, 0.0)), 0.0)
            rows.append(_dot_nt(qd[s * sub:(s + 1) * sub], (kd * scale).astype(BF16)))
        att = jnp.where(causal, jnp.concatenate(rows, axis=0), 0.0)
        s_old = state[h]
        o = _dot(att.astype(BF16), v) + _dot((q * jnp.exp(b)).astype(BF16), s_old.astype(BF16))
        b_last = b[c - 1:c, :]
        k_rem = (k * jnp.exp(b_last - b)).astype(BF16)
        state[h] = s_old * _column_scale(jnp.exp(b_last), GLA_DV) + _dot_tn(k_rem, v)
        o_ref[:, vs] = _rms_gate(o, gn_ref[h], g_ref[:, vs]).astype(o_ref.dtype)

    @pl.when(i == pl.num_programs(1) - 1)
    def _():
        s_ref[0] = state[...]


def _gla_prompt(proj, log_a, gn, batch, seq):
    c = GLA_CHUNK
    nc = seq // c

    def col(start, width):
        return pl.BlockSpec((c, width), lambda b, i: (b * nc + i, start // width))

    return pl.pallas_call(
        _gla_prompt_kernel,
        grid=(batch, nc),
        in_specs=[col(COL_GQ, GLA_KW), col(COL_GK, GLA_KW), col(COL_GV, GLA_VW), col(COL_GG, GLA_VW),
                  pl.BlockSpec((c, GLA_KW), lambda b, i: (b * nc + i, 0)),
                  pl.BlockSpec((GLA_HEADS, 1, GLA_DV), lambda b, i: (0, 0, 0))],
        out_specs=[pl.BlockSpec((c, GLA_VW), lambda b, i: (b * nc + i, 0)),
                   pl.BlockSpec((1, GLA_HEADS, GLA_DK, GLA_DV), lambda b, i: (b, 0, 0, 0))],
        out_shape=[jax.ShapeDtypeStruct((batch * seq, GLA_VW), BF16),
                   jax.ShapeDtypeStruct((batch, GLA_HEADS, GLA_DK, GLA_DV), F32)],
        scratch_shapes=[pltpu.VMEM((GLA_HEADS, GLA_DK, GLA_DV), F32)],
        compiler_params=_cp(("parallel", "arbitrary")),
        name="gla_prompt",
    )(proj, proj, proj, proj, log_a, gn.reshape(GLA_HEADS, 1, GLA_DV))


def _gla_sample_kernel(seq, q_ref, k_ref, v_ref, g_ref, la_ref, gn_ref, s_in, o_ref, s_out):
    rows = DEC_PAIR * seq
    rid = lax.broadcasted_iota(jnp.int32, (rows, GLA_DK), 0)
    pos = rid % seq
    batch_of_row = rid // seq
    ii = lax.broadcasted_iota(jnp.int32, (rows, rows), 0)
    jj = lax.broadcasted_iota(jnp.int32, (rows, rows), 1)
    visible = (ii // seq == jj // seq) & (ii >= jj)
    for h in range(GLA_HEADS):
        ks = slice(h * GLA_DK, (h + 1) * GLA_DK)
        vs = slice(h * GLA_DV, (h + 1) * GLA_DV)
        la = la_ref[:, ks]
        b = la
        for d in range(1, seq):
            b = b + jnp.where(pos >= d, pltpu.roll(la, d, axis=0), 0.0)
        q = q_ref[:, ks]
        k = k_ref[:, ks] * (GLA_DK ** -0.5)
        v = v_ref[:, vs].astype(BF16)
        qb = (q * jnp.exp(b)).astype(BF16)
        kb = (k * jnp.exp(-b)).astype(BF16)
        att = jnp.where(visible, _dot_nt(qb, kb), 0.0)
        o = _dot(att.astype(BF16), v)
        for bi in range(DEC_PAIR):
            mine = batch_of_row == bi
            last = bi * seq + seq - 1
            b_last = b[last:last + 1, :]
            s_old = s_in[bi, h]
            o = o + jnp.where(mine[:, :1], _dot(qb, s_old.astype(BF16)), 0.0)
            k_rem = jnp.where(mine, k * jnp.exp(b_last - b), 0.0).astype(BF16)
            s_out[bi, h] = s_old * _column_scale(jnp.exp(b_last), GLA_DV) + _dot_tn(k_rem, v)
        o_ref[:, vs] = _rms_gate(o, gn_ref[h], g_ref[:, vs]).astype(o_ref.dtype)


def _gla_sample(proj, row0, log_a, gn, state, batch, seq):
    rows = DEC_PAIR * seq
    blk0 = row0 // rows
    st = pl.BlockSpec((DEC_PAIR, GLA_HEADS, GLA_DK, GLA_DV), lambda i: (i, 0, 0, 0))
    return pl.pallas_call(
        functools.partial(_gla_sample_kernel, seq),
        grid=(batch // DEC_PAIR,),
        in_specs=[pl.BlockSpec((rows, GLA_KW), lambda i: (i + blk0, COL_GQ // GLA_KW)),
                  pl.BlockSpec((rows, GLA_KW), lambda i: (i + blk0, COL_GK // GLA_KW)),
                  pl.BlockSpec((rows, GLA_VW), lambda i: (i + blk0, COL_GV // GLA_VW)),
                  pl.BlockSpec((rows, GLA_VW), lambda i: (i + blk0, COL_GG // GLA_VW)),
                  pl.BlockSpec((rows, GLA_KW), lambda i: (i + blk0, 0)),
                  pl.BlockSpec((GLA_HEADS, 1, GLA_DV), lambda i: (0, 0, 0)), st],
        out_specs=[pl.BlockSpec((rows, GLA_VW), lambda i: (i, 0)), st],
        out_shape=[jax.ShapeDtypeStruct((batch * seq, GLA_VW), BF16),
                   jax.ShapeDtypeStruct(state.shape, F32)],
        compiler_params=_cp(("parallel",)),
        name="gla_sample",
    )(proj, proj, proj, proj, log_a, gn.reshape(GLA_HEADS, 1, GLA_DV), state)


def _router_kernel(n_p_tiles, m_ref, xp_ref, xs_ref, g_ref, wr_ref, br_ref, x_ref, hm_ref, idx_ref, gate_ref):
    tr = m_ref.shape[0]
    i = pl.program_id(0)

    @pl.when(i < n_p_tiles)
    def _():
        x_ref[...] = xp_ref[...] + m_ref[...]

    @pl.when(i >= n_p_tiles)
    def _():
        x_ref[...] = xs_ref[...] + m_ref[...]

    hm = _rms_norm(x_ref[...], g_ref[...])
    hm_ref[...] = _pack_bf16_pairs(hm)
    h1, h2, _ = _split3(hm)
    w = wr_ref[...]
    w1 = w.astype(BF16)
    w2 = (w - w1.astype(F32)).astype(BF16)
    logits = _dot(h1, w1) + _dot(h1, w2) + _dot(h2, w1) + br_ref[...]
    lane = lax.broadcasted_iota(jnp.int32, (tr, LANES), 1).astype(F32)
    neg, far = -1e30, 1e9
    is_group = lane < N_GROUPS
    gl = jnp.where(is_group, logits, neg)
    gmax = jnp.max(gl, axis=-1, keepdims=True)
    gidx = jnp.min(jnp.where(gl == gmax, lane, far), axis=-1, keepdims=True)
    gsum = jnp.sum(jnp.where(is_group, jnp.exp(gl - gmax), 0.0), axis=-1, keepdims=True)
    g_p = 1.0 / gsum
    lo = N_GROUPS + EXPERTS_PER_GROUP * gidx
    in_sel = (lane >= lo) & (lane < lo + EXPERTS_PER_GROUP)
    el = jnp.where(in_sel, logits, neg)
    emax = jnp.max(el, axis=-1, keepdims=True)
    e1 = jnp.min(jnp.where(el == emax, lane, far), axis=-1, keepdims=True)
    esum = jnp.sum(jnp.where(in_sel, jnp.exp(el - emax), 0.0), axis=-1, keepdims=True)
    el2 = jnp.where(lane == e1, neg, el)
    m2 = jnp.max(el2, axis=-1, keepdims=True)
    e2 = jnp.min(jnp.where(el2 == m2, lane, far), axis=-1, keepdims=True)
    p1 = 1.0 / esum
    p2 = jnp.exp(m2 - emax) / esum
    den = p1 + p2
    idx_ref[...] = jnp.where(lane == 0, e1 - N_GROUPS, jnp.where(lane == 1, e2 - N_GROUPS, 0.0)).astype(jnp.int32)
    gate_ref[...] = jnp.where(lane == 0, g_p * p1 / den, jnp.where(lane == 1, g_p * p2 / den, 0.0))


def _router(m, x_p, x_s, g, w_r, b_r):
    t, d = m.shape
    t_p = x_p.shape[0]
    tr = ROW_TILE
    p_spec, s_spec = _two_group_specs((d,), tr, t_p // tr)
    row = pl.BlockSpec((tr, d), lambda i: (i, 0))
    lane_row = pl.BlockSpec((tr, LANES), lambda i: (i, 0))
    return pl.pallas_call(
        functools.partial(_router_kernel, t_p // tr),
        grid=(t // tr,),
        in_specs=[row, p_spec, s_spec, pl.BlockSpec((1, d), lambda i: (0, 0)),
                  pl.BlockSpec((d, LANES), lambda i: (0, 0)), pl.BlockSpec((1, LANES), lambda i: (0, 0))],
        out_specs=[row, pl.BlockSpec((tr, d // 2), lambda i: (i, 0)), lane_row, lane_row],
        out_shape=[jax.ShapeDtypeStruct((t, d), F32), jax.ShapeDtypeStruct((t, d // 2), jnp.uint32),
                   jax.ShapeDtypeStruct((t, LANES), jnp.int32), jax.ShapeDtypeStruct((t, LANES), F32)],
        compiler_params=_cp(("arbitrary",)),
        name="moe_router",
    )(m, x_p, x_s, g.reshape(1, d), w_r, b_r)


def _moe_kernel(n_used, item_e, item_start, item_n, tok, dst,
                hm_hbm, wg_ref, wu_ref, wd_ref, y_hbm,
                rows_in, rows_out, hid, sem_in, sem_out):
    del n_used, item_e
    i = pl.program_id(0)
    j = pl.program_id(1)
    n_items = pl.num_programs(0)
    n_steps = pl.num_programs(1)
    n_up = D_EXPERT // MOE_TJ
    max_blk = MOE_ROWS // MOE_BLK
    n, s0 = item_n[i], item_start[i]
    nxt = jnp.minimum(i + 1, n_items - 1)
    n_next, s_next = jnp.where(i + 1 < n_items, item_n[nxt], 0), item_start[nxt]
    prv = jnp.maximum(i - 1, 0)
    n_prev, s_prev = jnp.where(i > 0, item_n[prv], 0), item_start[prv]
    nblk = (n + MOE_BLK - 1) // MOE_BLK

    def gather(s, r):
        return pltpu.make_async_copy(hm_hbm.at[pl.ds(tok[s + r], 1), :], rows_in.at[pl.ds(r, 1), :], sem_in)

    def scatter(s, r):
        return pltpu.make_async_copy(rows_out.at[pl.ds(r, 1), :], y_hbm.at[pl.ds(dst[s + r], 1), :], sem_out)

    def gathered_group(c):
        rows = pl.ds(pl.multiple_of(c * MOE_DMA_UNROLL, MOE_DMA_UNROLL), MOE_DMA_UNROLL)
        return pltpu.make_async_copy(hm_hbm.at[pl.ds(0, MOE_DMA_UNROLL), :], rows_in.at[rows, :], sem_in)

    def scattered_group(c):
        rows = pl.ds(pl.multiple_of(c * MOE_DMA_UNROLL, MOE_DMA_UNROLL), MOE_DMA_UNROLL)
        return pltpu.make_async_copy(rows_out.at[rows, :], y_hbm.at[pl.ds(0, MOE_DMA_UNROLL), :], sem_out)

    def for_rows(count, fn, group_fn=None):
        groups = count // MOE_DMA_UNROLL

        def group(c, carry):
            if group_fn is None:
                for u in range(MOE_DMA_UNROLL):
                    fn(c * MOE_DMA_UNROLL + u)
            else:
                group_fn(c)
            return carry
        lax.fori_loop(0, groups, group, 0)

        def single(r, carry):
            fn(r)
            return carry
        lax.fori_loop(groups * MOE_DMA_UNROLL, count, single, 0)

    def for_row_count(fn):
        for k in range(1, max_blk + 1):
            @pl.when(nblk == k)
            def _(k=k):
                fn(k * MOE_BLK)

    @pl.when((i == 0) & (j == 0))
    def _():
        rows_in[...] = jnp.zeros_like(rows_in)
        for_rows(n, lambda r: gather(s0, r).start())

    @pl.when(j == 0)
    def _():
        for_rows(n, lambda r: gather(s0, r).wait(), lambda c: gathered_group(c).wait())

    @pl.when(j < n_up)
    def _():
        def up(m):
            x = _unpack_bf16_pairs(rows_in[0:m, :])
            act = _silu(_dot(x, wg_ref[0].astype(BF16))) * _dot(x, wu_ref[0].astype(BF16))
            hid[j, 0:m, :] = act.astype(BF16)
        for_row_count(up)

    @pl.when(j == n_up)
    def _():
        for_rows(n_next, lambda r: gather(s_next, r).start())
        for_rows(n_prev, lambda r: scatter(s_prev, r).wait(), lambda c: scattered_group(c).wait())

    @pl.when(j >= n_up)
    def _():
        col = pl.multiple_of((j - n_up) * MOE_TN, MOE_TN)

        def down(m):
            act = jnp.concatenate([hid[u, 0:m, :] for u in range(n_up)], axis=-1)
            rows_out[0:m, pl.ds(col, MOE_TN)] = _dot(act, wd_ref[0].astype(BF16))
        for_row_count(down)

    @pl.when(j == n_steps - 1)
    def _():
        for_rows(n, lambda r: scatter(s0, r).start())

    @pl.when((j == n_steps - 1) & (i == n_items - 1))
    def _():
        for_rows(n, lambda r: scatter(s0, r).wait(), lambda c: scattered_group(c).wait())


def _moe_experts(hm, w_gate, w_up, w_down, items, tok, dst, n_assign):
    n_used, item_e, item_start, item_n = items
    n_up = D_EXPERT // MOE_TJ
    n_dn = D_MODEL // MOE_TN

    def up_map(i, j, nu, e, st, n, tok, dst):
        return (e[i], 0, jnp.minimum(j, n_up - 1))

    def dn_map(i, j, nu, e, st, n, tok, dst):
        return (e[i], 0, jnp.maximum(j - n_up, 0))

    up_spec = pl.BlockSpec((1, D_MODEL, MOE_TJ), up_map)
    return pl.pallas_call(
        _moe_kernel,
        grid_spec=pltpu.PrefetchScalarGridSpec(
            num_scalar_prefetch=6,
            grid=(n_used[0], n_up + n_dn),
            in_specs=[pl.BlockSpec(memory_space=pl.ANY), up_spec, up_spec,
                      pl.BlockSpec((1, D_EXPERT, MOE_TN), dn_map)],
            out_specs=pl.BlockSpec(memory_space=pl.ANY),
            scratch_shapes=[pltpu.VMEM((MOE_ROWS, D_MODEL // 2), jnp.uint32),
                            pltpu.VMEM((MOE_ROWS, D_MODEL), F32),
                            pltpu.VMEM((n_up, MOE_ROWS, MOE_TJ), BF16),
                            pltpu.SemaphoreType.DMA(()),
                            pltpu.SemaphoreType.DMA(())],
        ),
        out_shape=jax.ShapeDtypeStruct((n_assign, D_MODEL), F32),
        compiler_params=_cp(("arbitrary", "arbitrary")),
        name="moe_experts",
    )(n_used, item_e, item_start, item_n, tok, dst, hm, w_gate, w_up, w_down)


def _moe_plan(expert_idx):
    n_assign = expert_idx.size
    flat_e = expert_idx.reshape(n_assign)
    order = jnp.argsort(flat_e).astype(jnp.int32)
    counts = jnp.bincount(flat_e, length=N_EXPERTS).astype(jnp.int32)
    start = jnp.cumsum(counts) - counts
    per_e = (counts + MOE_ROWS - 1) // MOE_ROWS
    item_end = jnp.cumsum(per_e)
    n_items = n_assign // MOE_ROWS + N_EXPERTS
    ids = jnp.arange(n_items, dtype=jnp.int32)
    used = ids < item_end[-1]
    last = jnp.maximum(item_end[-1] - 1, 0)
    e_of = jnp.minimum(jnp.searchsorted(item_end, jnp.minimum(ids, last), side="right"), N_EXPERTS - 1).astype(jnp.int32)
    local = jnp.minimum(ids, last) - (item_end - per_e)[e_of]
    item_start = start[e_of] + local * MOE_ROWS
    item_n = jnp.where(used, jnp.clip(counts[e_of] - local * MOE_ROWS, 0, MOE_ROWS), 0)
    items = (item_end[-1:].astype(jnp.int32), e_of, item_start.astype(jnp.int32), item_n.astype(jnp.int32))
    token = order // TOP_K
    dst_row = (order % TOP_K) * (n_assign // TOP_K) + token
    return items, token, dst_row


def _combine_kernel(x_ref, y0_ref, y1_ref, gate_ref, o_ref, ob_ref):
    x = x_ref[...] + (y0_ref[...] * gate_ref[:, 0:1] + y1_ref[...] * gate_ref[:, 1:2])
    o_ref[...] = x
    ob_ref[...] = x.astype(BF16)


def _combine(x, y, gate):
    t, d = x.shape
    tr = ROW_TILE
    row = pl.BlockSpec((tr, d), lambda i: (i, 0))
    return pl.pallas_call(
        _combine_kernel,
        grid=(t // tr,),
        in_specs=[row, row, pl.BlockSpec((tr, d), lambda i: (i + t // tr, 0)),
                  pl.BlockSpec((tr, LANES), lambda i: (i, 0))],
        out_specs=[row, row],
        out_shape=[jax.ShapeDtypeStruct((t, d), F32), jax.ShapeDtypeStruct((t, d), BF16)],
        compiler_params=_cp(("parallel",)),
        name="moe_combine",
    )(x, y, y, gate)


def _rope_tables(pos):
    half = RET_DK // 2
    inv = ROPE_BASE ** (-jnp.arange(half, dtype=F32) / half)
    ang = pos.astype(F32)[:, None] * inv[None, :]
    return jnp.cos(ang), jnp.sin(ang)


def kernel(x_prompt, x_sample, state_ret, state_gla, p_prompt, p_sample, g_mix, w_in, w_gla_up, b_gla,
           ret_norm_g, gla_norm_g, w_out, g_moe, w_rg, b_rg, w_re, b_re, w_gate, w_up, w_down, w_pg, w_pp,
           g_final):
    n_p, l_p, d = x_prompt.shape
    n_s, l_s, _ = x_sample.shape
    depth = g_mix.shape[0]
    t_p, t_s = n_p * l_p, n_s * l_s
    t = t_p + t_s

    log_g_py = [math.log1p(-(2.0 ** (-5.0 - h))) for h in range(RET_HEADS)]
    cos_p, sin_p = _rope_tables(jnp.arange(l_p, dtype=jnp.int32))
    cos_s, sin_s = _rope_tables(PAST_LEN + jnp.arange(l_s, dtype=jnp.int32))
    cos_s, sin_s = jnp.tile(cos_s, (DEC_PAIR, 1)), jnp.tile(sin_s, (DEC_PAIR, 1))

    x_p = x_prompt.reshape(t_p, d)
    x_s = x_sample.reshape(t_s, d)
    x = None
    ret_p, ret_s, gla_p, gla_s = [], [], [], []
    for l in range(depth):
        w_in_t = jnp.swapaxes(w_in[l], 0, 1)
        w_ga = jnp.pad(w_in_t[N_MAIN:], ((0, LANES - GLA_RANK), (0, 0))).astype(BF16)
        w_gup = jnp.pad(w_gla_up[l], ((0, LANES - GLA_RANK), (0, 0))).astype(BF16)
        w_r = jnp.pad(jnp.concatenate([w_rg[l], w_re[l]], axis=1),
                      ((0, 0), (0, LANES - N_GROUPS - N_EXPERTS)))
        b_r = jnp.pad(jnp.concatenate([b_rg[l], b_re[l]]), (0, LANES - N_GROUPS - N_EXPERTS)).reshape(1, LANES)
        p = jnp.concatenate([p_prompt[l].reshape(t_p, -1), p_sample[l].reshape(t_s, -1)], axis=0)
        if x is not None:
            x_p, x_s = x[:t_p], x[t_p:]

        h, log_a = _norm_in(x_p, x_s, g_mix[l], w_ga, w_gup, b_gla[l])
        proj = _in_proj(h, w_in_t, N_MAIN, t // 16, 1024)
        ro_p, sr_p = _ret_prompt(proj, log_g_py, cos_p, sin_p, ret_norm_g[l], n_p, l_p)
        go_p, sg_p = _gla_prompt(proj, log_a, gla_norm_g[l], n_p, l_p)
        ro_s, sr_s = _ret_sample(proj, t_p, log_g_py, cos_s, sin_s, ret_norm_g[l], state_ret[l], n_s, l_s)
        go_s, sg_s = _gla_sample(proj, t_p, log_a, gla_norm_g[l], state_gla[l], n_s, l_s)
        ret_p.append(sr_p)
        ret_s.append(sr_s)
        gla_p.append(sg_p)
        gla_s.append(sg_s)
        mix = _out_proj(ro_p, ro_s, go_p, go_s, w_out[l], 1024)

        x, hm, idx, gate = _router(mix, x_p, x_s, g_moe[l], w_r, b_r)
        items, tok, dst = _moe_plan(idx[:, :TOP_K])
        y = _moe_experts(hm, w_gate[l], w_up[l], w_down[l], items, tok, dst, t * TOP_K)
        x, xb = _combine(x, y, gate)

        x = _ple(xb, x, p, w_pg[l], w_pp[l], t // 8, 512)

    y_prompt = _norm_out(x, g_final, 0, t_p).reshape(n_p, l_p, d)
    y_sample = _norm_out(x, g_final, t_p, t_s).reshape(n_s, l_s, d)
    return (y_prompt, y_sample,
            jnp.stack(ret_p).astype(state_ret.dtype), jnp.stack(ret_s).astype(state_ret.dtype),
            jnp.stack(gla_p).astype(state_gla.dtype), jnp.stack(gla_s).astype(state_gla.dtype))
```

```python
import functools
import math

import jax
import jax.numpy as jnp
from jax import lax
from jax.experimental import pallas as pl
from jax.experimental.pallas import tpu as pltpu

F32 = jnp.float32
BF16 = jnp.bfloat16

D_MODEL = 4096
RET_HEADS = 8
RET_DK = 256
RET_DV = 256
GLA_HEADS = 4
GLA_DK = 256
GLA_DV = 512
GLA_RANK = 16
GLA_GATE_TEMP = 16.0
ROPE_BASE = 10000.0
PAST_LEN = 16384
N_GROUPS = 4
EXPERTS_PER_GROUP = 8
N_EXPERTS = N_GROUPS * EXPERTS_PER_GROUP
TOP_K = 2
D_EXPERT = D_MODEL // 4
EPS = 1e-6

RET_W = RET_HEADS * RET_DK
GLA_KW = GLA_HEADS * GLA_DK
GLA_VW = GLA_HEADS * GLA_DV
N_MAIN = 4 * RET_W + 2 * GLA_KW + 2 * GLA_VW
COL_GQ = 4 * RET_W
COL_GK = COL_GQ + GLA_KW
COL_GV = COL_GK + GLA_KW
COL_GG = COL_GV + GLA_VW

LANES = 128
ROW_TILE = 256
LHS_TILE = 512
RET_CHUNK = 128
GLA_CHUNK = 128
GLA_SUB = 32
DEC_PAIR = 2
MOE_ROWS = 768
MOE_BLK = 128
MOE_TJ = 256
MOE_TN = 1024
MOE_DMA_UNROLL = 8
VMEM_LIMIT = 56 * 1024 * 1024


def _cp(semantics, vmem=VMEM_LIMIT):
    return pltpu.CompilerParams(dimension_semantics=semantics, vmem_limit_bytes=vmem)


def _sigmoid(x):
    return 1.0 / (1.0 + jnp.exp(-x))


def _silu(x):
    return x * _sigmoid(x)


def _dot(a, b):
    return jnp.dot(a, b, preferred_element_type=F32)


def _dot_nt(a, b):
    return lax.dot_general(a, b, (((1,), (1,)), ((), ())), preferred_element_type=F32)


def _dot_tn(a, b):
    return lax.dot_general(a, b, (((0,), (0,)), ((), ())), preferred_element_type=F32)


def _rms_norm(x, g):
    ms = jnp.mean(x * x, axis=-1, keepdims=True)
    return x * lax.rsqrt(ms + EPS) * g


_HIGH_HALF = 0xFFFF0000


def _pack_bf16_pairs(x):
    half = x.shape[-1] // 2
    bits = lax.bitcast_convert_type(x.astype(BF16).astype(F32), jnp.uint32)
    return (bits[:, :half] >> 16) | (bits[:, half:] & jnp.uint32(_HIGH_HALF))


def _unpack_bf16_pairs(w):
    lo = lax.bitcast_convert_type(w << 16, F32)
    hi = lax.bitcast_convert_type(w & jnp.uint32(_HIGH_HALF), F32)
    return jnp.concatenate([lo, hi], axis=-1).astype(BF16)


def _two_group_specs(shape_tail, tile, n_p_tiles):
    zeros = (0,) * len(shape_tail)

    def p_map(*idx):
        return (jnp.minimum(idx[-1], n_p_tiles - 1),) + zeros

    def s_map(*idx):
        return (jnp.maximum(idx[-1] - n_p_tiles, 0),) + zeros

    return pl.BlockSpec((tile,) + shape_tail, p_map), pl.BlockSpec((tile,) + shape_tail, s_map)


def _log_sigmoid(z):
    return jnp.minimum(z, 0.0) - jnp.log(1.0 + jnp.exp(-jnp.abs(z)))


def _norm_in_kernel(n_p_tiles, xp_ref, xs_ref, g_ref, wga_ref, wup_ref, b_ref, h_ref, la_ref):
    i = pl.program_id(0)

    def emit(x):
        h = _rms_norm(x, g_ref[...]).astype(BF16)
        h_ref[...] = h
        ga = _dot_nt(h, wga_ref[...])
        z = _dot(ga.astype(BF16), wup_ref[...]) + b_ref[...]
        la_ref[...] = _log_sigmoid(z) * (1.0 / GLA_GATE_TEMP)

    @pl.when(i < n_p_tiles)
    def _():
        emit(xp_ref[...])

    @pl.when(i >= n_p_tiles)
    def _():
        emit(xs_ref[...])


def _norm_in(x_p, x_s, g, w_ga, w_up, b):
    (t_p, d), t_s = x_p.shape, x_s.shape[0]
    tr = ROW_TILE
    p_spec, s_spec = _two_group_specs((d,), tr, t_p // tr)
    return pl.pallas_call(
        functools.partial(_norm_in_kernel, t_p // tr),
        grid=((t_p + t_s) // tr,),
        in_specs=[p_spec, s_spec, pl.BlockSpec((1, d), lambda i: (0, 0)),
                  pl.BlockSpec((LANES, d), lambda i: (0, 0)),
                  pl.BlockSpec((LANES, GLA_KW), lambda i: (0, 0)),
                  pl.BlockSpec((1, GLA_KW), lambda i: (0, 0))],
        out_specs=[pl.BlockSpec((tr, d), lambda i: (i, 0)), pl.BlockSpec((tr, GLA_KW), lambda i: (i, 0))],
        out_shape=[jax.ShapeDtypeStruct((t_p + t_s, d), BF16), jax.ShapeDtypeStruct((t_p + t_s, GLA_KW), F32)],
        compiler_params=_cp(("arbitrary",)),
        name="norm_in",
    )(x_p, x_s, g.reshape(1, d), w_ga, w_up, b.reshape(1, GLA_KW))


def _norm_out_kernel(x_ref, g_ref, o_ref):
    o_ref[...] = _rms_norm(x_ref[...], g_ref[...])


def _norm_out(x, g, row0, n_rows):
    d = x.shape[1]
    tr = ROW_TILE
    blk0 = row0 // tr
    return pl.pallas_call(
        _norm_out_kernel,
        grid=(n_rows // tr,),
        in_specs=[pl.BlockSpec((tr, d), lambda i: (i + blk0, 0)), pl.BlockSpec((1, d), lambda i: (0, 0))],
        out_specs=pl.BlockSpec((tr, d), lambda i: (i, 0)),
        out_shape=jax.ShapeDtypeStruct((n_rows, d), F32),
        compiler_params=_cp(("parallel",)),
        name="norm_out",
    )(x, g.reshape(1, d))


def _in_proj_kernel(a_ref, wt_ref, o_ref):
    o_ref[...] = _dot_nt(a_ref[...], wt_ref[...].astype(BF16))


def _in_proj(h, w_t, n_cols, tm, tn):
    t, d = h.shape
    return pl.pallas_call(
        _in_proj_kernel,
        grid=(n_cols // tn, t // tm),
        in_specs=[pl.BlockSpec((tm, d), lambda j, i: (i, 0)), pl.BlockSpec((tn, d), lambda j, i: (j, 0))],
        out_specs=pl.BlockSpec((tm, tn), lambda j, i: (i, j)),
        out_shape=jax.ShapeDtypeStruct((t, n_cols), F32),
        compiler_params=_cp(("parallel", "parallel")),
        name="in_proj",
    )(h, w_t)


def _out_proj_kernel(n_p_tiles, rp_ref, rs_ref, gp_ref, gs_ref, w_ref, o_ref):
    i = pl.program_id(1)

    def emit(r_ref, g_ref):
        o_ref[...] = (_dot(r_ref[...], w_ref[:RET_W, :].astype(BF16))
                      + _dot(g_ref[...], w_ref[RET_W:, :].astype(BF16)))

    @pl.when(i < n_p_tiles)
    def _():
        emit(rp_ref, gp_ref)

    @pl.when(i >= n_p_tiles)
    def _():
        emit(rs_ref, gs_ref)


def _out_proj(ro_p, ro_s, go_p, go_s, w, tn):
    t_p, t_s = ro_p.shape[0], ro_s.shape[0]
    d_in, d_out = w.shape
    tm = LHS_TILE
    rp_spec, rs_spec = _two_group_specs((RET_W,), tm, t_p // tm)
    gp_spec, gs_spec = _two_group_specs((GLA_VW,), tm, t_p // tm)
    return pl.pallas_call(
        functools.partial(_out_proj_kernel, t_p // tm),
        grid=(d_out // tn, (t_p + t_s) // tm),
        in_specs=[rp_spec, rs_spec, gp_spec, gs_spec, pl.BlockSpec((d_in, tn), lambda j, i: (0, j))],
        out_specs=pl.BlockSpec((tm, tn), lambda j, i: (i, j)),
        out_shape=jax.ShapeDtypeStruct((t_p + t_s, d_out), F32),
        compiler_params=_cp(("arbitrary", "arbitrary")),
        name="out_proj",
    )(ro_p, ro_s, go_p, go_s, w)


def _ple_kernel(a_ref, w_ref, p_ref, wp_ref, r_ref, o_ref):
    gate = _sigmoid(_dot(a_ref[...], w_ref[...].astype(BF16)))
    emb = _dot(p_ref[...].astype(BF16), wp_ref[...].astype(BF16))
    o_ref[...] = r_ref[...] + gate * emb


def _ple(xb, x, p, w_pg, w_pp, tm, tn):
    t, d = x.shape
    kp = p.shape[1]
    o_spec = pl.BlockSpec((tm, tn), lambda j, i: (i, j))
    return pl.pallas_call(
        _ple_kernel,
        grid=(d // tn, t // tm),
        in_specs=[pl.BlockSpec((tm, d), lambda j, i: (i, 0)), pl.BlockSpec((d, tn), lambda j, i: (0, j)),
                  pl.BlockSpec((tm, kp), lambda j, i: (i, 0)), pl.BlockSpec((kp, tn), lambda j, i: (0, j)),
                  o_spec],
        out_specs=o_spec,
        out_shape=jax.ShapeDtypeStruct((t, d), F32),
        compiler_params=_cp(("parallel", "parallel")),
        name="ple",
    )(xb, w_pg, p, w_pp, x)


def _rotary(x, cos, sin):
    half = x.shape[-1] // 2
    x1, x2 = x[:, :half], x[:, half:]
    return jnp.concatenate([x1 * cos - x2 * sin, x1 * sin + x2 * cos], axis=-1)


def _group_norm_gate(o, gain, gate):
    mu = jnp.mean(o, axis=-1, keepdims=True)
    d = o - mu
    var = jnp.mean(d * d, axis=-1, keepdims=True)
    return d * lax.rsqrt(var + EPS) * gain * _silu(gate)


def _rms_gate(o, gain, gate):
    ms = jnp.mean(o * o, axis=-1, keepdims=True)
    return o * lax.rsqrt(ms + EPS) * gain * _silu(gate)


def _ret_prompt_body(log_g, first, i, n_chunks, q_ref, k_ref, v_ref, g_ref, cos_ref, sin_ref, gn_ref, o_ref, s_ref,
                     state, intra_tab, q_tab, k_tab):
    c = RET_CHUNK

    @pl.when(first)
    def _():
        row = lax.broadcasted_iota(jnp.int32, (c, RET_DK), 0).astype(F32)
        ii = lax.broadcasted_iota(jnp.int32, (c, c), 0)
        jj = lax.broadcasted_iota(jnp.int32, (c, c), 1)
        diff = (ii - jj).astype(F32)
        for h in range(RET_HEADS):
            intra_tab[h] = jnp.where(diff >= 0, jnp.exp(jnp.maximum(diff, 0.0) * log_g[h]), 0.0)
            q_tab[h] = jnp.exp((row + 1.0) * log_g[h])
            k_tab[h] = jnp.exp((c - 1.0 - row) * log_g[h])

    @pl.when(i == 0)
    def _():
        state[...] = jnp.zeros_like(state)

    cos, sin = cos_ref[...], sin_ref[...]
    for h in range(RET_HEADS):
        sl = slice(h * RET_DK, (h + 1) * RET_DK)
        q = _rotary(q_ref[:, sl], cos, sin)
        k = _rotary(k_ref[:, sl], cos, sin) * (RET_DK ** -0.5)
        v = v_ref[:, sl].astype(BF16)
        att = _dot_nt(q.astype(BF16), k.astype(BF16)) * intra_tab[h]
        s_old = state[h]
        o = _dot(att.astype(BF16), v) + _dot((q * q_tab[h]).astype(BF16), s_old.astype(BF16))
        state[h] = s_old * math.exp(c * log_g[h]) + _dot_tn((k * k_tab[h]).astype(BF16), v)
        o_ref[:, sl] = _group_norm_gate(o, gn_ref[h], g_ref[:, sl]).astype(o_ref.dtype)

    @pl.when(i == n_chunks - 1)
    def _():
        s_ref[0] = state[...]


def _ret_sample_body(log_g, seq, q_ref, k_ref, v_ref, g_ref, cos_ref, sin_ref, gn_ref, s_in, o_ref, s_out):
    rows = DEC_PAIR * seq
    cos, sin = cos_ref[...], sin_ref[...]
    rid = lax.broadcasted_iota(jnp.int32, (rows, RET_DK), 0)
    pos = (rid % seq).astype(F32)
    batch_of_row = rid // seq
    ii = lax.broadcasted_iota(jnp.int32, (rows, rows), 0)
    jj = lax.broadcasted_iota(jnp.int32, (rows, rows), 1)
    visible = (ii // seq == jj // seq) & (ii >= jj)
    diff = jnp.maximum(ii - jj, 0).astype(F32)
    for h in range(RET_HEADS):
        lg = log_g[h]
        sl = slice(h * RET_DK, (h + 1) * RET_DK)
        q = _rotary(q_ref[:, sl], cos, sin)
        k = _rotary(k_ref[:, sl], cos, sin) * (RET_DK ** -0.5)
        v = v_ref[:, sl].astype(BF16)
        intra = jnp.where(visible, jnp.exp(diff * lg), 0.0)
        att = _dot_nt(q.astype(BF16), k.astype(BF16)) * intra
        o = _dot(att.astype(BF16), v)
        qd = (q * jnp.exp((pos + 1.0) * lg)).astype(BF16)
        kd = k * jnp.exp((seq - 1.0 - pos) * lg)
        for b in range(DEC_PAIR):
            mine = batch_of_row == b
            s_old = s_in[b, h]
            o = o + jnp.where(mine, _dot(qd, s_old.astype(BF16)), 0.0)
            s_out[b, h] = s_old * math.exp(seq * lg) + _dot_tn(jnp.where(mine, kd, 0.0).astype(BF16), v)
        o_ref[:, sl] = _group_norm_gate(o, gn_ref[h], g_ref[:, sl]).astype(o_ref.dtype)


def _ret_mix_kernel(log_g, seq, n_chunks, qp, kp, vp, gp, cos_p, sin_p, gn, qs, ks, vs, gs, cos_s, sin_s, s_in,
                    o_p, s_p, o_s, s_out, state, intra_tab, q_tab, k_tab):
    s = pl.program_id(0)
    _ret_prompt_body(log_g, s == 0, s % n_chunks, n_chunks, qp, kp, vp, gp, cos_p, sin_p, gn, o_p, s_p,
                     state, intra_tab, q_tab, k_tab)
    _ret_sample_body(log_g, seq, qs, ks, vs, gs, cos_s, sin_s, gn, s_in, o_s, s_out)


def _ret_mix(proj, log_g, cos_p, sin_p, cos_s, sin_s, gn, state, n_p, l_p, n_s, l_s):
    c = RET_CHUNK
    nc = l_p // c
    rows = DEC_PAIR * l_s
    steps = n_p * nc
    assert steps * DEC_PAIR == n_s, (steps, n_s)
    blk0 = (n_p * l_p) // rows

    def col_p(group):
        return pl.BlockSpec((c, RET_W), lambda s: (s, group))

    def col_s(group):
        return pl.BlockSpec((rows, RET_W), lambda s: (s + blk0, group))

    tab_p = pl.BlockSpec((c, RET_DK // 2), lambda s: (s % nc, 0))
    tab_s = pl.BlockSpec((rows, RET_DK // 2), lambda s: (0, 0))
    st = pl.BlockSpec((DEC_PAIR, RET_HEADS, RET_DK, RET_DV), lambda s: (s, 0, 0, 0))
    return pl.pallas_call(
        functools.partial(_ret_mix_kernel, log_g, l_s, nc),
        grid=(steps,),
        in_specs=[col_p(0), col_p(1), col_p(2), col_p(3), tab_p, tab_p,
                  pl.BlockSpec((RET_HEADS, 1, RET_DV), lambda s: (0, 0, 0)),
                  col_s(0), col_s(1), col_s(2), col_s(3), tab_s, tab_s, st],
        out_specs=[pl.BlockSpec((c, RET_W), lambda s: (s, 0)),
                   pl.BlockSpec((1, RET_HEADS, RET_DK, RET_DV), lambda s: (s // nc, 0, 0, 0)),
                   pl.BlockSpec((rows, RET_W), lambda s: (s, 0)), st],
        out_shape=[jax.ShapeDtypeStruct((n_p * l_p, RET_W), BF16),
                   jax.ShapeDtypeStruct((n_p, RET_HEADS, RET_DK, RET_DV), F32),
                   jax.ShapeDtypeStruct((n_s * l_s, RET_W), BF16),
                   jax.ShapeDtypeStruct(state.shape, F32)],
        scratch_shapes=[pltpu.VMEM((RET_HEADS, RET_DK, RET_DV), F32),
                        pltpu.VMEM((RET_HEADS, c, c), F32),
                        pltpu.VMEM((RET_HEADS, c, RET_DK), F32),
                        pltpu.VMEM((RET_HEADS, c, RET_DK), F32)],
        compiler_params=_cp(("arbitrary",)),
        name="retention_mix",
    )(proj, proj, proj, proj, cos_p, sin_p, gn.reshape(RET_HEADS, 1, RET_DV),
      proj, proj, proj, proj, cos_s, sin_s, state)


def _split3(x):
    hi = x.astype(BF16)
    r1 = x - hi.astype(F32)
    mid = r1.astype(BF16)
    lo = (r1 - mid.astype(F32)).astype(BF16)
    return hi, mid, lo


def _column_scale(row_vec, width):
    n = row_vec.shape[-1]
    t = jnp.transpose(jnp.broadcast_to(row_vec, (LANES, n)))
    return jnp.concatenate([t] * (width // LANES), axis=-1)


def _gla_prompt_body(i, n_chunks, q_ref, k_ref, v_ref, g_ref, la_ref, gn_ref, o_ref, s_ref, state):
    c, sub = GLA_CHUNK, GLA_SUB
    nsub = c // sub

    @pl.when(i == 0)
    def _():
        state[...] = jnp.zeros_like(state)

    ii = lax.broadcasted_iota(jnp.int32, (c, c), 0)
    jj = lax.broadcasted_iota(jnp.int32, (c, c), 1)
    causal = ii >= jj
    tri = jnp.where(causal, 1.0, 0.0).astype(BF16)
    blk = lax.broadcasted_iota(jnp.int32, (c, GLA_DK), 0) // sub
    for h in range(GLA_HEADS):
        ks = slice(h * GLA_DK, (h + 1) * GLA_DK)
        vs = slice(h * GLA_DV, (h + 1) * GLA_DV)
        q = q_ref[:, ks]
        k = k_ref[:, ks] * (GLA_DK ** -0.5)
        v = v_ref[:, vs].astype(BF16)
        hi, mid, lo = _split3(la_ref[:, ks])
        b = _dot(tri, hi) + _dot(tri, mid) + _dot(tri, lo)
        mids = [b[s * sub + sub // 2 - 1: s * sub + sub // 2, :] for s in range(nsub)]
        ref_lvl = jnp.concatenate([jnp.broadcast_to(m, (sub, GLA_DK)) for m in mids], axis=0)
        qd = (q * jnp.exp(b - ref_lvl)).astype(BF16)
        kd = k * jnp.exp(ref_lvl - b)
        rows = []
        for s in range(nsub):
            scale = jnp.where(blk <= s, jnp.exp(jnp.minimum(mids[s] - ref_lvl, 0.0)), 0.0)
            rows.append(_dot_nt(qd[s * sub:(s + 1) * sub], (kd * scale).astype(BF16)))
        att = jnp.where(causal, jnp.concatenate(rows, axis=0), 0.0)
        s_old = state[h]
        o = _dot(att.astype(BF16), v) + _dot((q * jnp.exp(b)).astype(BF16), s_old.astype(BF16))
        b_last = b[c - 1:c, :]
        k_rem = (k * jnp.exp(b_last - b)).astype(BF16)
        state[h] = s_old * _column_scale(jnp.exp(b_last), GLA_DV) + _dot_tn(k_rem, v)
        o_ref[:, vs] = _rms_gate(o, gn_ref[h], g_ref[:, vs]).astype(o_ref.dtype)

    @pl.when(i == n_chunks - 1)
    def _():
        s_ref[0] = state[...]


def _gla_sample_body(seq, q_ref, k_ref, v_ref, g_ref, la_ref, gn_ref, s_in, o_ref, s_out):
    rows = DEC_PAIR * seq
    rid = lax.broadcasted_iota(jnp.int32, (rows, GLA_DK), 0)
    pos = rid % seq
    batch_of_row = rid // seq
    ii = lax.broadcasted_iota(jnp.int32, (rows, rows), 0)
    jj = lax.broadcasted_iota(jnp.int32, (rows, rows), 1)
    visible = (ii // seq == jj // seq) & (ii >= jj)
    for h in range(GLA_HEADS):
        ks = slice(h * GLA_DK, (h + 1) * GLA_DK)
        vs = slice(h * GLA_DV, (h + 1) * GLA_DV)
        la = la_ref[:, ks]
        b = la
        for d in range(1, seq):
            b = b + jnp.where(pos >= d, pltpu.roll(la, d, axis=0), 0.0)
        q = q_ref[:, ks]
        k = k_ref[:, ks] * (GLA_DK ** -0.5)
        v = v_ref[:, vs].astype(BF16)
        qb = (q * jnp.exp(b)).astype(BF16)
        kb = (k * jnp.exp(-b)).astype(BF16)
        att = jnp.where(visible, _dot_nt(qb, kb), 0.0)
        o = _dot(att.astype(BF16), v)
        for bi in range(DEC_PAIR):
            mine = batch_of_row == bi
            last = bi * seq + seq - 1
            b_last = b[last:last + 1, :]
            s_old = s_in[bi, h]
            o = o + jnp.where(mine[:, :1], _dot(qb, s_old.astype(BF16)), 0.0)
            k_rem = jnp.where(mine, k * jnp.exp(b_last - b), 0.0).astype(BF16)
            s_out[bi, h] = s_old * _column_scale(jnp.exp(b_last), GLA_DV) + _dot_tn(k_rem, v)
        o_ref[:, vs] = _rms_gate(o, gn_ref[h], g_ref[:, vs]).astype(o_ref.dtype)


def _gla_mix_kernel(seq, n_chunks, qp, kp, vp, gp, la_p, gn, qs, ks, vs, gs, la_s, s_in,
                    o_p, s_p, o_s, s_out, state):
    s = pl.program_id(0)
    _gla_prompt_body(s % n_chunks, n_chunks, qp, kp, vp, gp, la_p, gn, o_p, s_p, state)
    _gla_sample_body(seq, qs, ks, vs, gs, la_s, gn, s_in, o_s, s_out)


def _gla_mix(proj, log_a, gn, state, n_p, l_p, n_s, l_s):
    c = GLA_CHUNK
    nc = l_p // c
    rows = DEC_PAIR * l_s
    steps = n_p * nc
    assert steps * DEC_PAIR == n_s, (steps, n_s)
    blk0 = (n_p * l_p) // rows

    def col_p(start, width):
        return pl.BlockSpec((c, width), lambda s: (s, start // width))

    def col_s(start, width):
        return pl.BlockSpec((rows, width), lambda s: (s + blk0, start // width))

    st = pl.BlockSpec((DEC_PAIR, GLA_HEADS, GLA_DK, GLA_DV), lambda s: (s, 0, 0, 0))
    return pl.pallas_call(
        functools.partial(_gla_mix_kernel, l_s, nc),
        grid=(steps,),
        in_specs=[col_p(COL_GQ, GLA_KW), col_p(COL_GK, GLA_KW), col_p(COL_GV, GLA_VW), col_p(COL_GG, GLA_VW),
                  pl.BlockSpec((c, GLA_KW), lambda s: (s, 0)),
                  pl.BlockSpec((GLA_HEADS, 1, GLA_DV), lambda s: (0, 0, 0)),
                  col_s(COL_GQ, GLA_KW), col_s(COL_GK, GLA_KW), col_s(COL_GV, GLA_VW), col_s(COL_GG, GLA_VW),
                  pl.BlockSpec((rows, GLA_KW), lambda s: (s + blk0, 0)), st],
        out_specs=[pl.BlockSpec((c, GLA_VW), lambda s: (s, 0)),
                   pl.BlockSpec((1, GLA_HEADS, GLA_DK, GLA_DV), lambda s: (s // nc, 0, 0, 0)),
                   pl.BlockSpec((rows, GLA_VW), lambda s: (s, 0)), st],
        out_shape=[jax.ShapeDtypeStruct((n_p * l_p, GLA_VW), BF16),
                   jax.ShapeDtypeStruct((n_p, GLA_HEADS, GLA_DK, GLA_DV), F32),
                   jax.ShapeDtypeStruct((n_s * l_s, GLA_VW), BF16),
                   jax.ShapeDtypeStruct(state.shape, F32)],
        scratch_shapes=[pltpu.VMEM((GLA_HEADS, GLA_DK, GLA_DV), F32)],
        compiler_params=_cp(("arbitrary",)),
        name="gla_mix",
    )(proj, proj, proj, proj, log_a, gn.reshape(GLA_HEADS, 1, GLA_DV),
      proj, proj, proj, proj, log_a, state)


def _router_kernel(n_p_tiles, m_ref, xp_ref, xs_ref, g_ref, wr_ref, br_ref, x_ref, hm_ref, idx_ref, gate_ref):
    tr = m_ref.shape[0]
    i = pl.program_id(0)

    @pl.when(i < n_p_tiles)
    def _():
        x_ref[...] = xp_ref[...] + m_ref[...]

    @pl.when(i >= n_p_tiles)
    def _():
        x_ref[...] = xs_ref[...] + m_ref[...]

    hm = _rms_norm(x_ref[...], g_ref[...])
    hm_ref[...] = _pack_bf16_pairs(hm)
    h1, h2, _ = _split3(hm)
    w = wr_ref[...]
    w1 = w.astype(BF16)
    w2 = (w - w1.astype(F32)).astype(BF16)
    logits = _dot(h1, w1) + _dot(h1, w2) + _dot(h2, w1) + br_ref[...]
    lane = lax.broadcasted_iota(jnp.int32, (tr, LANES), 1).astype(F32)
    neg, far = -1e30, 1e9
    is_group = lane < N_GROUPS
    gl = jnp.where(is_group, logits, neg)
    gmax = jnp.max(gl, axis=-1, keepdims=True)
    gidx = jnp.min(jnp.where(gl == gmax, lane, far), axis=-1, keepdims=True)
    gsum = jnp.sum(jnp.where(is_group, jnp.exp(gl - gmax), 0.0), axis=-1, keepdims=True)
    g_p = 1.0 / gsum
    lo = N_GROUPS + EXPERTS_PER_GROUP * gidx
    in_sel = (lane >= lo) & (lane < lo + EXPERTS_PER_GROUP)
    el = jnp.where(in_sel, logits, neg)
    emax = jnp.max(el, axis=-1, keepdims=True)
    e1 = jnp.min(jnp.where(el == emax, lane, far), axis=-1, keepdims=True)
    esum = jnp.sum(jnp.where(in_sel, jnp.exp(el - emax), 0.0), axis=-1, keepdims=True)
    el2 = jnp.where(lane == e1, neg, el)
    m2 = jnp.max(el2, axis=-1, keepdims=True)
    e2 = jnp.min(jnp.where(el2 == m2, lane, far), axis=-1, keepdims=True)
    p1 = 1.0 / esum
    p2 = jnp.exp(m2 - emax) / esum
    den = p1 + p2
    idx_ref[...] = jnp.where(lane == 0, e1 - N_GROUPS, jnp.where(lane == 1, e2 - N_GROUPS, 0.0)).astype(jnp.int32)
    gate_ref[...] = jnp.where(lane == 0, g_p * p1 / den, jnp.where(lane == 1, g_p * p2 / den, 0.0))


def _router(m, x_p, x_s, g, w_r, b_r):
    t, d = m.shape
    t_p = x_p.shape[0]
    tr = ROW_TILE
    p_spec, s_spec = _two_group_specs((d,), tr, t_p // tr)
    row = pl.BlockSpec((tr, d), lambda i: (i, 0))
    lane_row = pl.BlockSpec((tr, LANES), lambda i: (i, 0))
    return pl.pallas_call(
        functools.partial(_router_kernel, t_p // tr),
        grid=(t // tr,),
        in_specs=[row, p_spec, s_spec, pl.BlockSpec((1, d), lambda i: (0, 0)),
                  pl.BlockSpec((d, LANES), lambda i: (0, 0)), pl.BlockSpec((1, LANES), lambda i: (0, 0))],
        out_specs=[row, pl.BlockSpec((tr, d // 2), lambda i: (i, 0)), lane_row, lane_row],
        out_shape=[jax.ShapeDtypeStruct((t, d), F32), jax.ShapeDtypeStruct((t, d // 2), jnp.uint32),
                   jax.ShapeDtypeStruct((t, LANES), jnp.int32), jax.ShapeDtypeStruct((t, LANES), F32)],
        compiler_params=_cp(("arbitrary",)),
        name="moe_router",
    )(m, x_p, x_s, g.reshape(1, d), w_r, b_r)


def _moe_kernel(n_used, item_e, item_start, item_n, tok, dst,
                hm_hbm, wg_ref, wu_ref, wd_ref, y_hbm,
                rows_in, rows_out, hid, sem_in, sem_out):
    del n_used, item_e
    i = pl.program_id(0)
    j = pl.program_id(1)
    n_items = pl.num_programs(0)
    n_steps = pl.num_programs(1)
    n_up = D_EXPERT // MOE_TJ
    max_blk = MOE_ROWS // MOE_BLK
    n, s0 = item_n[i], item_start[i]
    nxt = jnp.minimum(i + 1, n_items - 1)
    n_next, s_next = jnp.where(i + 1 < n_items, item_n[nxt], 0), item_start[nxt]
    prv = jnp.maximum(i - 1, 0)
    n_prev, s_prev = jnp.where(i > 0, item_n[prv], 0), item_start[prv]
    nblk = (n + MOE_BLK - 1) // MOE_BLK

    def gather(s, r):
        return pltpu.make_async_copy(hm_hbm.at[pl.ds(tok[s + r], 1), :], rows_in.at[pl.ds(r, 1), :], sem_in)

    def scatter(s, r):
        return pltpu.make_async_copy(rows_out.at[pl.ds(r, 1), :], y_hbm.at[pl.ds(dst[s + r], 1), :], sem_out)

    def gathered_group(c):
        rows = pl.ds(pl.multiple_of(c * MOE_DMA_UNROLL, MOE_DMA_UNROLL), MOE_DMA_UNROLL)
        return pltpu.make_async_copy(hm_hbm.at[pl.ds(0, MOE_DMA_UNROLL), :], rows_in.at[rows, :], sem_in)

    def scattered_group(c):
        rows = pl.ds(pl.multiple_of(c * MOE_DMA_UNROLL, MOE_DMA_UNROLL), MOE_DMA_UNROLL)
        return pltpu.make_async_copy(rows_out.at[rows, :], y_hbm.at[pl.ds(0, MOE_DMA_UNROLL), :], sem_out)

    def for_rows(count, fn, group_fn=None):
        groups = count // MOE_DMA_UNROLL

        def group(c, carry):
            if group_fn is None:
                for u in range(MOE_DMA_UNROLL):
                    fn(c * MOE_DMA_UNROLL + u)
            else:
                group_fn(c)
            return carry
        lax.fori_loop(0, groups, group, 0)

        def single(r, carry):
            fn(r)
            return carry
        lax.fori_loop(groups * MOE_DMA_UNROLL, count, single, 0)

    def for_row_count(fn):
        for k in range(1, max_blk + 1):
            @pl.when(nblk == k)
            def _(k=k):
                fn(k * MOE_BLK)

    @pl.when((i == 0) & (j == 0))
    def _():
        rows_in[...] = jnp.zeros_like(rows_in)
        for_rows(n, lambda r: gather(s0, r).start())

    @pl.when(j == 0)
    def _():
        for_rows(n, lambda r: gather(s0, r).wait(), lambda c: gathered_group(c).wait())

    @pl.when(j < n_up)
    def _():
        def up(m):
            x = _unpack_bf16_pairs(rows_in[0:m, :])
            act = _silu(_dot(x, wg_ref[0].astype(BF16))) * _dot(x, wu_ref[0].astype(BF16))
            hid[j, 0:m, :] = act.astype(BF16)
        for_row_count(up)

    @pl.when(j == n_up)
    def _():
        for_rows(n_next, lambda r: gather(s_next, r).start())
        for_rows(n_prev, lambda r: scatter(s_prev, r).wait(), lambda c: scattered_group(c).wait())

    @pl.when(j >= n_up)
    def _():
        col = pl.multiple_of((j - n_up) * MOE_TN, MOE_TN)

        def down(m):
            act = jnp.concatenate([hid[u, 0:m, :] for u in range(n_up)], axis=-1)
            rows_out[0:m, pl.ds(col, MOE_TN)] = _dot(act, wd_ref[0].astype(BF16))
        for_row_count(down)

    @pl.when(j == n_steps - 1)
    def _():
        for_rows(n, lambda r: scatter(s0, r).start())

    @pl.when((j == n_steps - 1) & (i == n_items - 1))
    def _():
        for_rows(n, lambda r: scatter(s0, r).wait(), lambda c: scattered_group(c).wait())


def _moe_experts(hm, w_gate, w_up, w_down, items, tok, dst, n_assign):
    n_used, item_e, item_start, item_n = items
    n_up = D_EXPERT // MOE_TJ
    n_dn = D_MODEL // MOE_TN

    def up_map(i, j, nu, e, st, n, tok, dst):
        return (e[i], 0, jnp.minimum(j, n_up - 1))

    def dn_map(i, j, nu, e, st, n, tok, dst):
        return (e[i], 0, jnp.maximum(j - n_up, 0))

    up_spec = pl.BlockSpec((1, D_MODEL, MOE_TJ), up_map)
    return pl.pallas_call(
        _moe_kernel,
        grid_spec=pltpu.PrefetchScalarGridSpec(
            num_scalar_prefetch=6,
            grid=(n_used[0], n_up + n_dn),
            in_specs=[pl.BlockSpec(memory_space=pl.ANY), up_spec, up_spec,
                      pl.BlockSpec((1, D_EXPERT, MOE_TN), dn_map)],
            out_specs=pl.BlockSpec(memory_space=pl.ANY),
            scratch_shapes=[pltpu.VMEM((MOE_ROWS, D_MODEL // 2), jnp.uint32),
                            pltpu.VMEM((MOE_ROWS, D_MODEL), F32),
                            pltpu.VMEM((n_up, MOE_ROWS, MOE_TJ), BF16),
                            pltpu.SemaphoreType.DMA(()),
                            pltpu.SemaphoreType.DMA(())],
        ),
        out_shape=jax.ShapeDtypeStruct((n_assign, D_MODEL), F32),
        compiler_params=_cp(("arbitrary", "arbitrary")),
        name="moe_experts",
    )(n_used, item_e, item_start, item_n, tok, dst, hm, w_gate, w_up, w_down)


def _moe_plan(expert_idx):
    n_assign = expert_idx.size
    flat_e = expert_idx.reshape(n_assign)
    order = jnp.argsort(flat_e).astype(jnp.int32)
    counts = jnp.bincount(flat_e, length=N_EXPERTS).astype(jnp.int32)
    start = jnp.cumsum(counts) - counts
    per_e = (counts + MOE_ROWS - 1) // MOE_ROWS
    item_end = jnp.cumsum(per_e)
    n_items = n_assign // MOE_ROWS + N_EXPERTS
    ids = jnp.arange(n_items, dtype=jnp.int32)
    used = ids < item_end[-1]
    last = jnp.maximum(item_end[-1] - 1, 0)
    e_of = jnp.minimum(jnp.searchsorted(item_end, jnp.minimum(ids, last), side="right"), N_EXPERTS - 1).astype(jnp.int32)
    local = jnp.minimum(ids, last) - (item_end - per_e)[e_of]
    item_start = start[e_of] + local * MOE_ROWS
    item_n = jnp.where(used, jnp.clip(counts[e_of] - local * MOE_ROWS, 0, MOE_ROWS), 0)
    items = (item_end[-1:].astype(jnp.int32), e_of, item_start.astype(jnp.int32), item_n.astype(jnp.int32))
    token = order // TOP_K
    dst_row = (order % TOP_K) * (n_assign // TOP_K) + token
    return items, token, dst_row


def _combine_kernel(x_ref, y0_ref, y1_ref, gate_ref, o_ref, ob_ref):
    x = x_ref[...] + (y0_ref[...] * gate_ref[:, 0:1] + y1_ref[...] * gate_ref[:, 1:2])
    o_ref[...] = x
    ob_ref[...] = x.astype(BF16)


def _combine(x, y, gate):
    t, d = x.shape
    tr = ROW_TILE
    row = pl.BlockSpec((tr, d), lambda i: (i, 0))
    return pl.pallas_call(
        _combine_kernel,
        grid=(t // tr,),
        in_specs=[row, row, pl.BlockSpec((tr, d), lambda i: (i + t // tr, 0)),
                  pl.BlockSpec((tr, LANES), lambda i: (i, 0))],
        out_specs=[row, row],
        out_shape=[jax.ShapeDtypeStruct((t, d), F32), jax.ShapeDtypeStruct((t, d), BF16)],
        compiler_params=_cp(("parallel",)),
        name="moe_combine",
    )(x, y, y, gate)


def _rope_tables(pos):
    half = RET_DK // 2
    inv = ROPE_BASE ** (-jnp.arange(half, dtype=F32) / half)
    ang = pos.astype(F32)[:, None] * inv[None, :]
    return jnp.cos(ang), jnp.sin(ang)


def kernel(x_prompt, x_sample, state_ret, state_gla, p_prompt, p_sample, g_mix, w_in, w_gla_up, b_gla,
           ret_norm_g, gla_norm_g, w_out, g_moe, w_rg, b_rg, w_re, b_re, w_gate, w_up, w_down, w_pg, w_pp,
           g_final):
    n_p, l_p, d = x_prompt.shape
    n_s, l_s, _ = x_sample.shape
    depth = g_mix.shape[0]
    t_p, t_s = n_p * l_p, n_s * l_s
    t = t_p + t_s

    log_g_py = [math.log1p(-(2.0 ** (-5.0 - h))) for h in range(RET_HEADS)]
    cos_p, sin_p = _rope_tables(jnp.arange(l_p, dtype=jnp.int32))
    cos_s, sin_s = _rope_tables(PAST_LEN + jnp.arange(l_s, dtype=jnp.int32))
    cos_s, sin_s = jnp.tile(cos_s, (DEC_PAIR, 1)), jnp.tile(sin_s, (DEC_PAIR, 1))

    x_p = x_prompt.reshape(t_p, d)
    x_s = x_sample.reshape(t_s, d)
    x = None
    ret_p, ret_s, gla_p, gla_s = [], [], [], []
    for l in range(depth):
        w_in_t = jnp.swapaxes(w_in[l], 0, 1)
        w_ga = jnp.pad(w_in_t[N_MAIN:], ((0, LANES - GLA_RANK), (0, 0))).astype(BF16)
        w_gup = jnp.pad(w_gla_up[l], ((0, LANES - GLA_RANK), (0, 0))).astype(BF16)
        w_r = jnp.pad(jnp.concatenate([w_rg[l], w_re[l]], axis=1),
                      ((0, 0), (0, LANES - N_GROUPS - N_EXPERTS)))
        b_r = jnp.pad(jnp.concatenate([b_rg[l], b_re[l]]), (0, LANES - N_GROUPS - N_EXPERTS)).reshape(1, LANES)
        p = jnp.concatenate([p_prompt[l].reshape(t_p, -1), p_sample[l].reshape(t_s, -1)], axis=0)
        if x is not None:
            x_p, x_s = x[:t_p], x[t_p:]

        h, log_a = _norm_in(x_p, x_s, g_mix[l], w_ga, w_gup, b_gla[l])
        proj = _in_proj(h, w_in_t, N_MAIN, t // 16, 1024)
        ro_p, sr_p, ro_s, sr_s = _ret_mix(proj, log_g_py, cos_p, sin_p, cos_s, sin_s, ret_norm_g[l], state_ret[l],
                                          n_p, l_p, n_s, l_s)
        go_p, sg_p, go_s, sg_s = _gla_mix(proj, log_a, gla_norm_g[l], state_gla[l], n_p, l_p, n_s, l_s)
        ret_p.append(sr_p)
        ret_s.append(sr_s)
        gla_p.append(sg_p)
        gla_s.append(sg_s)
        mix = _out_proj(ro_p, ro_s, go_p, go_s, w_out[l], 1024)

        x, hm, idx, gate = _router(mix, x_p, x_s, g_moe[l], w_r, b_r)
        items, tok, dst = _moe_plan(idx[:, :TOP_K])
        y = _moe_experts(hm, w_gate[l], w_up[l], w_down[l], items, tok, dst, t * TOP_K)
        x, xb = _combine(x, y, gate)

        x = _ple(xb, x, p, w_pg[l], w_pp[l], t // 8, 512)

    y_prompt = _norm_out(x, g_final, 0, t_p).reshape(n_p, l_p, d)
    y_sample = _norm_out(x, g_final, t_p, t_s).reshape(n_s, l_s, d)
    return (y_prompt, y_sample,
            jnp.stack(ret_p).astype(state_ret.dtype), jnp.stack(ret_s).astype(state_ret.dtype),
            jnp.stack(gla_p).astype(state_gla.dtype), jnp.stack(gla_s).astype(state_gla.dtype))
```

```python
import functools
import math

import jax
import jax.numpy as jnp
from jax import lax
from jax.experimental import pallas as pl
from jax.experimental.pallas import tpu as pltpu

F32 = jnp.float32
BF16 = jnp.bfloat16

D_MODEL = 4096
RET_HEADS = 8
RET_DK = 256
RET_DV = 256
GLA_HEADS = 4
GLA_DK = 256
GLA_DV = 512
GLA_RANK = 16
GLA_GATE_TEMP = 16.0
ROPE_BASE = 10000.0
PAST_LEN = 16384
N_GROUPS = 4
EXPERTS_PER_GROUP = 8
N_EXPERTS = N_GROUPS * EXPERTS_PER_GROUP
TOP_K = 2
D_EXPERT = D_MODEL // 4
EPS = 1e-6

RET_W = RET_HEADS * RET_DK
GLA_KW = GLA_HEADS * GLA_DK
GLA_VW = GLA_HEADS * GLA_DV
N_MAIN = 4 * RET_W + 2 * GLA_KW + 2 * GLA_VW
COL_GQ = 4 * RET_W
COL_GK = COL_GQ + GLA_KW
COL_GV = COL_GK + GLA_KW
COL_GG = COL_GV + GLA_VW

LANES = 128
ROW_TILE = 256
LHS_TILE = 512
RET_CHUNK = 128
GLA_CHUNK = 128
GLA_SUB = 32
DEC_PAIR = 2
MOE_ROWS = 768
MOE_BLK = 128
MOE_TJ = 256
MOE_TN = 2048
MOE_DMA_UNROLL = 8
VMEM_LIMIT = 56 * 1024 * 1024


def _cp(semantics, vmem=VMEM_LIMIT):
    return pltpu.CompilerParams(dimension_semantics=semantics, vmem_limit_bytes=vmem)


def _sigmoid(x):
    return 1.0 / (1.0 + jnp.exp(-x))


def _silu(x):
    return x * _sigmoid(x)


def _dot(a, b):
    return jnp.dot(a, b, preferred_element_type=F32)


def _dot_nt(a, b):
    return lax.dot_general(a, b, (((1,), (1,)), ((), ())), preferred_element_type=F32)


def _dot_tn(a, b):
    return lax.dot_general(a, b, (((0,), (0,)), ((), ())), preferred_element_type=F32)


def _rms_norm(x, g):
    ms = jnp.mean(x * x, axis=-1, keepdims=True)
    return x * lax.rsqrt(ms + EPS) * g


_HIGH_HALF = 0xFFFF0000


def _pack_bf16_pairs(x):
    half = x.shape[-1] // 2
    bits = lax.bitcast_convert_type(x.astype(BF16).astype(F32), jnp.uint32)
    return (bits[:, :half] >> 16) | (bits[:, half:] & jnp.uint32(_HIGH_HALF))


def _unpack_bf16_pairs(w):
    lo = lax.bitcast_convert_type(w << 16, F32)
    hi = lax.bitcast_convert_type(w & jnp.uint32(_HIGH_HALF), F32)
    return jnp.concatenate([lo, hi], axis=-1).astype(BF16)


def _two_group_specs(shape_tail, tile, n_p_tiles):
    zeros = (0,) * len(shape_tail)

    def p_map(*idx):
        return (jnp.minimum(idx[-1], n_p_tiles - 1),) + zeros

    def s_map(*idx):
        return (jnp.maximum(idx[-1] - n_p_tiles, 0),) + zeros

    return pl.BlockSpec((tile,) + shape_tail, p_map), pl.BlockSpec((tile,) + shape_tail, s_map)


def _log_sigmoid(z):
    return jnp.minimum(z, 0.0) - jnp.log(1.0 + jnp.exp(-jnp.abs(z)))


def _norm_in_kernel(n_p_tiles, xp_ref, xs_ref, g_ref, wga_ref, wup_ref, b_ref, h_ref, la_ref):
    i = pl.program_id(0)

    def emit(x):
        h = _rms_norm(x, g_ref[...]).astype(BF16)
        h_ref[...] = h
        ga = _dot_nt(h, wga_ref[...])
        z = _dot(ga.astype(BF16), wup_ref[...]) + b_ref[...]
        la_ref[...] = _log_sigmoid(z) * (1.0 / GLA_GATE_TEMP)

    @pl.when(i < n_p_tiles)
    def _():
        emit(xp_ref[...])

    @pl.when(i >= n_p_tiles)
    def _():
        emit(xs_ref[...])


def _norm_in(x_p, x_s, g, w_ga, w_up, b):
    (t_p, d), t_s = x_p.shape, x_s.shape[0]
    tr = ROW_TILE
    p_spec, s_spec = _two_group_specs((d,), tr, t_p // tr)
    return pl.pallas_call(
        functools.partial(_norm_in_kernel, t_p // tr),
        grid=((t_p + t_s) // tr,),
        in_specs=[p_spec, s_spec, pl.BlockSpec((1, d), lambda i: (0, 0)),
                  pl.BlockSpec((LANES, d), lambda i: (0, 0)),
                  pl.BlockSpec((LANES, GLA_KW), lambda i: (0, 0)),
                  pl.BlockSpec((1, GLA_KW), lambda i: (0, 0))],
        out_specs=[pl.BlockSpec((tr, d), lambda i: (i, 0)), pl.BlockSpec((tr, GLA_KW), lambda i: (i, 0))],
        out_shape=[jax.ShapeDtypeStruct((t_p + t_s, d), BF16), jax.ShapeDtypeStruct((t_p + t_s, GLA_KW), F32)],
        compiler_params=_cp(("arbitrary",)),
        name="norm_in",
    )(x_p, x_s, g.reshape(1, d), w_ga, w_up, b.reshape(1, GLA_KW))


def _norm_out_kernel(x_ref, g_ref, o_ref):
    o_ref[...] = _rms_norm(x_ref[...], g_ref[...])


def _norm_out(x, g, row0, n_rows):
    d = x.shape[1]
    tr = ROW_TILE
    blk0 = row0 // tr
    return pl.pallas_call(
        _norm_out_kernel,
        grid=(n_rows // tr,),
        in_specs=[pl.BlockSpec((tr, d), lambda i: (i + blk0, 0)), pl.BlockSpec((1, d), lambda i: (0, 0))],
        out_specs=pl.BlockSpec((tr, d), lambda i: (i, 0)),
        out_shape=jax.ShapeDtypeStruct((n_rows, d), F32),
        compiler_params=_cp(("parallel",)),
        name="norm_out",
    )(x, g.reshape(1, d))


def _in_proj_kernel(a_ref, wt_ref, o_ref):
    o_ref[...] = _dot_nt(a_ref[...], wt_ref[...].astype(BF16))


def _in_proj(h, w_t, n_cols, tm, tn):
    t, d = h.shape
    return pl.pallas_call(
        _in_proj_kernel,
        grid=(n_cols // tn, t // tm),
        in_specs=[pl.BlockSpec((tm, d), lambda j, i: (i, 0)), pl.BlockSpec((tn, d), lambda j, i: (j, 0))],
        out_specs=pl.BlockSpec((tm, tn), lambda j, i: (i, j)),
        out_shape=jax.ShapeDtypeStruct((t, n_cols), F32),
        compiler_params=_cp(("parallel", "parallel")),
        name="in_proj",
    )(h, w_t)


def _out_proj_kernel(n_p_tiles, rp_ref, rs_ref, gp_ref, gs_ref, w_ref, o_ref):
    i = pl.program_id(1)

    def emit(r_ref, g_ref):
        o_ref[...] = (_dot(r_ref[...], w_ref[:RET_W, :].astype(BF16))
                      + _dot(g_ref[...], w_ref[RET_W:, :].astype(BF16)))

    @pl.when(i < n_p_tiles)
    def _():
        emit(rp_ref, gp_ref)

    @pl.when(i >= n_p_tiles)
    def _():
        emit(rs_ref, gs_ref)


def _out_proj(ro_p, ro_s, go_p, go_s, w, tn):
    t_p, t_s = ro_p.shape[0], ro_s.shape[0]
    d_in, d_out = w.shape
    tm = LHS_TILE
    rp_spec, rs_spec = _two_group_specs((RET_W,), tm, t_p // tm)
    gp_spec, gs_spec = _two_group_specs((GLA_VW,), tm, t_p // tm)
    return pl.pallas_call(
        functools.partial(_out_proj_kernel, t_p // tm),
        grid=(d_out // tn, (t_p + t_s) // tm),
        in_specs=[rp_spec, rs_spec, gp_spec, gs_spec, pl.BlockSpec((d_in, tn), lambda j, i: (0, j))],
        out_specs=pl.BlockSpec((tm, tn), lambda j, i: (i, j)),
        out_shape=jax.ShapeDtypeStruct((t_p + t_s, d_out), F32),
        compiler_params=_cp(("arbitrary", "arbitrary")),
        name="out_proj",
    )(ro_p, ro_s, go_p, go_s, w)


def _ple_kernel(a_ref, w_ref, p_ref, wp_ref, r_ref, o_ref):
    gate = _sigmoid(_dot(a_ref[...], w_ref[...].astype(BF16)))
    emb = _dot(p_ref[...].astype(BF16), wp_ref[...].astype(BF16))
    o_ref[...] = r_ref[...] + gate * emb


def _ple(xb, x, p, w_pg, w_pp, tm, tn):
    t, d = x.shape
    kp = p.shape[1]
    o_spec = pl.BlockSpec((tm, tn), lambda j, i: (i, j))
    return pl.pallas_call(
        _ple_kernel,
        grid=(d // tn, t // tm),
        in_specs=[pl.BlockSpec((tm, d), lambda j, i: (i, 0)), pl.BlockSpec((d, tn), lambda j, i: (0, j)),
                  pl.BlockSpec((tm, kp), lambda j, i: (i, 0)), pl.BlockSpec((kp, tn), lambda j, i: (0, j)),
                  o_spec],
        out_specs=o_spec,
        out_shape=jax.ShapeDtypeStruct((t, d), F32),
        compiler_params=_cp(("parallel", "parallel")),
        name="ple",
    )(xb, w_pg, p, w_pp, x)


def _rotary(x, cos, sin):
    half = x.shape[-1] // 2
    x1, x2 = x[:, :half], x[:, half:]
    return jnp.concatenate([x1 * cos - x2 * sin, x1 * sin + x2 * cos], axis=-1)


def _group_norm_gate(o, gain, gate):
    mu = jnp.mean(o, axis=-1, keepdims=True)
    d = o - mu
    var = jnp.mean(d * d, axis=-1, keepdims=True)
    return d * lax.rsqrt(var + EPS) * gain * _silu(gate)


def _rms_gate(o, gain, gate):
    ms = jnp.mean(o * o, axis=-1, keepdims=True)
    return o * lax.rsqrt(ms + EPS) * gain * _silu(gate)


def _ret_prompt_body(log_g, first, i, n_chunks, q_ref, k_ref, v_ref, g_ref, cos_ref, sin_ref, gn_ref, o_ref, s_ref,
                     state, intra_tab, q_tab, k_tab):
    c = RET_CHUNK

    @pl.when(first)
    def _():
        row = lax.broadcasted_iota(jnp.int32, (c, RET_DK), 0).astype(F32)
        ii = lax.broadcasted_iota(jnp.int32, (c, c), 0)
        jj = lax.broadcasted_iota(jnp.int32, (c, c), 1)
        diff = (ii - jj).astype(F32)
        for h in range(RET_HEADS):
            intra_tab[h] = jnp.where(diff >= 0, jnp.exp(jnp.maximum(diff, 0.0) * log_g[h]), 0.0)
            q_tab[h] = jnp.exp((row + 1.0) * log_g[h])
            k_tab[h] = jnp.exp((c - 1.0 - row) * log_g[h])

    @pl.when(i == 0)
    def _():
        state[...] = jnp.zeros_like(state)

    cos, sin = cos_ref[...], sin_ref[...]
    for h in range(RET_HEADS):
        sl = slice(h * RET_DK, (h + 1) * RET_DK)
        q = _rotary(q_ref[:, sl], cos, sin)
        k = _rotary(k_ref[:, sl], cos, sin) * (RET_DK ** -0.5)
        v = v_ref[:, sl].astype(BF16)
        att = _dot_nt(q.astype(BF16), k.astype(BF16)) * intra_tab[h]
        s_old = state[h]
        o = _dot(att.astype(BF16), v) + _dot((q * q_tab[h]).astype(BF16), s_old.astype(BF16))
        state[h] = s_old * math.exp(c * log_g[h]) + _dot_tn((k * k_tab[h]).astype(BF16), v)
        o_ref[:, sl] = _group_norm_gate(o, gn_ref[h], g_ref[:, sl]).astype(o_ref.dtype)

    @pl.when(i == n_chunks - 1)
    def _():
        s_ref[0] = state[...]


def _ret_sample_body(log_g, seq, q_ref, k_ref, v_ref, g_ref, cos_ref, sin_ref, gn_ref, s_in, o_ref, s_out):
    rows = DEC_PAIR * seq
    cos, sin = cos_ref[...], sin_ref[...]
    rid = lax.broadcasted_iota(jnp.int32, (rows, RET_DK), 0)
    pos = (rid % seq).astype(F32)
    batch_of_row = rid // seq
    ii = lax.broadcasted_iota(jnp.int32, (rows, rows), 0)
    jj = lax.broadcasted_iota(jnp.int32, (rows, rows), 1)
    visible = (ii // seq == jj // seq) & (ii >= jj)
    diff = jnp.maximum(ii - jj, 0).astype(F32)
    for h in range(RET_HEADS):
        lg = log_g[h]
        sl = slice(h * RET_DK, (h + 1) * RET_DK)
        q = _rotary(q_ref[:, sl], cos, sin)
        k = _rotary(k_ref[:, sl], cos, sin) * (RET_DK ** -0.5)
        v = v_ref[:, sl].astype(BF16)
        intra = jnp.where(visible, jnp.exp(diff * lg), 0.0)
        att = _dot_nt(q.astype(BF16), k.astype(BF16)) * intra
        o = _dot(att.astype(BF16), v)
        qd = (q * jnp.exp((pos + 1.0) * lg)).astype(BF16)
        kd = k * jnp.exp((seq - 1.0 - pos) * lg)
        for b in range(DEC_PAIR):
            mine = batch_of_row == b
            s_old = s_in[b, h]
            o = o + jnp.where(mine, _dot(qd, s_old.astype(BF16)), 0.0)
            s_out[b, h] = s_old * math.exp(seq * lg) + _dot_tn(jnp.where(mine, kd, 0.0).astype(BF16), v)
        o_ref[:, sl] = _group_norm_gate(o, gn_ref[h], g_ref[:, sl]).astype(o_ref.dtype)


def _ret_mix_kernel(log_g, seq, n_chunks, qp, kp, vp, gp, cos_p, sin_p, gn, qs, ks, vs, gs, cos_s, sin_s, s_in,
                    o_p, s_p, o_s, s_out, state, intra_tab, q_tab, k_tab):
    s = pl.program_id(0)
    _ret_prompt_body(log_g, s == 0, s % n_chunks, n_chunks, qp, kp, vp, gp, cos_p, sin_p, gn, o_p, s_p,
                     state, intra_tab, q_tab, k_tab)
    _ret_sample_body(log_g, seq, qs, ks, vs, gs, cos_s, sin_s, gn, s_in, o_s, s_out)


def _ret_mix(proj, log_g, cos_p, sin_p, cos_s, sin_s, gn, state, n_p, l_p, n_s, l_s):
    c = RET_CHUNK
    nc = l_p // c
    rows = DEC_PAIR * l_s
    steps = n_p * nc
    assert steps * DEC_PAIR == n_s, (steps, n_s)
    blk0 = (n_p * l_p) // rows

    def col_p(group):
        return pl.BlockSpec((c, RET_W), lambda s: (s, group))

    def col_s(group):
        return pl.BlockSpec((rows, RET_W), lambda s: (s + blk0, group))

    tab_p = pl.BlockSpec((c, RET_DK // 2), lambda s: (s % nc, 0))
    tab_s = pl.BlockSpec((rows, RET_DK // 2), lambda s: (0, 0))
    st = pl.BlockSpec((DEC_PAIR, RET_HEADS, RET_DK, RET_DV), lambda s: (s, 0, 0, 0))
    return pl.pallas_call(
        functools.partial(_ret_mix_kernel, log_g, l_s, nc),
        grid=(steps,),
        in_specs=[col_p(0), col_p(1), col_p(2), col_p(3), tab_p, tab_p,
                  pl.BlockSpec((RET_HEADS, 1, RET_DV), lambda s: (0, 0, 0)),
                  col_s(0), col_s(1), col_s(2), col_s(3), tab_s, tab_s, st],
        out_specs=[pl.BlockSpec((c, RET_W), lambda s: (s, 0)),
                   pl.BlockSpec((1, RET_HEADS, RET_DK, RET_DV), lambda s: (s // nc, 0, 0, 0)),
                   pl.BlockSpec((rows, RET_W), lambda s: (s, 0)), st],
        out_shape=[jax.ShapeDtypeStruct((n_p * l_p, RET_W), BF16),
                   jax.ShapeDtypeStruct((n_p, RET_HEADS, RET_DK, RET_DV), F32),
                   jax.ShapeDtypeStruct((n_s * l_s, RET_W), BF16),
                   jax.ShapeDtypeStruct(state.shape, F32)],
        scratch_shapes=[pltpu.VMEM((RET_HEADS, RET_DK, RET_DV), F32),
                        pltpu.VMEM((RET_HEADS, c, c), F32),
                        pltpu.VMEM((RET_HEADS, c, RET_DK), F32),
                        pltpu.VMEM((RET_HEADS, c, RET_DK), F32)],
        compiler_params=_cp(("arbitrary",)),
        name="retention_mix",
    )(proj, proj, proj, proj, cos_p, sin_p, gn.reshape(RET_HEADS, 1, RET_DV),
      proj, proj, proj, proj, cos_s, sin_s, state)


def _split3(x):
    hi = x.astype(BF16)
    r1 = x - hi.astype(F32)
    mid = r1.astype(BF16)
    lo = (r1 - mid.astype(F32)).astype(BF16)
    return hi, mid, lo


def _column_scale(row_vec, width):
    n = row_vec.shape[-1]
    t = jnp.transpose(jnp.broadcast_to(row_vec, (LANES, n)))
    return jnp.concatenate([t] * (width // LANES), axis=-1)


def _gla_prompt_body(i, n_chunks, q_ref, k_ref, v_ref, g_ref, la_ref, gn_ref, o_ref, s_ref, state):
    c, sub = GLA_CHUNK, GLA_SUB
    nsub = c // sub

    @pl.when(i == 0)
    def _():
        state[...] = jnp.zeros_like(state)

    ii = lax.broadcasted_iota(jnp.int32, (c, c), 0)
    jj = lax.broadcasted_iota(jnp.int32, (c, c), 1)
    causal = ii >= jj
    tri = jnp.where(causal, 1.0, 0.0).astype(BF16)
    blk = lax.broadcasted_iota(jnp.int32, (c, GLA_DK), 0) // sub
    for h in range(GLA_HEADS):
        ks = slice(h * GLA_DK, (h + 1) * GLA_DK)
        vs = slice(h * GLA_DV, (h + 1) * GLA_DV)
        q = q_ref[:, ks]
        k = k_ref[:, ks] * (GLA_DK ** -0.5)
        v = v_ref[:, vs].astype(BF16)
        hi, mid, lo = _split3(la_ref[:, ks])
        b = _dot(tri, hi) + _dot(tri, mid) + _dot(tri, lo)
        mids = [b[s * sub + sub // 2 - 1: s * sub + sub // 2, :] for s in range(nsub)]
        ref_lvl = jnp.concatenate([jnp.broadcast_to(m, (sub, GLA_DK)) for m in mids], axis=0)
        qd = (q * jnp.exp(b - ref_lvl)).astype(BF16)
        kd = k * jnp.exp(ref_lvl - b)
        rows = []
        for s in range(nsub):
            scale = jnp.where(blk <= s, jnp.exp(jnp.minimum(mids[s] - ref_lvl, 0.0)), 0.0)
            rows.append(_dot_nt(qd[s * sub:(s + 1) * sub], (kd * scale).astype(BF16)))
        att = jnp.where(causal, jnp.concatenate(rows, axis=0), 0.0)
        s_old = state[h]
        o = _dot(att.astype(BF16), v) + _dot((q * jnp.exp(b)).astype(BF16), s_old.astype(BF16))
        b_last = b[c - 1:c, :]
        k_rem = (k * jnp.exp(b_last - b)).astype(BF16)
        state[h] = s_old * _column_scale(jnp.exp(b_last), GLA_DV) + _dot_tn(k_rem, v)
        o_ref[:, vs] = _rms_gate(o, gn_ref[h], g_ref[:, vs]).astype(o_ref.dtype)

    @pl.when(i == n_chunks - 1)
    def _():
        s_ref[0] = state[...]


def _gla_sample_body(seq, q_ref, k_ref, v_ref, g_ref, la_ref, gn_ref, s_in, o_ref, s_out):
    rows = DEC_PAIR * seq
    rid = lax.broadcasted_iota(jnp.int32, (rows, GLA_DK), 0)
    pos = rid % seq
    batch_of_row = rid // seq
    ii = lax.broadcasted_iota(jnp.int32, (rows, rows), 0)
    jj = lax.broadcasted_iota(jnp.int32, (rows, rows), 1)
    visible = (ii // seq == jj // seq) & (ii >= jj)
    for h in range(GLA_HEADS):
        ks = slice(h * GLA_DK, (h + 1) * GLA_DK)
        vs = slice(h * GLA_DV, (h + 1) * GLA_DV)
        la = la_ref[:, ks]
        b = la
        for d in range(1, seq):
            b = b + jnp.where(pos >= d, pltpu.roll(la, d, axis=0), 0.0)
        q = q_ref[:, ks]
        k = k_ref[:, ks] * (GLA_DK ** -0.5)
        v = v_ref[:, vs].astype(BF16)
        qb = (q * jnp.exp(b)).astype(BF16)
        kb = (k * jnp.exp(-b)).astype(BF16)
        att = jnp.where(visible, _dot_nt(qb, kb), 0.0)
        o = _dot(att.astype(BF16), v)
        for bi in range(DEC_PAIR):
            mine = batch_of_row == bi
            last = bi * seq + seq - 1
            b_last = b[last:last + 1, :]
            s_old = s_in[bi, h]
            o = o + jnp.where(mine[:, :1], _dot(qb, s_old.astype(BF16)), 0.0)
            k_rem = jnp.where(mine, k * jnp.exp(b_last - b), 0.0).astype(BF16)
            s_out[bi, h] = s_old * _column_scale(jnp.exp(b_last), GLA_DV) + _dot_tn(k_rem, v)
        o_ref[:, vs] = _rms_gate(o, gn_ref[h], g_ref[:, vs]).astype(o_ref.dtype)


def _gla_mix_kernel(seq, n_chunks, qp, kp, vp, gp, la_p, gn, qs, ks, vs, gs, la_s, s_in,
                    o_p, s_p, o_s, s_out, state):
    s = pl.program_id(0)
    _gla_prompt_body(s % n_chunks, n_chunks, qp, kp, vp, gp, la_p, gn, o_p, s_p, state)
    _gla_sample_body(seq, qs, ks, vs, gs, la_s, gn, s_in, o_s, s_out)


def _gla_mix(proj, log_a, gn, state, n_p, l_p, n_s, l_s):
    c = GLA_CHUNK
    nc = l_p // c
    rows = DEC_PAIR * l_s
    steps = n_p * nc
    assert steps * DEC_PAIR == n_s, (steps, n_s)
    blk0 = (n_p * l_p) // rows

    def col_p(start, width):
        return pl.BlockSpec((c, width), lambda s: (s, start // width))

    def col_s(start, width):
        return pl.BlockSpec((rows, width), lambda s: (s + blk0, start // width))

    st = pl.BlockSpec((DEC_PAIR, GLA_HEADS, GLA_DK, GLA_DV), lambda s: (s, 0, 0, 0))
    return pl.pallas_call(
        functools.partial(_gla_mix_kernel, l_s, nc),
        grid=(steps,),
        in_specs=[col_p(COL_GQ, GLA_KW), col_p(COL_GK, GLA_KW), col_p(COL_GV, GLA_VW), col_p(COL_GG, GLA_VW),
                  pl.BlockSpec((c, GLA_KW), lambda s: (s, 0)),
                  pl.BlockSpec((GLA_HEADS, 1, GLA_DV), lambda s: (0, 0, 0)),
                  col_s(COL_GQ, GLA_KW), col_s(COL_GK, GLA_KW), col_s(COL_GV, GLA_VW), col_s(COL_GG, GLA_VW),
                  pl.BlockSpec((rows, GLA_KW), lambda s: (s + blk0, 0)), st],
        out_specs=[pl.BlockSpec((c, GLA_VW), lambda s: (s, 0)),
                   pl.BlockSpec((1, GLA_HEADS, GLA_DK, GLA_DV), lambda s: (s // nc, 0, 0, 0)),
                   pl.BlockSpec((rows, GLA_VW), lambda s: (s, 0)), st],
        out_shape=[jax.ShapeDtypeStruct((n_p * l_p, GLA_VW), BF16),
                   jax.ShapeDtypeStruct((n_p, GLA_HEADS, GLA_DK, GLA_DV), F32),
                   jax.ShapeDtypeStruct((n_s * l_s, GLA_VW), BF16),
                   jax.ShapeDtypeStruct(state.shape, F32)],
        scratch_shapes=[pltpu.VMEM((GLA_HEADS, GLA_DK, GLA_DV), F32)],
        compiler_params=_cp(("arbitrary",)),
        name="gla_mix",
    )(proj, proj, proj, proj, log_a, gn.reshape(GLA_HEADS, 1, GLA_DV),
      proj, proj, proj, proj, log_a, state)


def _router_kernel(n_p_tiles, m_ref, xp_ref, xs_ref, g_ref, wr_ref, br_ref, x_ref, hm_ref, idx_ref, gate_ref):
    tr = m_ref.shape[0]
    i = pl.program_id(0)

    @pl.when(i < n_p_tiles)
    def _():
        x_ref[...] = xp_ref[...] + m_ref[...]

    @pl.when(i >= n_p_tiles)
    def _():
        x_ref[...] = xs_ref[...] + m_ref[...]

    hm = _rms_norm(x_ref[...], g_ref[...])
    hm_ref[...] = _pack_bf16_pairs(hm)
    h1, h2, _ = _split3(hm)
    w = wr_ref[...]
    w1 = w.astype(BF16)
    w2 = (w - w1.astype(F32)).astype(BF16)
    logits = _dot(h1, w1) + _dot(h1, w2) + _dot(h2, w1) + br_ref[...]
    lane = lax.broadcasted_iota(jnp.int32, (tr, LANES), 1).astype(F32)
    neg, far = -1e30, 1e9
    is_group = lane < N_GROUPS
    gl = jnp.where(is_group, logits, neg)
    gmax = jnp.max(gl, axis=-1, keepdims=True)
    gidx = jnp.min(jnp.where(gl == gmax, lane, far), axis=-1, keepdims=True)
    gsum = jnp.sum(jnp.where(is_group, jnp.exp(gl - gmax), 0.0), axis=-1, keepdims=True)
    g_p = 1.0 / gsum
    lo = N_GROUPS + EXPERTS_PER_GROUP * gidx
    in_sel = (lane >= lo) & (lane < lo + EXPERTS_PER_GROUP)
    el = jnp.where(in_sel, logits, neg)
    emax = jnp.max(el, axis=-1, keepdims=True)
    e1 = jnp.min(jnp.where(el == emax, lane, far), axis=-1, keepdims=True)
    esum = jnp.sum(jnp.where(in_sel, jnp.exp(el - emax), 0.0), axis=-1, keepdims=True)
    el2 = jnp.where(lane == e1, neg, el)
    m2 = jnp.max(el2, axis=-1, keepdims=True)
    e2 = jnp.min(jnp.where(el2 == m2, lane, far), axis=-1, keepdims=True)
    p1 = 1.0 / esum
    p2 = jnp.exp(m2 - emax) / esum
    den = p1 + p2
    idx_ref[...] = jnp.where(lane == 0, e1 - N_GROUPS, jnp.where(lane == 1, e2 - N_GROUPS, 0.0)).astype(jnp.int32)
    gate_ref[...] = jnp.where(lane == 0, g_p * p1 / den, jnp.where(lane == 1, g_p * p2 / den, 0.0))


def _router(m, x_p, x_s, g, w_r, b_r):
    t, d = m.shape
    t_p = x_p.shape[0]
    tr = ROW_TILE
    p_spec, s_spec = _two_group_specs((d,), tr, t_p // tr)
    row = pl.BlockSpec((tr, d), lambda i: (i, 0))
    lane_row = pl.BlockSpec((tr, LANES), lambda i: (i, 0))
    return pl.pallas_call(
        functools.partial(_router_kernel, t_p // tr),
        grid=(t // tr,),
        in_specs=[row, p_spec, s_spec, pl.BlockSpec((1, d), lambda i: (0, 0)),
                  pl.BlockSpec((d, LANES), lambda i: (0, 0)), pl.BlockSpec((1, LANES), lambda i: (0, 0))],
        out_specs=[row, pl.BlockSpec((tr, d // 2), lambda i: (i, 0)), lane_row, lane_row],
        out_shape=[jax.ShapeDtypeStruct((t, d), F32), jax.ShapeDtypeStruct((t, d // 2), jnp.uint32),
                   jax.ShapeDtypeStruct((t, LANES), jnp.int32), jax.ShapeDtypeStruct((t, LANES), F32)],
        compiler_params=_cp(("arbitrary",)),
        name="moe_router",
    )(m, x_p, x_s, g.reshape(1, d), w_r, b_r)


def _moe_kernel(n_used, item_e, item_start, item_n, tok, dst,
                hm_hbm, wg_ref, wu_ref, wd_ref, y_hbm,
                rows_in, rows_out, hid, sem_in, sem_out):
    del n_used, item_e
    i = pl.program_id(0)
    j = pl.program_id(1)
    n_items = pl.num_programs(0)
    n_steps = pl.num_programs(1)
    n_up = D_EXPERT // MOE_TJ
    max_blk = MOE_ROWS // MOE_BLK
    n, s0 = item_n[i], item_start[i]
    nxt = jnp.minimum(i + 1, n_items - 1)
    n_next, s_next = jnp.where(i + 1 < n_items, item_n[nxt], 0), item_start[nxt]
    prv = jnp.maximum(i - 1, 0)
    n_prev, s_prev = jnp.where(i > 0, item_n[prv], 0), item_start[prv]
    nblk = (n + MOE_BLK - 1) // MOE_BLK

    def gather(s, r):
        return pltpu.make_async_copy(hm_hbm.at[pl.ds(tok[s + r], 1), :], rows_in.at[pl.ds(r, 1), :], sem_in)

    def scatter(s, r):
        return pltpu.make_async_copy(rows_out.at[pl.ds(r, 1), :], y_hbm.at[pl.ds(dst[s + r], 1), :], sem_out)

    def gathered_group(c):
        rows = pl.ds(pl.multiple_of(c * MOE_DMA_UNROLL, MOE_DMA_UNROLL), MOE_DMA_UNROLL)
        return pltpu.make_async_copy(hm_hbm.at[pl.ds(0, MOE_DMA_UNROLL), :], rows_in.at[rows, :], sem_in)

    def scattered_group(c):
        rows = pl.ds(pl.multiple_of(c * MOE_DMA_UNROLL, MOE_DMA_UNROLL), MOE_DMA_UNROLL)
        return pltpu.make_async_copy(rows_out.at[rows, :], y_hbm.at[pl.ds(0, MOE_DMA_UNROLL), :], sem_out)

    def for_rows(count, fn, group_fn=None):
        groups = count // MOE_DMA_UNROLL

        def group(c, carry):
            if group_fn is None:
                for u in range(MOE_DMA_UNROLL):
                    fn(c * MOE_DMA_UNROLL + u)
            else:
                group_fn(c)
            return carry
        lax.fori_loop(0, groups, group, 0)

        def single(r, carry):
            fn(r)
            return carry
        lax.fori_loop(groups * MOE_DMA_UNROLL, count, single, 0)

    def for_row_count(fn):
        for k in range(1, max_blk + 1):
            @pl.when(nblk == k)
            def _(k=k):
                fn(k * MOE_BLK)

    @pl.when((i == 0) & (j == 0))
    def _():
        rows_in[...] = jnp.zeros_like(rows_in)
        for_rows(n, lambda r: gather(s0, r).start())

    @pl.when(j == 0)
    def _():
        for_rows(n, lambda r: gather(s0, r).wait(), lambda c: gathered_group(c).wait())

    @pl.when(j < n_up)
    def _():
        def up(m):
            x = _unpack_bf16_pairs(rows_in[0:m, :])
            act = _silu(_dot(x, wg_ref[0].astype(BF16))) * _dot(x, wu_ref[0].astype(BF16))
            hid[j, 0:m, :] = act.astype(BF16)
        for_row_count(up)

    @pl.when(j == n_up)
    def _():
        for_rows(n_next, lambda r: gather(s_next, r).start())
        for_rows(n_prev, lambda r: scatter(s_prev, r).wait(), lambda c: scattered_group(c).wait())

    @pl.when(j >= n_up)
    def _():
        col = pl.multiple_of((j - n_up) * MOE_TN, MOE_TN)

        def down(m):
            act = jnp.concatenate([hid[u, 0:m, :] for u in range(n_up)], axis=-1)
            rows_out[0:m, pl.ds(col, MOE_TN)] = _dot(act, wd_ref[0].astype(BF16))
        for_row_count(down)

    @pl.when(j == n_steps - 1)
    def _():
        for_rows(n, lambda r: scatter(s0, r).start())

    @pl.when((j == n_steps - 1) & (i == n_items - 1))
    def _():
        for_rows(n, lambda r: scatter(s0, r).wait(), lambda c: scattered_group(c).wait())


def _moe_experts(hm, w_gate, w_up, w_down, items, tok, dst, n_assign):
    n_used, item_e, item_start, item_n = items
    n_up = D_EXPERT // MOE_TJ
    n_dn = D_MODEL // MOE_TN

    def up_map(i, j, nu, e, st, n, tok, dst):
        return (e[i], 0, jnp.minimum(j, n_up - 1))

    def dn_map(i, j, nu, e, st, n, tok, dst):
        return (e[i], 0, jnp.maximum(j - n_up, 0))

    up_spec = pl.BlockSpec((1, D_MODEL, MOE_TJ), up_map)
    return pl.pallas_call(
        _moe_kernel,
        grid_spec=pltpu.PrefetchScalarGridSpec(
            num_scalar_prefetch=6,
            grid=(n_used[0], n_up + n_dn),
            in_specs=[pl.BlockSpec(memory_space=pl.ANY), up_spec, up_spec,
                      pl.BlockSpec((1, D_EXPERT, MOE_TN), dn_map)],
            out_specs=pl.BlockSpec(memory_space=pl.ANY),
            scratch_shapes=[pltpu.VMEM((MOE_ROWS, D_MODEL // 2), jnp.uint32),
                            pltpu.VMEM((MOE_ROWS, D_MODEL), F32),
                            pltpu.VMEM((n_up, MOE_ROWS, MOE_TJ), BF16),
                            pltpu.SemaphoreType.DMA(()),
                            pltpu.SemaphoreType.DMA(())],
        ),
        out_shape=jax.ShapeDtypeStruct((n_assign, D_MODEL), F32),
        compiler_params=_cp(("arbitrary", "arbitrary")),
        name="moe_experts",
    )(n_used, item_e, item_start, item_n, tok, dst, hm, w_gate, w_up, w_down)


def _moe_plan(expert_idx):
    n_assign = expert_idx.size
    flat_e = expert_idx.reshape(n_assign)
    order = jnp.argsort(flat_e).astype(jnp.int32)
    counts = jnp.bincount(flat_e, length=N_EXPERTS).astype(jnp.int32)
    start = jnp.cumsum(counts) - counts
    per_e = (counts + MOE_ROWS - 1) // MOE_ROWS
    item_end = jnp.cumsum(per_e)
    n_items = n_assign // MOE_ROWS + N_EXPERTS
    ids = jnp.arange(n_items, dtype=jnp.int32)
    used = ids < item_end[-1]
    last = jnp.maximum(item_end[-1] - 1, 0)
    e_of = jnp.minimum(jnp.searchsorted(item_end, jnp.minimum(ids, last), side="right"), N_EXPERTS - 1).astype(jnp.int32)
    local = jnp.minimum(ids, last) - (item_end - per_e)[e_of]
    item_start = start[e_of] + local * MOE_ROWS
    item_n = jnp.where(used, jnp.clip(counts[e_of] - local * MOE_ROWS, 0, MOE_ROWS), 0)
    items = (item_end[-1:].astype(jnp.int32), e_of, item_start.astype(jnp.int32), item_n.astype(jnp.int32))
    token = order // TOP_K
    dst_row = (order % TOP_K) * (n_assign // TOP_K) + token
    return items, token, dst_row


def _combine_kernel(x_ref, y0_ref, y1_ref, gate_ref, o_ref, ob_ref):
    x = x_ref[...] + (y0_ref[...] * gate_ref[:, 0:1] + y1_ref[...] * gate_ref[:, 1:2])
    o_ref[...] = x
    ob_ref[...] = x.astype(BF16)


def _combine(x, y, gate):
    t, d = x.shape
    tr = ROW_TILE
    row = pl.BlockSpec((tr, d), lambda i: (i, 0))
    return pl.pallas_call(
        _combine_kernel,
        grid=(t // tr,),
        in_specs=[row, row, pl.BlockSpec((tr, d), lambda i: (i + t // tr, 0)),
                  pl.BlockSpec((tr, LANES), lambda i: (i, 0))],
        out_specs=[row, row],
        out_shape=[jax.ShapeDtypeStruct((t, d), F32), jax.ShapeDtypeStruct((t, d), BF16)],
        compiler_params=_cp(("parallel",)),
        name="moe_combine",
    )(x, y, y, gate)


def _rope_tables(pos):
    half = RET_DK // 2
    inv = ROPE_BASE ** (-jnp.arange(half, dtype=F32) / half)
    ang = pos.astype(F32)[:, None] * inv[None, :]
    return jnp.cos(ang), jnp.sin(ang)


def kernel(x_prompt, x_sample, state_ret, state_gla, p_prompt, p_sample, g_mix, w_in, w_gla_up, b_gla,
           ret_norm_g, gla_norm_g, w_out, g_moe, w_rg, b_rg, w_re, b_re, w_gate, w_up, w_down, w_pg, w_pp,
           g_final):
    n_p, l_p, d = x_prompt.shape
    n_s, l_s, _ = x_sample.shape
    depth = g_mix.shape[0]
    t_p, t_s = n_p * l_p, n_s * l_s
    t = t_p + t_s

    log_g_py = [math.log1p(-(2.0 ** (-5.0 - h))) for h in range(RET_HEADS)]
    cos_p, sin_p = _rope_tables(jnp.arange(l_p, dtype=jnp.int32))
    cos_s, sin_s = _rope_tables(PAST_LEN + jnp.arange(l_s, dtype=jnp.int32))
    cos_s, sin_s = jnp.tile(cos_s, (DEC_PAIR, 1)), jnp.tile(sin_s, (DEC_PAIR, 1))

    x_p = x_prompt.reshape(t_p, d)
    x_s = x_sample.reshape(t_s, d)
    x = None
    ret_p, ret_s, gla_p, gla_s = [], [], [], []
    for l in range(depth):
        w_in_t = jnp.swapaxes(w_in[l], 0, 1)
        w_ga = jnp.pad(w_in_t[N_MAIN:], ((0, LANES - GLA_RANK), (0, 0))).astype(BF16)
        w_gup = jnp.pad(w_gla_up[l], ((0, LANES - GLA_RANK), (0, 0))).astype(BF16)
        w_r = jnp.pad(jnp.concatenate([w_rg[l], w_re[l]], axis=1),
                      ((0, 0), (0, LANES - N_GROUPS - N_EXPERTS)))
        b_r = jnp.pad(jnp.concatenate([b_rg[l], b_re[l]]), (0, LANES - N_GROUPS - N_EXPERTS)).reshape(1, LANES)
        p = jnp.concatenate([p_prompt[l].reshape(t_p, -1), p_sample[l].reshape(t_s, -1)], axis=0)
        if x is not None:
            x_p, x_s = x[:t_p], x[t_p:]

        h, log_a = _norm_in(x_p, x_s, g_mix[l], w_ga, w_gup, b_gla[l])
        proj = _in_proj(h, w_in_t, N_MAIN, t // 16, 1024)
        ro_p, sr_p, ro_s, sr_s = _ret_mix(proj, log_g_py, cos_p, sin_p, cos_s, sin_s, ret_norm_g[l], state_ret[l],
                                          n_p, l_p, n_s, l_s)
        go_p, sg_p, go_s, sg_s = _gla_mix(proj, log_a, gla_norm_g[l], state_gla[l], n_p, l_p, n_s, l_s)
        ret_p.append(sr_p)
        ret_s.append(sr_s)
        gla_p.append(sg_p)
        gla_s.append(sg_s)
        mix = _out_proj(ro_p, ro_s, go_p, go_s, w_out[l], 1024)

        x, hm, idx, gate = _router(mix, x_p, x_s, g_moe[l], w_r, b_r)
        items, tok, dst = _moe_plan(idx[:, :TOP_K])
        y = _moe_experts(hm, w_gate[l], w_up[l], w_down[l], items, tok, dst, t * TOP_K)
        x, xb = _combine(x, y, gate)

        x = _ple(xb, x, p, w_pg[l], w_pp[l], t // 8, 512)

    y_prompt = _norm_out(x, g_final, 0, t_p).reshape(n_p, l_p, d)
    y_sample = _norm_out(x, g_final, t_p, t_s).reshape(n_s, l_s, d)
    return (y_prompt, y_sample,
            jnp.stack(ret_p).astype(state_ret.dtype), jnp.stack(ret_s).astype(state_ret.dtype),
            jnp.stack(gla_p).astype(state_gla.dtype), jnp.stack(gla_s).astype(state_gla.dtype))
```

```python
import functools
import math

import jax
import jax.numpy as jnp
from jax import lax
from jax.experimental import pallas as pl
from jax.experimental.pallas import tpu as pltpu

F32 = jnp.float32
BF16 = jnp.bfloat16

D_MODEL = 4096
RET_HEADS = 8
RET_DK = 256
RET_DV = 256
GLA_HEADS = 4
GLA_DK = 256
GLA_DV = 512
GLA_RANK = 16
GLA_GATE_TEMP = 16.0
ROPE_BASE = 10000.0
PAST_LEN = 16384
N_GROUPS = 4
EXPERTS_PER_GROUP = 8
N_EXPERTS = N_GROUPS * EXPERTS_PER_GROUP
TOP_K = 2
D_EXPERT = D_MODEL // 4
EPS = 1e-6

RET_W = RET_HEADS * RET_DK
GLA_KW = GLA_HEADS * GLA_DK
GLA_VW = GLA_HEADS * GLA_DV
N_MAIN = 4 * RET_W + 2 * GLA_KW + 2 * GLA_VW
COL_GQ = 4 * RET_W
COL_GK = COL_GQ + GLA_KW
COL_GV = COL_GK + GLA_KW
COL_GG = COL_GV + GLA_VW

LANES = 128
ROW_TILE = 256
LHS_TILE = 512
RET_CHUNK = 128
GLA_CHUNK = 128
GLA_SUB = 32
DEC_PAIR = 2
MOE_ROWS = 768
MOE_BLK = 64
MOE_TJ = 256
MOE_TN = 2048
MOE_DMA_UNROLL = 8
VMEM_LIMIT = 56 * 1024 * 1024


def _cp(semantics, vmem=VMEM_LIMIT):
    return pltpu.CompilerParams(dimension_semantics=semantics, vmem_limit_bytes=vmem)


def _sigmoid(x):
    return 1.0 / (1.0 + jnp.exp(-x))


def _silu(x):
    return x * _sigmoid(x)


def _dot(a, b):
    return jnp.dot(a, b, preferred_element_type=F32)


def _dot_nt(a, b):
    return lax.dot_general(a, b, (((1,), (1,)), ((), ())), preferred_element_type=F32)


def _dot_tn(a, b):
    return lax.dot_general(a, b, (((0,), (0,)), ((), ())), preferred_element_type=F32)


def _rms_norm(x, g):
    ms = jnp.mean(x * x, axis=-1, keepdims=True)
    return x * lax.rsqrt(ms + EPS) * g


_HIGH_HALF = 0xFFFF0000


def _pack_bf16_pairs(x):
    half = x.shape[-1] // 2
    bits = lax.bitcast_convert_type(x.astype(BF16).astype(F32), jnp.uint32)
    return (bits[:, :half] >> 16) | (bits[:, half:] & jnp.uint32(_HIGH_HALF))


def _unpack_bf16_pairs(w):
    lo = lax.bitcast_convert_type(w << 16, F32)
    hi = lax.bitcast_convert_type(w & jnp.uint32(_HIGH_HALF), F32)
    return jnp.concatenate([lo, hi], axis=-1).astype(BF16)


def _two_group_specs(shape_tail, tile, n_p_tiles):
    zeros = (0,) * len(shape_tail)

    def p_map(*idx):
        return (jnp.minimum(idx[-1], n_p_tiles - 1),) + zeros

    def s_map(*idx):
        return (jnp.maximum(idx[-1] - n_p_tiles, 0),) + zeros

    return pl.BlockSpec((tile,) + shape_tail, p_map), pl.BlockSpec((tile,) + shape_tail, s_map)


def _log_sigmoid(z):
    return jnp.minimum(z, 0.0) - jnp.log(1.0 + jnp.exp(-jnp.abs(z)))


def _norm_in_kernel(n_p_tiles, xp_ref, xs_ref, g_ref, wga_ref, wup_ref, b_ref, h_ref, la_ref):
    i = pl.program_id(0)

    def emit(x):
        h = _rms_norm(x, g_ref[...]).astype(BF16)
        h_ref[...] = h
        ga = _dot_nt(h, wga_ref[...])
        z = _dot(ga.astype(BF16), wup_ref[...]) + b_ref[...]
        la_ref[...] = _log_sigmoid(z) * (1.0 / GLA_GATE_TEMP)

    @pl.when(i < n_p_tiles)
    def _():
        emit(xp_ref[...])

    @pl.when(i >= n_p_tiles)
    def _():
        emit(xs_ref[...])


def _norm_in(x_p, x_s, g, w_ga, w_up, b):
    (t_p, d), t_s = x_p.shape, x_s.shape[0]
    tr = ROW_TILE
    p_spec, s_spec = _two_group_specs((d,), tr, t_p // tr)
    return pl.pallas_call(
        functools.partial(_norm_in_kernel, t_p // tr),
        grid=((t_p + t_s) // tr,),
        in_specs=[p_spec, s_spec, pl.BlockSpec((1, d), lambda i: (0, 0)),
                  pl.BlockSpec((LANES, d), lambda i: (0, 0)),
                  pl.BlockSpec((LANES, GLA_KW), lambda i: (0, 0)),
                  pl.BlockSpec((1, GLA_KW), lambda i: (0, 0))],
        out_specs=[pl.BlockSpec((tr, d), lambda i: (i, 0)), pl.BlockSpec((tr, GLA_KW), lambda i: (i, 0))],
        out_shape=[jax.ShapeDtypeStruct((t_p + t_s, d), BF16), jax.ShapeDtypeStruct((t_p + t_s, GLA_KW), F32)],
        compiler_params=_cp(("arbitrary",)),
        name="norm_in",
    )(x_p, x_s, g.reshape(1, d), w_ga, w_up, b.reshape(1, GLA_KW))


def _norm_out_kernel(x_ref, g_ref, o_ref):
    o_ref[...] = _rms_norm(x_ref[...], g_ref[...])


def _norm_out(x, g, row0, n_rows):
    d = x.shape[1]
    tr = ROW_TILE
    blk0 = row0 // tr
    return pl.pallas_call(
        _norm_out_kernel,
        grid=(n_rows // tr,),
        in_specs=[pl.BlockSpec((tr, d), lambda i: (i + blk0, 0)), pl.BlockSpec((1, d), lambda i: (0, 0))],
        out_specs=pl.BlockSpec((tr, d), lambda i: (i, 0)),
        out_shape=jax.ShapeDtypeStruct((n_rows, d), F32),
        compiler_params=_cp(("parallel",)),
        name="norm_out",
    )(x, g.reshape(1, d))


def _in_proj_kernel(a_ref, wt_ref, o_ref):
    o_ref[...] = _dot_nt(a_ref[...], wt_ref[...].astype(BF16))


def _in_proj(h, w_t, n_cols, tm, tn):
    t, d = h.shape
    return pl.pallas_call(
        _in_proj_kernel,
        grid=(n_cols // tn, t // tm),
        in_specs=[pl.BlockSpec((tm, d), lambda j, i: (i, 0)), pl.BlockSpec((tn, d), lambda j, i: (j, 0))],
        out_specs=pl.BlockSpec((tm, tn), lambda j, i: (i, j)),
        out_shape=jax.ShapeDtypeStruct((t, n_cols), F32),
        compiler_params=_cp(("parallel", "parallel")),
        name="in_proj",
    )(h, w_t)


def _out_proj_kernel(n_p_tiles, rp_ref, rs_ref, gp_ref, gs_ref, w_ref, o_ref):
    i = pl.program_id(1)

    def emit(r_ref, g_ref):
        o_ref[...] = (_dot(r_ref[...], w_ref[:RET_W, :].astype(BF16))
                      + _dot(g_ref[...], w_ref[RET_W:, :].astype(BF16)))

    @pl.when(i < n_p_tiles)
    def _():
        emit(rp_ref, gp_ref)

    @pl.when(i >= n_p_tiles)
    def _():
        emit(rs_ref, gs_ref)


def _out_proj(ro_p, ro_s, go_p, go_s, w, tn):
    t_p, t_s = ro_p.shape[0], ro_s.shape[0]
    d_in, d_out = w.shape
    tm = LHS_TILE
    rp_spec, rs_spec = _two_group_specs((RET_W,), tm, t_p // tm)
    gp_spec, gs_spec = _two_group_specs((GLA_VW,), tm, t_p // tm)
    return pl.pallas_call(
        functools.partial(_out_proj_kernel, t_p // tm),
        grid=(d_out // tn, (t_p + t_s) // tm),
        in_specs=[rp_spec, rs_spec, gp_spec, gs_spec, pl.BlockSpec((d_in, tn), lambda j, i: (0, j))],
        out_specs=pl.BlockSpec((tm, tn), lambda j, i: (i, j)),
        out_shape=jax.ShapeDtypeStruct((t_p + t_s, d_out), F32),
        compiler_params=_cp(("arbitrary", "arbitrary")),
        name="out_proj",
    )(ro_p, ro_s, go_p, go_s, w)


def _ple_kernel(a_ref, w_ref, p_ref, wp_ref, r_ref, o_ref):
    gate = _sigmoid(_dot(a_ref[...], w_ref[...].astype(BF16)))
    emb = _dot(p_ref[...].astype(BF16), wp_ref[...].astype(BF16))
    o_ref[...] = r_ref[...] + gate * emb


def _ple(xb, x, p, w_pg, w_pp, tm, tn):
    t, d = x.shape
    kp = p.shape[1]
    o_spec = pl.BlockSpec((tm, tn), lambda j, i: (i, j))
    return pl.pallas_call(
        _ple_kernel,
        grid=(d // tn, t // tm),
        in_specs=[pl.BlockSpec((tm, d), lambda j, i: (i, 0)), pl.BlockSpec((d, tn), lambda j, i: (0, j)),
                  pl.BlockSpec((tm, kp), lambda j, i: (i, 0)), pl.BlockSpec((kp, tn), lambda j, i: (0, j)),
                  o_spec],
        out_specs=o_spec,
        out_shape=jax.ShapeDtypeStruct((t, d), F32),
        compiler_params=_cp(("parallel", "parallel")),
        name="ple",
    )(xb, w_pg, p, w_pp, x)


def _rotary(x, cos, sin):
    half = x.shape[-1] // 2
    x1, x2 = x[:, :half], x[:, half:]
    return jnp.concatenate([x1 * cos - x2 * sin, x1 * sin + x2 * cos], axis=-1)


def _group_norm_gate(o, gain, gate):
    mu = jnp.mean(o, axis=-1, keepdims=True)
    d = o - mu
    var = jnp.mean(d * d, axis=-1, keepdims=True)
    return d * lax.rsqrt(var + EPS) * gain * _silu(gate)


def _rms_gate(o, gain, gate):
    ms = jnp.mean(o * o, axis=-1, keepdims=True)
    return o * lax.rsqrt(ms + EPS) * gain * _silu(gate)


def _ret_prompt_body(log_g, first, i, n_chunks, q_ref, k_ref, v_ref, g_ref, cos_ref, sin_ref, gn_ref, o_ref, s_ref,
                     state, intra_tab, q_tab, k_tab):
    c = RET_CHUNK

    @pl.when(first)
    def _():
        row = lax.broadcasted_iota(jnp.int32, (c, RET_DK), 0).astype(F32)
        ii = lax.broadcasted_iota(jnp.int32, (c, c), 0)
        jj = lax.broadcasted_iota(jnp.int32, (c, c), 1)
        diff = (ii - jj).astype(F32)
        for h in range(RET_HEADS):
            intra_tab[h] = jnp.where(diff >= 0, jnp.exp(jnp.maximum(diff, 0.0) * log_g[h]), 0.0)
            q_tab[h] = jnp.exp((row + 1.0) * log_g[h])
            k_tab[h] = jnp.exp((c - 1.0 - row) * log_g[h])

    @pl.when(i == 0)
    def _():
        state[...] = jnp.zeros_like(state)

    cos, sin = cos_ref[...], sin_ref[...]
    for h in range(RET_HEADS):
        sl = slice(h * RET_DK, (h + 1) * RET_DK)
        q = _rotary(q_ref[:, sl], cos, sin)
        k = _rotary(k_ref[:, sl], cos, sin) * (RET_DK ** -0.5)
        v = v_ref[:, sl].astype(BF16)
        att = _dot_nt(q.astype(BF16), k.astype(BF16)) * intra_tab[h]
        s_old = state[h]
        o = _dot(att.astype(BF16), v) + _dot((q * q_tab[h]).astype(BF16), s_old.astype(BF16))
        state[h] = s_old * math.exp(c * log_g[h]) + _dot_tn((k * k_tab[h]).astype(BF16), v)
        o_ref[:, sl] = _group_norm_gate(o, gn_ref[h], g_ref[:, sl]).astype(o_ref.dtype)

    @pl.when(i == n_chunks - 1)
    def _():
        s_ref[0] = state[...]


def _ret_sample_body(log_g, seq, q_ref, k_ref, v_ref, g_ref, cos_ref, sin_ref, gn_ref, s_in, o_ref, s_out):
    rows = DEC_PAIR * seq
    cos, sin = cos_ref[...], sin_ref[...]
    rid = lax.broadcasted_iota(jnp.int32, (rows, RET_DK), 0)
    pos = (rid % seq).astype(F32)
    batch_of_row = rid // seq
    ii = lax.broadcasted_iota(jnp.int32, (rows, rows), 0)
    jj = lax.broadcasted_iota(jnp.int32, (rows, rows), 1)
    visible = (ii // seq == jj // seq) & (ii >= jj)
    diff = jnp.maximum(ii - jj, 0).astype(F32)
    for h in range(RET_HEADS):
        lg = log_g[h]
        sl = slice(h * RET_DK, (h + 1) * RET_DK)
        q = _rotary(q_ref[:, sl], cos, sin)
        k = _rotary(k_ref[:, sl], cos, sin) * (RET_DK ** -0.5)
        v = v_ref[:, sl].astype(BF16)
        intra = jnp.where(visible, jnp.exp(diff * lg), 0.0)
        att = _dot_nt(q.astype(BF16), k.astype(BF16)) * intra
        o = _dot(att.astype(BF16), v)
        qd = (q * jnp.exp((pos + 1.0) * lg)).astype(BF16)
        kd = k * jnp.exp((seq - 1.0 - pos) * lg)
        for b in range(DEC_PAIR):
            mine = batch_of_row == b
            s_old = s_in[b, h]
            o = o + jnp.where(mine, _dot(qd, s_old.astype(BF16)), 0.0)
            s_out[b, h] = s_old * math.exp(seq * lg) + _dot_tn(jnp.where(mine, kd, 0.0).astype(BF16), v)
        o_ref[:, sl] = _group_norm_gate(o, gn_ref[h], g_ref[:, sl]).astype(o_ref.dtype)


def _ret_mix_kernel(log_g, seq, n_chunks, qp, kp, vp, gp, cos_p, sin_p, gn, qs, ks, vs, gs, cos_s, sin_s, s_in,
                    o_p, s_p, o_s, s_out, state, intra_tab, q_tab, k_tab):
    s = pl.program_id(0)
    _ret_prompt_body(log_g, s == 0, s % n_chunks, n_chunks, qp, kp, vp, gp, cos_p, sin_p, gn, o_p, s_p,
                     state, intra_tab, q_tab, k_tab)
    _ret_sample_body(log_g, seq, qs, ks, vs, gs, cos_s, sin_s, gn, s_in, o_s, s_out)


def _ret_mix(proj, log_g, cos_p, sin_p, cos_s, sin_s, gn, state, n_p, l_p, n_s, l_s):
    c = RET_CHUNK
    nc = l_p // c
    rows = DEC_PAIR * l_s
    steps = n_p * nc
    assert steps * DEC_PAIR == n_s, (steps, n_s)
    blk0 = (n_p * l_p) // rows

    def col_p(group):
        return pl.BlockSpec((c, RET_W), lambda s: (s, group))

    def col_s(group):
        return pl.BlockSpec((rows, RET_W), lambda s: (s + blk0, group))

    tab_p = pl.BlockSpec((c, RET_DK // 2), lambda s: (s % nc, 0))
    tab_s = pl.BlockSpec((rows, RET_DK // 2), lambda s: (0, 0))
    st = pl.BlockSpec((DEC_PAIR, RET_HEADS, RET_DK, RET_DV), lambda s: (s, 0, 0, 0))
    return pl.pallas_call(
        functools.partial(_ret_mix_kernel, log_g, l_s, nc),
        grid=(steps,),
        in_specs=[col_p(0), col_p(1), col_p(2), col_p(3), tab_p, tab_p,
                  pl.BlockSpec((RET_HEADS, 1, RET_DV), lambda s: (0, 0, 0)),
                  col_s(0), col_s(1), col_s(2), col_s(3), tab_s, tab_s, st],
        out_specs=[pl.BlockSpec((c, RET_W), lambda s: (s, 0)),
                   pl.BlockSpec((1, RET_HEADS, RET_DK, RET_DV), lambda s: (s // nc, 0, 0, 0)),
                   pl.BlockSpec((rows, RET_W), lambda s: (s, 0)), st],
        out_shape=[jax.ShapeDtypeStruct((n_p * l_p, RET_W), BF16),
                   jax.ShapeDtypeStruct((n_p, RET_HEADS, RET_DK, RET_DV), F32),
                   jax.ShapeDtypeStruct((n_s * l_s, RET_W), BF16),
                   jax.ShapeDtypeStruct(state.shape, F32)],
        scratch_shapes=[pltpu.VMEM((RET_HEADS, RET_DK, RET_DV), F32),
                        pltpu.VMEM((RET_HEADS, c, c), F32),
                        pltpu.VMEM((RET_HEADS, c, RET_DK), F32),
                        pltpu.VMEM((RET_HEADS, c, RET_DK), F32)],
        compiler_params=_cp(("arbitrary",)),
        name="retention_mix",
    )(proj, proj, proj, proj, cos_p, sin_p, gn.reshape(RET_HEADS, 1, RET_DV),
      proj, proj, proj, proj, cos_s, sin_s, state)


def _split3(x):
    hi = x.astype(BF16)
    r1 = x - hi.astype(F32)
    mid = r1.astype(BF16)
    lo = (r1 - mid.astype(F32)).astype(BF16)
    return hi, mid, lo


def _column_scale(row_vec, width):
    n = row_vec.shape[-1]
    t = jnp.transpose(jnp.broadcast_to(row_vec, (LANES, n)))
    return jnp.concatenate([t] * (width // LANES), axis=-1)


def _gla_prompt_body(i, n_chunks, q_ref, k_ref, v_ref, g_ref, la_ref, gn_ref, o_ref, s_ref, state):
    c, sub = GLA_CHUNK, GLA_SUB
    nsub = c // sub

    @pl.when(i == 0)
    def _():
        state[...] = jnp.zeros_like(state)

    ii = lax.broadcasted_iota(jnp.int32, (c, c), 0)
    jj = lax.broadcasted_iota(jnp.int32, (c, c), 1)
    causal = ii >= jj
    tri = jnp.where(causal, 1.0, 0.0).astype(BF16)
    blk = lax.broadcasted_iota(jnp.int32, (c, GLA_DK), 0) // sub
    for h in range(GLA_HEADS):
        ks = slice(h * GLA_DK, (h + 1) * GLA_DK)
        vs = slice(h * GLA_DV, (h + 1) * GLA_DV)
        q = q_ref[:, ks]
        k = k_ref[:, ks] * (GLA_DK ** -0.5)
        v = v_ref[:, vs].astype(BF16)
        hi, mid, lo = _split3(la_ref[:, ks])
        b = _dot(tri, hi) + _dot(tri, mid) + _dot(tri, lo)
        mids = [b[s * sub + sub // 2 - 1: s * sub + sub // 2, :] for s in range(nsub)]
        ref_lvl = jnp.concatenate([jnp.broadcast_to(m, (sub, GLA_DK)) for m in mids], axis=0)
        qd = (q * jnp.exp(b - ref_lvl)).astype(BF16)
        kd = k * jnp.exp(ref_lvl - b)
        rows = []
        for s in range(nsub):
            scale = jnp.where(blk <= s, jnp.exp(jnp.minimum(mids[s] - ref_lvl, 0.0)), 0.0)
            rows.append(_dot_nt(qd[s * sub:(s + 1) * sub], (kd * scale).astype(BF16)))
        att = jnp.where(causal, jnp.concatenate(rows, axis=0), 0.0)
        s_old = state[h]
        o = _dot(att.astype(BF16), v) + _dot((q * jnp.exp(b)).astype(BF16), s_old.astype(BF16))
        b_last = b[c - 1:c, :]
        k_rem = (k * jnp.exp(b_last - b)).astype(BF16)
        state[h] = s_old * _column_scale(jnp.exp(b_last), GLA_DV) + _dot_tn(k_rem, v)
        o_ref[:, vs] = _rms_gate(o, gn_ref[h], g_ref[:, vs]).astype(o_ref.dtype)

    @pl.when(i == n_chunks - 1)
    def _():
        s_ref[0] = state[...]


def _gla_sample_body(seq, q_ref, k_ref, v_ref, g_ref, la_ref, gn_ref, s_in, o_ref, s_out):
    rows = DEC_PAIR * seq
    rid = lax.broadcasted_iota(jnp.int32, (rows, GLA_DK), 0)
    pos = rid % seq
    batch_of_row = rid // seq
    ii = lax.broadcasted_iota(jnp.int32, (rows, rows), 0)
    jj = lax.broadcasted_iota(jnp.int32, (rows, rows), 1)
    visible = (ii // seq == jj // seq) & (ii >= jj)
    for h in range(GLA_HEADS):
        ks = slice(h * GLA_DK, (h + 1) * GLA_DK)
        vs = slice(h * GLA_DV, (h + 1) * GLA_DV)
        la = la_ref[:, ks]
        b = la
        for d in range(1, seq):
            b = b + jnp.where(pos >= d, pltpu.roll(la, d, axis=0), 0.0)
        q = q_ref[:, ks]
        k = k_ref[:, ks] * (GLA_DK ** -0.5)
        v = v_ref[:, vs].astype(BF16)
        qb = (q * jnp.exp(b)).astype(BF16)
        kb = (k * jnp.exp(-b)).astype(BF16)
        att = jnp.where(visible, _dot_nt(qb, kb), 0.0)
        o = _dot(att.astype(BF16), v)
        for bi in range(DEC_PAIR):
            mine = batch_of_row == bi
            last = bi * seq + seq - 1
            b_last = b[last:last + 1, :]
            s_old = s_in[bi, h]
            o = o + jnp.where(mine[:, :1], _dot(qb, s_old.astype(BF16)), 0.0)
            k_rem = jnp.where(mine, k * jnp.exp(b_last - b), 0.0).astype(BF16)
            s_out[bi, h] = s_old * _column_scale(jnp.exp(b_last), GLA_DV) + _dot_tn(k_rem, v)
        o_ref[:, vs] = _rms_gate(o, gn_ref[h], g_ref[:, vs]).astype(o_ref.dtype)


def _gla_mix_kernel(seq, n_chunks, qp, kp, vp, gp, la_p, gn, qs, ks, vs, gs, la_s, s_in,
                    o_p, s_p, o_s, s_out, state):
    s = pl.program_id(0)
    _gla_prompt_body(s % n_chunks, n_chunks, qp, kp, vp, gp, la_p, gn, o_p, s_p, state)
    _gla_sample_body(seq, qs, ks, vs, gs, la_s, gn, s_in, o_s, s_out)


def _gla_mix(proj, log_a, gn, state, n_p, l_p, n_s, l_s):
    c = GLA_CHUNK
    nc = l_p // c
    rows = DEC_PAIR * l_s
    steps = n_p * nc
    assert steps * DEC_PAIR == n_s, (steps, n_s)
    blk0 = (n_p * l_p) // rows

    def col_p(start, width):
        return pl.BlockSpec((c, width), lambda s: (s, start // width))

    def col_s(start, width):
        return pl.BlockSpec((rows, width), lambda s: (s + blk0, start // width))

    st = pl.BlockSpec((DEC_PAIR, GLA_HEADS, GLA_DK, GLA_DV), lambda s: (s, 0, 0, 0))
    return pl.pallas_call(
        functools.partial(_gla_mix_kernel, l_s, nc),
        grid=(steps,),
        in_specs=[col_p(COL_GQ, GLA_KW), col_p(COL_GK, GLA_KW), col_p(COL_GV, GLA_VW), col_p(COL_GG, GLA_VW),
                  pl.BlockSpec((c, GLA_KW), lambda s: (s, 0)),
                  pl.BlockSpec((GLA_HEADS, 1, GLA_DV), lambda s: (0, 0, 0)),
                  col_s(COL_GQ, GLA_KW), col_s(COL_GK, GLA_KW), col_s(COL_GV, GLA_VW), col_s(COL_GG, GLA_VW),
                  pl.BlockSpec((rows, GLA_KW), lambda s: (s + blk0, 0)), st],
        out_specs=[pl.BlockSpec((c, GLA_VW), lambda s: (s, 0)),
                   pl.BlockSpec((1, GLA_HEADS, GLA_DK, GLA_DV), lambda s: (s // nc, 0, 0, 0)),
                   pl.BlockSpec((rows, GLA_VW), lambda s: (s, 0)), st],
        out_shape=[jax.ShapeDtypeStruct((n_p * l_p, GLA_VW), BF16),
                   jax.ShapeDtypeStruct((n_p, GLA_HEADS, GLA_DK, GLA_DV), F32),
                   jax.ShapeDtypeStruct((n_s * l_s, GLA_VW), BF16),
                   jax.ShapeDtypeStruct(state.shape, F32)],
        scratch_shapes=[pltpu.VMEM((GLA_HEADS, GLA_DK, GLA_DV), F32)],
        compiler_params=_cp(("arbitrary",)),
        name="gla_mix",
    )(proj, proj, proj, proj, log_a, gn.reshape(GLA_HEADS, 1, GLA_DV),
      proj, proj, proj, proj, log_a, state)


def _router_kernel(n_p_tiles, m_ref, xp_ref, xs_ref, g_ref, wr_ref, br_ref, x_ref, hm_ref, idx_ref, gate_ref):
    tr = m_ref.shape[0]
    i = pl.program_id(0)

    @pl.when(i < n_p_tiles)
    def _():
        x_ref[...] = xp_ref[...] + m_ref[...]

    @pl.when(i >= n_p_tiles)
    def _():
        x_ref[...] = xs_ref[...] + m_ref[...]

    hm = _rms_norm(x_ref[...], g_ref[...])
    hm_ref[...] = _pack_bf16_pairs(hm)
    h1, h2, _ = _split3(hm)
    w = wr_ref[...]
    w1 = w.astype(BF16)
    w2 = (w - w1.astype(F32)).astype(BF16)
    logits = _dot(h1, w1) + _dot(h1, w2) + _dot(h2, w1) + br_ref[...]
    lane = lax.broadcasted_iota(jnp.int32, (tr, LANES), 1).astype(F32)
    neg, far = -1e30, 1e9
    is_group = lane < N_GROUPS
    gl = jnp.where(is_group, logits, neg)
    gmax = jnp.max(gl, axis=-1, keepdims=True)
    gidx = jnp.min(jnp.where(gl == gmax, lane, far), axis=-1, keepdims=True)
    gsum = jnp.sum(jnp.where(is_group, jnp.exp(gl - gmax), 0.0), axis=-1, keepdims=True)
    g_p = 1.0 / gsum
    lo = N_GROUPS + EXPERTS_PER_GROUP * gidx
    in_sel = (lane >= lo) & (lane < lo + EXPERTS_PER_GROUP)
    el = jnp.where(in_sel, logits, neg)
    emax = jnp.max(el, axis=-1, keepdims=True)
    e1 = jnp.min(jnp.where(el == emax, lane, far), axis=-1, keepdims=True)
    esum = jnp.sum(jnp.where(in_sel, jnp.exp(el - emax), 0.0), axis=-1, keepdims=True)
    el2 = jnp.where(lane == e1, neg, el)
    m2 = jnp.max(el2, axis=-1, keepdims=True)
    e2 = jnp.min(jnp.where(el2 == m2, lane, far), axis=-1, keepdims=True)
    p1 = 1.0 / esum
    p2 = jnp.exp(m2 - emax) / esum
    den = p1 + p2
    idx_ref[...] = jnp.where(lane == 0, e1 - N_GROUPS, jnp.where(lane == 1, e2 - N_GROUPS, 0.0)).astype(jnp.int32)
    gate_ref[...] = jnp.where(lane == 0, g_p * p1 / den, jnp.where(lane == 1, g_p * p2 / den, 0.0))


def _router(m, x_p, x_s, g, w_r, b_r):
    t, d = m.shape
    t_p = x_p.shape[0]
    tr = ROW_TILE
    p_spec, s_spec = _two_group_specs((d,), tr, t_p // tr)
    row = pl.BlockSpec((tr, d), lambda i: (i, 0))
    lane_row = pl.BlockSpec((tr, LANES), lambda i: (i, 0))
    return pl.pallas_call(
        functools.partial(_router_kernel, t_p // tr),
        grid=(t // tr,),
        in_specs=[row, p_spec, s_spec, pl.BlockSpec((1, d), lambda i: (0, 0)),
                  pl.BlockSpec((d, LANES), lambda i: (0, 0)), pl.BlockSpec((1, LANES), lambda i: (0, 0))],
        out_specs=[row, pl.BlockSpec((tr, d // 2), lambda i: (i, 0)), lane_row, lane_row],
        out_shape=[jax.ShapeDtypeStruct((t, d), F32), jax.ShapeDtypeStruct((t, d // 2), jnp.uint32),
                   jax.ShapeDtypeStruct((t, LANES), jnp.int32), jax.ShapeDtypeStruct((t, LANES), F32)],
        compiler_params=_cp(("arbitrary",)),
        name="moe_router",
    )(m, x_p, x_s, g.reshape(1, d), w_r, b_r)


def _moe_kernel(n_used, item_e, item_start, item_n, tok, dst,
                hm_hbm, wg_ref, wu_ref, wd_ref, y_hbm,
                rows_in, rows_out, hid, sem_in, sem_out):
    del n_used, item_e
    i = pl.program_id(0)
    j = pl.program_id(1)
    n_items = pl.num_programs(0)
    n_steps = pl.num_programs(1)
    n_up = D_EXPERT // MOE_TJ
    max_blk = MOE_ROWS // MOE_BLK
    n, s0 = item_n[i], item_start[i]
    nxt = jnp.minimum(i + 1, n_items - 1)
    n_next, s_next = jnp.where(i + 1 < n_items, item_n[nxt], 0), item_start[nxt]
    prv = jnp.maximum(i - 1, 0)
    n_prev, s_prev = jnp.where(i > 0, item_n[prv], 0), item_start[prv]
    nblk = (n + MOE_BLK - 1) // MOE_BLK

    def gather(s, r):
        return pltpu.make_async_copy(hm_hbm.at[pl.ds(tok[s + r], 1), :], rows_in.at[pl.ds(r, 1), :], sem_in)

    def scatter(s, r):
        return pltpu.make_async_copy(rows_out.at[pl.ds(r, 1), :], y_hbm.at[pl.ds(dst[s + r], 1), :], sem_out)

    def gathered_group(c):
        rows = pl.ds(pl.multiple_of(c * MOE_DMA_UNROLL, MOE_DMA_UNROLL), MOE_DMA_UNROLL)
        return pltpu.make_async_copy(hm_hbm.at[pl.ds(0, MOE_DMA_UNROLL), :], rows_in.at[rows, :], sem_in)

    def scattered_group(c):
        rows = pl.ds(pl.multiple_of(c * MOE_DMA_UNROLL, MOE_DMA_UNROLL), MOE_DMA_UNROLL)
        return pltpu.make_async_copy(rows_out.at[rows, :], y_hbm.at[pl.ds(0, MOE_DMA_UNROLL), :], sem_out)

    def for_rows(count, fn, group_fn=None):
        groups = count // MOE_DMA_UNROLL

        def group(c, carry):
            if group_fn is None:
                for u in range(MOE_DMA_UNROLL):
                    fn(c * MOE_DMA_UNROLL + u)
            else:
                group_fn(c)
            return carry
        lax.fori_loop(0, groups, group, 0)

        def single(r, carry):
            fn(r)
            return carry
        lax.fori_loop(groups * MOE_DMA_UNROLL, count, single, 0)

    def for_row_count(fn):
        for k in range(1, max_blk + 1):
            @pl.when(nblk == k)
            def _(k=k):
                fn(k * MOE_BLK)

    @pl.when((i == 0) & (j == 0))
    def _():
        rows_in[...] = jnp.zeros_like(rows_in)
        for_rows(n, lambda r: gather(s0, r).start())

    @pl.when(j == 0)
    def _():
        for_rows(n, lambda r: gather(s0, r).wait(), lambda c: gathered_group(c).wait())

    @pl.when(j < n_up)
    def _():
        def up(m):
            x = _unpack_bf16_pairs(rows_in[0:m, :])
            act = _silu(_dot(x, wg_ref[0].astype(BF16))) * _dot(x, wu_ref[0].astype(BF16))
            hid[j, 0:m, :] = act.astype(BF16)
        for_row_count(up)

    @pl.when(j == n_up)
    def _():
        for_rows(n_next, lambda r: gather(s_next, r).start())
        for_rows(n_prev, lambda r: scatter(s_prev, r).wait(), lambda c: scattered_group(c).wait())

    @pl.when(j >= n_up)
    def _():
        col = pl.multiple_of((j - n_up) * MOE_TN, MOE_TN)

        def down(m):
            act = jnp.concatenate([hid[u, 0:m, :] for u in range(n_up)], axis=-1)
            rows_out[0:m, pl.ds(col, MOE_TN)] = _dot(act, wd_ref[0].astype(BF16))
        for_row_count(down)

    @pl.when(j == n_steps - 1)
    def _():
        for_rows(n, lambda r: scatter(s0, r).start())

    @pl.when((j == n_steps - 1) & (i == n_items - 1))
    def _():
        for_rows(n, lambda r: scatter(s0, r).wait(), lambda c: scattered_group(c).wait())


def _moe_experts(hm, w_gate, w_up, w_down, items, tok, dst, n_assign):
    n_used, item_e, item_start, item_n = items
    n_up = D_EXPERT // MOE_TJ
    n_dn = D_MODEL // MOE_TN

    def up_map(i, j, nu, e, st, n, tok, dst):
        return (e[i], 0, jnp.minimum(j, n_up - 1))

    def dn_map(i, j, nu, e, st, n, tok, dst):
        return (e[i], 0, jnp.maximum(j - n_up, 0))

    up_spec = pl.BlockSpec((1, D_MODEL, MOE_TJ), up_map)
    return pl.pallas_call(
        _moe_kernel,
        grid_spec=pltpu.PrefetchScalarGridSpec(
            num_scalar_prefetch=6,
            grid=(n_used[0], n_up + n_dn),
            in_specs=[pl.BlockSpec(memory_space=pl.ANY), up_spec, up_spec,
                      pl.BlockSpec((1, D_EXPERT, MOE_TN), dn_map)],
            out_specs=pl.BlockSpec(memory_space=pl.ANY),
            scratch_shapes=[pltpu.VMEM((MOE_ROWS, D_MODEL // 2), jnp.uint32),
                            pltpu.VMEM((MOE_ROWS, D_MODEL), F32),
                            pltpu.VMEM((n_up, MOE_ROWS, MOE_TJ), BF16),
                            pltpu.SemaphoreType.DMA(()),
                            pltpu.SemaphoreType.DMA(())],
        ),
        out_shape=jax.ShapeDtypeStruct((n_assign, D_MODEL), F32),
        compiler_params=_cp(("arbitrary", "arbitrary")),
        name="moe_experts",
    )(n_used, item_e, item_start, item_n, tok, dst, hm, w_gate, w_up, w_down)


def _moe_plan(expert_idx):
    n_assign = expert_idx.size
    flat_e = expert_idx.reshape(n_assign)
    order = jnp.argsort(flat_e).astype(jnp.int32)
    counts = jnp.bincount(flat_e, length=N_EXPERTS).astype(jnp.int32)
    start = jnp.cumsum(counts) - counts
    per_e = (counts + MOE_ROWS - 1) // MOE_ROWS
    item_end = jnp.cumsum(per_e)
    n_items = n_assign // MOE_ROWS + N_EXPERTS
    ids = jnp.arange(n_items, dtype=jnp.int32)
    used = ids < item_end[-1]
    last = jnp.maximum(item_end[-1] - 1, 0)
    e_of = jnp.minimum(jnp.searchsorted(item_end, jnp.minimum(ids, last), side="right"), N_EXPERTS - 1).astype(jnp.int32)
    local = jnp.minimum(ids, last) - (item_end - per_e)[e_of]
    item_start = start[e_of] + local * MOE_ROWS
    item_n = jnp.where(used, jnp.clip(counts[e_of] - local * MOE_ROWS, 0, MOE_ROWS), 0)
    items = (item_end[-1:].astype(jnp.int32), e_of, item_start.astype(jnp.int32), item_n.astype(jnp.int32))
    token = order // TOP_K
    dst_row = (order % TOP_K) * (n_assign // TOP_K) + token
    return items, token, dst_row


def _combine_kernel(x_ref, y0_ref, y1_ref, gate_ref, o_ref, ob_ref):
    x = x_ref[...] + (y0_ref[...] * gate_ref[:, 0:1] + y1_ref[...] * gate_ref[:, 1:2])
    o_ref[...] = x
    ob_ref[...] = x.astype(BF16)


def _combine(x, y, gate):
    t, d = x.shape
    tr = ROW_TILE
    row = pl.BlockSpec((tr, d), lambda i: (i, 0))
    return pl.pallas_call(
        _combine_kernel,
        grid=(t // tr,),
        in_specs=[row, row, pl.BlockSpec((tr, d), lambda i: (i + t // tr, 0)),
                  pl.BlockSpec((tr, LANES), lambda i: (i, 0))],
        out_specs=[row, row],
        out_shape=[jax.ShapeDtypeStruct((t, d), F32), jax.ShapeDtypeStruct((t, d), BF16)],
        compiler_params=_cp(("parallel",)),
        name="moe_combine",
    )(x, y, y, gate)


def _rope_tables(pos):
    half = RET_DK // 2
    inv = ROPE_BASE ** (-jnp.arange(half, dtype=F32) / half)
    ang = pos.astype(F32)[:, None] * inv[None, :]
    return jnp.cos(ang), jnp.sin(ang)


def kernel(x_prompt, x_sample, state_ret, state_gla, p_prompt, p_sample, g_mix, w_in, w_gla_up, b_gla,
           ret_norm_g, gla_norm_g, w_out, g_moe, w_rg, b_rg, w_re, b_re, w_gate, w_up, w_down, w_pg, w_pp,
           g_final):
    n_p, l_p, d = x_prompt.shape
    n_s, l_s, _ = x_sample.shape
    depth = g_mix.shape[0]
    t_p, t_s = n_p * l_p, n_s * l_s
    t = t_p + t_s

    log_g_py = [math.log1p(-(2.0 ** (-5.0 - h))) for h in range(RET_HEADS)]
    cos_p, sin_p = _rope_tables(jnp.arange(l_p, dtype=jnp.int32))
    cos_s, sin_s = _rope_tables(PAST_LEN + jnp.arange(l_s, dtype=jnp.int32))
    cos_s, sin_s = jnp.tile(cos_s, (DEC_PAIR, 1)), jnp.tile(sin_s, (DEC_PAIR, 1))

    x_p = x_prompt.reshape(t_p, d)
    x_s = x_sample.reshape(t_s, d)
    x = None
    ret_p, ret_s, gla_p, gla_s = [], [], [], []
    for l in range(depth):
        w_in_t = jnp.swapaxes(w_in[l], 0, 1)
        w_ga = jnp.pad(w_in_t[N_MAIN:], ((0, LANES - GLA_RANK), (0, 0))).astype(BF16)
        w_gup = jnp.pad(w_gla_up[l], ((0, LANES - GLA_RANK), (0, 0))).astype(BF16)
        w_r = jnp.pad(jnp.concatenate([w_rg[l], w_re[l]], axis=1),
                      ((0, 0), (0, LANES - N_GROUPS - N_EXPERTS)))
        b_r = jnp.pad(jnp.concatenate([b_rg[l], b_re[l]]), (0, LANES - N_GROUPS - N_EXPERTS)).reshape(1, LANES)
        p = jnp.concatenate([p_prompt[l].reshape(t_p, -1), p_sample[l].reshape(t_s, -1)], axis=0)
        if x is not None:
            x_p, x_s = x[:t_p], x[t_p:]

        h, log_a = _norm_in(x_p, x_s, g_mix[l], w_ga, w_gup, b_gla[l])
        proj = _in_proj(h, w_in_t, N_MAIN, t // 16, 1024)
        ro_p, sr_p, ro_s, sr_s = _ret_mix(proj, log_g_py, cos_p, sin_p, cos_s, sin_s, ret_norm_g[l], state_ret[l],
                                          n_p, l_p, n_s, l_s)
        go_p, sg_p, go_s, sg_s = _gla_mix(proj, log_a, gla_norm_g[l], state_gla[l], n_p, l_p, n_s, l_s)
        ret_p.append(sr_p)
        ret_s.append(sr_s)
        gla_p.append(sg_p)
        gla_s.append(sg_s)
        mix = _out_proj(ro_p, ro_s, go_p, go_s, w_out[l], 1024)

        x, hm, idx, gate = _router(mix, x_p, x_s, g_moe[l], w_r, b_r)
        items, tok, dst = _moe_plan(idx[:, :TOP_K])
        y = _moe_experts(hm, w_gate[l], w_up[l], w_down[l], items, tok, dst, t * TOP_K)
        x, xb = _combine(x, y, gate)

        x = _ple(xb, x, p, w_pg[l], w_pp[l], t // 8, 512)

    y_prompt = _norm_out(x, g_final, 0, t_p).reshape(n_p, l_p, d)
    y_sample = _norm_out(x, g_final, t_p, t_s).reshape(n_s, l_s, d)
    return (y_prompt, y_sample,
            jnp.stack(ret_p).astype(state_ret.dtype), jnp.stack(ret_s).astype(state_ret.dtype),
            jnp.stack(gla_p).astype(state_gla.dtype), jnp.stack(gla_s).astype(state_gla.dtype))
```

```python
import functools
import math

import jax
import jax.numpy as jnp
from jax import lax
from jax.experimental import pallas as pl
from jax.experimental.pallas import tpu as pltpu

F32 = jnp.float32
BF16 = jnp.bfloat16

D_MODEL = 4096
RET_HEADS = 8
RET_DK = 256
RET_DV = 256
GLA_HEADS = 4
GLA_DK = 256
GLA_DV = 512
GLA_RANK = 16
GLA_GATE_TEMP = 16.0
ROPE_BASE = 10000.0
PAST_LEN = 16384
N_GROUPS = 4
EXPERTS_PER_GROUP = 8
N_EXPERTS = N_GROUPS * EXPERTS_PER_GROUP
TOP_K = 2
D_EXPERT = D_MODEL // 4
EPS = 1e-6

RET_W = RET_HEADS * RET_DK
GLA_KW = GLA_HEADS * GLA_DK
GLA_VW = GLA_HEADS * GLA_DV
N_MAIN = 4 * RET_W + 2 * GLA_KW + 2 * GLA_VW
COL_GQ = 4 * RET_W
COL_GK = COL_GQ + GLA_KW
COL_GV = COL_GK + GLA_KW
COL_GG = COL_GV + GLA_VW

LANES = 128
ROW_TILE = 256
LHS_TILE = 512
RET_CHUNK = 128
GLA_CHUNK = 128
GLA_SUB = 32
DEC_PAIR = 2
MOE_ROWS = 768
MOE_BLK = 128
MOE_TJ = 256
MOE_TN = 2048
MOE_DMA_UNROLL = 8
VMEM_LIMIT = 56 * 1024 * 1024


def _cp(semantics, vmem=VMEM_LIMIT):
    return pltpu.CompilerParams(dimension_semantics=semantics, vmem_limit_bytes=vmem)


def _sigmoid(x):
    return 1.0 / (1.0 + jnp.exp(-x))


def _silu(x):
    return x * _sigmoid(x)


def _dot(a, b):
    return jnp.dot(a, b, preferred_element_type=F32)


def _dot_nt(a, b):
    return lax.dot_general(a, b, (((1,), (1,)), ((), ())), preferred_element_type=F32)


def _dot_tn(a, b):
    return lax.dot_general(a, b, (((0,), (0,)), ((), ())), preferred_element_type=F32)


def _rms_norm(x, g):
    ms = jnp.mean(x * x, axis=-1, keepdims=True)
    return x * lax.rsqrt(ms + EPS) * g


_HIGH_HALF = 0xFFFF0000


def _pack_bf16_pairs(x):
    half = x.shape[-1] // 2
    bits = lax.bitcast_convert_type(x.astype(BF16).astype(F32), jnp.uint32)
    return (bits[:, :half] >> 16) | (bits[:, half:] & jnp.uint32(_HIGH_HALF))


def _unpack_bf16_pairs(w):
    lo = lax.bitcast_convert_type(w << 16, F32)
    hi = lax.bitcast_convert_type(w & jnp.uint32(_HIGH_HALF), F32)
    return jnp.concatenate([lo, hi], axis=-1).astype(BF16)


def _two_group_specs(shape_tail, tile, n_p_tiles):
    zeros = (0,) * len(shape_tail)

    def p_map(*idx):
        return (jnp.minimum(idx[-1], n_p_tiles - 1),) + zeros

    def s_map(*idx):
        return (jnp.maximum(idx[-1] - n_p_tiles, 0),) + zeros

    return pl.BlockSpec((tile,) + shape_tail, p_map), pl.BlockSpec((tile,) + shape_tail, s_map)


def _log_sigmoid(z):
    return jnp.minimum(z, 0.0) - jnp.log(1.0 + jnp.exp(-jnp.abs(z)))


def _norm_in_kernel(n_p_tiles, xp_ref, xs_ref, g_ref, wga_ref, wup_ref, b_ref, h_ref, la_ref):
    i = pl.program_id(0)

    def emit(x):
        h = _rms_norm(x, g_ref[...]).astype(BF16)
        h_ref[...] = h
        ga = _dot_nt(h, wga_ref[...])
        z = _dot(ga.astype(BF16), wup_ref[...]) + b_ref[...]
        la_ref[...] = _log_sigmoid(z) * (1.0 / GLA_GATE_TEMP)

    @pl.when(i < n_p_tiles)
    def _():
        emit(xp_ref[...])

    @pl.when(i >= n_p_tiles)
    def _():
        emit(xs_ref[...])


def _norm_in(x_p, x_s, g, w_ga, w_up, b):
    (t_p, d), t_s = x_p.shape, x_s.shape[0]
    tr = ROW_TILE
    p_spec, s_spec = _two_group_specs((d,), tr, t_p // tr)
    return pl.pallas_call(
        functools.partial(_norm_in_kernel, t_p // tr),
        grid=((t_p + t_s) // tr,),
        in_specs=[p_spec, s_spec, pl.BlockSpec((1, d), lambda i: (0, 0)),
                  pl.BlockSpec((LANES, d), lambda i: (0, 0)),
                  pl.BlockSpec((LANES, GLA_KW), lambda i: (0, 0)),
                  pl.BlockSpec((1, GLA_KW), lambda i: (0, 0))],
        out_specs=[pl.BlockSpec((tr, d), lambda i: (i, 0)), pl.BlockSpec((tr, GLA_KW), lambda i: (i, 0))],
        out_shape=[jax.ShapeDtypeStruct((t_p + t_s, d), BF16), jax.ShapeDtypeStruct((t_p + t_s, GLA_KW), F32)],
        compiler_params=_cp(("arbitrary",)),
        name="norm_in",
    )(x_p, x_s, g.reshape(1, d), w_ga, w_up, b.reshape(1, GLA_KW))


def _norm_out_kernel(x_ref, g_ref, o_ref):
    o_ref[...] = _rms_norm(x_ref[...], g_ref[...])


def _norm_out(x, g, row0, n_rows):
    d = x.shape[1]
    tr = ROW_TILE
    blk0 = row0 // tr
    return pl.pallas_call(
        _norm_out_kernel,
        grid=(n_rows // tr,),
        in_specs=[pl.BlockSpec((tr, d), lambda i: (i + blk0, 0)), pl.BlockSpec((1, d), lambda i: (0, 0))],
        out_specs=pl.BlockSpec((tr, d), lambda i: (i, 0)),
        out_shape=jax.ShapeDtypeStruct((n_rows, d), F32),
        compiler_params=_cp(("parallel",)),
        name="norm_out",
    )(x, g.reshape(1, d))


def _in_proj_kernel(a_ref, wt_ref, o_ref):
    o_ref[...] = _dot_nt(a_ref[...], wt_ref[...].astype(BF16))


def _in_proj(h, w_t, n_cols, tm, tn):
    t, d = h.shape
    return pl.pallas_call(
        _in_proj_kernel,
        grid=(n_cols // tn, t // tm),
        in_specs=[pl.BlockSpec((tm, d), lambda j, i: (i, 0)), pl.BlockSpec((tn, d), lambda j, i: (j, 0))],
        out_specs=pl.BlockSpec((tm, tn), lambda j, i: (i, j)),
        out_shape=jax.ShapeDtypeStruct((t, n_cols), F32),
        compiler_params=_cp(("parallel", "parallel")),
        name="in_proj",
    )(h, w_t)


def _out_proj_kernel(n_p_tiles, rp_ref, rs_ref, gp_ref, gs_ref, w_ref, o_ref):
    i = pl.program_id(1)

    def emit(r_ref, g_ref):
        o_ref[...] = (_dot(r_ref[...], w_ref[:RET_W, :].astype(BF16))
                      + _dot(g_ref[...], w_ref[RET_W:, :].astype(BF16)))

    @pl.when(i < n_p_tiles)
    def _():
        emit(rp_ref, gp_ref)

    @pl.when(i >= n_p_tiles)
    def _():
        emit(rs_ref, gs_ref)


def _out_proj(ro_p, ro_s, go_p, go_s, w, tn):
    t_p, t_s = ro_p.shape[0], ro_s.shape[0]
    d_in, d_out = w.shape
    tm = LHS_TILE
    rp_spec, rs_spec = _two_group_specs((RET_W,), tm, t_p // tm)
    gp_spec, gs_spec = _two_group_specs((GLA_VW,), tm, t_p // tm)
    return pl.pallas_call(
        functools.partial(_out_proj_kernel, t_p // tm),
        grid=(d_out // tn, (t_p + t_s) // tm),
        in_specs=[rp_spec, rs_spec, gp_spec, gs_spec, pl.BlockSpec((d_in, tn), lambda j, i: (0, j))],
        out_specs=pl.BlockSpec((tm, tn), lambda j, i: (i, j)),
        out_shape=jax.ShapeDtypeStruct((t_p + t_s, d_out), F32),
        compiler_params=_cp(("arbitrary", "arbitrary")),
        name="out_proj",
    )(ro_p, ro_s, go_p, go_s, w)


def _ple_kernel(a_ref, w_ref, p_ref, wp_ref, r_ref, o_ref):
    gate = _sigmoid(_dot(a_ref[...], w_ref[...].astype(BF16)))
    emb = _dot(p_ref[...].astype(BF16), wp_ref[...].astype(BF16))
    o_ref[...] = r_ref[...] + gate * emb


def _ple(xb, x, p, w_pg, w_pp, tm, tn):
    t, d = x.shape
    kp = p.shape[1]
    o_spec = pl.BlockSpec((tm, tn), lambda j, i: (i, j))
    return pl.pallas_call(
        _ple_kernel,
        grid=(d // tn, t // tm),
        in_specs=[pl.BlockSpec((tm, d), lambda j, i: (i, 0)), pl.BlockSpec((d, tn), lambda j, i: (0, j)),
                  pl.BlockSpec((tm, kp), lambda j, i: (i, 0)), pl.BlockSpec((kp, tn), lambda j, i: (0, j)),
                  o_spec],
        out_specs=o_spec,
        out_shape=jax.ShapeDtypeStruct((t, d), F32),
        compiler_params=_cp(("parallel", "parallel")),
        name="ple",
    )(xb, w_pg, p, w_pp, x)


def _rotary(x, cos, sin):
    half = x.shape[-1] // 2
    x1, x2 = x[:, :half], x[:, half:]
    return jnp.concatenate([x1 * cos - x2 * sin, x1 * sin + x2 * cos], axis=-1)


def _group_norm_gate(o, gain, gate):
    mu = jnp.mean(o, axis=-1, keepdims=True)
    d = o - mu
    var = jnp.mean(d * d, axis=-1, keepdims=True)
    return d * lax.rsqrt(var + EPS) * gain * _silu(gate)


def _rms_gate(o, gain, gate):
    ms = jnp.mean(o * o, axis=-1, keepdims=True)
    return o * lax.rsqrt(ms + EPS) * gain * _silu(gate)


def _ret_prompt_body(log_g, first, i, n_chunks, q_ref, k_ref, v_ref, g_ref, cos_ref, sin_ref, gn_ref, o_ref, s_ref,
                     state, intra_tab, q_tab, k_tab):
    c = RET_CHUNK

    @pl.when(first)
    def _():
        row = lax.broadcasted_iota(jnp.int32, (c, RET_DK), 0).astype(F32)
        ii = lax.broadcasted_iota(jnp.int32, (c, c), 0)
        jj = lax.broadcasted_iota(jnp.int32, (c, c), 1)
        diff = (ii - jj).astype(F32)
        for h in range(RET_HEADS):
            intra_tab[h] = jnp.where(diff >= 0, jnp.exp(jnp.maximum(diff, 0.0) * log_g[h]), 0.0)
            q_tab[h] = jnp.exp((row + 1.0) * log_g[h])
            k_tab[h] = jnp.exp((c - 1.0 - row) * log_g[h])

    @pl.when(i == 0)
    def _():
        state[...] = jnp.zeros_like(state)

    cos, sin = cos_ref[...], sin_ref[...]
    for h in range(RET_HEADS):
        sl = slice(h * RET_DK, (h + 1) * RET_DK)
        q = _rotary(q_ref[:, sl], cos, sin)
        k = _rotary(k_ref[:, sl], cos, sin) * (RET_DK ** -0.5)
        v = v_ref[:, sl].astype(BF16)
        att = _dot_nt(q.astype(BF16), k.astype(BF16)) * intra_tab[h]
        s_old = state[h]
        o = _dot(att.astype(BF16), v) + _dot((q * q_tab[h]).astype(BF16), s_old.astype(BF16))
        state[h] = s_old * math.exp(c * log_g[h]) + _dot_tn((k * k_tab[h]).astype(BF16), v)
        o_ref[:, sl] = _group_norm_gate(o, gn_ref[h], g_ref[:, sl]).astype(o_ref.dtype)

    @pl.when(i == n_chunks - 1)
    def _():
        s_ref[0] = state[...]


def _ret_sample_body(log_g, seq, q_ref, k_ref, v_ref, g_ref, cos_ref, sin_ref, gn_ref, s_in, o_ref, s_out):
    rows = DEC_PAIR * seq
    cos, sin = cos_ref[...], sin_ref[...]
    rid = lax.broadcasted_iota(jnp.int32, (rows, RET_DK), 0)
    pos = (rid % seq).astype(F32)
    batch_of_row = rid // seq
    ii = lax.broadcasted_iota(jnp.int32, (rows, rows), 0)
    jj = lax.broadcasted_iota(jnp.int32, (rows, rows), 1)
    visible = (ii // seq == jj // seq) & (ii >= jj)
    diff = jnp.maximum(ii - jj, 0).astype(F32)
    for h in range(RET_HEADS):
        lg = log_g[h]
        sl = slice(h * RET_DK, (h + 1) * RET_DK)
        q = _rotary(q_ref[:, sl], cos, sin)
        k = _rotary(k_ref[:, sl], cos, sin) * (RET_DK ** -0.5)
        v = v_ref[:, sl].astype(BF16)
        intra = jnp.where(visible, jnp.exp(diff * lg), 0.0)
        att = _dot_nt(q.astype(BF16), k.astype(BF16)) * intra
        o = _dot(att.astype(BF16), v)
        qd = (q * jnp.exp((pos + 1.0) * lg)).astype(BF16)
        kd = k * jnp.exp((seq - 1.0 - pos) * lg)
        for b in range(DEC_PAIR):
            mine = batch_of_row == b
            s_old = s_in[b, h]
            o = o + jnp.where(mine, _dot(qd, s_old.astype(BF16)), 0.0)
            s_out[b, h] = s_old * math.exp(seq * lg) + _dot_tn(jnp.where(mine, kd, 0.0).astype(BF16), v)
        o_ref[:, sl] = _group_norm_gate(o, gn_ref[h], g_ref[:, sl]).astype(o_ref.dtype)


def _ret_mix_kernel(log_g, seq, n_chunks, qp, kp, vp, gp, cos_p, sin_p, gn, qs, ks, vs, gs, cos_s, sin_s, s_in,
                    o_p, s_p, o_s, s_out, state, intra_tab, q_tab, k_tab):
    s = pl.program_id(0)
    _ret_prompt_body(log_g, s == 0, s % n_chunks, n_chunks, qp, kp, vp, gp, cos_p, sin_p, gn, o_p, s_p,
                     state, intra_tab, q_tab, k_tab)
    _ret_sample_body(log_g, seq, qs, ks, vs, gs, cos_s, sin_s, gn, s_in, o_s, s_out)


def _ret_mix(proj, log_g, cos_p, sin_p, cos_s, sin_s, gn, state, n_p, l_p, n_s, l_s):
    c = RET_CHUNK
    nc = l_p // c
    rows = DEC_PAIR * l_s
    steps = n_p * nc
    assert steps * DEC_PAIR == n_s, (steps, n_s)
    blk0 = (n_p * l_p) // rows

    def col_p(group):
        return pl.BlockSpec((c, RET_W), lambda s: (s, group))

    def col_s(group):
        return pl.BlockSpec((rows, RET_W), lambda s: (s + blk0, group))

    tab_p = pl.BlockSpec((c, RET_DK // 2), lambda s: (s % nc, 0))
    tab_s = pl.BlockSpec((rows, RET_DK // 2), lambda s: (0, 0))
    st = pl.BlockSpec((DEC_PAIR, RET_HEADS, RET_DK, RET_DV), lambda s: (s, 0, 0, 0))
    return pl.pallas_call(
        functools.partial(_ret_mix_kernel, log_g, l_s, nc),
        grid=(steps,),
        in_specs=[col_p(0), col_p(1), col_p(2), col_p(3), tab_p, tab_p,
                  pl.BlockSpec((RET_HEADS, 1, RET_DV), lambda s: (0, 0, 0)),
                  col_s(0), col_s(1), col_s(2), col_s(3), tab_s, tab_s, st],
        out_specs=[pl.BlockSpec((c, RET_W), lambda s: (s, 0)),
                   pl.BlockSpec((1, RET_HEADS, RET_DK, RET_DV), lambda s: (s // nc, 0, 0, 0)),
                   pl.BlockSpec((rows, RET_W), lambda s: (s, 0)), st],
        out_shape=[jax.ShapeDtypeStruct((n_p * l_p, RET_W), BF16),
                   jax.ShapeDtypeStruct((n_p, RET_HEADS, RET_DK, RET_DV), F32),
                   jax.ShapeDtypeStruct((n_s * l_s, RET_W), BF16),
                   jax.ShapeDtypeStruct(state.shape, F32)],
        scratch_shapes=[pltpu.VMEM((RET_HEADS, RET_DK, RET_DV), F32),
                        pltpu.VMEM((RET_HEADS, c, c), F32),
                        pltpu.VMEM((RET_HEADS, c, RET_DK), F32),
                        pltpu.VMEM((RET_HEADS, c, RET_DK), F32)],
        compiler_params=_cp(("arbitrary",)),
        name="retention_mix",
    )(proj, proj, proj, proj, cos_p, sin_p, gn.reshape(RET_HEADS, 1, RET_DV),
      proj, proj, proj, proj, cos_s, sin_s, state)


def _split3(x):
    hi = x.astype(BF16)
    r1 = x - hi.astype(F32)
    mid = r1.astype(BF16)
    lo = (r1 - mid.astype(F32)).astype(BF16)
    return hi, mid, lo


def _column_scale(row_vec, width):
    n = row_vec.shape[-1]
    t = jnp.transpose(jnp.broadcast_to(row_vec, (LANES, n)))
    return jnp.concatenate([t] * (width // LANES), axis=-1)


def _gla_prompt_body(i, n_chunks, q_ref, k_ref, v_ref, g_ref, la_ref, gn_ref, o_ref, s_ref, state):
    c, sub = GLA_CHUNK, GLA_SUB
    nsub = c // sub

    @pl.when(i == 0)
    def _():
        state[...] = jnp.zeros_like(state)

    ii = lax.broadcasted_iota(jnp.int32, (c, c), 0)
    jj = lax.broadcasted_iota(jnp.int32, (c, c), 1)
    causal = ii >= jj
    tri = jnp.where(causal, 1.0, 0.0).astype(BF16)
    for h in range(GLA_HEADS):
        ks = slice(h * GLA_DK, (h + 1) * GLA_DK)
        vs = slice(h * GLA_DV, (h + 1) * GLA_DV)
        q = q_ref[:, ks]
        k = k_ref[:, ks] * (GLA_DK ** -0.5)
        v = v_ref[:, vs].astype(BF16)
        hi, mid, lo = _split3(la_ref[:, ks])
        b = _dot(tri, hi) + _dot(tri, mid) + _dot(tri, lo)
        mids = [b[s * sub + sub // 2 - 1: s * sub + sub // 2, :] for s in range(nsub)]
        ref_lvl = jnp.concatenate([jnp.broadcast_to(m, (sub, GLA_DK)) for m in mids], axis=0)
        qd = (q * jnp.exp(b - ref_lvl)).astype(BF16)
        kd = k * jnp.exp(ref_lvl - b)
        rows = []
        for s in range(nsub):
            live = (s + 1) * sub
            keys = kd[:live] * jnp.exp(jnp.minimum(mids[s] - ref_lvl[:live], 0.0))
            if live < c:
                keys = jnp.concatenate([keys, jnp.zeros((c - live, GLA_DK), F32)], axis=0)
            rows.append(_dot_nt(qd[s * sub:(s + 1) * sub], keys.astype(BF16)))
        att = jnp.where(causal, jnp.concatenate(rows, axis=0), 0.0)
        s_old = state[h]
        o = _dot(att.astype(BF16), v) + _dot((q * jnp.exp(b)).astype(BF16), s_old.astype(BF16))
        b_last = b[c - 1:c, :]
        k_rem = (k * jnp.exp(b_last - b)).astype(BF16)
        state[h] = s_old * _column_scale(jnp.exp(b_last), GLA_DV) + _dot_tn(k_rem, v)
        o_ref[:, vs] = _rms_gate(o, gn_ref[h], g_ref[:, vs]).astype(o_ref.dtype)

    @pl.when(i == n_chunks - 1)
    def _():
        s_ref[0] = state[...]


def _gla_sample_body(seq, q_ref, k_ref, v_ref, g_ref, la_ref, gn_ref, s_in, o_ref, s_out):
    rows = DEC_PAIR * seq
    rid = lax.broadcasted_iota(jnp.int32, (rows, GLA_DK), 0)
    pos = rid % seq
    batch_of_row = rid // seq
    ii = lax.broadcasted_iota(jnp.int32, (rows, rows), 0)
    jj = lax.broadcasted_iota(jnp.int32, (rows, rows), 1)
    visible = (ii // seq == jj // seq) & (ii >= jj)
    for h in range(GLA_HEADS):
        ks = slice(h * GLA_DK, (h + 1) * GLA_DK)
        vs = slice(h * GLA_DV, (h + 1) * GLA_DV)
        la = la_ref[:, ks]
        b = la
        for d in range(1, seq):
            b = b + jnp.where(pos >= d, pltpu.roll(la, d, axis=0), 0.0)
        q = q_ref[:, ks]
        k = k_ref[:, ks] * (GLA_DK ** -0.5)
        v = v_ref[:, vs].astype(BF16)
        qb = (q * jnp.exp(b)).astype(BF16)
        kb = (k * jnp.exp(-b)).astype(BF16)
        att = jnp.where(visible, _dot_nt(qb, kb), 0.0)
        o = _dot(att.astype(BF16), v)
        for bi in range(DEC_PAIR):
            mine = batch_of_row == bi
            last = bi * seq + seq - 1
            b_last = b[last:last + 1, :]
            s_old = s_in[bi, h]
            o = o + jnp.where(mine[:, :1], _dot(qb, s_old.astype(BF16)), 0.0)
            k_rem = jnp.where(mine, k * jnp.exp(b_last - b), 0.0).astype(BF16)
            s_out[bi, h] = s_old * _column_scale(jnp.exp(b_last), GLA_DV) + _dot_tn(k_rem, v)
        o_ref[:, vs] = _rms_gate(o, gn_ref[h], g_ref[:, vs]).astype(o_ref.dtype)


def _gla_mix_kernel(seq, n_chunks, qp, kp, vp, gp, la_p, gn, qs, ks, vs, gs, la_s, s_in,
                    o_p, s_p, o_s, s_out, state):
    s = pl.program_id(0)
    _gla_prompt_body(s % n_chunks, n_chunks, qp, kp, vp, gp, la_p, gn, o_p, s_p, state)
    _gla_sample_body(seq, qs, ks, vs, gs, la_s, gn, s_in, o_s, s_out)


def _gla_mix(proj, log_a, gn, state, n_p, l_p, n_s, l_s):
    c = GLA_CHUNK
    nc = l_p // c
    rows = DEC_PAIR * l_s
    steps = n_p * nc
    assert steps * DEC_PAIR == n_s, (steps, n_s)
    blk0 = (n_p * l_p) // rows

    def col_p(start, width):
        return pl.BlockSpec((c, width), lambda s: (s, start // width))

    def col_s(start, width):
        return pl.BlockSpec((rows, width), lambda s: (s + blk0, start // width))

    st = pl.BlockSpec((DEC_PAIR, GLA_HEADS, GLA_DK, GLA_DV), lambda s: (s, 0, 0, 0))
    return pl.pallas_call(
        functools.partial(_gla_mix_kernel, l_s, nc),
        grid=(steps,),
        in_specs=[col_p(COL_GQ, GLA_KW), col_p(COL_GK, GLA_KW), col_p(COL_GV, GLA_VW), col_p(COL_GG, GLA_VW),
                  pl.BlockSpec((c, GLA_KW), lambda s: (s, 0)),
                  pl.BlockSpec((GLA_HEADS, 1, GLA_DV), lambda s: (0, 0, 0)),
                  col_s(COL_GQ, GLA_KW), col_s(COL_GK, GLA_KW), col_s(COL_GV, GLA_VW), col_s(COL_GG, GLA_VW),
                  pl.BlockSpec((rows, GLA_KW), lambda s: (s + blk0, 0)), st],
        out_specs=[pl.BlockSpec((c, GLA_VW), lambda s: (s, 0)),
                   pl.BlockSpec((1, GLA_HEADS, GLA_DK, GLA_DV), lambda s: (s // nc, 0, 0, 0)),
                   pl.BlockSpec((rows, GLA_VW), lambda s: (s, 0)), st],
        out_shape=[jax.ShapeDtypeStruct((n_p * l_p, GLA_VW), BF16),
                   jax.ShapeDtypeStruct((n_p, GLA_HEADS, GLA_DK, GLA_DV), F32),
                   jax.ShapeDtypeStruct((n_s * l_s, GLA_VW), BF16),
                   jax.ShapeDtypeStruct(state.shape, F32)],
        scratch_shapes=[pltpu.VMEM((GLA_HEADS, GLA_DK, GLA_DV), F32)],
        compiler_params=_cp(("arbitrary",)),
        name="gla_mix",
    )(proj, proj, proj, proj, log_a, gn.reshape(GLA_HEADS, 1, GLA_DV),
      proj, proj, proj, proj, log_a, state)


def _router_kernel(n_p_tiles, m_ref, xp_ref, xs_ref, g_ref, wr_ref, br_ref, x_ref, hm_ref, idx_ref, gate_ref):
    tr = m_ref.shape[0]
    i = pl.program_id(0)

    @pl.when(i < n_p_tiles)
    def _():
        x_ref[...] = xp_ref[...] + m_ref[...]

    @pl.when(i >= n_p_tiles)
    def _():
        x_ref[...] = xs_ref[...] + m_ref[...]

    hm = _rms_norm(x_ref[...], g_ref[...])
    hm_ref[...] = _pack_bf16_pairs(hm)
    h1, h2, _ = _split3(hm)
    w = wr_ref[...]
    w1 = w.astype(BF16)
    w2 = (w - w1.astype(F32)).astype(BF16)
    logits = _dot(h1, w1) + _dot(h1, w2) + _dot(h2, w1) + br_ref[...]
    lane = lax.broadcasted_iota(jnp.int32, (tr, LANES), 1).astype(F32)
    neg, far = -1e30, 1e9
    is_group = lane < N_GROUPS
    gl = jnp.where(is_group, logits, neg)
    gmax = jnp.max(gl, axis=-1, keepdims=True)
    gidx = jnp.min(jnp.where(gl == gmax, lane, far), axis=-1, keepdims=True)
    gsum = jnp.sum(jnp.where(is_group, jnp.exp(gl - gmax), 0.0), axis=-1, keepdims=True)
    g_p = 1.0 / gsum
    lo = N_GROUPS + EXPERTS_PER_GROUP * gidx
    in_sel = (lane >= lo) & (lane < lo + EXPERTS_PER_GROUP)
    el = jnp.where(in_sel, logits, neg)
    emax = jnp.max(el, axis=-1, keepdims=True)
    e1 = jnp.min(jnp.where(el == emax, lane, far), axis=-1, keepdims=True)
    esum = jnp.sum(jnp.where(in_sel, jnp.exp(el - emax), 0.0), axis=-1, keepdims=True)
    el2 = jnp.where(lane == e1, neg, el)
    m2 = jnp.max(el2, axis=-1, keepdims=True)
    e2 = jnp.min(jnp.where(el2 == m2, lane, far), axis=-1, keepdims=True)
    p1 = 1.0 / esum
    p2 = jnp.exp(m2 - emax) / esum
    den = p1 + p2
    idx_ref[...] = jnp.where(lane == 0, e1 - N_GROUPS, jnp.where(lane == 1, e2 - N_GROUPS, 0.0)).astype(jnp.int32)
    gate_ref[...] = jnp.where(lane == 0, g_p * p1 / den, jnp.where(lane == 1, g_p * p2 / den, 0.0))


def _router(m, x_p, x_s, g, w_r, b_r):
    t, d = m.shape
    t_p = x_p.shape[0]
    tr = ROW_TILE
    p_spec, s_spec = _two_group_specs((d,), tr, t_p // tr)
    row = pl.BlockSpec((tr, d), lambda i: (i, 0))
    lane_row = pl.BlockSpec((tr, LANES), lambda i: (i, 0))
    return pl.pallas_call(
        functools.partial(_router_kernel, t_p // tr),
        grid=(t // tr,),
        in_specs=[row, p_spec, s_spec, pl.BlockSpec((1, d), lambda i: (0, 0)),
                  pl.BlockSpec((d, LANES), lambda i: (0, 0)), pl.BlockSpec((1, LANES), lambda i: (0, 0))],
        out_specs=[row, pl.BlockSpec((tr, d // 2), lambda i: (i, 0)), lane_row, lane_row],
        out_shape=[jax.ShapeDtypeStruct((t, d), F32), jax.ShapeDtypeStruct((t, d // 2), jnp.uint32),
                   jax.ShapeDtypeStruct((t, LANES), jnp.int32), jax.ShapeDtypeStruct((t, LANES), F32)],
        compiler_params=_cp(("arbitrary",)),
        name="moe_router",
    )(m, x_p, x_s, g.reshape(1, d), w_r, b_r)


def _moe_kernel(n_used, item_e, item_start, item_n, tok, dst,
                hm_hbm, wg_ref, wu_ref, wd_ref, y_hbm,
                rows_in, rows_out, hid, sem_in, sem_out):
    del n_used, item_e
    i = pl.program_id(0)
    j = pl.program_id(1)
    n_items = pl.num_programs(0)
    n_steps = pl.num_programs(1)
    n_up = D_EXPERT // MOE_TJ
    max_blk = MOE_ROWS // MOE_BLK
    n, s0 = item_n[i], item_start[i]
    nxt = jnp.minimum(i + 1, n_items - 1)
    n_next, s_next = jnp.where(i + 1 < n_items, item_n[nxt], 0), item_start[nxt]
    prv = jnp.maximum(i - 1, 0)
    n_prev, s_prev = jnp.where(i > 0, item_n[prv], 0), item_start[prv]
    nblk = (n + MOE_BLK - 1) // MOE_BLK

    def gather(s, r):
        return pltpu.make_async_copy(hm_hbm.at[pl.ds(tok[s + r], 1), :], rows_in.at[pl.ds(r, 1), :], sem_in)

    def scatter(s, r):
        return pltpu.make_async_copy(rows_out.at[pl.ds(r, 1), :], y_hbm.at[pl.ds(dst[s + r], 1), :], sem_out)

    def gathered_group(c):
        rows = pl.ds(pl.multiple_of(c * MOE_DMA_UNROLL, MOE_DMA_UNROLL), MOE_DMA_UNROLL)
        return pltpu.make_async_copy(hm_hbm.at[pl.ds(0, MOE_DMA_UNROLL), :], rows_in.at[rows, :], sem_in)

    def scattered_group(c):
        rows = pl.ds(pl.multiple_of(c * MOE_DMA_UNROLL, MOE_DMA_UNROLL), MOE_DMA_UNROLL)
        return pltpu.make_async_copy(rows_out.at[rows, :], y_hbm.at[pl.ds(0, MOE_DMA_UNROLL), :], sem_out)

    def for_rows(count, fn, group_fn=None):
        groups = count // MOE_DMA_UNROLL

        def group(c, carry):
            if group_fn is None:
                for u in range(MOE_DMA_UNROLL):
                    fn(c * MOE_DMA_UNROLL + u)
            else:
                group_fn(c)
            return carry
        lax.fori_loop(0, groups, group, 0)

        def single(r, carry):
            fn(r)
            return carry
        lax.fori_loop(groups * MOE_DMA_UNROLL, count, single, 0)

    def for_row_count(fn):
        for k in range(1, max_blk + 1):
            @pl.when(nblk == k)
            def _(k=k):
                fn(k * MOE_BLK)

    @pl.when((i == 0) & (j == 0))
    def _():
        rows_in[...] = jnp.zeros_like(rows_in)
        for_rows(n, lambda r: gather(s0, r).start())

    @pl.when(j == 0)
    def _():
        for_rows(n, lambda r: gather(s0, r).wait(), lambda c: gathered_group(c).wait())

    @pl.when(j < n_up)
    def _():
        def up(m):
            x = _unpack_bf16_pairs(rows_in[0:m, :])
            act = _silu(_dot(x, wg_ref[0].astype(BF16))) * _dot(x, wu_ref[0].astype(BF16))
            hid[j, 0:m, :] = act.astype(BF16)
        for_row_count(up)

    @pl.when(j == n_up)
    def _():
        for_rows(n_next, lambda r: gather(s_next, r).start())
        for_rows(n_prev, lambda r: scatter(s_prev, r).wait(), lambda c: scattered_group(c).wait())

    @pl.when(j >= n_up)
    def _():
        col = pl.multiple_of((j - n_up) * MOE_TN, MOE_TN)

        def down(m):
            act = jnp.concatenate([hid[u, 0:m, :] for u in range(n_up)], axis=-1)
            rows_out[0:m, pl.ds(col, MOE_TN)] = _dot(act, wd_ref[0].astype(BF16))
        for_row_count(down)

    @pl.when(j == n_steps - 1)
    def _():
        for_rows(n, lambda r: scatter(s0, r).start())

    @pl.when((j == n_steps - 1) & (i == n_items - 1))
    def _():
        for_rows(n, lambda r: scatter(s0, r).wait(), lambda c: scattered_group(c).wait())


def _moe_experts(hm, w_gate, w_up, w_down, items, tok, dst, n_assign):
    n_used, item_e, item_start, item_n = items
    n_up = D_EXPERT // MOE_TJ
    n_dn = D_MODEL // MOE_TN

    def up_map(i, j, nu, e, st, n, tok, dst):
        return (e[i], 0, jnp.minimum(j, n_up - 1))

    def dn_map(i, j, nu, e, st, n, tok, dst):
        return (e[i], 0, jnp.maximum(j - n_up, 0))

    up_spec = pl.BlockSpec((1, D_MODEL, MOE_TJ), up_map)
    return pl.pallas_call(
        _moe_kernel,
        grid_spec=pltpu.PrefetchScalarGridSpec(
            num_scalar_prefetch=6,
            grid=(n_used[0], n_up + n_dn),
            in_specs=[pl.BlockSpec(memory_space=pl.ANY), up_spec, up_spec,
                      pl.BlockSpec((1, D_EXPERT, MOE_TN), dn_map)],
            out_specs=pl.BlockSpec(memory_space=pl.ANY),
            scratch_shapes=[pltpu.VMEM((MOE_ROWS, D_MODEL // 2), jnp.uint32),
                            pltpu.VMEM((MOE_ROWS, D_MODEL), F32),
                            pltpu.VMEM((n_up, MOE_ROWS, MOE_TJ), BF16),
                            pltpu.SemaphoreType.DMA(()),
                            pltpu.SemaphoreType.DMA(())],
        ),
        out_shape=jax.ShapeDtypeStruct((n_assign, D_MODEL), F32),
        compiler_params=_cp(("arbitrary", "arbitrary")),
        name="moe_experts",
    )(n_used, item_e, item_start, item_n, tok, dst, hm, w_gate, w_up, w_down)


def _moe_plan(expert_idx):
    n_assign = expert_idx.size
    flat_e = expert_idx.reshape(n_assign)
    order = jnp.argsort(flat_e).astype(jnp.int32)
    counts = jnp.bincount(flat_e, length=N_EXPERTS).astype(jnp.int32)
    start = jnp.cumsum(counts) - counts
    per_e = (counts + MOE_ROWS - 1) // MOE_ROWS
    item_end = jnp.cumsum(per_e)
    n_items = n_assign // MOE_ROWS + N_EXPERTS
    ids = jnp.arange(n_items, dtype=jnp.int32)
    used = ids < item_end[-1]
    last = jnp.maximum(item_end[-1] - 1, 0)
    e_of = jnp.minimum(jnp.searchsorted(item_end, jnp.minimum(ids, last), side="right"), N_EXPERTS - 1).astype(jnp.int32)
    local = jnp.minimum(ids, last) - (item_end - per_e)[e_of]
    item_start = start[e_of] + local * MOE_ROWS
    item_n = jnp.where(used, jnp.clip(counts[e_of] - local * MOE_ROWS, 0, MOE_ROWS), 0)
    items = (item_end[-1:].astype(jnp.int32), e_of, item_start.astype(jnp.int32), item_n.astype(jnp.int32))
    token = order // TOP_K
    dst_row = (order % TOP_K) * (n_assign // TOP_K) + token
    return items, token, dst_row


def _combine_kernel(x_ref, y0_ref, y1_ref, gate_ref, o_ref, ob_ref):
    x = x_ref[...] + (y0_ref[...] * gate_ref[:, 0:1] + y1_ref[...] * gate_ref[:, 1:2])
    o_ref[...] = x
    ob_ref[...] = x.astype(BF16)


def _combine(x, y, gate):
    t, d = x.shape
    tr = ROW_TILE
    row = pl.BlockSpec((tr, d), lambda i: (i, 0))
    return pl.pallas_call(
        _combine_kernel,
        grid=(t // tr,),
        in_specs=[row, row, pl.BlockSpec((tr, d), lambda i: (i + t // tr, 0)),
                  pl.BlockSpec((tr, LANES), lambda i: (i, 0))],
        out_specs=[row, row],
        out_shape=[jax.ShapeDtypeStruct((t, d), F32), jax.ShapeDtypeStruct((t, d), BF16)],
        compiler_params=_cp(("parallel",)),
        name="moe_combine",
    )(x, y, y, gate)


def _rope_tables(pos):
    half = RET_DK // 2
    inv = ROPE_BASE ** (-jnp.arange(half, dtype=F32) / half)
    ang = pos.astype(F32)[:, None] * inv[None, :]
    return jnp.cos(ang), jnp.sin(ang)


def kernel(x_prompt, x_sample, state_ret, state_gla, p_prompt, p_sample, g_mix, w_in, w_gla_up, b_gla,
           ret_norm_g, gla_norm_g, w_out, g_moe, w_rg, b_rg, w_re, b_re, w_gate, w_up, w_down, w_pg, w_pp,
           g_final):
    n_p, l_p, d = x_prompt.shape
    n_s, l_s, _ = x_sample.shape
    depth = g_mix.shape[0]
    t_p, t_s = n_p * l_p, n_s * l_s
    t = t_p + t_s

    log_g_py = [math.log1p(-(2.0 ** (-5.0 - h))) for h in range(RET_HEADS)]
    cos_p, sin_p = _rope_tables(jnp.arange(l_p, dtype=jnp.int32))
    cos_s, sin_s = _rope_tables(PAST_LEN + jnp.arange(l_s, dtype=jnp.int32))
    cos_s, sin_s = jnp.tile(cos_s, (DEC_PAIR, 1)), jnp.tile(sin_s, (DEC_PAIR, 1))

    x_p = x_prompt.reshape(t_p, d)
    x_s = x_sample.reshape(t_s, d)
    x = None
    ret_p, ret_s, gla_p, gla_s = [], [], [], []
    for l in range(depth):
        w_in_t = jnp.swapaxes(w_in[l], 0, 1)
        w_ga = jnp.pad(w_in_t[N_MAIN:], ((0, LANES - GLA_RANK), (0, 0))).astype(BF16)
        w_gup = jnp.pad(w_gla_up[l], ((0, LANES - GLA_RANK), (0, 0))).astype(BF16)
        w_r = jnp.pad(jnp.concatenate([w_rg[l], w_re[l]], axis=1),
                      ((0, 0), (0, LANES - N_GROUPS - N_EXPERTS)))
        b_r = jnp.pad(jnp.concatenate([b_rg[l], b_re[l]]), (0, LANES - N_GROUPS - N_EXPERTS)).reshape(1, LANES)
        p = jnp.concatenate([p_prompt[l].reshape(t_p, -1), p_sample[l].reshape(t_s, -1)], axis=0)
        if x is not None:
            x_p, x_s = x[:t_p], x[t_p:]

        h, log_a = _norm_in(x_p, x_s, g_mix[l], w_ga, w_gup, b_gla[l])
        proj = _in_proj(h, w_in_t, N_MAIN, t // 16, 1024)
        ro_p, sr_p, ro_s, sr_s = _ret_mix(proj, log_g_py, cos_p, sin_p, cos_s, sin_s, ret_norm_g[l], state_ret[l],
                                          n_p, l_p, n_s, l_s)
        go_p, sg_p, go_s, sg_s = _gla_mix(proj, log_a, gla_norm_g[l], state_gla[l], n_p, l_p, n_s, l_s)
        ret_p.append(sr_p)
        ret_s.append(sr_s)
        gla_p.append(sg_p)
        gla_s.append(sg_s)
        mix = _out_proj(ro_p, ro_s, go_p, go_s, w_out[l], 1024)

        x, hm, idx, gate = _router(mix, x_p, x_s, g_moe[l], w_r, b_r)
        items, tok, dst = _moe_plan(idx[:, :TOP_K])
        y = _moe_experts(hm, w_gate[l], w_up[l], w_down[l], items, tok, dst, t * TOP_K)
        x, xb = _combine(x, y, gate)

        x = _ple(xb, x, p, w_pg[l], w_pp[l], t // 8, 512)

    y_prompt = _norm_out(x, g_final, 0, t_p).reshape(n_p, l_p, d)
    y_sample = _norm_out(x, g_final, t_p, t_s).reshape(n_s, l_s, d)
    return (y_prompt, y_sample,
            jnp.stack(ret_p).astype(state_ret.dtype), jnp.stack(ret_s).astype(state_ret.dtype),
            jnp.stack(gla_p).astype(state_gla.dtype), jnp.stack(gla_s).astype(state_gla.dtype))
```

```python
import functools
import math

import jax
import jax.numpy as jnp
from jax import lax
from jax.experimental import pallas as pl
from jax.experimental.pallas import tpu as pltpu

F32 = jnp.float32
BF16 = jnp.bfloat16

D_MODEL = 4096
RET_HEADS = 8
RET_DK = 256
RET_DV = 256
GLA_HEADS = 4
GLA_DK = 256
GLA_DV = 512
GLA_RANK = 16
GLA_GATE_TEMP = 16.0
ROPE_BASE = 10000.0
PAST_LEN = 16384
N_GROUPS = 4
EXPERTS_PER_GROUP = 8
N_EXPERTS = N_GROUPS * EXPERTS_PER_GROUP
TOP_K = 2
D_EXPERT = D_MODEL // 4
EPS = 1e-6

RET_W = RET_HEADS * RET_DK
GLA_KW = GLA_HEADS * GLA_DK
GLA_VW = GLA_HEADS * GLA_DV
N_MAIN = 4 * RET_W + 2 * GLA_KW + 2 * GLA_VW
COL_GQ = 4 * RET_W
COL_GK = COL_GQ + GLA_KW
COL_GV = COL_GK + GLA_KW
COL_GG = COL_GV + GLA_VW

LANES = 128
ROW_TILE = 256
LHS_TILE = 512
RET_CHUNK = 128
GLA_CHUNK = 128
GLA_SUB = 32
DEC_PAIR = 2
MOE_ROWS = 768
MOE_BLK = 128
MOE_TJ = 256
MOE_TN = 2048
MOE_DMA_UNROLL = 16
VMEM_LIMIT = 56 * 1024 * 1024


def _cp(semantics, vmem=VMEM_LIMIT):
    return pltpu.CompilerParams(dimension_semantics=semantics, vmem_limit_bytes=vmem)


def _sigmoid(x):
    return 1.0 / (1.0 + jnp.exp(-x))


def _silu(x):
    return x * _sigmoid(x)


def _dot(a, b):
    return jnp.dot(a, b, preferred_element_type=F32)


def _dot_nt(a, b):
    return lax.dot_general(a, b, (((1,), (1,)), ((), ())), preferred_element_type=F32)


def _dot_tn(a, b):
    return lax.dot_general(a, b, (((0,), (0,)), ((), ())), preferred_element_type=F32)


def _rms_norm(x, g):
    ms = jnp.mean(x * x, axis=-1, keepdims=True)
    return x * lax.rsqrt(ms + EPS) * g


_HIGH_HALF = 0xFFFF0000


def _pack_bf16_pairs(x):
    half = x.shape[-1] // 2
    bits = lax.bitcast_convert_type(x.astype(BF16).astype(F32), jnp.uint32)
    return (bits[:, :half] >> 16) | (bits[:, half:] & jnp.uint32(_HIGH_HALF))


def _unpack_bf16_pairs(w):
    lo = lax.bitcast_convert_type(w << 16, F32)
    hi = lax.bitcast_convert_type(w & jnp.uint32(_HIGH_HALF), F32)
    return jnp.concatenate([lo, hi], axis=-1).astype(BF16)


def _two_group_specs(shape_tail, tile, n_p_tiles):
    zeros = (0,) * len(shape_tail)

    def p_map(*idx):
        return (jnp.minimum(idx[-1], n_p_tiles - 1),) + zeros

    def s_map(*idx):
        return (jnp.maximum(idx[-1] - n_p_tiles, 0),) + zeros

    return pl.BlockSpec((tile,) + shape_tail, p_map), pl.BlockSpec((tile,) + shape_tail, s_map)


def _log_sigmoid(z):
    return jnp.minimum(z, 0.0) - jnp.log(1.0 + jnp.exp(-jnp.abs(z)))


def _norm_in_kernel(n_p_tiles, xp_ref, xs_ref, g_ref, wga_ref, wup_ref, b_ref, h_ref, la_ref):
    i = pl.program_id(0)

    def emit(x):
        h = _rms_norm(x, g_ref[...]).astype(BF16)
        h_ref[...] = h
        ga = _dot_nt(h, wga_ref[...])
        z = _dot(ga.astype(BF16), wup_ref[...]) + b_ref[...]
        la_ref[...] = _log_sigmoid(z) * (1.0 / GLA_GATE_TEMP)

    @pl.when(i < n_p_tiles)
    def _():
        emit(xp_ref[...])

    @pl.when(i >= n_p_tiles)
    def _():
        emit(xs_ref[...])


def _norm_in(x_p, x_s, g, w_ga, w_up, b):
    (t_p, d), t_s = x_p.shape, x_s.shape[0]
    tr = ROW_TILE
    p_spec, s_spec = _two_group_specs((d,), tr, t_p // tr)
    return pl.pallas_call(
        functools.partial(_norm_in_kernel, t_p // tr),
        grid=((t_p + t_s) // tr,),
        in_specs=[p_spec, s_spec, pl.BlockSpec((1, d), lambda i: (0, 0)),
                  pl.BlockSpec((LANES, d), lambda i: (0, 0)),
                  pl.BlockSpec((LANES, GLA_KW), lambda i: (0, 0)),
                  pl.BlockSpec((1, GLA_KW), lambda i: (0, 0))],
        out_specs=[pl.BlockSpec((tr, d), lambda i: (i, 0)), pl.BlockSpec((tr, GLA_KW), lambda i: (i, 0))],
        out_shape=[jax.ShapeDtypeStruct((t_p + t_s, d), BF16), jax.ShapeDtypeStruct((t_p + t_s, GLA_KW), F32)],
        compiler_params=_cp(("arbitrary",)),
        name="norm_in",
    )(x_p, x_s, g.reshape(1, d), w_ga, w_up, b.reshape(1, GLA_KW))


def _norm_out_kernel(x_ref, g_ref, o_ref):
    o_ref[...] = _rms_norm(x_ref[...], g_ref[...])


def _norm_out(x, g, row0, n_rows):
    d = x.shape[1]
    tr = ROW_TILE
    blk0 = row0 // tr
    return pl.pallas_call(
        _norm_out_kernel,
        grid=(n_rows // tr,),
        in_specs=[pl.BlockSpec((tr, d), lambda i: (i + blk0, 0)), pl.BlockSpec((1, d), lambda i: (0, 0))],
        out_specs=pl.BlockSpec((tr, d), lambda i: (i, 0)),
        out_shape=jax.ShapeDtypeStruct((n_rows, d), F32),
        compiler_params=_cp(("parallel",)),
        name="norm_out",
    )(x, g.reshape(1, d))


def _in_proj_kernel(a_ref, wt_ref, o_ref):
    o_ref[...] = _dot_nt(a_ref[...], wt_ref[...].astype(BF16))


def _in_proj(h, w_t, n_cols, tm, tn):
    t, d = h.shape
    return pl.pallas_call(
        _in_proj_kernel,
        grid=(n_cols // tn, t // tm),
        in_specs=[pl.BlockSpec((tm, d), lambda j, i: (i, 0)), pl.BlockSpec((tn, d), lambda j, i: (j, 0))],
        out_specs=pl.BlockSpec((tm, tn), lambda j, i: (i, j)),
        out_shape=jax.ShapeDtypeStruct((t, n_cols), F32),
        compiler_params=_cp(("parallel", "parallel")),
        name="in_proj",
    )(h, w_t)


def _out_proj_kernel(n_p_tiles, rp_ref, rs_ref, gp_ref, gs_ref, w_ref, o_ref):
    i = pl.program_id(1)

    def emit(r_ref, g_ref):
        o_ref[...] = (_dot(r_ref[...], w_ref[:RET_W, :].astype(BF16))
                      + _dot(g_ref[...], w_ref[RET_W:, :].astype(BF16)))

    @pl.when(i < n_p_tiles)
    def _():
        emit(rp_ref, gp_ref)

    @pl.when(i >= n_p_tiles)
    def _():
        emit(rs_ref, gs_ref)


def _out_proj(ro_p, ro_s, go_p, go_s, w, tn):
    t_p, t_s = ro_p.shape[0], ro_s.shape[0]
    d_in, d_out = w.shape
    tm = LHS_TILE
    rp_spec, rs_spec = _two_group_specs((RET_W,), tm, t_p // tm)
    gp_spec, gs_spec = _two_group_specs((GLA_VW,), tm, t_p // tm)
    return pl.pallas_call(
        functools.partial(_out_proj_kernel, t_p // tm),
        grid=(d_out // tn, (t_p + t_s) // tm),
        in_specs=[rp_spec, rs_spec, gp_spec, gs_spec, pl.BlockSpec((d_in, tn), lambda j, i: (0, j))],
        out_specs=pl.BlockSpec((tm, tn), lambda j, i: (i, j)),
        out_shape=jax.ShapeDtypeStruct((t_p + t_s, d_out), F32),
        compiler_params=_cp(("arbitrary", "arbitrary")),
        name="out_proj",
    )(ro_p, ro_s, go_p, go_s, w)


def _ple_kernel(a_ref, w_ref, p_ref, wp_ref, r_ref, o_ref):
    gate = _sigmoid(_dot(a_ref[...], w_ref[...].astype(BF16)))
    emb = _dot(p_ref[...].astype(BF16), wp_ref[...].astype(BF16))
    o_ref[...] = r_ref[...] + gate * emb


def _ple(xb, x, p, w_pg, w_pp, tm, tn):
    t, d = x.shape
    kp = p.shape[1]
    o_spec = pl.BlockSpec((tm, tn), lambda j, i: (i, j))
    return pl.pallas_call(
        _ple_kernel,
        grid=(d // tn, t // tm),
        in_specs=[pl.BlockSpec((tm, d), lambda j, i: (i, 0)), pl.BlockSpec((d, tn), lambda j, i: (0, j)),
                  pl.BlockSpec((tm, kp), lambda j, i: (i, 0)), pl.BlockSpec((kp, tn), lambda j, i: (0, j)),
                  o_spec],
        out_specs=o_spec,
        out_shape=jax.ShapeDtypeStruct((t, d), F32),
        compiler_params=_cp(("parallel", "parallel")),
        name="ple",
    )(xb, w_pg, p, w_pp, x)


def _rotary(x, cos, sin):
    half = x.shape[-1] // 2
    x1, x2 = x[:, :half], x[:, half:]
    return jnp.concatenate([x1 * cos - x2 * sin, x1 * sin + x2 * cos], axis=-1)


def _group_norm_gate(o, gain, gate):
    mu = jnp.mean(o, axis=-1, keepdims=True)
    d = o - mu
    var = jnp.mean(d * d, axis=-1, keepdims=True)
    return d * lax.rsqrt(var + EPS) * gain * _silu(gate)


def _rms_gate(o, gain, gate):
    ms = jnp.mean(o * o, axis=-1, keepdims=True)
    return o * lax.rsqrt(ms + EPS) * gain * _silu(gate)


def _ret_prompt_body(log_g, first, i, n_chunks, q_ref, k_ref, v_ref, g_ref, cos_ref, sin_ref, gn_ref, o_ref, s_ref,
                     state, intra_tab, q_tab, k_tab):
    c = RET_CHUNK

    @pl.when(first)
    def _():
        row = lax.broadcasted_iota(jnp.int32, (c, RET_DK), 0).astype(F32)
        ii = lax.broadcasted_iota(jnp.int32, (c, c), 0)
        jj = lax.broadcasted_iota(jnp.int32, (c, c), 1)
        diff = (ii - jj).astype(F32)
        for h in range(RET_HEADS):
            intra_tab[h] = jnp.where(diff >= 0, jnp.exp(jnp.maximum(diff, 0.0) * log_g[h]), 0.0)
            q_tab[h] = jnp.exp((row + 1.0) * log_g[h])
            k_tab[h] = jnp.exp((c - 1.0 - row) * log_g[h])

    @pl.when(i == 0)
    def _():
        state[...] = jnp.zeros_like(state)

    cos, sin = cos_ref[...], sin_ref[...]
    for h in range(RET_HEADS):
        sl = slice(h * RET_DK, (h + 1) * RET_DK)
        q = _rotary(q_ref[:, sl], cos, sin)
        k = _rotary(k_ref[:, sl], cos, sin) * (RET_DK ** -0.5)
        v = v_ref[:, sl].astype(BF16)
        att = _dot_nt(q.astype(BF16), k.astype(BF16)) * intra_tab[h]
        s_old = state[h]
        o = _dot(att.astype(BF16), v) + _dot((q * q_tab[h]).astype(BF16), s_old.astype(BF16))
        state[h] = s_old * math.exp(c * log_g[h]) + _dot_tn((k * k_tab[h]).astype(BF16), v)
        o_ref[:, sl] = _group_norm_gate(o, gn_ref[h], g_ref[:, sl]).astype(o_ref.dtype)

    @pl.when(i == n_chunks - 1)
    def _():
        s_ref[0] = state[...]


def _ret_sample_body(log_g, seq, q_ref, k_ref, v_ref, g_ref, cos_ref, sin_ref, gn_ref, s_in, o_ref, s_out):
    rows = DEC_PAIR * seq
    cos, sin = cos_ref[...], sin_ref[...]
    rid = lax.broadcasted_iota(jnp.int32, (rows, RET_DK), 0)
    pos = (rid % seq).astype(F32)
    batch_of_row = rid // seq
    ii = lax.broadcasted_iota(jnp.int32, (rows, rows), 0)
    jj = lax.broadcasted_iota(jnp.int32, (rows, rows), 1)
    visible = (ii // seq == jj // seq) & (ii >= jj)
    diff = jnp.maximum(ii - jj, 0).astype(F32)
    for h in range(RET_HEADS):
        lg = log_g[h]
        sl = slice(h * RET_DK, (h + 1) * RET_DK)
        q = _rotary(q_ref[:, sl], cos, sin)
        k = _rotary(k_ref[:, sl], cos, sin) * (RET_DK ** -0.5)
        v = v_ref[:, sl].astype(BF16)
        intra = jnp.where(visible, jnp.exp(diff * lg), 0.0)
        att = _dot_nt(q.astype(BF16), k.astype(BF16)) * intra
        o = _dot(att.astype(BF16), v)
        qd = (q * jnp.exp((pos + 1.0) * lg)).astype(BF16)
        kd = k * jnp.exp((seq - 1.0 - pos) * lg)
        for b in range(DEC_PAIR):
            mine = batch_of_row == b
            s_old = s_in[b, h]
            o = o + jnp.where(mine, _dot(qd, s_old.astype(BF16)), 0.0)
            s_out[b, h] = s_old * math.exp(seq * lg) + _dot_tn(jnp.where(mine, kd, 0.0).astype(BF16), v)
        o_ref[:, sl] = _group_norm_gate(o, gn_ref[h], g_ref[:, sl]).astype(o_ref.dtype)


def _ret_mix_kernel(log_g, seq, n_chunks, qp, kp, vp, gp, cos_p, sin_p, gn, qs, ks, vs, gs, cos_s, sin_s, s_in,
                    o_p, s_p, o_s, s_out, state, intra_tab, q_tab, k_tab):
    s = pl.program_id(0)
    _ret_prompt_body(log_g, s == 0, s % n_chunks, n_chunks, qp, kp, vp, gp, cos_p, sin_p, gn, o_p, s_p,
                     state, intra_tab, q_tab, k_tab)
    _ret_sample_body(log_g, seq, qs, ks, vs, gs, cos_s, sin_s, gn, s_in, o_s, s_out)


def _ret_mix(proj, log_g, cos_p, sin_p, cos_s, sin_s, gn, state, n_p, l_p, n_s, l_s):
    c = RET_CHUNK
    nc = l_p // c
    rows = DEC_PAIR * l_s
    steps = n_p * nc
    assert steps * DEC_PAIR == n_s, (steps, n_s)
    blk0 = (n_p * l_p) // rows

    def col_p(group):
        return pl.BlockSpec((c, RET_W), lambda s: (s, group))

    def col_s(group):
        return pl.BlockSpec((rows, RET_W), lambda s: (s + blk0, group))

    tab_p = pl.BlockSpec((c, RET_DK // 2), lambda s: (s % nc, 0))
    tab_s = pl.BlockSpec((rows, RET_DK // 2), lambda s: (0, 0))
    st = pl.BlockSpec((DEC_PAIR, RET_HEADS, RET_DK, RET_DV), lambda s: (s, 0, 0, 0))
    return pl.pallas_call(
        functools.partial(_ret_mix_kernel, log_g, l_s, nc),
        grid=(steps,),
        in_specs=[col_p(0), col_p(1), col_p(2), col_p(3), tab_p, tab_p,
                  pl.BlockSpec((RET_HEADS, 1, RET_DV), lambda s: (0, 0, 0)),
                  col_s(0), col_s(1), col_s(2), col_s(3), tab_s, tab_s, st],
        out_specs=[pl.BlockSpec((c, RET_W), lambda s: (s, 0)),
                   pl.BlockSpec((1, RET_HEADS, RET_DK, RET_DV), lambda s: (s // nc, 0, 0, 0)),
                   pl.BlockSpec((rows, RET_W), lambda s: (s, 0)), st],
        out_shape=[jax.ShapeDtypeStruct((n_p * l_p, RET_W), BF16),
                   jax.ShapeDtypeStruct((n_p, RET_HEADS, RET_DK, RET_DV), F32),
                   jax.ShapeDtypeStruct((n_s * l_s, RET_W), BF16),
                   jax.ShapeDtypeStruct(state.shape, F32)],
        scratch_shapes=[pltpu.VMEM((RET_HEADS, RET_DK, RET_DV), F32),
                        pltpu.VMEM((RET_HEADS, c, c), F32),
                        pltpu.VMEM((RET_HEADS, c, RET_DK), F32),
                        pltpu.VMEM((RET_HEADS, c, RET_DK), F32)],
        compiler_params=_cp(("arbitrary",)),
        name="retention_mix",
    )(proj, proj, proj, proj, cos_p, sin_p, gn.reshape(RET_HEADS, 1, RET_DV),
      proj, proj, proj, proj, cos_s, sin_s, state)


def _split3(x):
    hi = x.astype(BF16)
    r1 = x - hi.astype(F32)
    mid = r1.astype(BF16)
    lo = (r1 - mid.astype(F32)).astype(BF16)
    return hi, mid, lo


def _column_scale(row_vec, width):
    n = row_vec.shape[-1]
    t = jnp.transpose(jnp.broadcast_to(row_vec, (LANES, n)))
    return jnp.concatenate([t] * (width // LANES), axis=-1)


def _gla_prompt_body(i, n_chunks, q_ref, k_ref, v_ref, g_ref, la_ref, gn_ref, o_ref, s_ref, state):
    c, sub = GLA_CHUNK, GLA_SUB
    nsub = c // sub

    @pl.when(i == 0)
    def _():
        state[...] = jnp.zeros_like(state)

    ii = lax.broadcasted_iota(jnp.int32, (c, c), 0)
    jj = lax.broadcasted_iota(jnp.int32, (c, c), 1)
    causal = ii >= jj
    tri = jnp.where(causal, 1.0, 0.0).astype(BF16)
    for h in range(GLA_HEADS):
        ks = slice(h * GLA_DK, (h + 1) * GLA_DK)
        vs = slice(h * GLA_DV, (h + 1) * GLA_DV)
        q = q_ref[:, ks]
        k = k_ref[:, ks] * (GLA_DK ** -0.5)
        v = v_ref[:, vs].astype(BF16)
        hi, mid, lo = _split3(la_ref[:, ks])
        b = _dot(tri, hi) + _dot(tri, mid) + _dot(tri, lo)
        mids = [b[s * sub + sub // 2 - 1: s * sub + sub // 2, :] for s in range(nsub)]
        ref_lvl = jnp.concatenate([jnp.broadcast_to(m, (sub, GLA_DK)) for m in mids], axis=0)
        qd = (q * jnp.exp(b - ref_lvl)).astype(BF16)
        kd = k * jnp.exp(ref_lvl - b)
        rows = []
        for s in range(nsub):
            live = (s + 1) * sub
            keys = kd[:live] * jnp.exp(jnp.minimum(mids[s] - ref_lvl[:live], 0.0))
            if live < c:
                keys = jnp.concatenate([keys, jnp.zeros((c - live, GLA_DK), F32)], axis=0)
            rows.append(_dot_nt(qd[s * sub:(s + 1) * sub], keys.astype(BF16)))
        att = jnp.where(causal, jnp.concatenate(rows, axis=0), 0.0)
        s_old = state[h]
        o = _dot(att.astype(BF16), v) + _dot((q * jnp.exp(b)).astype(BF16), s_old.astype(BF16))
        b_last = b[c - 1:c, :]
        k_rem = (k * jnp.exp(b_last - b)).astype(BF16)
        state[h] = s_old * _column_scale(jnp.exp(b_last), GLA_DV) + _dot_tn(k_rem, v)
        o_ref[:, vs] = _rms_gate(o, gn_ref[h], g_ref[:, vs]).astype(o_ref.dtype)

    @pl.when(i == n_chunks - 1)
    def _():
        s_ref[0] = state[...]


def _gla_sample_body(seq, q_ref, k_ref, v_ref, g_ref, la_ref, gn_ref, s_in, o_ref, s_out):
    rows = DEC_PAIR * seq
    rid = lax.broadcasted_iota(jnp.int32, (rows, GLA_DK), 0)
    pos = rid % seq
    batch_of_row = rid // seq
    ii = lax.broadcasted_iota(jnp.int32, (rows, rows), 0)
    jj = lax.broadcasted_iota(jnp.int32, (rows, rows), 1)
    visible = (ii // seq == jj // seq) & (ii >= jj)
    for h in range(GLA_HEADS):
        ks = slice(h * GLA_DK, (h + 1) * GLA_DK)
        vs = slice(h * GLA_DV, (h + 1) * GLA_DV)
        la = la_ref[:, ks]
        b = la
        for d in range(1, seq):
            b = b + jnp.where(pos >= d, pltpu.roll(la, d, axis=0), 0.0)
        q = q_ref[:, ks]
        k = k_ref[:, ks] * (GLA_DK ** -0.5)
        v = v_ref[:, vs].astype(BF16)
        qb = (q * jnp.exp(b)).astype(BF16)
        kb = (k * jnp.exp(-b)).astype(BF16)
        att = jnp.where(visible, _dot_nt(qb, kb), 0.0)
        o = _dot(att.astype(BF16), v)
        for bi in range(DEC_PAIR):
            mine = batch_of_row == bi
            last = bi * seq + seq - 1
            b_last = b[last:last + 1, :]
            s_old = s_in[bi, h]
            o = o + jnp.where(mine[:, :1], _dot(qb, s_old.astype(BF16)), 0.0)
            k_rem = jnp.where(mine, k * jnp.exp(b_last - b), 0.0).astype(BF16)
            s_out[bi, h] = s_old * _column_scale(jnp.exp(b_last), GLA_DV) + _dot_tn(k_rem, v)
        o_ref[:, vs] = _rms_gate(o, gn_ref[h], g_ref[:, vs]).astype(o_ref.dtype)


def _gla_mix_kernel(seq, n_chunks, qp, kp, vp, gp, la_p, gn, qs, ks, vs, gs, la_s, s_in,
                    o_p, s_p, o_s, s_out, state):
    s = pl.program_id(0)
    _gla_prompt_body(s % n_chunks, n_chunks, qp, kp, vp, gp, la_p, gn, o_p, s_p, state)
    _gla_sample_body(seq, qs, ks, vs, gs, la_s, gn, s_in, o_s, s_out)


def _gla_mix(proj, log_a, gn, state, n_p, l_p, n_s, l_s):
    c = GLA_CHUNK
    nc = l_p // c
    rows = DEC_PAIR * l_s
    steps = n_p * nc
    assert steps * DEC_PAIR == n_s, (steps, n_s)
    blk0 = (n_p * l_p) // rows

    def col_p(start, width):
        return pl.BlockSpec((c, width), lambda s: (s, start // width))

    def col_s(start, width):
        return pl.BlockSpec((rows, width), lambda s: (s + blk0, start // width))

    st = pl.BlockSpec((DEC_PAIR, GLA_HEADS, GLA_DK, GLA_DV), lambda s: (s, 0, 0, 0))
    return pl.pallas_call(
        functools.partial(_gla_mix_kernel, l_s, nc),
        grid=(steps,),
        in_specs=[col_p(COL_GQ, GLA_KW), col_p(COL_GK, GLA_KW), col_p(COL_GV, GLA_VW), col_p(COL_GG, GLA_VW),
                  pl.BlockSpec((c, GLA_KW), lambda s: (s, 0)),
                  pl.BlockSpec((GLA_HEADS, 1, GLA_DV), lambda s: (0, 0, 0)),
                  col_s(COL_GQ, GLA_KW), col_s(COL_GK, GLA_KW), col_s(COL_GV, GLA_VW), col_s(COL_GG, GLA_VW),
                  pl.BlockSpec((rows, GLA_KW), lambda s: (s + blk0, 0)), st],
        out_specs=[pl.BlockSpec((c, GLA_VW), lambda s: (s, 0)),
                   pl.BlockSpec((1, GLA_HEADS, GLA_DK, GLA_DV), lambda s: (s // nc, 0, 0, 0)),
                   pl.BlockSpec((rows, GLA_VW), lambda s: (s, 0)), st],
        out_shape=[jax.ShapeDtypeStruct((n_p * l_p, GLA_VW), BF16),
                   jax.ShapeDtypeStruct((n_p, GLA_HEADS, GLA_DK, GLA_DV), F32),
                   jax.ShapeDtypeStruct((n_s * l_s, GLA_VW), BF16),
                   jax.ShapeDtypeStruct(state.shape, F32)],
        scratch_shapes=[pltpu.VMEM((GLA_HEADS, GLA_DK, GLA_DV), F32)],
        compiler_params=_cp(("arbitrary",)),
        name="gla_mix",
    )(proj, proj, proj, proj, log_a, gn.reshape(GLA_HEADS, 1, GLA_DV),
      proj, proj, proj, proj, log_a, state)


def _router_kernel(n_p_tiles, m_ref, xp_ref, xs_ref, g_ref, wr_ref, br_ref, x_ref, hm_ref, idx_ref, gate_ref):
    tr = m_ref.shape[0]
    i = pl.program_id(0)

    @pl.when(i < n_p_tiles)
    def _():
        x_ref[...] = xp_ref[...] + m_ref[...]

    @pl.when(i >= n_p_tiles)
    def _():
        x_ref[...] = xs_ref[...] + m_ref[...]

    hm = _rms_norm(x_ref[...], g_ref[...])
    hm_ref[...] = _pack_bf16_pairs(hm)
    h1, h2, _ = _split3(hm)
    w = wr_ref[...]
    w1 = w.astype(BF16)
    w2 = (w - w1.astype(F32)).astype(BF16)
    logits = _dot(h1, w1) + _dot(h1, w2) + _dot(h2, w1) + br_ref[...]
    lane = lax.broadcasted_iota(jnp.int32, (tr, LANES), 1).astype(F32)
    neg, far = -1e30, 1e9
    is_group = lane < N_GROUPS
    gl = jnp.where(is_group, logits, neg)
    gmax = jnp.max(gl, axis=-1, keepdims=True)
    gidx = jnp.min(jnp.where(gl == gmax, lane, far), axis=-1, keepdims=True)
    gsum = jnp.sum(jnp.where(is_group, jnp.exp(gl - gmax), 0.0), axis=-1, keepdims=True)
    g_p = 1.0 / gsum
    lo = N_GROUPS + EXPERTS_PER_GROUP * gidx
    in_sel = (lane >= lo) & (lane < lo + EXPERTS_PER_GROUP)
    el = jnp.where(in_sel, logits, neg)
    emax = jnp.max(el, axis=-1, keepdims=True)
    e1 = jnp.min(jnp.where(el == emax, lane, far), axis=-1, keepdims=True)
    esum = jnp.sum(jnp.where(in_sel, jnp.exp(el - emax), 0.0), axis=-1, keepdims=True)
    el2 = jnp.where(lane == e1, neg, el)
    m2 = jnp.max(el2, axis=-1, keepdims=True)
    e2 = jnp.min(jnp.where(el2 == m2, lane, far), axis=-1, keepdims=True)
    p1 = 1.0 / esum
    p2 = jnp.exp(m2 - emax) / esum
    den = p1 + p2
    idx_ref[...] = jnp.where(lane == 0, e1 - N_GROUPS, jnp.where(lane == 1, e2 - N_GROUPS, 0.0)).astype(jnp.int32)
    gate_ref[...] = jnp.where(lane == 0, g_p * p1 / den, jnp.where(lane == 1, g_p * p2 / den, 0.0))


def _router(m, x_p, x_s, g, w_r, b_r):
    t, d = m.shape
    t_p = x_p.shape[0]
    tr = ROW_TILE
    p_spec, s_spec = _two_group_specs((d,), tr, t_p // tr)
    row = pl.BlockSpec((tr, d), lambda i: (i, 0))
    lane_row = pl.BlockSpec((tr, LANES), lambda i: (i, 0))
    return pl.pallas_call(
        functools.partial(_router_kernel, t_p // tr),
        grid=(t // tr,),
        in_specs=[row, p_spec, s_spec, pl.BlockSpec((1, d), lambda i: (0, 0)),
                  pl.BlockSpec((d, LANES), lambda i: (0, 0)), pl.BlockSpec((1, LANES), lambda i: (0, 0))],
        out_specs=[row, pl.BlockSpec((tr, d // 2), lambda i: (i, 0)), lane_row, lane_row],
        out_shape=[jax.ShapeDtypeStruct((t, d), F32), jax.ShapeDtypeStruct((t, d // 2), jnp.uint32),
                   jax.ShapeDtypeStruct((t, LANES), jnp.int32), jax.ShapeDtypeStruct((t, LANES), F32)],
        compiler_params=_cp(("arbitrary",)),
        name="moe_router",
    )(m, x_p, x_s, g.reshape(1, d), w_r, b_r)


def _moe_kernel(n_used, item_e, item_start, item_n, tok, dst,
                hm_hbm, wg_ref, wu_ref, wd_ref, y_hbm,
                rows_in, rows_out, hid, sem_in, sem_out):
    del n_used, item_e
    i = pl.program_id(0)
    j = pl.program_id(1)
    n_items = pl.num_programs(0)
    n_steps = pl.num_programs(1)
    n_up = D_EXPERT // MOE_TJ
    max_blk = MOE_ROWS // MOE_BLK
    n, s0 = item_n[i], item_start[i]
    nxt = jnp.minimum(i + 1, n_items - 1)
    n_next, s_next = jnp.where(i + 1 < n_items, item_n[nxt], 0), item_start[nxt]
    prv = jnp.maximum(i - 1, 0)
    n_prev, s_prev = jnp.where(i > 0, item_n[prv], 0), item_start[prv]
    nblk = (n + MOE_BLK - 1) // MOE_BLK

    def gather(s, r):
        return pltpu.make_async_copy(hm_hbm.at[pl.ds(tok[s + r], 1), :], rows_in.at[pl.ds(r, 1), :], sem_in)

    def scatter(s, r):
        return pltpu.make_async_copy(rows_out.at[pl.ds(r, 1), :], y_hbm.at[pl.ds(dst[s + r], 1), :], sem_out)

    def gathered_group(c):
        rows = pl.ds(pl.multiple_of(c * MOE_DMA_UNROLL, MOE_DMA_UNROLL), MOE_DMA_UNROLL)
        return pltpu.make_async_copy(hm_hbm.at[pl.ds(0, MOE_DMA_UNROLL), :], rows_in.at[rows, :], sem_in)

    def scattered_group(c):
        rows = pl.ds(pl.multiple_of(c * MOE_DMA_UNROLL, MOE_DMA_UNROLL), MOE_DMA_UNROLL)
        return pltpu.make_async_copy(rows_out.at[rows, :], y_hbm.at[pl.ds(0, MOE_DMA_UNROLL), :], sem_out)

    def for_rows(count, fn, group_fn=None):
        groups = count // MOE_DMA_UNROLL

        def group(c, carry):
            if group_fn is None:
                for u in range(MOE_DMA_UNROLL):
                    fn(c * MOE_DMA_UNROLL + u)
            else:
                group_fn(c)
            return carry
        lax.fori_loop(0, groups, group, 0)

        def single(r, carry):
            fn(r)
            return carry
        lax.fori_loop(groups * MOE_DMA_UNROLL, count, single, 0)

    def for_row_count(fn):
        for k in range(1, max_blk + 1):
            @pl.when(nblk == k)
            def _(k=k):
                fn(k * MOE_BLK)

    @pl.when((i == 0) & (j == 0))
    def _():
        rows_in[...] = jnp.zeros_like(rows_in)
        for_rows(n, lambda r: gather(s0, r).start())

    @pl.when(j == 0)
    def _():
        for_rows(n, lambda r: gather(s0, r).wait(), lambda c: gathered_group(c).wait())

    @pl.when(j < n_up)
    def _():
        def up(m):
            x = _unpack_bf16_pairs(rows_in[0:m, :])
            act = _silu(_dot(x, wg_ref[0].astype(BF16))) * _dot(x, wu_ref[0].astype(BF16))
            hid[j, 0:m, :] = act.astype(BF16)
        for_row_count(up)

    @pl.when(j == n_up)
    def _():
        for_rows(n_next, lambda r: gather(s_next, r).start())
        for_rows(n_prev, lambda r: scatter(s_prev, r).wait(), lambda c: scattered_group(c).wait())

    @pl.when(j >= n_up)
    def _():
        col = pl.multiple_of((j - n_up) * MOE_TN, MOE_TN)

        def down(m):
            act = jnp.concatenate([hid[u, 0:m, :] for u in range(n_up)], axis=-1)
            rows_out[0:m, pl.ds(col, MOE_TN)] = _dot(act, wd_ref[0].astype(BF16))
        for_row_count(down)

    @pl.when(j == n_steps - 1)
    def _():
        for_rows(n, lambda r: scatter(s0, r).start())

    @pl.when((j == n_steps - 1) & (i == n_items - 1))
    def _():
        for_rows(n, lambda r: scatter(s0, r).wait(), lambda c: scattered_group(c).wait())


def _moe_experts(hm, w_gate, w_up, w_down, items, tok, dst, n_assign):
    n_used, item_e, item_start, item_n = items
    n_up = D_EXPERT // MOE_TJ
    n_dn = D_MODEL // MOE_TN

    def up_map(i, j, nu, e, st, n, tok, dst):
        return (e[i], 0, jnp.minimum(j, n_up - 1))

    def dn_map(i, j, nu, e, st, n, tok, dst):
        return (e[i], 0, jnp.maximum(j - n_up, 0))

    up_spec = pl.BlockSpec((1, D_MODEL, MOE_TJ), up_map)
    return pl.pallas_call(
        _moe_kernel,
        grid_spec=pltpu.PrefetchScalarGridSpec(
            num_scalar_prefetch=6,
            grid=(n_used[0], n_up + n_dn),
            in_specs=[pl.BlockSpec(memory_space=pl.ANY), up_spec, up_spec,
                      pl.BlockSpec((1, D_EXPERT, MOE_TN), dn_map)],
            out_specs=pl.BlockSpec(memory_space=pl.ANY),
            scratch_shapes=[pltpu.VMEM((MOE_ROWS, D_MODEL // 2), jnp.uint32),
                            pltpu.VMEM((MOE_ROWS, D_MODEL), F32),
                            pltpu.VMEM((n_up, MOE_ROWS, MOE_TJ), BF16),
                            pltpu.SemaphoreType.DMA(()),
                            pltpu.SemaphoreType.DMA(())],
        ),
        out_shape=jax.ShapeDtypeStruct((n_assign, D_MODEL), F32),
        compiler_params=_cp(("arbitrary", "arbitrary")),
        name="moe_experts",
    )(n_used, item_e, item_start, item_n, tok, dst, hm, w_gate, w_up, w_down)


def _moe_plan(expert_idx):
    n_assign = expert_idx.size
    flat_e = expert_idx.reshape(n_assign)
    order = jnp.argsort(flat_e).astype(jnp.int32)
    counts = jnp.bincount(flat_e, length=N_EXPERTS).astype(jnp.int32)
    start = jnp.cumsum(counts) - counts
    per_e = (counts + MOE_ROWS - 1) // MOE_ROWS
    item_end = jnp.cumsum(per_e)
    n_items = n_assign // MOE_ROWS + N_EXPERTS
    ids = jnp.arange(n_items, dtype=jnp.int32)
    used = ids < item_end[-1]
    last = jnp.maximum(item_end[-1] - 1, 0)
    e_of = jnp.minimum(jnp.searchsorted(item_end, jnp.minimum(ids, last), side="right"), N_EXPERTS - 1).astype(jnp.int32)
    local = jnp.minimum(ids, last) - (item_end - per_e)[e_of]
    item_start = start[e_of] + local * MOE_ROWS
    item_n = jnp.where(used, jnp.clip(counts[e_of] - local * MOE_ROWS, 0, MOE_ROWS), 0)
    items = (item_end[-1:].astype(jnp.int32), e_of, item_start.astype(jnp.int32), item_n.astype(jnp.int32))
    token = order // TOP_K
    dst_row = (order % TOP_K) * (n_assign // TOP_K) + token
    return items, token, dst_row


def _combine_kernel(x_ref, y0_ref, y1_ref, gate_ref, o_ref, ob_ref):
    x = x_ref[...] + (y0_ref[...] * gate_ref[:, 0:1] + y1_ref[...] * gate_ref[:, 1:2])
    o_ref[...] = x
    ob_ref[...] = x.astype(BF16)


def _combine(x, y, gate):
    t, d = x.shape
    tr = ROW_TILE
    row = pl.BlockSpec((tr, d), lambda i: (i, 0))
    return pl.pallas_call(
        _combine_kernel,
        grid=(t // tr,),
        in_specs=[row, row, pl.BlockSpec((tr, d), lambda i: (i + t // tr, 0)),
                  pl.BlockSpec((tr, LANES), lambda i: (i, 0))],
        out_specs=[row, row],
        out_shape=[jax.ShapeDtypeStruct((t, d), F32), jax.ShapeDtypeStruct((t, d), BF16)],
        compiler_params=_cp(("parallel",)),
        name="moe_combine",
    )(x, y, y, gate)


def _rope_tables(pos):
    half = RET_DK // 2
    inv = ROPE_BASE ** (-jnp.arange(half, dtype=F32) / half)
    ang = pos.astype(F32)[:, None] * inv[None, :]
    return jnp.cos(ang), jnp.sin(ang)


def kernel(x_prompt, x_sample, state_ret, state_gla, p_prompt, p_sample, g_mix, w_in, w_gla_up, b_gla,
           ret_norm_g, gla_norm_g, w_out, g_moe, w_rg, b_rg, w_re, b_re, w_gate, w_up, w_down, w_pg, w_pp,
           g_final):
    n_p, l_p, d = x_prompt.shape
    n_s, l_s, _ = x_sample.shape
    depth = g_mix.shape[0]
    t_p, t_s = n_p * l_p, n_s * l_s
    t = t_p + t_s

    log_g_py = [math.log1p(-(2.0 ** (-5.0 - h))) for h in range(RET_HEADS)]
    cos_p, sin_p = _rope_tables(jnp.arange(l_p, dtype=jnp.int32))
    cos_s, sin_s = _rope_tables(PAST_LEN + jnp.arange(l_s, dtype=jnp.int32))
    cos_s, sin_s = jnp.tile(cos_s, (DEC_PAIR, 1)), jnp.tile(sin_s, (DEC_PAIR, 1))

    x_p = x_prompt.reshape(t_p, d)
    x_s = x_sample.reshape(t_s, d)
    x = None
    ret_p, ret_s, gla_p, gla_s = [], [], [], []
    for l in range(depth):
        w_in_t = jnp.swapaxes(w_in[l], 0, 1)
        w_ga = jnp.pad(w_in_t[N_MAIN:], ((0, LANES - GLA_RANK), (0, 0))).astype(BF16)
        w_gup = jnp.pad(w_gla_up[l], ((0, LANES - GLA_RANK), (0, 0))).astype(BF16)
        w_r = jnp.pad(jnp.concatenate([w_rg[l], w_re[l]], axis=1),
                      ((0, 0), (0, LANES - N_GROUPS - N_EXPERTS)))
        b_r = jnp.pad(jnp.concatenate([b_rg[l], b_re[l]]), (0, LANES - N_GROUPS - N_EXPERTS)).reshape(1, LANES)
        p = jnp.concatenate([p_prompt[l].reshape(t_p, -1), p_sample[l].reshape(t_s, -1)], axis=0)
        if x is not None:
            x_p, x_s = x[:t_p], x[t_p:]

        h, log_a = _norm_in(x_p, x_s, g_mix[l], w_ga, w_gup, b_gla[l])
        proj = _in_proj(h, w_in_t, N_MAIN, t // 16, 1024)
        ro_p, sr_p, ro_s, sr_s = _ret_mix(proj, log_g_py, cos_p, sin_p, cos_s, sin_s, ret_norm_g[l], state_ret[l],
                                          n_p, l_p, n_s, l_s)
        go_p, sg_p, go_s, sg_s = _gla_mix(proj, log_a, gla_norm_g[l], state_gla[l], n_p, l_p, n_s, l_s)
        ret_p.append(sr_p)
        ret_s.append(sr_s)
        gla_p.append(sg_p)
        gla_s.append(sg_s)
        mix = _out_proj(ro_p, ro_s, go_p, go_s, w_out[l], 1024)

        x, hm, idx, gate = _router(mix, x_p, x_s, g_moe[l], w_r, b_r)
        items, tok, dst = _moe_plan(idx[:, :TOP_K])
        y = _moe_experts(hm, w_gate[l], w_up[l], w_down[l], items, tok, dst, t * TOP_K)
        x, xb = _combine(x, y, gate)

        x = _ple(xb, x, p, w_pg[l], w_pp[l], t // 8, 512)

    y_prompt = _norm_out(x, g_final, 0, t_p).reshape(n_p, l_p, d)
    y_sample = _norm_out(x, g_final, t_p, t_s).reshape(n_s, l_s, d)
    return (y_prompt, y_sample,
            jnp.stack(ret_p).astype(state_ret.dtype), jnp.stack(ret_s).astype(state_ret.dtype),
            jnp.stack(gla_p).astype(state_gla.dtype), jnp.stack(gla_s).astype(state_gla.dtype))
```

```python
import functools
import math

import jax
import jax.numpy as jnp
from jax import lax
from jax.experimental import pallas as pl
from jax.experimental.pallas import tpu as pltpu

F32 = jnp.float32
BF16 = jnp.bfloat16

D_MODEL = 4096
RET_HEADS = 8
RET_DK = 256
RET_DV = 256
GLA_HEADS = 4
GLA_DK = 256
GLA_DV = 512
GLA_RANK = 16
GLA_GATE_TEMP = 16.0
ROPE_BASE = 10000.0
PAST_LEN = 16384
N_GROUPS = 4
EXPERTS_PER_GROUP = 8
N_EXPERTS = N_GROUPS * EXPERTS_PER_GROUP
TOP_K = 2
D_EXPERT = D_MODEL // 4
EPS = 1e-6

RET_W = RET_HEADS * RET_DK
GLA_KW = GLA_HEADS * GLA_DK
GLA_VW = GLA_HEADS * GLA_DV
N_MAIN = 4 * RET_W + 2 * GLA_KW + 2 * GLA_VW
COL_GQ = 4 * RET_W
COL_GK = COL_GQ + GLA_KW
COL_GV = COL_GK + GLA_KW
COL_GG = COL_GV + GLA_VW

LANES = 128
ROW_TILE = 256
LHS_TILE = 512
RET_CHUNK = 128
GLA_CHUNK = 128
GLA_SUB = 32
DEC_PAIR = 2
MOE_ROWS = 768
MOE_BLK = 128
MOE_TJ = 256
MOE_TN = 2048
MOE_DMA_UNROLL = 32
VMEM_LIMIT = 56 * 1024 * 1024


def _cp(semantics, vmem=VMEM_LIMIT):
    return pltpu.CompilerParams(dimension_semantics=semantics, vmem_limit_bytes=vmem)


def _sigmoid(x):
    return 1.0 / (1.0 + jnp.exp(-x))


def _silu(x):
    return x * _sigmoid(x)


def _dot(a, b):
    return jnp.dot(a, b, preferred_element_type=F32)


def _dot_nt(a, b):
    return lax.dot_general(a, b, (((1,), (1,)), ((), ())), preferred_element_type=F32)


def _dot_tn(a, b):
    return lax.dot_general(a, b, (((0,), (0,)), ((), ())), preferred_element_type=F32)


def _rms_norm(x, g):
    ms = jnp.mean(x * x, axis=-1, keepdims=True)
    return x * lax.rsqrt(ms + EPS) * g


_HIGH_HALF = 0xFFFF0000


def _pack_bf16_pairs(x):
    half = x.shape[-1] // 2
    bits = lax.bitcast_convert_type(x.astype(BF16).astype(F32), jnp.uint32)
    return (bits[:, :half] >> 16) | (bits[:, half:] & jnp.uint32(_HIGH_HALF))


def _unpack_bf16_pairs(w):
    lo = lax.bitcast_convert_type(w << 16, F32)
    hi = lax.bitcast_convert_type(w & jnp.uint32(_HIGH_HALF), F32)
    return jnp.concatenate([lo, hi], axis=-1).astype(BF16)


def _two_group_specs(shape_tail, tile, n_p_tiles):
    zeros = (0,) * len(shape_tail)

    def p_map(*idx):
        return (jnp.minimum(idx[-1], n_p_tiles - 1),) + zeros

    def s_map(*idx):
        return (jnp.maximum(idx[-1] - n_p_tiles, 0),) + zeros

    return pl.BlockSpec((tile,) + shape_tail, p_map), pl.BlockSpec((tile,) + shape_tail, s_map)


def _log_sigmoid(z):
    return jnp.minimum(z, 0.0) - jnp.log(1.0 + jnp.exp(-jnp.abs(z)))


def _norm_in_kernel(n_p_tiles, xp_ref, xs_ref, g_ref, wga_ref, wup_ref, b_ref, h_ref, la_ref):
    i = pl.program_id(0)

    def emit(x):
        h = _rms_norm(x, g_ref[...]).astype(BF16)
        h_ref[...] = h
        ga = _dot_nt(h, wga_ref[...])
        z = _dot(ga.astype(BF16), wup_ref[...]) + b_ref[...]
        la_ref[...] = _log_sigmoid(z) * (1.0 / GLA_GATE_TEMP)

    @pl.when(i < n_p_tiles)
    def _():
        emit(xp_ref[...])

    @pl.when(i >= n_p_tiles)
    def _():
        emit(xs_ref[...])


def _norm_in(x_p, x_s, g, w_ga, w_up, b):
    (t_p, d), t_s = x_p.shape, x_s.shape[0]
    tr = ROW_TILE
    p_spec, s_spec = _two_group_specs((d,), tr, t_p // tr)
    return pl.pallas_call(
        functools.partial(_norm_in_kernel, t_p // tr),
        grid=((t_p + t_s) // tr,),
        in_specs=[p_spec, s_spec, pl.BlockSpec((1, d), lambda i: (0, 0)),
                  pl.BlockSpec((LANES, d), lambda i: (0, 0)),
                  pl.BlockSpec((LANES, GLA_KW), lambda i: (0, 0)),
                  pl.BlockSpec((1, GLA_KW), lambda i: (0, 0))],
        out_specs=[pl.BlockSpec((tr, d), lambda i: (i, 0)), pl.BlockSpec((tr, GLA_KW), lambda i: (i, 0))],
        out_shape=[jax.ShapeDtypeStruct((t_p + t_s, d), BF16), jax.ShapeDtypeStruct((t_p + t_s, GLA_KW), F32)],
        compiler_params=_cp(("arbitrary",)),
        name="norm_in",
    )(x_p, x_s, g.reshape(1, d), w_ga, w_up, b.reshape(1, GLA_KW))


def _norm_out_kernel(x_ref, g_ref, o_ref):
    o_ref[...] = _rms_norm(x_ref[...], g_ref[...])


def _norm_out(x, g, row0, n_rows):
    d = x.shape[1]
    tr = ROW_TILE
    blk0 = row0 // tr
    return pl.pallas_call(
        _norm_out_kernel,
        grid=(n_rows // tr,),
        in_specs=[pl.BlockSpec((tr, d), lambda i: (i + blk0, 0)), pl.BlockSpec((1, d), lambda i: (0, 0))],
        out_specs=pl.BlockSpec((tr, d), lambda i: (i, 0)),
        out_shape=jax.ShapeDtypeStruct((n_rows, d), F32),
        compiler_params=_cp(("parallel",)),
        name="norm_out",
    )(x, g.reshape(1, d))


def _in_proj_kernel(a_ref, wt_ref, o_ref):
    o_ref[...] = _dot_nt(a_ref[...], wt_ref[...].astype(BF16))


def _in_proj(h, w_t, n_cols, tm, tn):
    t, d = h.shape
    return pl.pallas_call(
        _in_proj_kernel,
        grid=(n_cols // tn, t // tm),
        in_specs=[pl.BlockSpec((tm, d), lambda j, i: (i, 0)), pl.BlockSpec((tn, d), lambda j, i: (j, 0))],
        out_specs=pl.BlockSpec((tm, tn), lambda j, i: (i, j)),
        out_shape=jax.ShapeDtypeStruct((t, n_cols), F32),
        compiler_params=_cp(("parallel", "parallel")),
        name="in_proj",
    )(h, w_t)


def _out_proj_kernel(n_p_tiles, rp_ref, rs_ref, gp_ref, gs_ref, w_ref, o_ref):
    i = pl.program_id(1)

    def emit(r_ref, g_ref):
        o_ref[...] = (_dot(r_ref[...], w_ref[:RET_W, :].astype(BF16))
                      + _dot(g_ref[...], w_ref[RET_W:, :].astype(BF16)))

    @pl.when(i < n_p_tiles)
    def _():
        emit(rp_ref, gp_ref)

    @pl.when(i >= n_p_tiles)
    def _():
        emit(rs_ref, gs_ref)


def _out_proj(ro_p, ro_s, go_p, go_s, w, tn):
    t_p, t_s = ro_p.shape[0], ro_s.shape[0]
    d_in, d_out = w.shape
    tm = LHS_TILE
    rp_spec, rs_spec = _two_group_specs((RET_W,), tm, t_p // tm)
    gp_spec, gs_spec = _two_group_specs((GLA_VW,), tm, t_p // tm)
    return pl.pallas_call(
        functools.partial(_out_proj_kernel, t_p // tm),
        grid=(d_out // tn, (t_p + t_s) // tm),
        in_specs=[rp_spec, rs_spec, gp_spec, gs_spec, pl.BlockSpec((d_in, tn), lambda j, i: (0, j))],
        out_specs=pl.BlockSpec((tm, tn), lambda j, i: (i, j)),
        out_shape=jax.ShapeDtypeStruct((t_p + t_s, d_out), F32),
        compiler_params=_cp(("arbitrary", "arbitrary")),
        name="out_proj",
    )(ro_p, ro_s, go_p, go_s, w)


def _ple_kernel(a_ref, w_ref, p_ref, wp_ref, r_ref, o_ref):
    gate = _sigmoid(_dot(a_ref[...], w_ref[...].astype(BF16)))
    emb = _dot(p_ref[...].astype(BF16), wp_ref[...].astype(BF16))
    o_ref[...] = r_ref[...] + gate * emb


def _ple(xb, x, p, w_pg, w_pp, tm, tn):
    t, d = x.shape
    kp = p.shape[1]
    o_spec = pl.BlockSpec((tm, tn), lambda j, i: (i, j))
    return pl.pallas_call(
        _ple_kernel,
        grid=(d // tn, t // tm),
        in_specs=[pl.BlockSpec((tm, d), lambda j, i: (i, 0)), pl.BlockSpec((d, tn), lambda j, i: (0, j)),
                  pl.BlockSpec((tm, kp), lambda j, i: (i, 0)), pl.BlockSpec((kp, tn), lambda j, i: (0, j)),
                  o_spec],
        out_specs=o_spec,
        out_shape=jax.ShapeDtypeStruct((t, d), F32),
        compiler_params=_cp(("parallel", "parallel")),
        name="ple",
    )(xb, w_pg, p, w_pp, x)


def _rotary(x, cos, sin):
    half = x.shape[-1] // 2
    x1, x2 = x[:, :half], x[:, half:]
    return jnp.concatenate([x1 * cos - x2 * sin, x1 * sin + x2 * cos], axis=-1)


def _group_norm_gate(o, gain, gate):
    mu = jnp.mean(o, axis=-1, keepdims=True)
    d = o - mu
    var = jnp.mean(d * d, axis=-1, keepdims=True)
    return d * lax.rsqrt(var + EPS) * gain * _silu(gate)


def _rms_gate(o, gain, gate):
    ms = jnp.mean(o * o, axis=-1, keepdims=True)
    return o * lax.rsqrt(ms + EPS) * gain * _silu(gate)


def _ret_prompt_body(log_g, first, i, n_chunks, q_ref, k_ref, v_ref, g_ref, cos_ref, sin_ref, gn_ref, o_ref, s_ref,
                     state, intra_tab, q_tab, k_tab):
    c = RET_CHUNK

    @pl.when(first)
    def _():
        row = lax.broadcasted_iota(jnp.int32, (c, RET_DK), 0).astype(F32)
        ii = lax.broadcasted_iota(jnp.int32, (c, c), 0)
        jj = lax.broadcasted_iota(jnp.int32, (c, c), 1)
        diff = (ii - jj).astype(F32)
        for h in range(RET_HEADS):
            intra_tab[h] = jnp.where(diff >= 0, jnp.exp(jnp.maximum(diff, 0.0) * log_g[h]), 0.0)
            q_tab[h] = jnp.exp((row + 1.0) * log_g[h])
            k_tab[h] = jnp.exp((c - 1.0 - row) * log_g[h])

    @pl.when(i == 0)
    def _():
        state[...] = jnp.zeros_like(state)

    cos, sin = cos_ref[...], sin_ref[...]
    for h in range(RET_HEADS):
        sl = slice(h * RET_DK, (h + 1) * RET_DK)
        q = _rotary(q_ref[:, sl], cos, sin)
        k = _rotary(k_ref[:, sl], cos, sin) * (RET_DK ** -0.5)
        v = v_ref[:, sl].astype(BF16)
        att = _dot_nt(q.astype(BF16), k.astype(BF16)) * intra_tab[h]
        s_old = state[h]
        o = _dot(att.astype(BF16), v) + _dot((q * q_tab[h]).astype(BF16), s_old.astype(BF16))
        state[h] = s_old * math.exp(c * log_g[h]) + _dot_tn((k * k_tab[h]).astype(BF16), v)
        o_ref[:, sl] = _group_norm_gate(o, gn_ref[h], g_ref[:, sl]).astype(o_ref.dtype)

    @pl.when(i == n_chunks - 1)
    def _():
        s_ref[0] = state[...]


def _ret_sample_body(log_g, seq, q_ref, k_ref, v_ref, g_ref, cos_ref, sin_ref, gn_ref, s_in, o_ref, s_out):
    rows = DEC_PAIR * seq
    cos, sin = cos_ref[...], sin_ref[...]
    rid = lax.broadcasted_iota(jnp.int32, (rows, RET_DK), 0)
    pos = (rid % seq).astype(F32)
    batch_of_row = rid // seq
    ii = lax.broadcasted_iota(jnp.int32, (rows, rows), 0)
    jj = lax.broadcasted_iota(jnp.int32, (rows, rows), 1)
    visible = (ii // seq == jj // seq) & (ii >= jj)
    diff = jnp.maximum(ii - jj, 0).astype(F32)
    for h in range(RET_HEADS):
        lg = log_g[h]
        sl = slice(h * RET_DK, (h + 1) * RET_DK)
        q = _rotary(q_ref[:, sl], cos, sin)
        k = _rotary(k_ref[:, sl], cos, sin) * (RET_DK ** -0.5)
        v = v_ref[:, sl].astype(BF16)
        intra = jnp.where(visible, jnp.exp(diff * lg), 0.0)
        att = _dot_nt(q.astype(BF16), k.astype(BF16)) * intra
        o = _dot(att.astype(BF16), v)
        qd = (q * jnp.exp((pos + 1.0) * lg)).astype(BF16)
        kd = k * jnp.exp((seq - 1.0 - pos) * lg)
        for b in range(DEC_PAIR):
            mine = batch_of_row == b
            s_old = s_in[b, h]
            o = o + jnp.where(mine, _dot(qd, s_old.astype(BF16)), 0.0)
            s_out[b, h] = s_old * math.exp(seq * lg) + _dot_tn(jnp.where(mine, kd, 0.0).astype(BF16), v)
        o_ref[:, sl] = _group_norm_gate(o, gn_ref[h], g_ref[:, sl]).astype(o_ref.dtype)


def _ret_mix_kernel(log_g, seq, n_chunks, qp, kp, vp, gp, cos_p, sin_p, gn, qs, ks, vs, gs, cos_s, sin_s, s_in,
                    o_p, s_p, o_s, s_out, state, intra_tab, q_tab, k_tab):
    s = pl.program_id(0)
    _ret_prompt_body(log_g, s == 0, s % n_chunks, n_chunks, qp, kp, vp, gp, cos_p, sin_p, gn, o_p, s_p,
                     state, intra_tab, q_tab, k_tab)
    _ret_sample_body(log_g, seq, qs, ks, vs, gs, cos_s, sin_s, gn, s_in, o_s, s_out)


def _ret_mix(proj, log_g, cos_p, sin_p, cos_s, sin_s, gn, state, n_p, l_p, n_s, l_s):
    c = RET_CHUNK
    nc = l_p // c
    rows = DEC_PAIR * l_s
    steps = n_p * nc
    assert steps * DEC_PAIR == n_s, (steps, n_s)
    blk0 = (n_p * l_p) // rows

    def col_p(group):
        return pl.BlockSpec((c, RET_W), lambda s: (s, group))

    def col_s(group):
        return pl.BlockSpec((rows, RET_W), lambda s: (s + blk0, group))

    tab_p = pl.BlockSpec((c, RET_DK // 2), lambda s: (s % nc, 0))
    tab_s = pl.BlockSpec((rows, RET_DK // 2), lambda s: (0, 0))
    st = pl.BlockSpec((DEC_PAIR, RET_HEADS, RET_DK, RET_DV), lambda s: (s, 0, 0, 0))
    return pl.pallas_call(
        functools.partial(_ret_mix_kernel, log_g, l_s, nc),
        grid=(steps,),
        in_specs=[col_p(0), col_p(1), col_p(2), col_p(3), tab_p, tab_p,
                  pl.BlockSpec((RET_HEADS, 1, RET_DV), lambda s: (0, 0, 0)),
                  col_s(0), col_s(1), col_s(2), col_s(3), tab_s, tab_s, st],
        out_specs=[pl.BlockSpec((c, RET_W), lambda s: (s, 0)),
                   pl.BlockSpec((1, RET_HEADS, RET_DK, RET_DV), lambda s: (s // nc, 0, 0, 0)),
                   pl.BlockSpec((rows, RET_W), lambda s: (s, 0)), st],
        out_shape=[jax.ShapeDtypeStruct((n_p * l_p, RET_W), BF16),
                   jax.ShapeDtypeStruct((n_p, RET_HEADS, RET_DK, RET_DV), F32),
                   jax.ShapeDtypeStruct((n_s * l_s, RET_W), BF16),
                   jax.ShapeDtypeStruct(state.shape, F32)],
        scratch_shapes=[pltpu.VMEM((RET_HEADS, RET_DK, RET_DV), F32),
                        pltpu.VMEM((RET_HEADS, c, c), F32),
                        pltpu.VMEM((RET_HEADS, c, RET_DK), F32),
                        pltpu.VMEM((RET_HEADS, c, RET_DK), F32)],
        compiler_params=_cp(("arbitrary",)),
        name="retention_mix",
    )(proj, proj, proj, proj, cos_p, sin_p, gn.reshape(RET_HEADS, 1, RET_DV),
      proj, proj, proj, proj, cos_s, sin_s, state)


def _split3(x):
    hi = x.astype(BF16)
    r1 = x - hi.astype(F32)
    mid = r1.astype(BF16)
    lo = (r1 - mid.astype(F32)).astype(BF16)
    return hi, mid, lo


def _column_scale(row_vec, width):
    n = row_vec.shape[-1]
    t = jnp.transpose(jnp.broadcast_to(row_vec, (LANES, n)))
    return jnp.concatenate([t] * (width // LANES), axis=-1)


def _gla_prompt_body(i, n_chunks, q_ref, k_ref, v_ref, g_ref, la_ref, gn_ref, o_ref, s_ref, state):
    c, sub = GLA_CHUNK, GLA_SUB
    nsub = c // sub

    @pl.when(i == 0)
    def _():
        state[...] = jnp.zeros_like(state)

    ii = lax.broadcasted_iota(jnp.int32, (c, c), 0)
    jj = lax.broadcasted_iota(jnp.int32, (c, c), 1)
    causal = ii >= jj
    tri = jnp.where(causal, 1.0, 0.0).astype(BF16)
    for h in range(GLA_HEADS):
        ks = slice(h * GLA_DK, (h + 1) * GLA_DK)
        vs = slice(h * GLA_DV, (h + 1) * GLA_DV)
        q = q_ref[:, ks]
        k = k_ref[:, ks] * (GLA_DK ** -0.5)
        v = v_ref[:, vs].astype(BF16)
        hi, mid, lo = _split3(la_ref[:, ks])
        b = _dot(tri, hi) + _dot(tri, mid) + _dot(tri, lo)
        mids = [b[s * sub + sub // 2 - 1: s * sub + sub // 2, :] for s in range(nsub)]
        ref_lvl = jnp.concatenate([jnp.broadcast_to(m, (sub, GLA_DK)) for m in mids], axis=0)
        qd = (q * jnp.exp(b - ref_lvl)).astype(BF16)
        kd = k * jnp.exp(ref_lvl - b)
        rows = []
        for s in range(nsub):
            live = (s + 1) * sub
            keys = kd[:live] * jnp.exp(jnp.minimum(mids[s] - ref_lvl[:live], 0.0))
            if live < c:
                keys = jnp.concatenate([keys, jnp.zeros((c - live, GLA_DK), F32)], axis=0)
            rows.append(_dot_nt(qd[s * sub:(s + 1) * sub], keys.astype(BF16)))
        att = jnp.where(causal, jnp.concatenate(rows, axis=0), 0.0)
        s_old = state[h]
        o = _dot(att.astype(BF16), v) + _dot((q * jnp.exp(b)).astype(BF16), s_old.astype(BF16))
        b_last = b[c - 1:c, :]
        k_rem = (k * jnp.exp(b_last - b)).astype(BF16)
        state[h] = s_old * _column_scale(jnp.exp(b_last), GLA_DV) + _dot_tn(k_rem, v)
        o_ref[:, vs] = _rms_gate(o, gn_ref[h], g_ref[:, vs]).astype(o_ref.dtype)

    @pl.when(i == n_chunks - 1)
    def _():
        s_ref[0] = state[...]


def _gla_sample_body(seq, q_ref, k_ref, v_ref, g_ref, la_ref, gn_ref, s_in, o_ref, s_out):
    rows = DEC_PAIR * seq
    rid = lax.broadcasted_iota(jnp.int32, (rows, GLA_DK), 0)
    pos = rid % seq
    batch_of_row = rid // seq
    ii = lax.broadcasted_iota(jnp.int32, (rows, rows), 0)
    jj = lax.broadcasted_iota(jnp.int32, (rows, rows), 1)
    visible = (ii // seq == jj // seq) & (ii >= jj)
    for h in range(GLA_HEADS):
        ks = slice(h * GLA_DK, (h + 1) * GLA_DK)
        vs = slice(h * GLA_DV, (h + 1) * GLA_DV)
        la = la_ref[:, ks]
        b = la
        for d in range(1, seq):
            b = b + jnp.where(pos >= d, pltpu.roll(la, d, axis=0), 0.0)
        q = q_ref[:, ks]
        k = k_ref[:, ks] * (GLA_DK ** -0.5)
        v = v_ref[:, vs].astype(BF16)
        qb = (q * jnp.exp(b)).astype(BF16)
        kb = (k * jnp.exp(-b)).astype(BF16)
        att = jnp.where(visible, _dot_nt(qb, kb), 0.0)
        o = _dot(att.astype(BF16), v)
        for bi in range(DEC_PAIR):
            mine = batch_of_row == bi
            last = bi * seq + seq - 1
            b_last = b[last:last + 1, :]
            s_old = s_in[bi, h]
            o = o + jnp.where(mine[:, :1], _dot(qb, s_old.astype(BF16)), 0.0)
            k_rem = jnp.where(mine, k * jnp.exp(b_last - b), 0.0).astype(BF16)
            s_out[bi, h] = s_old * _column_scale(jnp.exp(b_last), GLA_DV) + _dot_tn(k_rem, v)
        o_ref[:, vs] = _rms_gate(o, gn_ref[h], g_ref[:, vs]).astype(o_ref.dtype)


def _gla_mix_kernel(seq, n_chunks, qp, kp, vp, gp, la_p, gn, qs, ks, vs, gs, la_s, s_in,
                    o_p, s_p, o_s, s_out, state):
    s = pl.program_id(0)
    _gla_prompt_body(s % n_chunks, n_chunks, qp, kp, vp, gp, la_p, gn, o_p, s_p, state)
    _gla_sample_body(seq, qs, ks, vs, gs, la_s, gn, s_in, o_s, s_out)


def _gla_mix(proj, log_a, gn, state, n_p, l_p, n_s, l_s):
    c = GLA_CHUNK
    nc = l_p // c
    rows = DEC_PAIR * l_s
    steps = n_p * nc
    assert steps * DEC_PAIR == n_s, (steps, n_s)
    blk0 = (n_p * l_p) // rows

    def col_p(start, width):
        return pl.BlockSpec((c, width), lambda s: (s, start // width))

    def col_s(start, width):
        return pl.BlockSpec((rows, width), lambda s: (s + blk0, start // width))

    st = pl.BlockSpec((DEC_PAIR, GLA_HEADS, GLA_DK, GLA_DV), lambda s: (s, 0, 0, 0))
    return pl.pallas_call(
        functools.partial(_gla_mix_kernel, l_s, nc),
        grid=(steps,),
        in_specs=[col_p(COL_GQ, GLA_KW), col_p(COL_GK, GLA_KW), col_p(COL_GV, GLA_VW), col_p(COL_GG, GLA_VW),
                  pl.BlockSpec((c, GLA_KW), lambda s: (s, 0)),
                  pl.BlockSpec((GLA_HEADS, 1, GLA_DV), lambda s: (0, 0, 0)),
                  col_s(COL_GQ, GLA_KW), col_s(COL_GK, GLA_KW), col_s(COL_GV, GLA_VW), col_s(COL_GG, GLA_VW),
                  pl.BlockSpec((rows, GLA_KW), lambda s: (s + blk0, 0)), st],
        out_specs=[pl.BlockSpec((c, GLA_VW), lambda s: (s, 0)),
                   pl.BlockSpec((1, GLA_HEADS, GLA_DK, GLA_DV), lambda s: (s // nc, 0, 0, 0)),
                   pl.BlockSpec((rows, GLA_VW), lambda s: (s, 0)), st],
        out_shape=[jax.ShapeDtypeStruct((n_p * l_p, GLA_VW), BF16),
                   jax.ShapeDtypeStruct((n_p, GLA_HEADS, GLA_DK, GLA_DV), F32),
                   jax.ShapeDtypeStruct((n_s * l_s, GLA_VW), BF16),
                   jax.ShapeDtypeStruct(state.shape, F32)],
        scratch_shapes=[pltpu.VMEM((GLA_HEADS, GLA_DK, GLA_DV), F32)],
        compiler_params=_cp(("arbitrary",)),
        name="gla_mix",
    )(proj, proj, proj, proj, log_a, gn.reshape(GLA_HEADS, 1, GLA_DV),
      proj, proj, proj, proj, log_a, state)


def _router_kernel(n_p_tiles, m_ref, xp_ref, xs_ref, g_ref, wr_ref, br_ref, x_ref, hm_ref, idx_ref, gate_ref):
    tr = m_ref.shape[0]
    i = pl.program_id(0)

    @pl.when(i < n_p_tiles)
    def _():
        x_ref[...] = xp_ref[...] + m_ref[...]

    @pl.when(i >= n_p_tiles)
    def _():
        x_ref[...] = xs_ref[...] + m_ref[...]

    hm = _rms_norm(x_ref[...], g_ref[...])
    hm_ref[...] = _pack_bf16_pairs(hm)
    h1, h2, _ = _split3(hm)
    w = wr_ref[...]
    w1 = w.astype(BF16)
    w2 = (w - w1.astype(F32)).astype(BF16)
    logits = _dot(h1, w1) + _dot(h1, w2) + _dot(h2, w1) + br_ref[...]
    lane = lax.broadcasted_iota(jnp.int32, (tr, LANES), 1).astype(F32)
    neg, far = -1e30, 1e9
    is_group = lane < N_GROUPS
    gl = jnp.where(is_group, logits, neg)
    gmax = jnp.max(gl, axis=-1, keepdims=True)
    gidx = jnp.min(jnp.where(gl == gmax, lane, far), axis=-1, keepdims=True)
    gsum = jnp.sum(jnp.where(is_group, jnp.exp(gl - gmax), 0.0), axis=-1, keepdims=True)
    g_p = 1.0 / gsum
    lo = N_GROUPS + EXPERTS_PER_GROUP * gidx
    in_sel = (lane >= lo) & (lane < lo + EXPERTS_PER_GROUP)
    el = jnp.where(in_sel, logits, neg)
    emax = jnp.max(el, axis=-1, keepdims=True)
    e1 = jnp.min(jnp.where(el == emax, lane, far), axis=-1, keepdims=True)
    esum = jnp.sum(jnp.where(in_sel, jnp.exp(el - emax), 0.0), axis=-1, keepdims=True)
    el2 = jnp.where(lane == e1, neg, el)
    m2 = jnp.max(el2, axis=-1, keepdims=True)
    e2 = jnp.min(jnp.where(el2 == m2, lane, far), axis=-1, keepdims=True)
    p1 = 1.0 / esum
    p2 = jnp.exp(m2 - emax) / esum
    den = p1 + p2
    idx_ref[...] = jnp.where(lane == 0, e1 - N_GROUPS, jnp.where(lane == 1, e2 - N_GROUPS, 0.0)).astype(jnp.int32)
    gate_ref[...] = jnp.where(lane == 0, g_p * p1 / den, jnp.where(lane == 1, g_p * p2 / den, 0.0))


def _router(m, x_p, x_s, g, w_r, b_r):
    t, d = m.shape
    t_p = x_p.shape[0]
    tr = ROW_TILE
    p_spec, s_spec = _two_group_specs((d,), tr, t_p // tr)
    row = pl.BlockSpec((tr, d), lambda i: (i, 0))
    lane_row = pl.BlockSpec((tr, LANES), lambda i: (i, 0))
    return pl.pallas_call(
        functools.partial(_router_kernel, t_p // tr),
        grid=(t // tr,),
        in_specs=[row, p_spec, s_spec, pl.BlockSpec((1, d), lambda i: (0, 0)),
                  pl.BlockSpec((d, LANES), lambda i: (0, 0)), pl.BlockSpec((1, LANES), lambda i: (0, 0))],
        out_specs=[row, pl.BlockSpec((tr, d // 2), lambda i: (i, 0)), lane_row, lane_row],
        out_shape=[jax.ShapeDtypeStruct((t, d), F32), jax.ShapeDtypeStruct((t, d // 2), jnp.uint32),
                   jax.ShapeDtypeStruct((t, LANES), jnp.int32), jax.ShapeDtypeStruct((t, LANES), F32)],
        compiler_params=_cp(("arbitrary",)),
        name="moe_router",
    )(m, x_p, x_s, g.reshape(1, d), w_r, b_r)


def _moe_kernel(n_used, item_e, item_start, item_n, tok, dst,
                hm_hbm, wg_ref, wu_ref, wd_ref, y_hbm,
                rows_in, rows_out, hid, sem_in, sem_out):
    del n_used, item_e
    i = pl.program_id(0)
    j = pl.program_id(1)
    n_items = pl.num_programs(0)
    n_steps = pl.num_programs(1)
    n_up = D_EXPERT // MOE_TJ
    max_blk = MOE_ROWS // MOE_BLK
    n, s0 = item_n[i], item_start[i]
    nxt = jnp.minimum(i + 1, n_items - 1)
    n_next, s_next = jnp.where(i + 1 < n_items, item_n[nxt], 0), item_start[nxt]
    prv = jnp.maximum(i - 1, 0)
    n_prev, s_prev = jnp.where(i > 0, item_n[prv], 0), item_start[prv]
    nblk = (n + MOE_BLK - 1) // MOE_BLK

    def gather(s, r):
        return pltpu.make_async_copy(hm_hbm.at[pl.ds(tok[s + r], 1), :], rows_in.at[pl.ds(r, 1), :], sem_in)

    def scatter(s, r):
        return pltpu.make_async_copy(rows_out.at[pl.ds(r, 1), :], y_hbm.at[pl.ds(dst[s + r], 1), :], sem_out)

    def gathered_group(c):
        rows = pl.ds(pl.multiple_of(c * MOE_DMA_UNROLL, MOE_DMA_UNROLL), MOE_DMA_UNROLL)
        return pltpu.make_async_copy(hm_hbm.at[pl.ds(0, MOE_DMA_UNROLL), :], rows_in.at[rows, :], sem_in)

    def scattered_group(c):
        rows = pl.ds(pl.multiple_of(c * MOE_DMA_UNROLL, MOE_DMA_UNROLL), MOE_DMA_UNROLL)
        return pltpu.make_async_copy(rows_out.at[rows, :], y_hbm.at[pl.ds(0, MOE_DMA_UNROLL), :], sem_out)

    def for_rows(count, fn, group_fn=None):
        groups = count // MOE_DMA_UNROLL

        def group(c, carry):
            if group_fn is None:
                for u in range(MOE_DMA_UNROLL):
                    fn(c * MOE_DMA_UNROLL + u)
            else:
                group_fn(c)
            return carry
        lax.fori_loop(0, groups, group, 0)

        def single(r, carry):
            fn(r)
            return carry
        lax.fori_loop(groups * MOE_DMA_UNROLL, count, single, 0)

    def for_row_count(fn):
        for k in range(1, max_blk + 1):
            @pl.when(nblk == k)
            def _(k=k):
                fn(k * MOE_BLK)

    @pl.when((i == 0) & (j == 0))
    def _():
        rows_in[...] = jnp.zeros_like(rows_in)
        for_rows(n, lambda r: gather(s0, r).start())

    @pl.when(j == 0)
    def _():
        for_rows(n, lambda r: gather(s0, r).wait(), lambda c: gathered_group(c).wait())

    @pl.when(j < n_up)
    def _():
        def up(m):
            x = _unpack_bf16_pairs(rows_in[0:m, :])
            act = _silu(_dot(x, wg_ref[0].astype(BF16))) * _dot(x, wu_ref[0].astype(BF16))
            hid[j, 0:m, :] = act.astype(BF16)
        for_row_count(up)

    @pl.when(j == n_up)
    def _():
        for_rows(n_next, lambda r: gather(s_next, r).start())
        for_rows(n_prev, lambda r: scatter(s_prev, r).wait(), lambda c: scattered_group(c).wait())

    @pl.when(j >= n_up)
    def _():
        col = pl.multiple_of((j - n_up) * MOE_TN, MOE_TN)

        def down(m):
            act = jnp.concatenate([hid[u, 0:m, :] for u in range(n_up)], axis=-1)
            rows_out[0:m, pl.ds(col, MOE_TN)] = _dot(act, wd_ref[0].astype(BF16))
        for_row_count(down)

    @pl.when(j == n_steps - 1)
    def _():
        for_rows(n, lambda r: scatter(s0, r).start())

    @pl.when((j == n_steps - 1) & (i == n_items - 1))
    def _():
        for_rows(n, lambda r: scatter(s0, r).wait(), lambda c: scattered_group(c).wait())


def _moe_experts(hm, w_gate, w_up, w_down, items, tok, dst, n_assign):
    n_used, item_e, item_start, item_n = items
    n_up = D_EXPERT // MOE_TJ
    n_dn = D_MODEL // MOE_TN

    def up_map(i, j, nu, e, st, n, tok, dst):
        return (e[i], 0, jnp.minimum(j, n_up - 1))

    def dn_map(i, j, nu, e, st, n, tok, dst):
        return (e[i], 0, jnp.maximum(j - n_up, 0))

    up_spec = pl.BlockSpec((1, D_MODEL, MOE_TJ), up_map)
    return pl.pallas_call(
        _moe_kernel,
        grid_spec=pltpu.PrefetchScalarGridSpec(
            num_scalar_prefetch=6,
            grid=(n_used[0], n_up + n_dn),
            in_specs=[pl.BlockSpec(memory_space=pl.ANY), up_spec, up_spec,
                      pl.BlockSpec((1, D_EXPERT, MOE_TN), dn_map)],
            out_specs=pl.BlockSpec(memory_space=pl.ANY),
            scratch_shapes=[pltpu.VMEM((MOE_ROWS, D_MODEL // 2), jnp.uint32),
                            pltpu.VMEM((MOE_ROWS, D_MODEL), F32),
                            pltpu.VMEM((n_up, MOE_ROWS, MOE_TJ), BF16),
                            pltpu.SemaphoreType.DMA(()),
                            pltpu.SemaphoreType.DMA(())],
        ),
        out_shape=jax.ShapeDtypeStruct((n_assign, D_MODEL), F32),
        compiler_params=_cp(("arbitrary", "arbitrary")),
        name="moe_experts",
    )(n_used, item_e, item_start, item_n, tok, dst, hm, w_gate, w_up, w_down)


def _moe_plan(expert_idx):
    n_assign = expert_idx.size
    flat_e = expert_idx.reshape(n_assign)
    order = jnp.argsort(flat_e).astype(jnp.int32)
    counts = jnp.bincount(flat_e, length=N_EXPERTS).astype(jnp.int32)
    start = jnp.cumsum(counts) - counts
    per_e = (counts + MOE_ROWS - 1) // MOE_ROWS
    item_end = jnp.cumsum(per_e)
    n_items = n_assign // MOE_ROWS + N_EXPERTS
    ids = jnp.arange(n_items, dtype=jnp.int32)
    used = ids < item_end[-1]
    last = jnp.maximum(item_end[-1] - 1, 0)
    e_of = jnp.minimum(jnp.searchsorted(item_end, jnp.minimum(ids, last), side="right"), N_EXPERTS - 1).astype(jnp.int32)
    local = jnp.minimum(ids, last) - (item_end - per_e)[e_of]
    item_start = start[e_of] + local * MOE_ROWS
    item_n = jnp.where(used, jnp.clip(counts[e_of] - local * MOE_ROWS, 0, MOE_ROWS), 0)
    items = (item_end[-1:].astype(jnp.int32), e_of, item_start.astype(jnp.int32), item_n.astype(jnp.int32))
    token = order // TOP_K
    dst_row = (order % TOP_K) * (n_assign // TOP_K) + token
    return items, token, dst_row


def _combine_kernel(x_ref, y0_ref, y1_ref, gate_ref, o_ref, ob_ref):
    x = x_ref[...] + (y0_ref[...] * gate_ref[:, 0:1] + y1_ref[...] * gate_ref[:, 1:2])
    o_ref[...] = x
    ob_ref[...] = x.astype(BF16)


def _combine(x, y, gate):
    t, d = x.shape
    tr = ROW_TILE
    row = pl.BlockSpec((tr, d), lambda i: (i, 0))
    return pl.pallas_call(
        _combine_kernel,
        grid=(t // tr,),
        in_specs=[row, row, pl.BlockSpec((tr, d), lambda i: (i + t // tr, 0)),
                  pl.BlockSpec((tr, LANES), lambda i: (i, 0))],
        out_specs=[row, row],
        out_shape=[jax.ShapeDtypeStruct((t, d), F32), jax.ShapeDtypeStruct((t, d), BF16)],
        compiler_params=_cp(("parallel",)),
        name="moe_combine",
    )(x, y, y, gate)


def _rope_tables(pos):
    half = RET_DK // 2
    inv = ROPE_BASE ** (-jnp.arange(half, dtype=F32) / half)
    ang = pos.astype(F32)[:, None] * inv[None, :]
    return jnp.cos(ang), jnp.sin(ang)


def kernel(x_prompt, x_sample, state_ret, state_gla, p_prompt, p_sample, g_mix, w_in, w_gla_up, b_gla,
           ret_norm_g, gla_norm_g, w_out, g_moe, w_rg, b_rg, w_re, b_re, w_gate, w_up, w_down, w_pg, w_pp,
           g_final):
    n_p, l_p, d = x_prompt.shape
    n_s, l_s, _ = x_sample.shape
    depth = g_mix.shape[0]
    t_p, t_s = n_p * l_p, n_s * l_s
    t = t_p + t_s

    log_g_py = [math.log1p(-(2.0 ** (-5.0 - h))) for h in range(RET_HEADS)]
    cos_p, sin_p = _rope_tables(jnp.arange(l_p, dtype=jnp.int32))
    cos_s, sin_s = _rope_tables(PAST_LEN + jnp.arange(l_s, dtype=jnp.int32))
    cos_s, sin_s = jnp.tile(cos_s, (DEC_PAIR, 1)), jnp.tile(sin_s, (DEC_PAIR, 1))

    x_p = x_prompt.reshape(t_p, d)
    x_s = x_sample.reshape(t_s, d)
    x = None
    ret_p, ret_s, gla_p, gla_s = [], [], [], []
    for l in range(depth):
        w_in_t = jnp.swapaxes(w_in[l], 0, 1)
        w_ga = jnp.pad(w_in_t[N_MAIN:], ((0, LANES - GLA_RANK), (0, 0))).astype(BF16)
        w_gup = jnp.pad(w_gla_up[l], ((0, LANES - GLA_RANK), (0, 0))).astype(BF16)
        w_r = jnp.pad(jnp.concatenate([w_rg[l], w_re[l]], axis=1),
                      ((0, 0), (0, LANES - N_GROUPS - N_EXPERTS)))
        b_r = jnp.pad(jnp.concatenate([b_rg[l], b_re[l]]), (0, LANES - N_GROUPS - N_EXPERTS)).reshape(1, LANES)
        p = jnp.concatenate([p_prompt[l].reshape(t_p, -1), p_sample[l].reshape(t_s, -1)], axis=0)
        if x is not None:
            x_p, x_s = x[:t_p], x[t_p:]

        h, log_a = _norm_in(x_p, x_s, g_mix[l], w_ga, w_gup, b_gla[l])
        proj = _in_proj(h, w_in_t, N_MAIN, t // 16, 1024)
        ro_p, sr_p, ro_s, sr_s = _ret_mix(proj, log_g_py, cos_p, sin_p, cos_s, sin_s, ret_norm_g[l], state_ret[l],
                                          n_p, l_p, n_s, l_s)
        go_p, sg_p, go_s, sg_s = _gla_mix(proj, log_a, gla_norm_g[l], state_gla[l], n_p, l_p, n_s, l_s)
        ret_p.append(sr_p)
        ret_s.append(sr_s)
        gla_p.append(sg_p)
        gla_s.append(sg_s)
        mix = _out_proj(ro_p, ro_s, go_p, go_s, w_out[l], 1024)

        x, hm, idx, gate = _router(mix, x_p, x_s, g_moe[l], w_r, b_r)
        items, tok, dst = _moe_plan(idx[:, :TOP_K])
        y = _moe_experts(hm, w_gate[l], w_up[l], w_down[l], items, tok, dst, t * TOP_K)
        x, xb = _combine(x, y, gate)

        x = _ple(xb, x, p, w_pg[l], w_pp[l], t // 8, 512)

    y_prompt = _norm_out(x, g_final, 0, t_p).reshape(n_p, l_p, d)
    y_sample = _norm_out(x, g_final, t_p, t_s).reshape(n_s, l_s, d)
    return (y_prompt, y_sample,
            jnp.stack(ret_p).astype(state_ret.dtype), jnp.stack(ret_s).astype(state_ret.dtype),
            jnp.stack(gla_p).astype(state_gla.dtype), jnp.stack(gla_s).astype(state_gla.dtype))
```
